```python
import math
import jax, jax.numpy as jnp
from jax import lax
import numpy as np

D_MODEL = 1024
BATCH = 4
SEQ = 4096
DEPTH = 4
DEC_BATCH = 32
DEC_SEQ = 1
PAST_LEN = 8192
PAGE_SIZE = 128

N_MIXERS = 2
N_SSD_LAYERS = (DEPTH + 1) // 2
N_NSA_LAYERS = DEPTH // 2
NORM_EPS = 1e-6

SSD_D_INNER = 2 * D_MODEL
SSD_HEAD_DIM = 64
SSD_N_HEADS = SSD_D_INNER // SSD_HEAD_DIM
SSD_N_GROUPS = 8
SSD_HEADS_PER_GROUP = SSD_N_HEADS // SSD_N_GROUPS
SSD_D_STATE = 128
SSD_CONV_W = 4
SSD_CONV_DIM = SSD_D_INNER + 2 * SSD_N_GROUPS * SSD_D_STATE
SSD_CHUNK = 128
SSD_IN_DIM = SSD_D_INNER + SSD_CONV_DIM + SSD_N_HEADS

NSA_N_HEADS = 16
NSA_HEAD_DIM = 64
NSA_N_KV = 4
NSA_HEADS_PER_KV = NSA_N_HEADS // NSA_N_KV
NSA_WIDTH = NSA_N_HEADS * NSA_HEAD_DIM
NSA_KV_WIDTH = NSA_N_KV * NSA_HEAD_DIM
N_BRANCH = 3
CMP_BLOCK = 32
CMP_STRIDE = 16
CMP_HIDDEN = 128
SEL_BLOCK = 64
SEL_TOPK = 16
WINDOW = 512
NSA_QBLOCK = 64
FORCE_SCORE = 1e4
NSA_IN_DIM = NSA_WIDTH + 6 * NSA_KV_WIDTH + NSA_N_HEADS * N_BRANCH + N_BRANCH * NSA_WIDTH

REL_BUCKETS = 32
REL_MAX_DIST = 1024

kernel_name = 'nsa_ssd_hybrid_step'


def rmsnorm(x, w):
    xf = x.astype(jnp.float32)
    y = xf * lax.rsqrt(jnp.mean(xf * xf, axis=-1, keepdims=True) + NORM_EPS)
    return (y * w.astype(jnp.float32)).astype(x.dtype)


def rel_bucket(dist):
    n = jnp.maximum(dist, 0)
    exact = REL_BUCKETS // 2
    logv = jnp.log(jnp.maximum(n, 1).astype(jnp.float32) / exact) / math.log(REL_MAX_DIST / exact)
    large = jnp.minimum(exact + (logv * (REL_BUCKETS - exact)).astype(jnp.int32), REL_BUCKETS - 1)
    return jnp.where(n < exact, n, large)


def masked_softmax(s, mask):
    s = jnp.where(mask, s, -jnp.inf)
    m = jnp.max(s, axis=-1, keepdims=True)
    m = jnp.where(jnp.isfinite(m), m, 0.0)
    e = jnp.where(mask, jnp.exp(s - m), 0.0)
    den = jnp.sum(e, axis=-1, keepdims=True)
    return e / jnp.where(den > 0, den, 1.0)


def ssd_scan(xs, dt, a, bm, cm, h0):
    bsz, t = xs.shape[:2]
    lc = min(SSD_CHUNK, t)
    nc = -(-t // lc)
    pad = nc * lc - t
    if pad:
        xs = jnp.pad(xs, ((0, 0), (0, pad), (0, 0), (0, 0), (0, 0)))
        dt = jnp.pad(dt, ((0, 0), (0, pad), (0, 0), (0, 0)))
        bm = jnp.pad(bm, ((0, 0), (0, pad), (0, 0), (0, 0)))
        cm = jnp.pad(cm, ((0, 0), (0, pad), (0, 0), (0, 0)))
    g, r, p, n = SSD_N_GROUPS, SSD_HEADS_PER_GROUP, SSD_HEAD_DIM, SSD_D_STATE
    xdt = (xs * dt[..., None]).reshape(bsz, nc, lc, g, r, p)
    a_cs = jnp.cumsum((dt * a).reshape(bsz, nc, lc, g, r), axis=2)
    bc = bm.reshape(bsz, nc, lc, g, n)
    cc = cm.reshape(bsz, nc, lc, g, n)
    causal = jnp.tril(jnp.ones((lc, lc), dtype=bool))
    seg = a_cs[:, :, :, None] - a_cs[:, :, None, :]
    decay = jnp.exp(jnp.where(causal[:, :, None, None], seg, -jnp.inf))
    cb = jnp.einsum('bclgn,bcsgn->bclsg', cc, bc)
    y_diag = jnp.einsum('bclsg,bclsgr,bcsgrp->bclgrp', cb, decay, xdt)
    to_end = jnp.exp(a_cs[:, :, -1:] - a_cs)
    chunk_states = jnp.einsum('bclgn,bclgr,bclgrp->bcgrpn', bc, to_end, xdt)
    chunk_decay = jnp.exp(a_cs[:, :, -1])

    def body(h, inp):
        st, dc = inp
        return h * dc[..., None, None] + st, h

    h_last, h_prev = lax.scan(body, h0, (chunk_states.swapaxes(0, 1), chunk_decay.swapaxes(0, 1)))
    h_prev = h_prev.swapaxes(0, 1)
    y_off = jnp.einsum('bclgn,bcgrpn,bclgr->bclgrp', cc, h_prev, jnp.exp(a_cs))
    y = (y_diag + y_off).reshape(bsz, nc * lc, g, r, p)[:, :t]
    return y, h_last


def ssd_layer(x, conv_prev, ssm_prev, norm_w, w_in, conv_w, conv_b, dt_bias, a_log, d_skip, gnorm, w_out):
    f32 = jnp.float32
    bsz, t = x.shape[:2]
    g, r, p, n = SSD_N_GROUPS, SSD_HEADS_PER_GROUP, SSD_HEAD_DIM, SSD_D_STATE
    u = rmsnorm(x, norm_w) @ w_in
    z, xbc, dt_raw = jnp.split(u, [SSD_D_INNER, SSD_D_INNER + SSD_CONV_DIM], axis=-1)
    xpad = jnp.concatenate([conv_prev.astype(xbc.dtype), xbc], axis=1)
    conv = conv_b
    for k in range(SSD_CONV_W):
        conv = conv + xpad[:, k:k + t] * conv_w[k]
    xbc_act = jax.nn.silu(conv.astype(f32))
    new_conv = xpad[:, -(SSD_CONV_W - 1):]
    xs, bm, cm = jnp.split(xbc_act, [SSD_D_INNER, SSD_D_INNER + g * n], axis=-1)
    dt = jax.nn.softplus(dt_raw.astype(f32) + dt_bias.astype(f32))
    a = -jnp.exp(a_log.astype(f32))
    xs = xs.reshape(bsz, t, g, r, p)
    y, h_last = ssd_scan(xs, dt.reshape(bsz, t, g, r), a.reshape(g, r), bm.reshape(bsz, t, g, n),
                         cm.reshape(bsz, t, g, n), ssm_prev.astype(f32).reshape(bsz, g, r, p, n))
    y = y + xs * d_skip.astype(f32).reshape(g, r)[..., None]
    y = y.reshape(bsz, t, SSD_D_INNER) * jax.nn.silu(z.astype(f32))
    yg = y.reshape(bsz, t, SSD_N_GROUPS, -1)
    yg = yg * lax.rsqrt(jnp.mean(yg * yg, axis=-1, keepdims=True) + NORM_EPS)
    y = yg.reshape(bsz, t, SSD_D_INNER) * gnorm.astype(f32)
    out = x + y.astype(x.dtype) @ w_out
    return out, new_conv, h_last.reshape(bsz, SSD_N_HEADS, p, n).astype(x.dtype)


def compress(k, pe, w1, w2, n_blocks):
    bsz = k.shape[0]
    idx = jnp.arange(n_blocks)[:, None] * CMP_STRIDE + jnp.arange(CMP_BLOCK)[None, :]
    blk = k[:, idx] + pe[:, None, :]
    flat = blk.transpose(0, 1, 3, 2, 4).reshape(bsz, n_blocks, NSA_N_KV, CMP_BLOCK * NSA_HEAD_DIM)
    return jax.nn.silu(flat @ w1) @ w2


def nsa_mix(q, q_start, kv_full, kvw_all, kw_pos0, pe, w1, w2, rel_bias):
    f32 = jnp.float32
    bsz, tq = q.shape[:2]
    t = kv_full.shape[1]
    g, r, dh = NSA_N_KV, NSA_HEADS_PER_KV, NSA_HEAD_DIM
    nc = (t - CMP_BLOCK) // CMP_STRIDE + 1
    kc = compress(kv_full[:, :, 0], pe[0], w1[0], w2[0], nc).astype(f32)
    vc = compress(kv_full[:, :, 1], pe[1], w1[1], w2[1], nc).astype(f32)
    c_start = jnp.arange(nc) * CMP_STRIDE
    c_end = c_start + CMP_BLOCK - 1
    ns = -(-t // SEL_BLOCK)
    s_ids = jnp.arange(ns)
    s_start = s_ids * SEL_BLOCK
    overlap = jnp.clip(jnp.minimum(c_end[:, None], s_start[None, :] + SEL_BLOCK - 1)
                       - jnp.maximum(c_start[:, None], s_start[None, :]) + 1, 0).astype(f32) / CMP_STRIDE
    sel = jnp.pad(kv_full[:, :, 2:4], ((0, 0), (0, ns * SEL_BLOCK - t), (0, 0), (0, 0), (0, 0)))
    sel = sel.reshape(bsz, ns, SEL_BLOCK, 2, g, dh).transpose(3, 0, 4, 1, 2, 5).astype(f32)
    ks_blk, vs_blk = sel[0], sel[1]
    qb_len = min(NSA_QBLOCK, tq)
    nqb = -(-tq // qb_len)
    pad_q = nqb * qb_len - tq
    kvw_pad = jnp.pad(kvw_all, ((0, 0), (WINDOW, qb_len), (0, 0), (0, 0), (0, 0))).astype(f32)
    q_p = jnp.pad(q, ((0, 0), (0, pad_q), (0, 0), (0, 0)))
    qblocks = q_p.reshape(bsz, nqb, qb_len, g, r, dh).swapaxes(0, 1)
    pblocks = (q_start + jnp.arange(nqb * qb_len)).reshape(nqb, qb_len)
    tab = rel_bias.astype(f32).reshape(REL_BUCKETS, g, r).transpose(1, 2, 0)
    k_top = min(SEL_TOPK, ns)
    scale = NSA_HEAD_DIM ** -0.5
    bi = jnp.arange(bsz)[:, None, None, None]
    gi = jnp.arange(g)[None, :, None, None]
    g6 = jnp.arange(g)[None, :, None, None, None, None]
    r6 = jnp.arange(r)[None, None, :, None, None, None]

    def block_step(args):
        qb, pb = args
        qg = qb.astype(f32) * scale
        dist_c = pb[:, None] - c_end[None, :]
        s_c = jnp.einsum('bqgrd,bcgd->bgrqc', qg, kc) + tab[:, :, rel_bucket(dist_c)]
        p_c = masked_softmax(s_c, dist_c >= 0)
        o_c = jnp.einsum('bgrqc,bcgd->bqgrd', p_c, vc)
        imp = jnp.einsum('bgrqc,cs->bgqs', p_c, overlap)
        cur = pb[:, None] // SEL_BLOCK
        forced = (s_ids[None] == 0) | (s_ids[None] == cur) | (s_ids[None] == cur - 1)
        imp = jnp.where(forced, FORCE_SCORE, imp)
        imp = jnp.where(s_start[None] <= pb[:, None], imp, -FORCE_SCORE)
        top_val, top_idx = lax.top_k(imp, k_top)
        top_ok = top_val > -0.5 * FORCE_SCORE
        k_sel = ks_blk[bi, gi, top_idx]
        v_sel = vs_blk[bi, gi, top_idx]
        dist_s = pb[None, None, :, None, None] - (top_idx[..., None] * SEL_BLOCK + jnp.arange(SEL_BLOCK))
        ok_s = (dist_s >= 0) & top_ok[..., None]
        s_s = jnp.einsum('bqgrd,bgqksd->bgrqks', qg, k_sel) + tab[g6, r6, rel_bucket(dist_s)[:, :, None]]
        shp = s_s.shape
        p_s = masked_softmax(s_s.reshape(shp[:4] + (k_top * SEL_BLOCK,)),
                             ok_s.reshape(bsz, g, 1, qb_len, k_top * SEL_BLOCK)).reshape(shp)
        o_s = jnp.einsum('bgrqks,bgqksd->bqgrd', p_s, v_sel)
        p0 = pb[0]
        kvw = lax.dynamic_slice_in_dim(kvw_pad, p0 - kw_pos0, WINDOW + qb_len, axis=1)
        kpos_w = p0 - WINDOW + jnp.arange(WINDOW + qb_len)
        dist_w = pb[:, None] - kpos_w[None, :]
        ok_w = (dist_w >= 0) & (dist_w <= WINDOW) & (kpos_w[None, :] >= kw_pos0)
        s_w = jnp.einsum('bqgrd,bkgd->bgrqk', qg, kvw[:, :, 0]) + tab[:, :, rel_bucket(dist_w)]
        p_w = masked_softmax(s_w, ok_w)
        o_w = jnp.einsum('bgrqk,bkgd->bqgrd', p_w, kvw[:, :, 1])
        return jnp.stack([o_c, o_s, o_w], axis=2)

    out = lax.map(block_step, (qblocks, pblocks))
    out = out.swapaxes(0, 1).reshape(bsz, nqb * qb_len, N_BRANCH, NSA_N_HEADS, dh)
    return out[:, :tq]


def nsa_layer(x, q_start, past_kv, past_win, norm_w, w_in, pe, w1, w2, w_out, rel_bias):
    f32 = jnp.float32
    bsz, t = x.shape[:2]
    g, dh, h = NSA_N_KV, NSA_HEAD_DIM, NSA_N_HEADS
    u = rmsnorm(x, norm_w) @ w_in
    cuts = np.cumsum([NSA_WIDTH, 4 * NSA_KV_WIDTH, 2 * NSA_KV_WIDTH, h * N_BRANCH]).tolist()
    q, kv, kvw, gate, z = jnp.split(u, cuts, axis=-1)
    q = q.reshape(bsz, t, h, dh)
    kv = kv.reshape(bsz, t, 4, g, dh)
    kvw = kvw.reshape(bsz, t, 2, g, dh)
    gate = jax.nn.sigmoid(gate.astype(f32)).reshape(bsz, t, h, N_BRANCH)
    z = z.reshape(bsz, t, N_BRANCH, h, dh).astype(f32)
    if past_kv is None:
        kv_full, kvw_all, kw_pos0 = kv, kvw, 0
    else:
        kv_full = jnp.concatenate([past_kv.astype(kv.dtype), kv], axis=1)
        kvw_all = jnp.concatenate([past_win.astype(kvw.dtype), kvw], axis=1)
        kw_pos0 = q_start - past_win.shape[1]
    o = nsa_mix(q, q_start, kv_full, kvw_all, kw_pos0, pe, w1, w2, rel_bias)
    mixed = jnp.einsum('btchd,bthc->bthd', o * jax.nn.silu(z), gate)
    y = x + mixed.reshape(bsz, t, NSA_WIDTH).astype(x.dtype) @ w_out
    return y, kv, kvw_all


def setup_inputs(seed: int = 0) -> dict:
    key = jax.random.key(seed)
    ks = jax.random.split(key, 24)
    f32 = jnp.float32
    n_pages = PAST_LEN // PAGE_SIZE
    used = DEC_BATCH * n_pages
    n_phys = used + max(1, used // 4)
    win_len = min(WINDOW, PAST_LEN)
    g, dh = NSA_N_KV, NSA_HEAD_DIM

    def nrm(k, shape, s):
        return (s * jax.random.normal(k, shape)).astype(f32)

    x_prompt = nrm(ks[0], (BATCH, SEQ, D_MODEL), 1.0)
    x_sample = nrm(ks[1], (DEC_BATCH, DEC_SEQ, D_MODEL), 1.0)
    state_ssm = nrm(ks[2], (N_SSD_LAYERS, DEC_BATCH, SSD_N_HEADS, SSD_HEAD_DIM, SSD_D_STATE), 0.1)
    state_conv = nrm(ks[3], (N_SSD_LAYERS, DEC_BATCH, SSD_CONV_W - 1, SSD_CONV_DIM), 1.0)
    cache_kv = nrm(ks[4], (n_phys, N_NSA_LAYERS, PAGE_SIZE, 4, g, dh), 1.0)
    cache_win = nrm(ks[5], (N_NSA_LAYERS, DEC_BATCH, win_len, 2, g, dh), 1.0)
    page_table = jax.random.permutation(ks[6], n_phys)[:used].reshape(DEC_BATCH, n_pages).astype(jnp.int32)
    rel_bias = nrm(ks[7], (REL_BUCKETS, NSA_N_HEADS), 0.2)
    final_norm = 1.0 + nrm(ks[8], (D_MODEL,), 0.01)
    ssd_norm = 1.0 + nrm(ks[9], (N_SSD_LAYERS, D_MODEL), 0.01)
    ssd_w_in = nrm(ks[10], (N_SSD_LAYERS, D_MODEL, SSD_IN_DIM), D_MODEL ** -0.5)
    ssd_conv_w = nrm(ks[11], (N_SSD_LAYERS, SSD_CONV_W, SSD_CONV_DIM), SSD_CONV_W ** -0.5)
    ssd_conv_b = nrm(ks[12], (N_SSD_LAYERS, SSD_CONV_DIM), 0.01)
    dt0 = jnp.exp(jax.random.uniform(ks[13], (N_SSD_LAYERS, SSD_N_HEADS), minval=math.log(1e-3), maxval=math.log(1e-1)))
    ssd_dt_bias = (dt0 + jnp.log(-jnp.expm1(-dt0))).astype(f32)
    ssd_a_log = jnp.log(jax.random.uniform(ks[14], (N_SSD_LAYERS, SSD_N_HEADS), minval=1.0, maxval=16.0)).astype(f32)
    ssd_d = 1.0 + nrm(ks[15], (N_SSD_LAYERS, SSD_N_HEADS), 0.01)
    ssd_gnorm = 1.0 + nrm(ks[16], (N_SSD_LAYERS, SSD_D_INNER), 0.01)
    ssd_w_out = nrm(ks[17], (N_SSD_LAYERS, SSD_D_INNER, D_MODEL), SSD_D_INNER ** -0.5)
    nsa_norm = 1.0 + nrm(ks[18], (N_NSA_LAYERS, D_MODEL), 0.01)
    nsa_w_in = nrm(ks[19], (N_NSA_LAYERS, D_MODEL, NSA_IN_DIM), D_MODEL ** -0.5)
    nsa_cmp_pe = nrm(ks[20], (N_NSA_LAYERS, 2, CMP_BLOCK, dh), 0.02)
    nsa_cmp_w1 = nrm(ks[21], (N_NSA_LAYERS, 2, CMP_BLOCK * dh, CMP_HIDDEN), (CMP_BLOCK * dh) ** -0.5)
    nsa_cmp_w2 = nrm(ks[22], (N_NSA_LAYERS, 2, CMP_HIDDEN, dh), CMP_HIDDEN ** -0.5)
    nsa_w_out = nrm(ks[23], (N_NSA_LAYERS, NSA_WIDTH, D_MODEL), NSA_WIDTH ** -0.5)
    return {'x_prompt': x_prompt, 'x_sample': x_sample, 'state_ssm': state_ssm, 'state_conv': state_conv,
            'cache_kv': cache_kv, 'cache_win': cache_win, 'page_table': page_table,
            'rel_bias': rel_bias, 'final_norm': final_norm,
            'ssd_norm': ssd_norm, 'ssd_w_in': ssd_w_in, 'ssd_conv_w': ssd_conv_w, 'ssd_conv_b': ssd_conv_b,
            'ssd_dt_bias': ssd_dt_bias, 'ssd_a_log': ssd_a_log, 'ssd_d': ssd_d, 'ssd_gnorm': ssd_gnorm,
            'ssd_w_out': ssd_w_out,
            'nsa_norm': nsa_norm, 'nsa_w_in': nsa_w_in, 'nsa_cmp_pe': nsa_cmp_pe, 'nsa_cmp_w1': nsa_cmp_w1,
            'nsa_cmp_w2': nsa_cmp_w2, 'nsa_w_out': nsa_w_out}


def reference(x_prompt, x_sample, state_ssm, state_conv, cache_kv, cache_win, page_table,
              rel_bias, final_norm,
              ssd_norm, ssd_w_in, ssd_conv_w, ssd_conv_b, ssd_dt_bias, ssd_a_log, ssd_d, ssd_gnorm, ssd_w_out,
              nsa_norm, nsa_w_in, nsa_cmp_pe, nsa_cmp_w1, nsa_cmp_w2, nsa_w_out):
    n_pages = page_table.shape[1]
    past_len = n_pages * PAGE_SIZE
    win_len_s = cache_win.shape[2]
    xp, xs = x_prompt, x_sample
    ssm_p, conv_p, kv_p, win_p = [], [], [], []
    ssm_s, conv_s, kv_s, win_s = [], [], [], []
    for layer in range(DEPTH):
        j = layer // N_MIXERS
        if layer % N_MIXERS == 0:
            prm = (ssd_norm[j], ssd_w_in[j], ssd_conv_w[j], ssd_conv_b[j], ssd_dt_bias[j], ssd_a_log[j],
                   ssd_d[j], ssd_gnorm[j], ssd_w_out[j])
            conv0 = jnp.zeros((xp.shape[0], SSD_CONV_W - 1, SSD_CONV_DIM), xp.dtype)
            ssm0 = jnp.zeros((xp.shape[0], SSD_N_HEADS, SSD_HEAD_DIM, SSD_D_STATE), xp.dtype)
            xp, c, s = ssd_layer(xp, conv0, ssm0, *prm)
            conv_p.append(c)
            ssm_p.append(s)
            xs, c, s = ssd_layer(xs, state_conv[j], state_ssm[j], *prm)
            conv_s.append(c)
            ssm_s.append(s)
        else:
            prm = (nsa_norm[j], nsa_w_in[j], nsa_cmp_pe[j], nsa_cmp_w1[j], nsa_cmp_w2[j], nsa_w_out[j], rel_bias)
            xp, kv, kvw_all = nsa_layer(xp, 0, None, None, *prm)
            kv_p.append(kv)
            win_p.append(kvw_all[:, -min(WINDOW, kvw_all.shape[1]):])
            past = cache_kv[page_table, j].reshape(xs.shape[0], past_len, 4, NSA_N_KV, NSA_HEAD_DIM)
            xs, kv, kvw_all = nsa_layer(xs, past_len, past, cache_win[j], *prm)
            kv_s.append(kv)
            win_s.append(kvw_all[:, -win_len_s:])
    y_prompt = rmsnorm(xp, final_norm)
    y_sample = rmsnorm(xs, final_norm)
    return (y_prompt, y_sample, jnp.stack(ssm_p), jnp.stack(conv_p), jnp.stack(kv_p, axis=2), jnp.stack(win_p),
            jnp.stack(ssm_s), jnp.stack(conv_s), jnp.stack(kv_s, axis=2), jnp.stack(win_s))
```

```python
import functools
import math

import jax
import jax.numpy as jnp
import numpy as np
from jax import lax
from jax.experimental import pallas as pl
from jax.experimental.pallas import tpu as pltpu

F32 = jnp.float32
BF16 = jnp.bfloat16
HIGHEST = lax.Precision.HIGHEST

NORM_EPS = 1e-6
SSD_HEAD_DIM = 64
SSD_N_GROUPS = 8
SSD_D_STATE = 128
SSD_CONV_W = 4
SSD_CHUNK = 128
NSA_N_HEADS = 16
NSA_HEAD_DIM = 64
NSA_N_KV = 4
NSA_R = NSA_N_HEADS // NSA_N_KV
N_BRANCH = 3
CMP_BLOCK = 32
CMP_STRIDE = 16
CMP_HIDDEN = 128
SEL_BLOCK = 64
SEL_TOPK = 16
WINDOW = 512
FORCE_SCORE = 1e4
REL_BUCKETS = 32
REL_MAX_DIST = 1024
PAGE_SIZE = 128

LANES = 128
SUBLANES = 8
VMEM_LIMIT = 56 * 1024 * 1024

NEG = -1e30


def _cparams(sem):
    return pltpu.CompilerParams(dimension_semantics=sem, vmem_limit_bytes=VMEM_LIMIT)


def _round_up(x, m):
    return (x + m - 1) // m * m


def _silu(x):
    return x * jax.nn.sigmoid(x)


def _norm_matmul_kernel(x_ref, nw_ref, w_ref, o_ref, xn_ref):
    @pl.when(pl.program_id(1) == 0)
    def _():
        x = x_ref[...]
        ms = jnp.mean(x * x, axis=-1, keepdims=True)
        xn_ref[...] = (x * lax.rsqrt(ms + NORM_EPS) * nw_ref[...]).astype(BF16)

    o_ref[...] = jnp.dot(xn_ref[...], w_ref[...], preferred_element_type=F32)


def _norm_matmul(x, nw, w, tn):
    m, k = x.shape
    n = w.shape[1]
    tm = min(m, 512)
    assert m % tm == 0 and n % tn == 0
    return pl.pallas_call(
        _norm_matmul_kernel,
        out_shape=jax.ShapeDtypeStruct((m, n), F32),
        grid=(m // tm, n // tn),
        in_specs=[
            pl.BlockSpec((tm, k), lambda i, j: (i, 0)),
            pl.BlockSpec((1, k), lambda i, j: (0, 0)),
            pl.BlockSpec((k, tn), lambda i, j: (0, j)),
        ],
        out_specs=pl.BlockSpec((tm, tn), lambda i, j: (i, j)),
        scratch_shapes=[pltpu.VMEM((tm, k), BF16)],
        compiler_params=_cparams(("parallel", "arbitrary")),
        name="norm_matmul",
    )(x, nw.reshape(1, k), w)


def _matmul_res_kernel(y_ref, w_ref, x_ref, o_ref):
    o_ref[...] = x_ref[...] + jnp.dot(y_ref[...], w_ref[...], preferred_element_type=F32)


def _matmul_res(y, w, x):
    m, k = y.shape
    n = w.shape[1]
    tm = min(m, 512)
    assert m % tm == 0
    return pl.pallas_call(
        _matmul_res_kernel,
        out_shape=jax.ShapeDtypeStruct((m, n), F32),
        grid=(m // tm,),
        in_specs=[
            pl.BlockSpec((tm, k), lambda i: (i, 0)),
            pl.BlockSpec((k, n), lambda i: (0, 0)),
            pl.BlockSpec((tm, n), lambda i: (i, 0)),
        ],
        out_specs=pl.BlockSpec((tm, n), lambda i: (i, 0)),
        compiler_params=_cparams(("parallel",)),
        name="matmul_res",
    )(y, w, x)


def _rmsnorm_kernel(x_ref, nw_ref, o_ref):
    x = x_ref[...]
    ms = jnp.mean(x * x, axis=-1, keepdims=True)
    o_ref[...] = x * lax.rsqrt(ms + NORM_EPS) * nw_ref[...]


def _rmsnorm(x, nw):
    m, k = x.shape
    tm = min(m, 512)
    assert m % tm == 0
    return pl.pallas_call(
        _rmsnorm_kernel,
        out_shape=jax.ShapeDtypeStruct((m, k), F32),
        grid=(m // tm,),
        in_specs=[pl.BlockSpec((tm, k), lambda i: (i, 0)), pl.BlockSpec((1, k), lambda i: (0, 0))],
        out_specs=pl.BlockSpec((tm, k), lambda i: (i, 0)),
        compiler_params=_cparams(("parallel",)),
        name="final_rmsnorm",
    )(x, nw.reshape(1, k))


def _ssd_chunk_kernel(z_ref, x_ref, bc_ref, dt_ref, cprev_ref, sprev_ref, cw_ref, cb_ref, dtb_ref, alog_ref,
                      dskip_ref, gn_ref, y_ref, hlast_ref, xpad_ref, state_ref, *, d_inner, n_heads):
    lc = SSD_CHUNK
    g_n, n, p = SSD_N_GROUPS, SSD_D_STATE, SSD_HEAD_DIM
    r_n = n_heads // g_n
    c = pl.program_id(1)

    @pl.when(c == 0)
    def _():
        xpad_ref[0:SUBLANES, :] = cprev_ref[0]
        state_ref[...] = sprev_ref[0]

    @pl.when(c > 0)
    def _():
        xpad_ref[0:SUBLANES, :] = xpad_ref[lc:lc + SUBLANES, :]

    xpad_ref[SUBLANES:SUBLANES + lc, 0:d_inner] = x_ref[...]
    xpad_ref[SUBLANES:SUBLANES + lc, d_inner:] = bc_ref[...]

    conv = cb_ref[...]
    for k in range(SSD_CONV_W):
        off = SUBLANES - (SSD_CONV_W - 1) + k
        conv = conv + xpad_ref[off:off + lc, :] * cw_ref[k:k + 1, :]
    act = _silu(conv)
    xs = act[:, :d_inner]
    bm = act[:, d_inner:d_inner + g_n * n]
    cm = act[:, d_inner + g_n * n:]

    dt = jax.nn.softplus(dt_ref[...] + dtb_ref[...])
    a = -jnp.exp(alog_ref[...])
    dta = dt * a
    row = lax.broadcasted_iota(jnp.int32, (lc, lc), 0)
    col = lax.broadcasted_iota(jnp.int32, (lc, lc), 1)
    causal = row >= col
    tril = causal.astype(F32)
    a_cs = jnp.dot(tril, dta, precision=HIGHEST, preferred_element_type=F32)
    a_cs_t = a_cs.T
    a_end = a_cs[lc - 1:lc, :]
    to_end = jnp.exp(a_end - a_cs)
    e_cs = jnp.exp(a_cs)
    e_end = jnp.exp(a_end)

    zs = _silu(z_ref[...])
    ys = []
    for g in range(g_n):
        bg = bm[:, g * n:(g + 1) * n]
        cg = cm[:, g * n:(g + 1) * n]
        bg16 = bg.astype(BF16)
        cg16 = cg.astype(BF16)
        cb = lax.dot_general(cg16, bg16, (((1,), (1,)), ((), ())), preferred_element_type=F32)
        yg = []
        for r in range(r_n):
            h = g * r_n + r
            xh = xs[:, h * p:(h + 1) * p]
            dth = dt[:, h:h + 1]
            xdt = xh * dth
            seg = a_cs[:, h:h + 1] - a_cs_t[h:h + 1, :]
            decay = jnp.exp(jnp.where(causal, seg, -jnp.inf))
            m_h = (cb * decay).astype(BF16)
            y_diag = jnp.dot(m_h, xdt.astype(BF16), preferred_element_type=F32)
            hprev = state_ref[h]
            y_off = lax.dot_general(cg16, hprev.astype(BF16), (((1,), (1,)), ((), ())),
                                    preferred_element_type=F32) * e_cs[:, h:h + 1]
            xw = (xdt * to_end[:, h:h + 1]).astype(BF16)
            st = lax.dot_general(xw, bg16, (((0,), (0,)), ((), ())), preferred_element_type=F32)
            state_ref[h] = hprev * e_end[:, h:h + 1] + st
            yh = y_diag + y_off + xh * dskip_ref[:, h:h + 1]
            yg.append(yh)
        yg = jnp.concatenate(yg, axis=1) * zs[:, g * r_n * p:(g + 1) * r_n * p]
        ms = jnp.mean(yg * yg, axis=-1, keepdims=True)
        ys.append(yg * lax.rsqrt(ms + NORM_EPS))
    y = jnp.concatenate(ys, axis=1) * gn_ref[...]
    y_ref[...] = y.astype(y_ref.dtype)

    @pl.when(c == pl.num_programs(1) - 1)
    def _():
        hlast_ref[0] = state_ref[...]


def _pad_lanes(v, width=LANES):
    return jnp.pad(v.astype(F32), (0, width - v.shape[0])).reshape(1, width)


def _ssd_prompt_core(u, bsz, t, conv_prev8, ssm_prev, conv_w, conv_b, dt_bias, a_log, d_skip, gnorm):
    n_heads = dt_bias.shape[0]
    d_inner = n_heads * SSD_HEAD_DIM
    conv_dim = conv_w.shape[1]
    assert conv_dim == 2 * d_inner and t % SSD_CHUNK == 0
    lc = SSD_CHUNK
    nc = t // lc
    dt_blk = (d_inner + conv_dim) // LANES
    kern = functools.partial(_ssd_chunk_kernel, d_inner=d_inner, n_heads=n_heads)
    const = lambda b, c: (0, 0)
    y, hlast = pl.pallas_call(
        kern,
        out_shape=(jax.ShapeDtypeStruct((bsz * t, d_inner), BF16),
                   jax.ShapeDtypeStruct((bsz, n_heads, SSD_HEAD_DIM, SSD_D_STATE), F32)),
        grid=(bsz, nc),
        in_specs=[
            pl.BlockSpec((lc, d_inner), lambda b, c: (b * nc + c, 0)),
            pl.BlockSpec((lc, d_inner), lambda b, c: (b * nc + c, 1)),
            pl.BlockSpec((lc, d_inner), lambda b, c: (b * nc + c, 2)),
            pl.BlockSpec((lc, LANES), lambda b, c: (b * nc + c, dt_blk)),
            pl.BlockSpec((1, SUBLANES, conv_dim), lambda b, c: (b, 0, 0)),
            pl.BlockSpec((1, n_heads, SSD_HEAD_DIM, SSD_D_STATE), lambda b, c: (b, 0, 0, 0)),
            pl.BlockSpec((SSD_CONV_W, conv_dim), const),
            pl.BlockSpec((1, conv_dim), const),
            pl.BlockSpec((1, LANES), const),
            pl.BlockSpec((1, LANES), const),
            pl.BlockSpec((1, LANES), const),
            pl.BlockSpec((1, d_inner), const),
        ],
        out_specs=(pl.BlockSpec((lc, d_inner), lambda b, c: (b * nc + c, 0)),
                   pl.BlockSpec((1, n_heads, SSD_HEAD_DIM, SSD_D_STATE), lambda b, c: (b, 0, 0, 0))),
        scratch_shapes=[pltpu.VMEM((lc + SUBLANES, conv_dim), F32),
                        pltpu.VMEM((n_heads, SSD_HEAD_DIM, SSD_D_STATE), F32)],
        compiler_params=_cparams(("parallel", "arbitrary")),
        name="ssd_chunk",
    )(u, u, u, u, conv_prev8, ssm_prev, conv_w, conv_b.reshape(1, conv_dim), _pad_lanes(dt_bias),
      _pad_lanes(a_log), _pad_lanes(d_skip), gnorm.reshape(1, d_inner))
    return y, hlast


def _ssd_in_weights(w_in, d_inner, conv_dim):
    k, n_in = w_in.shape
    main = d_inner + conv_dim
    n_dt = n_in - main
    w = jnp.concatenate([w_in[:, :main], jnp.pad(w_in[:, main:], ((0, 0), (0, LANES - n_dt)))], axis=1)
    return w.astype(BF16)


def _pick_tn(n, cap=1024):
    best = LANES
    for tn in range(LANES, cap + 1, LANES):
        if n % tn == 0:
            best = tn
    return best


def _ssd_layer_prompt(xp, prm):
    norm_w, w_in, conv_w, conv_b, dt_bias, a_log, d_skip, gnorm, w_out = prm
    bsz, t, d = xp.shape
    n_heads = dt_bias.shape[0]
    d_inner = n_heads * SSD_HEAD_DIM
    conv_dim = conv_w.shape[1]
    w = _ssd_in_weights(w_in, d_inner, conv_dim)
    x2 = xp.reshape(bsz * t, d)
    u = _norm_matmul(x2, norm_w, w, _pick_tn(w.shape[1]))
    conv_prev8 = jnp.zeros((bsz, SUBLANES, conv_dim), F32)
    ssm_prev = jnp.zeros((bsz, n_heads, SSD_HEAD_DIM, SSD_D_STATE), F32)
    y, hlast = _ssd_prompt_core(u, bsz, t, conv_prev8, ssm_prev, conv_w, conv_b, dt_bias, a_log, d_skip, gnorm)
    out = _matmul_res(y, w_out.astype(BF16), x2).reshape(bsz, t, d)
    u3 = u.reshape(bsz, t, -1)
    new_conv = u3[:, t - (SSD_CONV_W - 1):, d_inner:d_inner + conv_dim]
    return out, new_conv, hlast


def _ssd_step_pre_kernel(u_ref, cst_ref, cw_ref, cb_ref, dtb_ref, alog_ref,
                         xs_ref, xdt_ref, dec_ref, bm_ref, cm_ref, ncst_ref, *, d_inner, n_heads):
    conv_dim = cw_ref.shape[1]
    gn = SSD_N_GROUPS * SSD_D_STATE
    xbc = u_ref[:, d_inner:d_inner + conv_dim]
    conv = cb_ref[...]
    for k in range(SSD_CONV_W - 1):
        conv = conv + cst_ref[k] * cw_ref[k:k + 1, :]
    conv = conv + xbc * cw_ref[SSD_CONV_W - 1:SSD_CONV_W, :]
    for k in range(SSD_CONV_W - 2):
        ncst_ref[k] = cst_ref[k + 1]
    ncst_ref[SSD_CONV_W - 2] = xbc
    act = _silu(conv)
    xs = act[:, :d_inner]
    dt = jax.nn.softplus(u_ref[:, d_inner + conv_dim:d_inner + conv_dim + LANES] + dtb_ref[...])
    dec = jnp.exp(dt * (-jnp.exp(alog_ref[...])))
    hrow = lax.broadcasted_iota(jnp.int32, (LANES, d_inner), 0)
    hcol = lax.broadcasted_iota(jnp.int32, (LANES, d_inner), 1) // SSD_HEAD_DIM
    expand = (hrow == hcol).astype(F32)
    dt_e = jnp.dot(dt, expand, precision=HIGHEST, preferred_element_type=F32)
    dec_e = jnp.dot(dec, expand, precision=HIGHEST, preferred_element_type=F32)
    xs_ref[...] = xs
    xdt_ref[...] = xs * dt_e
    dec_ref[...] = dec_e
    bm_ref[...] = act[:, d_inner:d_inner + gn]
    cm_ref[...] = act[:, d_inner + gn:]


def _ssd_step_state_kernel(s_ref, xdt_ref, dec_ref, b_ref, c_ref, snew_ref, y_ref, *, rows_per_group):
    for g in range(SSD_N_GROUPS):
        sl = slice(g * rows_per_group, (g + 1) * rows_per_group)
        s = s_ref[0, sl, :]
        bg = b_ref[0, g:g + 1, :]
        cg = c_ref[0, g:g + 1, :]
        xdt = xdt_ref[0, sl, :]
        dec = dec_ref[0, sl, :]
        cb = jnp.sum(bg * cg, axis=1, keepdims=True)
        y_off = jnp.sum(s * cg, axis=1, keepdims=True)
        snew_ref[0, sl, :] = s * dec + xdt * bg
        y_ref[0, sl, :] = cb * xdt + dec * y_off


def _ssd_step_post_kernel(y_ref, xs_ref, u_ref, dsk_ref, gn_ref, o_ref, *, d_inner, group_width):
    y = (y_ref[...] + xs_ref[...] * dsk_ref[...]) * _silu(u_ref[:, :d_inner])
    outs = []
    for g in range(SSD_N_GROUPS):
        yg = y[:, g * group_width:(g + 1) * group_width]
        ms = jnp.mean(yg * yg, axis=-1, keepdims=True)
        outs.append(yg * lax.rsqrt(ms + NORM_EPS))
    o_ref[...] = (jnp.concatenate(outs, axis=1) * gn_ref[...]).astype(o_ref.dtype)


def _ssd_layer_sample(xs_in, conv_state, ssm_state, prm):
    norm_w, w_in, conv_w, conv_b, dt_bias, a_log, d_skip, gnorm, w_out = prm
    bsz, t, d = xs_in.shape
    assert t == 1
    n_heads = dt_bias.shape[0]
    d_inner = n_heads * SSD_HEAD_DIM
    conv_dim = conv_w.shape[1]
    gn = SSD_N_GROUPS * SSD_D_STATE
    w = _ssd_in_weights(w_in, d_inner, conv_dim)
    x2 = xs_in.reshape(bsz, d)
    u = _norm_matmul(x2, norm_w, w, _pick_tn(w.shape[1]))
    cst = conv_state.transpose(1, 0, 2)
    full = lambda shape: pl.BlockSpec(shape, lambda: tuple(0 for _ in shape))
    pre = functools.partial(_ssd_step_pre_kernel, d_inner=d_inner, n_heads=n_heads)
    xs, xdt, dec, bm, cm, ncst = pl.pallas_call(
        pre,
        out_shape=(jax.ShapeDtypeStruct((bsz, d_inner), F32), jax.ShapeDtypeStruct((bsz, d_inner), F32),
                   jax.ShapeDtypeStruct((bsz, d_inner), F32), jax.ShapeDtypeStruct((bsz, gn), F32),
                   jax.ShapeDtypeStruct((bsz, gn), F32), jax.ShapeDtypeStruct(cst.shape, F32)),
        compiler_params=pltpu.CompilerParams(vmem_limit_bytes=VMEM_LIMIT),
        name="ssd_step_pre",
    )(u, cst, conv_w, conv_b.reshape(1, conv_dim), _pad_lanes(dt_bias), _pad_lanes(a_log))
    rows = n_heads * SSD_HEAD_DIM
    rpg = rows // SSD_N_GROUPS
    st = functools.partial(_ssd_step_state_kernel, rows_per_group=rpg)
    snew, ycol = pl.pallas_call(
        st,
        out_shape=(jax.ShapeDtypeStruct((bsz, rows, SSD_D_STATE), F32), jax.ShapeDtypeStruct((bsz, rows, 1), F32)),
        grid=(bsz,),
        in_specs=[
            pl.BlockSpec((1, rows, SSD_D_STATE), lambda b: (b, 0, 0)),
            pl.BlockSpec((1, rows, 1), lambda b: (b, 0, 0)),
            pl.BlockSpec((1, rows, 1), lambda b: (b, 0, 0)),
            pl.BlockSpec((1, SSD_N_GROUPS, SSD_D_STATE), lambda b: (b, 0, 0)),
            pl.BlockSpec((1, SSD_N_GROUPS, SSD_D_STATE), lambda b: (b, 0, 0)),
        ],
        out_specs=(pl.BlockSpec((1, rows, SSD_D_STATE), lambda b: (b, 0, 0)),
                   pl.BlockSpec((1, rows, 1), lambda b: (b, 0, 0))),
        compiler_params=_cparams(("parallel",)),
        name="ssd_step_state",
    )(ssm_state.reshape(bsz, rows, SSD_D_STATE), xdt.reshape(bsz, rows, 1), dec.reshape(bsz, rows, 1),
      bm.reshape(bsz, SSD_N_GROUPS, SSD_D_STATE), cm.reshape(bsz, SSD_N_GROUPS, SSD_D_STATE))
    post = functools.partial(_ssd_step_post_kernel, d_inner=d_inner, group_width=d_inner // SSD_N_GROUPS)
    y = pl.pallas_call(
        post,
        out_shape=jax.ShapeDtypeStruct((bsz, d_inner), BF16),
        compiler_params=pltpu.CompilerParams(vmem_limit_bytes=VMEM_LIMIT),
        name="ssd_step_post",
    )(ycol.reshape(bsz, rows), xs, u, jnp.repeat(d_skip.astype(F32), SSD_HEAD_DIM).reshape(1, d_inner),
      gnorm.reshape(1, d_inner))
    out = _matmul_res(y, w_out.astype(BF16), x2).reshape(bsz, 1, d)
    return out, ncst.transpose(1, 0, 2), snew.reshape(ssm_state.shape)


NSA_Q_W = NSA_N_HEADS * NSA_HEAD_DIM
NSA_KV_W = NSA_N_KV * NSA_HEAD_DIM
COL_Q = 0
COL_KV = NSA_Q_W
COL_Z = COL_KV + 4 * NSA_KV_W
COL_KVW = COL_Z + N_BRANCH * NSA_Q_W
COL_GATE = COL_KVW + 2 * NSA_KV_W
NSA_N_PAD = COL_GATE + LANES
TQ = 128
KT = 128
N_REL_TILES = -(-(REL_MAX_DIST + TQ - 1) // KT)
CMP_NEAR = 128
CMP_FRONT = CMP_NEAR - TQ // CMP_STRIDE


def _nsa_in_weights(w_in):
    k = w_in.shape[0]
    cuts = np.cumsum([NSA_Q_W, 4 * NSA_KV_W, 2 * NSA_KV_W, NSA_N_HEADS * N_BRANCH]).tolist()
    q, kv, kvw, gate, z = (w_in[:, a:b] for a, b in zip([0] + cuts, cuts + [w_in.shape[1]]))
    gate = jnp.pad(gate, ((0, 0), (0, LANES - gate.shape[1])))
    return jnp.concatenate([q, kv, z, kvw, gate], axis=1).astype(BF16)


def _rel_bucket(dist):
    n = jnp.maximum(dist, 0)
    exact = REL_BUCKETS // 2
    logv = jnp.log(jnp.maximum(n, 1).astype(F32) / exact) / math.log(REL_MAX_DIST / exact)
    large = jnp.minimum(exact + (logv * (REL_BUCKETS - exact)).astype(jnp.int32), REL_BUCKETS - 1)
    return jnp.where(n < exact, n, large)


def _bias_of_dist(rel_bias, dist):
    return jnp.moveaxis(rel_bias.astype(F32)[_rel_bucket(dist)], -1, 0)


def _rows_by_group(t):
    h, q, n = t.shape
    return t.reshape(NSA_N_KV, NSA_R * q, n)


def _overlap_matrix(nc, ns):
    c_start = np.arange(nc) * CMP_STRIDE
    c_end = c_start + CMP_BLOCK - 1
    s_start = np.arange(ns) * SEL_BLOCK
    ov = np.clip(np.minimum(c_end[:, None], s_start[None, :] + SEL_BLOCK - 1)
                 - np.maximum(c_start[:, None], s_start[None, :]) + 1, 0, None).astype(np.float32) / CMP_STRIDE
    return ov


def _compress_kernel(ids_ref, page_ref, pe_ref, w1_ref, w2_ref, kc_ref, vc_ref, x_ref):
    del ids_ref
    p = pl.program_id(1)
    hb = PAGE_SIZE // CMP_STRIDE
    dh = NSA_HEAD_DIM
    row0 = pl.multiple_of(p * hb, hb)
    for kv in range(2):
        for g in range(NSA_N_KV):
            c0 = kv * NSA_KV_W + g * dh
            for l in range(CMP_STRIDE):
                x_ref[kv, g, pl.ds(row0, hb), l * dh:(l + 1) * dh] = page_ref[:, l, c0:c0 + dh]

    @pl.when(p == pl.num_programs(1) - 1)
    def _():
        half = CMP_STRIDE * dh
        for kv, out_ref in ((0, kc_ref), (1, vc_ref)):
            w_top = w1_ref[kv, 0:half, :]
            w_bot = w1_ref[kv, half:2 * half, :]
            w2 = w2_ref[kv]
            for g in range(NSA_N_KV):
                x = x_ref[kv, g]
                top = jnp.dot((x + pe_ref[kv, 0:1, :]).astype(BF16), w_top, preferred_element_type=F32)
                bot = jnp.dot((x + pe_ref[kv, 1:2, :]).astype(BF16), w_bot, preferred_element_type=F32)
                bot = jnp.concatenate([bot[1:], jnp.zeros((1, bot.shape[1]), F32)], axis=0)
                hid = _silu(top + bot).astype(BF16)
                out_ref[0, :, g * dh:(g + 1) * dh] = jnp.dot(hid, w2, preferred_element_type=F32)


def _compress(src, ids, col_block, pe, w1, w2):
    bsz, n_pages = ids.shape
    hb = PAGE_SIZE // CMP_STRIDE
    nh = n_pages * hb
    half = CMP_STRIDE * NSA_HEAD_DIM
    src = src.reshape(src.shape[0], hb, CMP_STRIDE, src.shape[2])
    grid_spec = pltpu.PrefetchScalarGridSpec(
        num_scalar_prefetch=1,
        grid=(bsz, n_pages),
        in_specs=[
            pl.BlockSpec((None, hb, CMP_STRIDE, 2 * NSA_KV_W), lambda b, p, ids: (ids[b, p], 0, 0, col_block)),
            pl.BlockSpec((2, 2, half), lambda b, p, ids: (0, 0, 0)),
            pl.BlockSpec((2, 2 * half, CMP_HIDDEN), lambda b, p, ids: (0, 0, 0)),
            pl.BlockSpec((2, CMP_HIDDEN, NSA_HEAD_DIM), lambda b, p, ids: (0, 0, 0)),
        ],
        out_specs=(pl.BlockSpec((1, nh, NSA_KV_W), lambda b, p, ids: (b, 0, 0)),
                   pl.BlockSpec((1, nh, NSA_KV_W), lambda b, p, ids: (b, 0, 0))),
        scratch_shapes=[pltpu.VMEM((2, NSA_N_KV, nh, half), F32)],
    )
    return pl.pallas_call(
        _compress_kernel,
        out_shape=(jax.ShapeDtypeStruct((bsz, nh, NSA_KV_W), F32), jax.ShapeDtypeStruct((bsz, nh, NSA_KV_W), F32)),
        grid_spec=grid_spec,
        compiler_params=_cparams(("parallel", "arbitrary")),
        name="nsa_compress",
    )(ids, src, pe.reshape(2, 2, half).astype(F32), w1.astype(BF16), w2.astype(BF16))


def _masked_softmax_parts(parts, valids):
    m = None
    for s, v in zip(parts, valids):
        pm = jnp.max(jnp.where(v, s, -jnp.inf), axis=-1, keepdims=True)
        m = pm if m is None else jnp.maximum(m, pm)
    m = jnp.where(m > -jnp.inf, m, 0.0)
    es = [jnp.where(v, jnp.exp(s - m), 0.0) for s, v in zip(parts, valids)]
    den = None
    for e in es:
        d = jnp.sum(e, axis=-1, keepdims=True)
        den = d if den is None else den + d
    inv = 1.0 / jnp.where(den > 0, den, 1.0)
    return [e * inv for e in es]


def _nsa_attn_kernel(q_ref, kvb_ref, kcp_ref, vcp_ref, ovp_ref, trel_ref, tcn_ref, gate_ref, z0_ref, z1_ref, z2_ref,
                     o_ref, sel_ref, m_ref, l_ref, acc_ref, ob_ref, *, n_sel):
    t = pl.program_id(1)
    dh = NSA_HEAD_DIM
    rq = NSA_R * TQ
    scale = dh ** -0.5
    n_far = kcp_ref.shape[1]
    qi = lax.broadcasted_iota(jnp.int32, (rq, KT), 0) % TQ
    kj = lax.broadcasted_iota(jnp.int32, (rq, KT), 1)
    far_const = N_REL_TILES

    qgs = []
    for g in range(NSA_N_KV):
        qg = jnp.concatenate([q_ref[:, (g * NSA_R + r) * dh:(g * NSA_R + r + 1) * dh] for r in range(NSA_R)], axis=0)
        qg = (qg * scale).astype(BF16)
        qgs.append(qg)
        gs = slice(g * dh, (g + 1) * dh)
        near0 = pl.multiple_of(t * (TQ // CMP_STRIDE), SUBLANES)
        kc_far = kcp_ref[0, :, gs].astype(BF16)
        vc_far = vcp_ref[0, :, gs].astype(BF16)
        kc_near = kcp_ref[0, pl.ds(near0, CMP_NEAR), gs].astype(BF16)
        vc_near = vcp_ref[0, pl.ds(near0, CMP_NEAR), gs].astype(BF16)
        nt = (((1,), (1,)), ((), ()))
        s_far = lax.dot_general(qg, kc_far, nt, preferred_element_type=F32) + trel_ref[g, far_const][:, 0:1]
        s_near = lax.dot_general(qg, kc_near, nt, preferred_element_type=F32) + tcn_ref[g]
        cf = lax.broadcasted_iota(jnp.int32, (rq, n_far), 1)
        v_far = (cf >= CMP_FRONT) & (cf < near0)
        qi_n = lax.broadcasted_iota(jnp.int32, (rq, CMP_NEAR), 0) % TQ
        cn = lax.broadcasted_iota(jnp.int32, (rq, CMP_NEAR), 1)
        dist_n = qi_n - CMP_STRIDE * cn + (CMP_STRIDE * CMP_FRONT - (CMP_BLOCK - 1))
        v_near = (dist_n >= 0) & (cn + near0 >= CMP_FRONT)
        p_far, p_near = _masked_softmax_parts([s_far, s_near], [v_far, v_near])
        o_c = (jnp.dot(p_far.astype(BF16), vc_far, preferred_element_type=F32)
               + jnp.dot(p_near.astype(BF16), vc_near, preferred_element_type=F32))
        ob_ref[0, g] = o_c
        pg_far = sum(p_far[r * TQ:(r + 1) * TQ] for r in range(NSA_R))
        pg_near = sum(p_near[r * TQ:(r + 1) * TQ] for r in range(NSA_R))
        imp = (jnp.dot(pg_far, ovp_ref[...], precision=HIGHEST, preferred_element_type=F32)
               + jnp.dot(pg_near, ovp_ref[pl.ds(near0, CMP_NEAR), :], precision=HIGHEST,
                         preferred_element_type=F32))
        qpos = t * TQ + lax.broadcasted_iota(jnp.int32, (TQ, LANES), 0)
        sid = lax.broadcasted_iota(jnp.int32, (TQ, LANES), 1)
        cur = qpos // SEL_BLOCK
        forced = (sid == 0) | (sid == cur) | (sid == cur - 1)
        imp = jnp.where(forced, FORCE_SCORE, imp)
        imp = jnp.where(sid * SEL_BLOCK <= qpos, imp, -FORCE_SCORE)
        imp_t = imp.T
        srow = lax.broadcasted_iota(jnp.int32, (LANES, TQ), 0)
        cnt = jnp.zeros((LANES, TQ), jnp.int32)
        for sp in range(n_sel):
            other = imp_t[sp:sp + 1, :]
            ahead = (other > imp_t) | ((other == imp_t) & (srow > sp))
            cnt = cnt + ahead.astype(jnp.int32)
        sel_t = ((cnt < SEL_TOPK) & (imp_t > -0.5 * FORCE_SCORE)).astype(F32)
        sel_ref[g] = sel_t.T

    m_ref[...] = jnp.full(m_ref.shape, NEG, F32)
    l_ref[...] = jnp.zeros(l_ref.shape, F32)
    acc_ref[...] = jnp.zeros(acc_ref.shape, F32)
    blk_per_tile = KT // SEL_BLOCK

    def sel_body(kt, carry):
        delta = t - kt
        bias_idx = jnp.minimum(delta, far_const)
        k0 = pl.multiple_of(kt * KT, KT)
        eb = lax.broadcasted_iota(jnp.int32, (LANES, KT), 0)
        ej = lax.broadcasted_iota(jnp.int32, (LANES, KT), 1)
        expand = (eb == kt * blk_per_tile + ej // SEL_BLOCK).astype(BF16)
        causal_ok = (delta > 0) | (qi >= kj)
        for g in range(NSA_N_KV):
            kt_g = kvb_ref[pl.ds(k0, KT), g * dh:(g + 1) * dh]
            vt_g = kvb_ref[pl.ds(k0, KT), NSA_KV_W + g * dh:NSA_KV_W + (g + 1) * dh]
            s = lax.dot_general(qgs[g], kt_g, (((1,), (1,)), ((), ())), preferred_element_type=F32)
            s = s + trel_ref[g, bias_idx]
            chosen = jnp.dot(sel_ref[g].astype(BF16), expand, preferred_element_type=F32)
            chosen = jnp.concatenate([chosen] * NSA_R, axis=0)
            s = jnp.where((chosen > 0.5) & causal_ok, s, NEG)
            m_old = m_ref[g]
            m_new = jnp.maximum(m_old, jnp.max(s, axis=-1, keepdims=True))
            alpha = jnp.exp(m_old - m_new)
            p = jnp.exp(s - m_new)
            l_ref[g] = l_ref[g] * alpha + jnp.sum(p, axis=-1, keepdims=True)
            acc_ref[g] = acc_ref[g] * alpha + jnp.dot(p.astype(BF16), vt_g, preferred_element_type=F32)
            m_ref[g] = m_new
        return carry

    lax.fori_loop(0, t + 1, sel_body, 0)
    for g in range(NSA_N_KV):
        ob_ref[1, g] = acc_ref[g] / l_ref[g]

    n_win_tiles = WINDOW // KT + 1
    for g in range(NSA_N_KV):
        parts, valids, vts = [], [], []
        for delta in range(n_win_tiles):
            kt = t - delta
            k0 = pl.multiple_of(jnp.maximum(kt, 0) * KT, KT)
            kt_g = kvb_ref[pl.ds(k0, KT), 2 * NSA_KV_W + g * dh:2 * NSA_KV_W + (g + 1) * dh]
            vts.append(kvb_ref[pl.ds(k0, KT), 3 * NSA_KV_W + g * dh:3 * NSA_KV_W + (g + 1) * dh])
            s = lax.dot_general(qgs[g], kt_g, (((1,), (1,)), ((), ())), preferred_element_type=F32)
            parts.append(s + trel_ref[g, delta])
            dist = delta * KT + qi - kj
            valids.append((dist >= 0) & (dist <= WINDOW) & (kt >= 0))
        ps = _masked_softmax_parts(parts, valids)
        o_w = sum(jnp.dot(p.astype(BF16), v, preferred_element_type=F32) for p, v in zip(ps, vts))
        ob_ref[2, g] = o_w

    gate = jax.nn.sigmoid(gate_ref[...])
    z_refs = (z0_ref, z1_ref, z2_ref)
    for h in range(NSA_N_HEADS):
        g, r = divmod(h, NSA_R)
        mixed = None
        for c in range(N_BRANCH):
            o_hc = ob_ref[c, g, r * TQ:(r + 1) * TQ, :]
            term = o_hc * _silu(z_refs[c][:, h * dh:(h + 1) * dh]) * gate[:, h * N_BRANCH + c:h * N_BRANCH + c + 1]
            mixed = term if mixed is None else mixed + term
        o_ref[:, h * dh:(h + 1) * dh] = mixed.astype(o_ref.dtype)


def _nsa_prompt_tables(rel_bias):
    qi = jnp.arange(TQ)[:, None]
    kj = jnp.arange(KT)[None, :]
    tiles = []
    for delta in range(N_REL_TILES):
        tiles.append(_rows_by_group(_bias_of_dist(rel_bias, delta * KT + qi - kj)))
    far = jnp.broadcast_to(rel_bias.astype(F32)[REL_BUCKETS - 1].reshape(NSA_N_KV, NSA_R, 1, 1),
                           (NSA_N_KV, NSA_R, TQ, KT)).reshape(NSA_N_KV, NSA_R * TQ, KT)
    tiles.append(far)
    trel = jnp.stack(tiles, axis=1)
    cn = jnp.arange(CMP_NEAR)[None, :]
    dist_n = qi - CMP_STRIDE * cn + (CMP_STRIDE * CMP_FRONT - (CMP_BLOCK - 1))
    tcn = _rows_by_group(_bias_of_dist(rel_bias, dist_n))
    return trel, tcn


def _nsa_layer_prompt(xp, prm):
    norm_w, w_in, pe, w1, w2, w_out, rel_bias = prm
    bsz, t, d = xp.shape
    assert t % TQ == 0 and t % PAGE_SIZE == 0
    n_tiles = t // TQ
    ns = t // SEL_BLOCK
    assert ns <= LANES
    x2 = xp.reshape(bsz * t, d)
    w = _nsa_in_weights(w_in)
    u = _norm_matmul(x2, norm_w, w, _pick_tn(NSA_N_PAD, 1152))
    n_pages = t // PAGE_SIZE
    ids = jnp.arange(bsz * n_pages, dtype=jnp.int32).reshape(bsz, n_pages)
    kc, vc = _compress(u.reshape(bsz * n_pages, PAGE_SIZE, NSA_N_PAD), ids, COL_KV // (2 * NSA_KV_W), pe, w1, w2)
    nh = kc.shape[1]
    n_far = _round_up(CMP_FRONT + nh, LANES)
    padc = ((0, 0), (CMP_FRONT, n_far - CMP_FRONT - nh), (0, 0))
    kcp = jnp.pad(kc, padc)
    vcp = jnp.pad(vc, padc)
    ov = np.zeros((n_far, LANES), np.float32)
    ov[CMP_FRONT:CMP_FRONT + nh - 1, :ns] = _overlap_matrix(nh - 1, ns)
    trel, tcn = _nsa_prompt_tables(rel_bias)
    kvb = jnp.concatenate([u[:, COL_KV + 2 * NSA_KV_W:COL_KV + 4 * NSA_KV_W], u[:, COL_KVW:COL_KVW + 2 * NSA_KV_W]],
                          axis=1).astype(BF16)
    rq = NSA_R * TQ
    const2 = lambda b, i: (0, 0)
    const3 = lambda b, i: (0, 0, 0)
    const4 = lambda b, i: (0, 0, 0, 0)
    zb = COL_Z // NSA_Q_W
    mixed = pl.pallas_call(
        functools.partial(_nsa_attn_kernel, n_sel=ns),
        out_shape=jax.ShapeDtypeStruct((bsz * t, NSA_Q_W), BF16),
        grid=(bsz, n_tiles),
        in_specs=[
            pl.BlockSpec((TQ, NSA_Q_W), lambda b, i: (b * n_tiles + i, 0)),
            pl.BlockSpec((t, 4 * NSA_KV_W), lambda b, i: (b, 0)),
            pl.BlockSpec((1, n_far, NSA_KV_W), lambda b, i: (b, 0, 0)),
            pl.BlockSpec((1, n_far, NSA_KV_W), lambda b, i: (b, 0, 0)),
            pl.BlockSpec((n_far, LANES), const2),
            pl.BlockSpec((NSA_N_KV, N_REL_TILES + 1, rq, KT), const4),
            pl.BlockSpec((NSA_N_KV, rq, CMP_NEAR), const3),
            pl.BlockSpec((TQ, LANES), lambda b, i: (b * n_tiles + i, COL_GATE // LANES)),
            pl.BlockSpec((TQ, NSA_Q_W), lambda b, i: (b * n_tiles + i, zb)),
            pl.BlockSpec((TQ, NSA_Q_W), lambda b, i: (b * n_tiles + i, zb + 1)),
            pl.BlockSpec((TQ, NSA_Q_W), lambda b, i: (b * n_tiles + i, zb + 2)),
        ],
        out_specs=pl.BlockSpec((TQ, NSA_Q_W), lambda b, i: (b * n_tiles + i, 0)),
        scratch_shapes=[
            pltpu.VMEM((NSA_N_KV, TQ, LANES), F32),
            pltpu.VMEM((NSA_N_KV, rq, 1), F32),
            pltpu.VMEM((NSA_N_KV, rq, 1), F32),
            pltpu.VMEM((NSA_N_KV, rq, NSA_HEAD_DIM), F32),
            pltpu.VMEM((N_BRANCH, NSA_N_KV, rq, NSA_HEAD_DIM), F32),
        ],
        compiler_params=_cparams(("parallel", "arbitrary")),
        name="nsa_attn",
    )(u, kvb, kcp, vcp, jnp.asarray(ov), trel, tcn, u, u, u, u)
    out = _matmul_res(mixed, w_out.astype(BF16), x2).reshape(bsz, t, d)
    u3 = u.reshape(bsz, t, NSA_N_PAD)
    kv = u3[:, :, COL_KV:COL_KV + 4 * NSA_KV_W]
    kvw = u3[:, :, COL_KVW:COL_KVW + 2 * NSA_KV_W]
    return out, kv, kvw


def _block_diag_q(q2):
    hg = lax.broadcasted_iota(jnp.int32, q2.shape, 0) // NSA_R
    return jnp.concatenate([jnp.where(hg == g, q2, 0.0) for g in range(NSA_N_KV)], axis=1)


def _group_lanes(o):
    dh = NSA_HEAD_DIM
    hg = lax.broadcasted_iota(jnp.int32, (o.shape[0], dh), 0) // NSA_R
    return sum(jnp.where(hg == g, o[:, g * dh:(g + 1) * dh], 0.0) for g in range(NSA_N_KV))


def _nsa_step_kernel(ids_ref, page_ref, q_ref, kvn_ref, kwn_ref, gate_ref, z_ref, kc_ref, vc_ref, win_ref, ovp_ref,
                     bsel_ref, bcmp_ref, bwin_ref, b0_ref, o_ref, sel_ref, m_ref, l_ref, acc_ref, oc_ref,
                     *, past_len, n_cmp, n_sel_pad):
    del ids_ref
    p = pl.program_id(1)
    nh = NSA_N_HEADS
    dh = NSA_HEAD_DIM
    nt = (((1,), (1,)), ((), ()))
    qbd = _block_diag_q(q_ref[0] * (dh ** -0.5))
    qbd16 = qbd.astype(BF16)
    cur = past_len // SEL_BLOCK

    @pl.when(p == 0)
    def _():
        s_c = lax.dot_general(qbd16, kc_ref[0].astype(BF16), nt, preferred_element_type=F32) + bcmp_ref[...]
        cid = lax.broadcasted_iota(jnp.int32, s_c.shape, 1)
        (p_c,) = _masked_softmax_parts([s_c], [cid < n_cmp])
        oc_ref[...] = _group_lanes(jnp.dot(p_c.astype(BF16), vc_ref[0].astype(BF16), preferred_element_type=F32))
        gr = lax.broadcasted_iota(jnp.int32, (SUBLANES, nh), 0)
        gh = lax.broadcasted_iota(jnp.int32, (SUBLANES, nh), 1) // NSA_R
        pg = jnp.dot((gr == gh).astype(F32), p_c, precision=HIGHEST, preferred_element_type=F32)
        imp = jnp.dot(pg, ovp_ref[...], precision=HIGHEST, preferred_element_type=F32)
        sid = lax.broadcasted_iota(jnp.int32, imp.shape, 1)
        forced = (sid == 0) | (sid == cur) | (sid == cur - 1)
        imp = jnp.where(forced, FORCE_SCORE, imp)
        imp = jnp.where(sid * SEL_BLOCK <= past_len, imp, -FORCE_SCORE)
        er = lax.broadcasted_iota(jnp.int32, (n_sel_pad, n_sel_pad), 0)
        ec = lax.broadcasted_iota(jnp.int32, (n_sel_pad, n_sel_pad), 1)
        rows = []
        for g in range(NSA_N_KV):
            row = imp[g:g + 1, :]
            col = jnp.sum(jnp.where(er == ec, row, 0.0), axis=1, keepdims=True)
            ahead = (col > row) | ((col == row) & (er < ec))
            cnt = jnp.sum(ahead.astype(jnp.int32), axis=0, keepdims=True)
            sel = ((cnt < SEL_TOPK) & (row > -0.5 * FORCE_SCORE)).astype(F32)
            rows.extend([sel] * NSA_R)
        sel_ref[...] = jnp.concatenate(rows, axis=0)
        m_ref[...] = jnp.full(m_ref.shape, NEG, F32)
        l_ref[...] = jnp.zeros(l_ref.shape, F32)
        acc_ref[...] = jnp.zeros(acc_ref.shape, F32)

    def online(s, v_rows):
        m_old = m_ref[...]
        m_new = jnp.maximum(m_old, jnp.max(s, axis=-1, keepdims=True))
        alpha = jnp.exp(m_old - m_new)
        pr = jnp.exp(s - m_new)
        l_ref[...] = l_ref[...] * alpha + jnp.sum(pr, axis=-1, keepdims=True)
        m_ref[...] = m_new
        return alpha, pr

    k_pg = page_ref[:, 0:NSA_KV_W].astype(BF16)
    v_pg = page_ref[:, NSA_KV_W:2 * NSA_KV_W].astype(BF16)
    s = lax.dot_general(qbd16, k_pg, nt, preferred_element_type=F32) + bsel_ref[0]
    eb = lax.broadcasted_iota(jnp.int32, (n_sel_pad, PAGE_SIZE), 0)
    ej = lax.broadcasted_iota(jnp.int32, (n_sel_pad, PAGE_SIZE), 1)
    expand = (eb == p * (PAGE_SIZE // SEL_BLOCK) + ej // SEL_BLOCK).astype(BF16)
    chosen = jnp.dot(sel_ref[...].astype(BF16), expand, preferred_element_type=F32)
    s = jnp.where(chosen > 0.5, s, NEG)
    alpha, pr = online(s, None)
    acc_ref[...] = acc_ref[...] * alpha + jnp.dot(pr.astype(BF16), v_pg, preferred_element_type=F32)

    @pl.when(p == pl.num_programs(1) - 1)
    def _():
        b0 = b0_ref[:, 0:1]
        kn = kvn_ref[0, :, 2 * NSA_KV_W:3 * NSA_KV_W]
        vn = kvn_ref[0, :, 3 * NSA_KV_W:4 * NSA_KV_W]
        s_n = jnp.sum(qbd * kn, axis=-1, keepdims=True) + b0
        s_n = jnp.where(sel_ref[:, cur:cur + 1] > 0.5, s_n, NEG)
        alpha, pr = online(s_n, None)
        acc = acc_ref[...] * alpha + pr * vn
        o_s = _group_lanes(acc / l_ref[...])
        wk = win_ref[0, :, 0:NSA_KV_W].astype(BF16)
        wv = win_ref[0, :, NSA_KV_W:2 * NSA_KV_W].astype(BF16)
        n_win = win_ref.shape[1]
        s_w = lax.dot_general(qbd16, wk, nt, preferred_element_type=F32) + bwin_ref[...]
        wi = lax.broadcasted_iota(jnp.int32, s_w.shape, 1)
        s_wn = jnp.sum(qbd * kwn_ref[0, :, 0:NSA_KV_W], axis=-1, keepdims=True) + b0
        p_w, p_wn = _masked_softmax_parts([s_w, s_wn], [n_win - wi <= WINDOW, jnp.full(s_wn.shape, True)])
        o_w = _group_lanes(jnp.dot(p_w.astype(BF16), wv, preferred_element_type=F32)
                           + p_wn * kwn_ref[0, :, NSA_KV_W:2 * NSA_KV_W])
        gate = jax.nn.sigmoid(gate_ref[0])
        mixed = (oc_ref[...] * _silu(z_ref[0, 0]) * gate[:, 0:1] + o_s * _silu(z_ref[0, 1]) * gate[:, 1:2]
                 + o_w * _silu(z_ref[0, 2]) * gate[:, 2:3])
        o_ref[0] = mixed


def _nsa_layer_sample(xs_in, cache_kv, cache_win_j, page_table, prm, layer):
    norm_w, w_in, pe, w1, w2, w_out, rel_bias = prm
    bsz, t, d = xs_in.shape
    assert t == 1
    n_phys, n_layers = cache_kv.shape[:2]
    n_pages = page_table.shape[1]
    past_len = n_pages * PAGE_SIZE
    n_win = cache_win_j.shape[1]
    nh, dh = NSA_N_HEADS, NSA_HEAD_DIM
    x2 = xs_in.reshape(bsz, d)
    u = _norm_matmul(x2, norm_w, _nsa_in_weights(w_in), _pick_tn(NSA_N_PAD, 1152))
    src = cache_kv.reshape(n_phys * n_layers, PAGE_SIZE, 4 * NSA_KV_W)
    ids = (page_table * n_layers + layer).astype(jnp.int32)
    kc, vc = _compress(src, ids, 0, pe, w1, w2)
    n_cmp_pad = kc.shape[1]
    n_cmp = n_cmp_pad - 1
    ns = past_len // SEL_BLOCK + 1
    n_sel_pad = _round_up(ns, LANES)
    ov = np.zeros((n_cmp_pad, n_sel_pad), np.float32)
    ov[:n_cmp, :ns] = _overlap_matrix(n_cmp, ns)
    kpos = jnp.arange(past_len).reshape(n_pages, PAGE_SIZE)
    bsel = jnp.moveaxis(_bias_of_dist(rel_bias, past_len - kpos), 0, 1)
    bcmp = _bias_of_dist(rel_bias, past_len - (jnp.arange(n_cmp_pad) * CMP_STRIDE + CMP_BLOCK - 1))
    bwin = _bias_of_dist(rel_bias, n_win - jnp.arange(n_win))
    b0 = jnp.broadcast_to(_bias_of_dist(rel_bias, jnp.zeros((1,), jnp.int32)), (nh, LANES))
    q3 = u[:, COL_Q:COL_Q + NSA_Q_W].reshape(bsz, nh, dh)
    kvn = u[:, COL_KV:COL_KV + 4 * NSA_KV_W].reshape(bsz, 1, 4 * NSA_KV_W)
    kwn = u[:, COL_KVW:COL_KVW + 2 * NSA_KV_W].reshape(bsz, 1, 2 * NSA_KV_W)
    gate = u[:, COL_GATE:COL_GATE + nh * N_BRANCH].reshape(bsz, nh, N_BRANCH)
    z4 = u[:, COL_Z:COL_Z + N_BRANCH * NSA_Q_W].reshape(bsz, N_BRANCH, nh, dh)
    win2 = cache_win_j.reshape(bsz, n_win, 2 * NSA_KV_W)
    per_b3 = lambda b, p, ids: (b, 0, 0)
    const2 = lambda b, p, ids: (0, 0)
    grid_spec = pltpu.PrefetchScalarGridSpec(
        num_scalar_prefetch=1,
        grid=(bsz, n_pages),
        in_specs=[
            pl.BlockSpec((None, PAGE_SIZE, 2 * NSA_KV_W), lambda b, p, ids: (ids[b, p], 0, 1)),
            pl.BlockSpec((1, nh, dh), per_b3),
            pl.BlockSpec((1, 1, 4 * NSA_KV_W), per_b3),
            pl.BlockSpec((1, 1, 2 * NSA_KV_W), per_b3),
            pl.BlockSpec((1, nh, N_BRANCH), per_b3),
            pl.BlockSpec((1, N_BRANCH, nh, dh), lambda b, p, ids: (b, 0, 0, 0)),
            pl.BlockSpec((1, n_cmp_pad, NSA_KV_W), per_b3),
            pl.BlockSpec((1, n_cmp_pad, NSA_KV_W), per_b3),
            pl.BlockSpec((1, n_win, 2 * NSA_KV_W), per_b3),
            pl.BlockSpec((n_cmp_pad, n_sel_pad), const2),
            pl.BlockSpec((1, nh, PAGE_SIZE), lambda b, p, ids: (p, 0, 0)),
            pl.BlockSpec((nh, n_cmp_pad), const2),
            pl.BlockSpec((nh, n_win), const2),
            pl.BlockSpec((nh, LANES), const2),
        ],
        out_specs=pl.BlockSpec((1, nh, dh), per_b3),
        scratch_shapes=[
            pltpu.VMEM((nh, n_sel_pad), F32),
            pltpu.VMEM((nh, 1), F32),
            pltpu.VMEM((nh, 1), F32),
            pltpu.VMEM((nh, NSA_KV_W), F32),
            pltpu.VMEM((nh, dh), F32),
        ],
    )
    mixed = pl.pallas_call(
        functools.partial(_nsa_step_kernel, past_len=past_len, n_cmp=n_cmp, n_sel_pad=n_sel_pad),
        out_shape=jax.ShapeDtypeStruct((bsz, nh, dh), F32),
        grid_spec=grid_spec,
        compiler_params=_cparams(("parallel", "arbitrary")),
        name="nsa_step",
    )(ids, src, q3, kvn, kwn, gate, z4, kc, vc, win2, jnp.asarray(ov), bsel, bcmp, bwin, b0)
    out = _matmul_res(mixed.reshape(bsz, NSA_Q_W).astype(BF16), w_out.astype(BF16), x2).reshape(bsz, 1, d)
    kv_new = u[:, COL_KV:COL_KV + 4 * NSA_KV_W].reshape(bsz, 1, 4 * NSA_KV_W)
    kvw_all = jnp.concatenate([win2, kwn], axis=1)
    return out, kv_new, kvw_all


def kernel(x_prompt, x_sample, state_ssm, state_conv, cache_kv, cache_win, page_table, rel_bias, final_norm,
           ssd_norm, ssd_w_in, ssd_conv_w, ssd_conv_b, ssd_dt_bias, ssd_a_log, ssd_d, ssd_gnorm, ssd_w_out,
           nsa_norm, nsa_w_in, nsa_cmp_pe, nsa_cmp_w1, nsa_cmp_w2, nsa_w_out):
    depth = ssd_norm.shape[0] + nsa_norm.shape[0]
    g, dh = NSA_N_KV, NSA_HEAD_DIM
    xp, xs = x_prompt, x_sample
    bp, tp, d = xp.shape
    bs = xs.shape[0]
    win_len_s = cache_win.shape[2]
    ssm_p, conv_p, kv_p, win_p = [], [], [], []
    ssm_s, conv_s, kv_s, win_s = [], [], [], []
    for layer in range(depth):
        j = layer // 2
        if layer % 2 == 0:
            prm = (ssd_norm[j], ssd_w_in[j], ssd_conv_w[j], ssd_conv_b[j], ssd_dt_bias[j], ssd_a_log[j],
                   ssd_d[j], ssd_gnorm[j], ssd_w_out[j])
            xp, c, s = _ssd_layer_prompt(xp, prm)
            conv_p.append(c)
            ssm_p.append(s)
            xs, c, s = _ssd_layer_sample(xs, state_conv[j], state_ssm[j], prm)
            conv_s.append(c)
            ssm_s.append(s)
        else:
            prm = (nsa_norm[j], nsa_w_in[j], nsa_cmp_pe[j], nsa_cmp_w1[j], nsa_cmp_w2[j], nsa_w_out[j], rel_bias)
            xp, kv, kvw = _nsa_layer_prompt(xp, prm)
            kv_p.append(kv.reshape(bp, tp, 4, g, dh))
            win_p.append(kvw[:, tp - min(WINDOW, tp):].reshape(bp, min(WINDOW, tp), 2, g, dh))
            xs, kv, kvw_all = _nsa_layer_sample(xs, cache_kv, cache_win[j], page_table, prm, j)
            kv_s.append(kv.reshape(bs, 1, 4, g, dh))
            win_s.append(kvw_all[:, kvw_all.shape[1] - win_len_s:].reshape(bs, win_len_s, 2, g, dh))
    y_prompt = _rmsnorm(xp.reshape(bp * tp, d), final_norm).reshape(bp, tp, d)
    y_sample = _rmsnorm(xs.reshape(bs, d), final_norm).reshape(bs, 1, d)
    return (y_prompt, y_sample, jnp.stack(ssm_p), jnp.stack(conv_p), jnp.stack(kv_p, axis=2), jnp.stack(win_p),
            jnp.stack(ssm_s), jnp.stack(conv_s), jnp.stack(kv_s, axis=2), jnp.stack(win_s))
```

```python
import functools
import math

import jax
import jax.numpy as jnp
import numpy as np
from jax import lax
from jax.experimental import pallas as pl
from jax.experimental.pallas import tpu as pltpu

F32 = jnp.float32
BF16 = jnp.bfloat16
HIGHEST = lax.Precision.HIGHEST

NORM_EPS = 1e-6
SSD_HEAD_DIM = 64
SSD_N_GROUPS = 8
SSD_D_STATE = 128
SSD_CONV_W = 4
SSD_CHUNK = 128
NSA_N_HEADS = 16
NSA_HEAD_DIM = 64
NSA_N_KV = 4
NSA_R = NSA_N_HEADS // NSA_N_KV
N_BRANCH = 3
CMP_BLOCK = 32
CMP_STRIDE = 16
CMP_HIDDEN = 128
SEL_BLOCK = 64
SEL_TOPK = 16
WINDOW = 512
FORCE_SCORE = 1e4
REL_BUCKETS = 32
REL_MAX_DIST = 1024
PAGE_SIZE = 128

LANES = 128
SUBLANES = 8
VMEM_LIMIT = 56 * 1024 * 1024

NEG = -1e30
NT_DIMS = (((1,), (1,)), ((), ()))
TN_DIMS = (((0,), (0,)), ((), ()))


def _cparams(sem):
    return pltpu.CompilerParams(dimension_semantics=sem, vmem_limit_bytes=VMEM_LIMIT)


def _round_up(x, m):
    return (x + m - 1) // m * m


def _silu(x):
    return x * jax.nn.sigmoid(x)


def _pick_tn(n, cap=1024):
    best = LANES
    for tn in range(LANES, cap + 1, LANES):
        if n % tn == 0:
            best = tn
    return best


def _norm_matmul_kernel(x_ref, nw_ref, w_ref, o_ref, xn_ref, *, transpose_out):
    @pl.when(pl.program_id(1) == 0)
    def _():
        x = x_ref[...]
        ms = jnp.mean(x * x, axis=-1, keepdims=True)
        xn = x * lax.rsqrt(ms + NORM_EPS) * nw_ref[...]
        if transpose_out:
            xn_ref[...] = xn.T.astype(BF16)
        else:
            xn_ref[...] = xn.astype(BF16)

    if transpose_out:
        o_ref[...] = jnp.dot(w_ref[...], xn_ref[...], preferred_element_type=F32).astype(o_ref.dtype)
    else:
        o_ref[...] = jnp.dot(xn_ref[...], w_ref[...], preferred_element_type=F32).astype(o_ref.dtype)


def _norm_matmul(x, nw, w, tn, out_dtype=F32):
    m, k = x.shape
    n = w.shape[1]
    tm = min(m, 512)
    assert m % tm == 0 and n % tn == 0
    return pl.pallas_call(
        functools.partial(_norm_matmul_kernel, transpose_out=False),
        out_shape=jax.ShapeDtypeStruct((m, n), out_dtype),
        grid=(m // tm, n // tn),
        in_specs=[
            pl.BlockSpec((tm, k), lambda i, j: (i, 0)),
            pl.BlockSpec((1, k), lambda i, j: (0, 0)),
            pl.BlockSpec((k, tn), lambda i, j: (0, j)),
        ],
        out_specs=pl.BlockSpec((tm, tn), lambda i, j: (i, j)),
        scratch_shapes=[pltpu.VMEM((tm, k), BF16)],
        compiler_params=_cparams(("parallel", "arbitrary")),
        name="norm_matmul",
    )(x, nw.reshape(1, k), w)


def _norm_matmul_t(x, nw, wt, bsz, tn):
    m, k = x.shape
    n = wt.shape[0]
    t = m // bsz
    tm = min(t, 512)
    assert t % tm == 0 and n % tn == 0
    tpb = t // tm
    return pl.pallas_call(
        functools.partial(_norm_matmul_kernel, transpose_out=True),
        out_shape=jax.ShapeDtypeStruct((bsz, n, t), F32),
        grid=(m // tm, n // tn),
        in_specs=[
            pl.BlockSpec((tm, k), lambda i, j: (i, 0)),
            pl.BlockSpec((1, k), lambda i, j: (0, 0)),
            pl.BlockSpec((tn, k), lambda i, j: (j, 0)),
        ],
        out_specs=pl.BlockSpec((None, tn, tm), lambda i, j: (i // tpb, j, i % tpb)),
        scratch_shapes=[pltpu.VMEM((k, tm), BF16)],
        compiler_params=_cparams(("parallel", "arbitrary")),
        name="norm_matmul_t",
    )(x, nw.reshape(1, k), wt)


def _matmul_res_kernel(y_ref, w_ref, x_ref, o_ref, *, y_transposed):
    dims = TN_DIMS if y_transposed else (((1,), (0,)), ((), ()))
    o_ref[...] = x_ref[...] + lax.dot_general(y_ref[...], w_ref[...], dims, preferred_element_type=F32)


def _matmul_res(y, w, x):
    m, k = y.shape
    n = w.shape[1]
    tm = min(m, 512)
    assert m % tm == 0
    return pl.pallas_call(
        functools.partial(_matmul_res_kernel, y_transposed=False),
        out_shape=jax.ShapeDtypeStruct((m, n), F32),
        grid=(m // tm,),
        in_specs=[
            pl.BlockSpec((tm, k), lambda i: (i, 0)),
            pl.BlockSpec((k, n), lambda i: (0, 0)),
            pl.BlockSpec((tm, n), lambda i: (i, 0)),
        ],
        out_specs=pl.BlockSpec((tm, n), lambda i: (i, 0)),
        compiler_params=_cparams(("parallel",)),
        name="matmul_res",
    )(y, w, x)


def _matmul_res_t(yt, w, x):
    bsz, k, t = yt.shape
    n = w.shape[1]
    tm = min(t, 512)
    assert t % tm == 0
    tpb = t // tm
    return pl.pallas_call(
        functools.partial(_matmul_res_kernel, y_transposed=True),
        out_shape=jax.ShapeDtypeStruct((bsz * t, n), F32),
        grid=(bsz * tpb,),
        in_specs=[
            pl.BlockSpec((None, k, tm), lambda i: (i // tpb, 0, i % tpb)),
            pl.BlockSpec((k, n), lambda i: (0, 0)),
            pl.BlockSpec((tm, n), lambda i: (i, 0)),
        ],
        out_specs=pl.BlockSpec((tm, n), lambda i: (i, 0)),
        compiler_params=_cparams(("parallel",)),
        name="matmul_res_t",
    )(yt, w, x)


def _rmsnorm_kernel(x_ref, nw_ref, o_ref):
    x = x_ref[...]
    ms = jnp.mean(x * x, axis=-1, keepdims=True)
    o_ref[...] = x * lax.rsqrt(ms + NORM_EPS) * nw_ref[...]


def _rmsnorm(x, nw):
    m, k = x.shape
    tm = min(m, 512)
    assert m % tm == 0
    return pl.pallas_call(
        _rmsnorm_kernel,
        out_shape=jax.ShapeDtypeStruct((m, k), F32),
        grid=(m // tm,),
        in_specs=[pl.BlockSpec((tm, k), lambda i: (i, 0)), pl.BlockSpec((1, k), lambda i: (0, 0))],
        out_specs=pl.BlockSpec((tm, k), lambda i: (i, 0)),
        compiler_params=_cparams(("parallel",)),
        name="final_rmsnorm",
    )(x, nw.reshape(1, k))


def _ssd_chunk_kernel(z_ref, x_ref, bc_ref, dt_ref, cprev_ref, sprev_ref, cw_ref, cb_ref, dtb_ref, alog_ref,
                      dskip_ref, gn_ref, y_ref, hlast_ref, xpad_ref, state_ref, *, d_inner, n_heads):
    lc = SSD_CHUNK
    g_n, n, p = SSD_N_GROUPS, SSD_D_STATE, SSD_HEAD_DIM
    r_n = n_heads // g_n
    c = pl.program_id(1)

    @pl.when(c == 0)
    def _():
        xpad_ref[0:SUBLANES, :] = cprev_ref[0]
        state_ref[...] = sprev_ref[0]

    @pl.when(c > 0)
    def _():
        xpad_ref[0:SUBLANES, :] = xpad_ref[lc:lc + SUBLANES, :]

    xpad_ref[SUBLANES:SUBLANES + lc, 0:d_inner] = x_ref[...]
    xpad_ref[SUBLANES:SUBLANES + lc, d_inner:] = bc_ref[...]

    conv = cb_ref[...]
    for k in range(SSD_CONV_W):
        off = SUBLANES - (SSD_CONV_W - 1) + k
        conv = conv + xpad_ref[off:off + lc, :] * cw_ref[k:k + 1, :]
    act = _silu(conv)
    xs = act[:, :d_inner]
    bm = act[:, d_inner:d_inner + g_n * n]
    cm = act[:, d_inner + g_n * n:]

    dt = jax.nn.softplus(dt_ref[...] + dtb_ref[...])
    a = -jnp.exp(alog_ref[...])
    dta = dt * a
    row = lax.broadcasted_iota(jnp.int32, (lc, lc), 0)
    col = lax.broadcasted_iota(jnp.int32, (lc, lc), 1)
    causal = row >= col
    tril = causal.astype(F32)
    a_cs = jnp.dot(tril, dta, precision=HIGHEST, preferred_element_type=F32)
    a_cs_t = a_cs.T
    a_end = a_cs[lc - 1:lc, :]
    to_end = jnp.exp(a_end - a_cs)
    e_cs = jnp.exp(a_cs)
    e_end = jnp.exp(a_end)

    zs = _silu(z_ref[...])
    ys = []
    for g in range(g_n):
        bg = bm[:, g * n:(g + 1) * n]
        cg = cm[:, g * n:(g + 1) * n]
        bg16 = bg.astype(BF16)
        cg16 = cg.astype(BF16)
        cb = lax.dot_general(cg16, bg16, NT_DIMS, preferred_element_type=F32)
        yg = []
        for r in range(r_n):
            h = g * r_n + r
            xh = xs[:, h * p:(h + 1) * p]
            dth = dt[:, h:h + 1]
            xdt = xh * dth
            seg = a_cs[:, h:h + 1] - a_cs_t[h:h + 1, :]
            decay = jnp.exp(jnp.where(causal, seg, -jnp.inf))
            m_h = (cb * decay).astype(BF16)
            y_diag = jnp.dot(m_h, xdt.astype(BF16), preferred_element_type=F32)
            hprev = state_ref[h]
            y_off = lax.dot_general(cg16, hprev.astype(BF16), NT_DIMS,
                                    preferred_element_type=F32) * e_cs[:, h:h + 1]
            xw = (xdt * to_end[:, h:h + 1]).astype(BF16)
            st = lax.dot_general(xw, bg16, TN_DIMS, preferred_element_type=F32)
            state_ref[h] = hprev * e_end[:, h:h + 1] + st
            yh = y_diag + y_off + xh * dskip_ref[:, h:h + 1]
            yg.append(yh)
        yg = jnp.concatenate(yg, axis=1) * zs[:, g * r_n * p:(g + 1) * r_n * p]
        ms = jnp.mean(yg * yg, axis=-1, keepdims=True)
        ys.append(yg * lax.rsqrt(ms + NORM_EPS))
    y = jnp.concatenate(ys, axis=1) * gn_ref[...]
    y_ref[...] = y.astype(y_ref.dtype)

    @pl.when(c == pl.num_programs(1) - 1)
    def _():
        hlast_ref[0] = state_ref[...]


def _pad_lanes(v, width=LANES):
    return jnp.pad(v.astype(F32), (0, width - v.shape[0])).reshape(1, width)


def _ssd_prompt_core(u, bsz, t, conv_prev8, ssm_prev, conv_w, conv_b, dt_bias, a_log, d_skip, gnorm):
    n_heads = dt_bias.shape[0]
    d_inner = n_heads * SSD_HEAD_DIM
    conv_dim = conv_w.shape[1]
    assert conv_dim == 2 * d_inner and t % SSD_CHUNK == 0
    lc = SSD_CHUNK
    nc = t // lc
    dt_blk = (d_inner + conv_dim) // LANES
    kern = functools.partial(_ssd_chunk_kernel, d_inner=d_inner, n_heads=n_heads)
    const = lambda b, c: (0, 0)
    y, hlast = pl.pallas_call(
        kern,
        out_shape=(jax.ShapeDtypeStruct((bsz * t, d_inner), BF16),
                   jax.ShapeDtypeStruct((bsz, n_heads, SSD_HEAD_DIM, SSD_D_STATE), F32)),
        grid=(bsz, nc),
        in_specs=[
            pl.BlockSpec((lc, d_inner), lambda b, c: (b * nc + c, 0)),
            pl.BlockSpec((lc, d_inner), lambda b, c: (b * nc + c, 1)),
            pl.BlockSpec((lc, d_inner), lambda b, c: (b * nc + c, 2)),
            pl.BlockSpec((lc, LANES), lambda b, c: (b * nc + c, dt_blk)),
            pl.BlockSpec((1, SUBLANES, conv_dim), lambda b, c: (b, 0, 0)),
            pl.BlockSpec((1, n_heads, SSD_HEAD_DIM, SSD_D_STATE), lambda b, c: (b, 0, 0, 0)),
            pl.BlockSpec((SSD_CONV_W, conv_dim), const),
            pl.BlockSpec((1, conv_dim), const),
            pl.BlockSpec((1, LANES), const),
            pl.BlockSpec((1, LANES), const),
            pl.BlockSpec((1, LANES), const),
            pl.BlockSpec((1, d_inner), const),
        ],
        out_specs=(pl.BlockSpec((lc, d_inner), lambda b, c: (b * nc + c, 0)),
                   pl.BlockSpec((1, n_heads, SSD_HEAD_DIM, SSD_D_STATE), lambda b, c: (b, 0, 0, 0))),
        scratch_shapes=[pltpu.VMEM((lc + SUBLANES, conv_dim), F32),
                        pltpu.VMEM((n_heads, SSD_HEAD_DIM, SSD_D_STATE), F32)],
        compiler_params=_cparams(("parallel", "arbitrary")),
        name="ssd_chunk",
    )(u, u, u, u, conv_prev8, ssm_prev, conv_w, conv_b.reshape(1, conv_dim), _pad_lanes(dt_bias),
      _pad_lanes(a_log), _pad_lanes(d_skip), gnorm.reshape(1, d_inner))
    return y, hlast


def _ssd_in_weights(w_in, d_inner, conv_dim):
    main = d_inner + conv_dim
    n_dt = w_in.shape[1] - main
    w = jnp.concatenate([w_in[:, :main], jnp.pad(w_in[:, main:], ((0, 0), (0, LANES - n_dt)))], axis=1)
    return w.astype(BF16)


def _ssd_layer_prompt(xp, prm):
    norm_w, w_in, conv_w, conv_b, dt_bias, a_log, d_skip, gnorm, w_out = prm
    bsz, t, d = xp.shape
    assert t >= SSD_CONV_W - 1
    n_heads = dt_bias.shape[0]
    d_inner = n_heads * SSD_HEAD_DIM
    conv_dim = conv_w.shape[1]
    w = _ssd_in_weights(w_in, d_inner, conv_dim)
    x2 = xp.reshape(bsz * t, d)
    u = _norm_matmul(x2, norm_w, w, _pick_tn(w.shape[1]))
    conv_prev8 = jnp.zeros((bsz, SUBLANES, conv_dim), F32)
    ssm_prev = jnp.zeros((bsz, n_heads, SSD_HEAD_DIM, SSD_D_STATE), F32)
    y, hlast = _ssd_prompt_core(u, bsz, t, conv_prev8, ssm_prev, conv_w, conv_b, dt_bias, a_log, d_skip, gnorm)
    out = _matmul_res(y, w_out.astype(BF16), x2).reshape(bsz, t, d)
    u3 = u.reshape(bsz, t, -1)
    new_conv = u3[:, t - (SSD_CONV_W - 1):, d_inner:d_inner + conv_dim]
    return out, new_conv, hlast


def _ssd_step_pre_kernel(u_ref, cst_ref, cw_ref, cb_ref, dtb_ref, alog_ref,
                         xs_ref, xdt_ref, dec_ref, bm_ref, cm_ref, ncst_ref, *, d_inner, n_heads):
    conv_dim = cw_ref.shape[1]
    gn = SSD_N_GROUPS * SSD_D_STATE
    xbc = u_ref[:, d_inner:d_inner + conv_dim]
    conv = cb_ref[...]
    for k in range(SSD_CONV_W - 1):
        conv = conv + cst_ref[k] * cw_ref[k:k + 1, :]
    conv = conv + xbc * cw_ref[SSD_CONV_W - 1:SSD_CONV_W, :]
    for k in range(SSD_CONV_W - 2):
        ncst_ref[k] = cst_ref[k + 1]
    ncst_ref[SSD_CONV_W - 2] = xbc
    act = _silu(conv)
    xs = act[:, :d_inner]
    dt = jax.nn.softplus(u_ref[:, d_inner + conv_dim:d_inner + conv_dim + LANES] + dtb_ref[...])
    dec = jnp.exp(dt * (-jnp.exp(alog_ref[...])))
    hrow = lax.broadcasted_iota(jnp.int32, (LANES, d_inner), 0)
    hcol = lax.broadcasted_iota(jnp.int32, (LANES, d_inner), 1) // SSD_HEAD_DIM
    expand = (hrow == hcol).astype(F32)
    dt_e = jnp.dot(dt, expand, precision=HIGHEST, preferred_element_type=F32)
    dec_e = jnp.dot(dec, expand, precision=HIGHEST, preferred_element_type=F32)
    xs_ref[...] = xs
    xdt_ref[...] = xs * dt_e
    dec_ref[...] = dec_e
    bm_ref[...] = act[:, d_inner:d_inner + gn]
    cm_ref[...] = act[:, d_inner + gn:]


def _ssd_step_state_kernel(s_ref, xdt_ref, dec_ref, b_ref, c_ref, snew_ref, y_ref, *, rows_per_group):
    for g in range(SSD_N_GROUPS):
        sl = slice(g * rows_per_group, (g + 1) * rows_per_group)
        s = s_ref[0, sl, :]
        bg = b_ref[0, g:g + 1, :]
        cg = c_ref[0, g:g + 1, :]
        xdt = xdt_ref[0, sl, :]
        dec = dec_ref[0, sl, :]
        cb = jnp.sum(bg * cg, axis=1, keepdims=True)
        y_off = jnp.sum(s * cg, axis=1, keepdims=True)
        snew_ref[0, sl, :] = s * dec + xdt * bg
        y_ref[0, sl, :] = cb * xdt + dec * y_off


def _ssd_step_post_kernel(y_ref, xs_ref, u_ref, dsk_ref, gn_ref, o_ref, *, d_inner, group_width):
    y = (y_ref[...] + xs_ref[...] * dsk_ref[...]) * _silu(u_ref[:, :d_inner])
    outs = []
    for g in range(SSD_N_GROUPS):
        yg = y[:, g * group_width:(g + 1) * group_width]
        ms = jnp.mean(yg * yg, axis=-1, keepdims=True)
        outs.append(yg * lax.rsqrt(ms + NORM_EPS))
    o_ref[...] = (jnp.concatenate(outs, axis=1) * gn_ref[...]).astype(o_ref.dtype)


def _ssd_layer_sample(xs_in, conv_state, ssm_state, prm):
    norm_w, w_in, conv_w, conv_b, dt_bias, a_log, d_skip, gnorm, w_out = prm
    bsz, t, d = xs_in.shape
    assert t == 1
    n_heads = dt_bias.shape[0]
    d_inner = n_heads * SSD_HEAD_DIM
    conv_dim = conv_w.shape[1]
    gn = SSD_N_GROUPS * SSD_D_STATE
    w = _ssd_in_weights(w_in, d_inner, conv_dim)
    x2 = xs_in.reshape(bsz, d)
    u = _norm_matmul(x2, norm_w, w, _pick_tn(w.shape[1]))
    cst = conv_state.transpose(1, 0, 2)
    pre = functools.partial(_ssd_step_pre_kernel, d_inner=d_inner, n_heads=n_heads)
    xs, xdt, dec, bm, cm, ncst = pl.pallas_call(
        pre,
        out_shape=(jax.ShapeDtypeStruct((bsz, d_inner), F32), jax.ShapeDtypeStruct((bsz, d_inner), F32),
                   jax.ShapeDtypeStruct((bsz, d_inner), F32), jax.ShapeDtypeStruct((bsz, gn), F32),
                   jax.ShapeDtypeStruct((bsz, gn), F32), jax.ShapeDtypeStruct(cst.shape, F32)),
        compiler_params=pltpu.CompilerParams(vmem_limit_bytes=VMEM_LIMIT),
        name="ssd_step_pre",
    )(u, cst, conv_w, conv_b.reshape(1, conv_dim), _pad_lanes(dt_bias), _pad_lanes(a_log))
    rows = n_heads * SSD_HEAD_DIM
    rpg = rows // SSD_N_GROUPS
    st = functools.partial(_ssd_step_state_kernel, rows_per_group=rpg)
    snew, ycol = pl.pallas_call(
        st,
        out_shape=(jax.ShapeDtypeStruct((bsz, rows, SSD_D_STATE), F32), jax.ShapeDtypeStruct((bsz, rows, 1), F32)),
        grid=(bsz,),
        in_specs=[
            pl.BlockSpec((1, rows, SSD_D_STATE), lambda b: (b, 0, 0)),
            pl.BlockSpec((1, rows, 1), lambda b: (b, 0, 0)),
            pl.BlockSpec((1, rows, 1), lambda b: (b, 0, 0)),
            pl.BlockSpec((1, SSD_N_GROUPS, SSD_D_STATE), lambda b: (b, 0, 0)),
            pl.BlockSpec((1, SSD_N_GROUPS, SSD_D_STATE), lambda b: (b, 0, 0)),
        ],
        out_specs=(pl.BlockSpec((1, rows, SSD_D_STATE), lambda b: (b, 0, 0)),
                   pl.BlockSpec((1, rows, 1), lambda b: (b, 0, 0))),
        compiler_params=_cparams(("parallel",)),
        name="ssd_step_state",
    )(ssm_state.reshape(bsz, rows, SSD_D_STATE), xdt.reshape(bsz, rows, 1), dec.reshape(bsz, rows, 1),
      bm.reshape(bsz, SSD_N_GROUPS, SSD_D_STATE), cm.reshape(bsz, SSD_N_GROUPS, SSD_D_STATE))
    post = functools.partial(_ssd_step_post_kernel, d_inner=d_inner, group_width=d_inner // SSD_N_GROUPS)
    y = pl.pallas_call(
        post,
        out_shape=jax.ShapeDtypeStruct((bsz, d_inner), BF16),
        compiler_params=pltpu.CompilerParams(vmem_limit_bytes=VMEM_LIMIT),
        name="ssd_step_post",
    )(ycol.reshape(bsz, rows), xs, u, jnp.repeat(d_skip.astype(F32), SSD_HEAD_DIM).reshape(1, d_inner),
      gnorm.reshape(1, d_inner))
    out = _matmul_res(y, w_out.astype(BF16), x2).reshape(bsz, 1, d)
    return out, ncst.transpose(1, 0, 2), snew.reshape(ssm_state.shape)


NSA_Q_W = NSA_N_HEADS * NSA_HEAD_DIM
NSA_KV_W = NSA_N_KV * NSA_HEAD_DIM
COL_Q = 0
COL_KV = NSA_Q_W
COL_Z = COL_KV + 4 * NSA_KV_W
COL_KVW = COL_Z + N_BRANCH * NSA_Q_W
COL_GATE = COL_KVW + 2 * NSA_KV_W
NSA_N_PAD = COL_GATE + LANES
TQ = 128
KT = 128
N_REL_TILES = -(-(REL_MAX_DIST + TQ - 1) // KT)
CMP_NEAR = 128
CMP_FRONT = CMP_NEAR - TQ // CMP_STRIDE
PAGES_PER_STEP = 8


def _nsa_in_weights(w_in):
    cuts = np.cumsum([NSA_Q_W, 4 * NSA_KV_W, 2 * NSA_KV_W, NSA_N_HEADS * N_BRANCH]).tolist()
    q, kv, kvw, gate, z = (w_in[:, a:b] for a, b in zip([0] + cuts, cuts + [w_in.shape[1]]))
    gate = jnp.pad(gate, ((0, 0), (0, LANES - gate.shape[1])))
    return jnp.concatenate([q, kv, z, kvw, gate], axis=1)


def _rel_bucket(dist):
    n = jnp.maximum(dist, 0)
    exact = REL_BUCKETS // 2
    logv = jnp.log(jnp.maximum(n, 1).astype(F32) / exact) / math.log(REL_MAX_DIST / exact)
    large = jnp.minimum(exact + (logv * (REL_BUCKETS - exact)).astype(jnp.int32), REL_BUCKETS - 1)
    return jnp.where(n < exact, n, large)


def _bias_of_dist(rel_bias, dist):
    return jnp.moveaxis(rel_bias.astype(F32)[_rel_bucket(dist)], -1, 0)


def _lanes_by_group(t):
    h, n, q = t.shape
    return t.reshape(NSA_N_KV, NSA_R, n, q).transpose(0, 2, 1, 3).reshape(NSA_N_KV, n, NSA_R * q)


def _overlap_matrix(nc, ns):
    c_start = np.arange(nc) * CMP_STRIDE
    c_end = c_start + CMP_BLOCK - 1
    s_start = np.arange(ns) * SEL_BLOCK
    ov = np.clip(np.minimum(c_end[:, None], s_start[None, :] + SEL_BLOCK - 1)
                 - np.maximum(c_start[:, None], s_start[None, :]) + 1, 0, None).astype(np.float32) / CMP_STRIDE
    return ov


def _masked_softmax_parts(parts, valids, axis):
    m = None
    for s, v in zip(parts, valids):
        pm = jnp.max(jnp.where(v, s, -jnp.inf), axis=axis, keepdims=True)
        m = pm if m is None else jnp.maximum(m, pm)
    m = jnp.where(m > -jnp.inf, m, 0.0)
    es = [jnp.where(v, jnp.exp(s - m), 0.0) for s, v in zip(parts, valids)]
    den = None
    for e in es:
        d = jnp.sum(e, axis=axis, keepdims=True)
        den = d if den is None else den + d
    inv = 1.0 / jnp.where(den > 0, den, 1.0)
    return [e * inv for e in es]


def _compress_kernel(*refs, n_pp):
    page_refs = refs[1:1 + n_pp]
    pe_ref, w1_ref, w2_ref, kc_ref, vc_ref, x_ref = refs[1 + n_pp:]
    step = pl.program_id(1)
    hb = PAGE_SIZE // CMP_STRIDE
    dh = NSA_HEAD_DIM
    for k in range(n_pp):
        pt = page_refs[k][...].T.reshape(hb, CMP_STRIDE, 2 * NSA_KV_W)
        row0 = pl.multiple_of((step * n_pp + k) * hb, hb)
        for l in range(CMP_STRIDE):
            row = pt[:, l, :]
            for kv in range(2):
                for g in range(NSA_N_KV):
                    c0 = kv * NSA_KV_W + g * dh
                    x_ref[kv, g, pl.ds(row0, hb), l * dh:(l + 1) * dh] = row[:, c0:c0 + dh]

    @pl.when(step == pl.num_programs(1) - 1)
    def _():
        half = CMP_STRIDE * dh
        for kv, out_ref in ((0, kc_ref), (1, vc_ref)):
            w_top = w1_ref[kv, 0:half, :]
            w_bot = w1_ref[kv, half:2 * half, :]
            w2 = w2_ref[kv]
            for g in range(NSA_N_KV):
                x = x_ref[kv, g]
                top = jnp.dot((x + pe_ref[kv, 0:1, :]).astype(BF16), w_top, preferred_element_type=F32)
                bot = jnp.dot((x + pe_ref[kv, 1:2, :]).astype(BF16), w_bot, preferred_element_type=F32)
                bot = jnp.concatenate([bot[1:], jnp.zeros((1, bot.shape[1]), F32)], axis=0)
                hid = _silu(top + bot).astype(BF16)
                out_ref[0, :, g * dh:(g + 1) * dh] = jnp.dot(hid, w2, preferred_element_type=F32)


def _compress(src, page_index, bsz, n_pages, ids, pe, w1, w2):
    n_pp = math.gcd(PAGES_PER_STEP, n_pages)
    hb = PAGE_SIZE // CMP_STRIDE
    nh = n_pages * hb
    half = CMP_STRIDE * NSA_HEAD_DIM
    blk = (None,) * (src.ndim - 2) + (2 * NSA_KV_W, PAGE_SIZE)
    page_specs = [pl.BlockSpec(blk, functools.partial(lambda b, s, ids, k: page_index(b, s * n_pp + k, ids), k=k))
                  for k in range(n_pp)]
    grid_spec = pltpu.PrefetchScalarGridSpec(
        num_scalar_prefetch=1,
        grid=(bsz, n_pages // n_pp),
        in_specs=page_specs + [
            pl.BlockSpec((2, 2, half), lambda b, s, ids: (0, 0, 0)),
            pl.BlockSpec((2, 2 * half, CMP_HIDDEN), lambda b, s, ids: (0, 0, 0)),
            pl.BlockSpec((2, CMP_HIDDEN, NSA_HEAD_DIM), lambda b, s, ids: (0, 0, 0)),
        ],
        out_specs=(pl.BlockSpec((1, nh, NSA_KV_W), lambda b, s, ids: (b, 0, 0)),
                   pl.BlockSpec((1, nh, NSA_KV_W), lambda b, s, ids: (b, 0, 0))),
        scratch_shapes=[pltpu.VMEM((2, NSA_N_KV, nh, half), F32)],
    )
    return pl.pallas_call(
        functools.partial(_compress_kernel, n_pp=n_pp),
        out_shape=(jax.ShapeDtypeStruct((bsz, nh, NSA_KV_W), F32), jax.ShapeDtypeStruct((bsz, nh, NSA_KV_W), F32)),
        grid_spec=grid_spec,
        compiler_params=_cparams(("parallel", "arbitrary")),
        name="nsa_compress",
    )(ids, *([src] * n_pp), pe.reshape(2, 2, half).astype(F32), w1.astype(BF16), w2.astype(BF16))


def _nsa_attn_kernel(qt_ref, kv_ref, kc_ref, vc_ref, ov_ref, trel_ref, tcn_ref, gate_ref, z0_ref, z1_ref, z2_ref,
                     o_ref, sel_ref, m_ref, l_ref, acc_ref, ob_ref, *, n_sel):
    t = pl.program_id(1)
    dh = NSA_HEAD_DIM
    rq = NSA_R * TQ
    scale = dh ** -0.5
    n_far = kc_ref.shape[1]
    far_idx = N_REL_TILES
    kj = lax.broadcasted_iota(jnp.int32, (KT, rq), 0)
    qi = lax.broadcasted_iota(jnp.int32, (KT, rq), 1) % TQ
    near0 = pl.multiple_of(t * (TQ // CMP_STRIDE), SUBLANES)

    qts = []
    for g in range(NSA_N_KV):
        qt = jnp.concatenate([qt_ref[(g * NSA_R + r) * dh:(g * NSA_R + r + 1) * dh, :] for r in range(NSA_R)], axis=1)
        qt = (qt * scale).astype(BF16)
        qts.append(qt)
        gs = slice(g * dh, (g + 1) * dh)
        kc_far = kc_ref[0, :, gs].astype(BF16)
        vc_far = vc_ref[0, :, gs].astype(BF16)
        kc_near = kc_ref[0, pl.ds(near0, CMP_NEAR), gs].astype(BF16)
        vc_near = vc_ref[0, pl.ds(near0, CMP_NEAR), gs].astype(BF16)
        s_far = jnp.dot(kc_far, qt, preferred_element_type=F32) + trel_ref[g, far_idx, 0:1, :].astype(F32)
        s_near = jnp.dot(kc_near, qt, preferred_element_type=F32) + tcn_ref[g]
        cf = lax.broadcasted_iota(jnp.int32, (n_far, rq), 0)
        v_far = (cf >= CMP_FRONT) & (cf < near0)
        cn = lax.broadcasted_iota(jnp.int32, (CMP_NEAR, rq), 0)
        qn = lax.broadcasted_iota(jnp.int32, (CMP_NEAR, rq), 1) % TQ
        dist_n = qn - CMP_STRIDE * cn + (CMP_STRIDE * CMP_FRONT - (CMP_BLOCK - 1))
        v_near = (dist_n >= 0) & (cn + near0 >= CMP_FRONT)
        p_far, p_near = _masked_softmax_parts([s_far, s_near], [v_far, v_near], axis=0)
        ob_ref[0, g] = (lax.dot_general(vc_far, p_far.astype(BF16), TN_DIMS, preferred_element_type=F32)
                        + lax.dot_general(vc_near, p_near.astype(BF16), TN_DIMS, preferred_element_type=F32))
        pg_far = sum(p_far[:, r * TQ:(r + 1) * TQ] for r in range(NSA_R))
        pg_near = sum(p_near[:, r * TQ:(r + 1) * TQ] for r in range(NSA_R))
        imp = (lax.dot_general(ov_ref[...], pg_far, TN_DIMS, precision=HIGHEST, preferred_element_type=F32)
               + lax.dot_general(ov_ref[pl.ds(near0, CMP_NEAR), :], pg_near, TN_DIMS, precision=HIGHEST,
                                 preferred_element_type=F32))
        sid = lax.broadcasted_iota(jnp.int32, (LANES, TQ), 0)
        qpos = t * TQ + lax.broadcasted_iota(jnp.int32, (LANES, TQ), 1)
        cur = qpos // SEL_BLOCK
        forced = (sid == 0) | (sid == cur) | (sid == cur - 1)
        imp = jnp.where(forced, FORCE_SCORE, imp)
        imp = jnp.where(sid * SEL_BLOCK <= qpos, imp, -FORCE_SCORE)
        cnt = jnp.zeros((LANES, TQ), jnp.int32)
        for sp in range(n_sel):
            other = imp[sp:sp + 1, :]
            ahead = (other > imp) | ((other == imp) & (sid > sp))
            cnt = cnt + ahead.astype(jnp.int32)
        sel_ref[g] = ((cnt < SEL_TOPK) & (imp > -0.5 * FORCE_SCORE)).astype(F32).astype(BF16)

    def reset():
        m_ref[...] = jnp.full(m_ref.shape, NEG, F32)
        l_ref[...] = jnp.zeros(l_ref.shape, F32)
        acc_ref[...] = jnp.zeros(acc_ref.shape, F32)

    def attend(g, k0, k_col, v_col, bias, ok):
        k_t = kv_ref[pl.ds(k0, KT), k_col + g * dh:k_col + (g + 1) * dh]
        v_t = kv_ref[pl.ds(k0, KT), v_col + g * dh:v_col + (g + 1) * dh]
        s = jnp.dot(k_t, qts[g], preferred_element_type=F32) + bias
        s = jnp.where(ok, s, NEG)
        m_old = m_ref[g]
        m_new = jnp.maximum(m_old, jnp.max(s, axis=0, keepdims=True))
        alpha = jnp.exp(m_old - m_new)
        p = jnp.exp(s - m_new)
        l_ref[g] = l_ref[g] * alpha + jnp.sum(p, axis=0, keepdims=True)
        acc_ref[g] = acc_ref[g] * alpha + lax.dot_general(v_t, p.astype(BF16), TN_DIMS, preferred_element_type=F32)
        m_ref[g] = m_new

    reset()
    blk_per_tile = KT // SEL_BLOCK

    def sel_body(kt, carry):
        delta = t - kt
        bias_idx = jnp.minimum(delta, far_idx)
        k0 = pl.multiple_of(kt * KT, KT)
        ej = lax.broadcasted_iota(jnp.int32, (KT, LANES), 0)
        eb = lax.broadcasted_iota(jnp.int32, (KT, LANES), 1)
        expand = (eb == kt * blk_per_tile + ej // SEL_BLOCK).astype(BF16)
        causal_ok = (delta > 0) | (kj <= qi)
        for g in range(NSA_N_KV):
            chosen = jnp.dot(expand, sel_ref[g], preferred_element_type=F32)
            chosen = jnp.concatenate([chosen] * NSA_R, axis=1)
            attend(g, k0, 0, NSA_KV_W, trel_ref[g, bias_idx].astype(F32), (chosen > 0.5) & causal_ok)
        return carry

    lax.fori_loop(0, t + 1, sel_body, 0)
    for g in range(NSA_N_KV):
        ob_ref[1, g] = acc_ref[g] / l_ref[g]

    reset()
    for delta in range(WINDOW // KT + 1):
        kt = t - delta
        k0 = pl.multiple_of(jnp.maximum(kt, 0) * KT, KT)
        dist = delta * KT + qi - kj
        ok = (dist >= 0) & (dist <= WINDOW) & (kt >= 0)
        for g in range(NSA_N_KV):
            attend(g, k0, 2 * NSA_KV_W, 3 * NSA_KV_W, trel_ref[g, delta].astype(F32), ok)
    for g in range(NSA_N_KV):
        ob_ref[2, g] = acc_ref[g] / l_ref[g]

    gate = jax.nn.sigmoid(gate_ref[...])
    z_refs = (z0_ref, z1_ref, z2_ref)
    for h in range(NSA_N_HEADS):
        g, r = divmod(h, NSA_R)
        mixed = None
        for c in range(N_BRANCH):
            o_hc = ob_ref[c, g, :, r * TQ:(r + 1) * TQ]
            term = o_hc * _silu(z_refs[c][h * dh:(h + 1) * dh, :]) * gate[h * N_BRANCH + c:h * N_BRANCH + c + 1, :]
            mixed = term if mixed is None else mixed + term
        o_ref[h * dh:(h + 1) * dh, :] = mixed.astype(o_ref.dtype)


def _nsa_prompt_tables(rel_bias):
    kj = jnp.arange(KT)[:, None]
    qi = jnp.arange(TQ)[None, :]
    tiles = [_lanes_by_group(_bias_of_dist(rel_bias, delta * KT + qi - kj)) for delta in range(N_REL_TILES)]
    far = jnp.broadcast_to(rel_bias.astype(F32)[REL_BUCKETS - 1].reshape(NSA_N_HEADS, 1, 1), (NSA_N_HEADS, KT, TQ))
    tiles.append(_lanes_by_group(far))
    trel = jnp.stack(tiles, axis=1).astype(BF16)
    cn = jnp.arange(CMP_NEAR)[:, None]
    dist_n = qi - CMP_STRIDE * cn + (CMP_STRIDE * CMP_FRONT - (CMP_BLOCK - 1))
    tcn = _lanes_by_group(_bias_of_dist(rel_bias, dist_n))
    return trel, tcn


def _nsa_layer_prompt(xp, prm):
    norm_w, w_in, pe, w1, w2, w_out, rel_bias = prm
    bsz, t, d = xp.shape
    assert t % TQ == 0 and t % PAGE_SIZE == 0 and t >= WINDOW
    n_tiles = t // TQ
    ns = t // SEL_BLOCK
    assert ns <= LANES
    x2 = xp.reshape(bsz * t, d)
    w = _nsa_in_weights(w_in)
    ut = _norm_matmul_t(x2, norm_w, w.T.astype(BF16), bsz, _pick_tn(NSA_N_PAD, 1152))
    w_kv = jnp.concatenate([w[:, COL_KV + 2 * NSA_KV_W:COL_KV + 4 * NSA_KV_W], w[:, COL_KVW:COL_KVW + 2 * NSA_KV_W]],
                           axis=1).astype(BF16)
    kvn = _norm_matmul(x2, norm_w, w_kv, 4 * NSA_KV_W, out_dtype=BF16)
    n_pages = t // PAGE_SIZE
    cmp_blk = COL_KV // (2 * NSA_KV_W)
    kc, vc = _compress(ut, lambda b, p, ids: (b, cmp_blk, p), bsz, n_pages, jnp.zeros((1, 1), jnp.int32), pe, w1, w2)
    nh = kc.shape[1]
    n_far = _round_up(CMP_FRONT + nh, LANES)
    padc = ((0, 0), (CMP_FRONT, n_far - CMP_FRONT - nh), (0, 0))
    kcp = jnp.pad(kc, padc)
    vcp = jnp.pad(vc, padc)
    ov = np.zeros((n_far, LANES), np.float32)
    ov[CMP_FRONT:CMP_FRONT + nh - 1, :ns] = _overlap_matrix(nh - 1, ns)
    trel, tcn = _nsa_prompt_tables(rel_bias)
    rq = NSA_R * TQ
    const2 = lambda b, i: (0, 0)
    const3 = lambda b, i: (0, 0, 0)
    const4 = lambda b, i: (0, 0, 0, 0)
    zb = COL_Z // NSA_Q_W
    mixed_t = pl.pallas_call(
        functools.partial(_nsa_attn_kernel, n_sel=ns),
        out_shape=jax.ShapeDtypeStruct((bsz, NSA_Q_W, t), BF16),
        grid=(bsz, n_tiles),
        in_specs=[
            pl.BlockSpec((None, NSA_Q_W, TQ), lambda b, i: (b, 0, i)),
            pl.BlockSpec((t, 4 * NSA_KV_W), lambda b, i: (b, 0)),
            pl.BlockSpec((1, n_far, NSA_KV_W), lambda b, i: (b, 0, 0)),
            pl.BlockSpec((1, n_far, NSA_KV_W), lambda b, i: (b, 0, 0)),
            pl.BlockSpec((n_far, LANES), const2),
            pl.BlockSpec((NSA_N_KV, N_REL_TILES + 1, KT, rq), const4),
            pl.BlockSpec((NSA_N_KV, CMP_NEAR, rq), const3),
            pl.BlockSpec((None, LANES, TQ), lambda b, i: (b, COL_GATE // LANES, i)),
            pl.BlockSpec((None, NSA_Q_W, TQ), lambda b, i: (b, zb, i)),
            pl.BlockSpec((None, NSA_Q_W, TQ), lambda b, i: (b, zb + 1, i)),
            pl.BlockSpec((None, NSA_Q_W, TQ), lambda b, i: (b, zb + 2, i)),
        ],
        out_specs=pl.BlockSpec((None, NSA_Q_W, TQ), lambda b, i: (b, 0, i)),
        scratch_shapes=[
            pltpu.VMEM((NSA_N_KV, LANES, TQ), BF16),
            pltpu.VMEM((NSA_N_KV, 1, rq), F32),
            pltpu.VMEM((NSA_N_KV, 1, rq), F32),
            pltpu.VMEM((NSA_N_KV, NSA_HEAD_DIM, rq), F32),
            pltpu.VMEM((N_BRANCH, NSA_N_KV, NSA_HEAD_DIM, rq), F32),
        ],
        compiler_params=_cparams(("parallel", "arbitrary")),
        name="nsa_attn",
    )(ut, kvn, kcp, vcp, jnp.asarray(ov), trel, tcn, ut, ut, ut, ut)
    out = _matmul_res_t(mixed_t, w_out.astype(BF16), x2).reshape(bsz, t, d)
    return out, ut


def _block_diag_q(q2):
    hg = lax.broadcasted_iota(jnp.int32, q2.shape, 0) // NSA_R
    return jnp.concatenate([jnp.where(hg == g, q2, 0.0) for g in range(NSA_N_KV)], axis=1)


def _group_lanes(o):
    dh = NSA_HEAD_DIM
    hg = lax.broadcasted_iota(jnp.int32, (o.shape[0], dh), 0) // NSA_R
    return sum(jnp.where(hg == g, o[:, g * dh:(g + 1) * dh], 0.0) for g in range(NSA_N_KV))


def _nsa_step_kernel(*refs, n_pp, past_len, n_cmp, n_sel_pad):
    page_refs = refs[1:1 + n_pp]
    (q_ref, kvn_ref, kwn_ref, gate_ref, z_ref, kc_ref, vc_ref, win_ref, ovp_ref, bsel_ref, bcmp_ref, bwin_ref, b0_ref,
     o_ref, sel_ref, m_ref, l_ref, acc_ref, oc_ref) = refs[1 + n_pp:]
    step = pl.program_id(1)
    nh = NSA_N_HEADS
    dh = NSA_HEAD_DIM
    qbd = _block_diag_q(q_ref[0] * (dh ** -0.5))
    qbd16 = qbd.astype(BF16)
    cur = past_len // SEL_BLOCK

    @pl.when(step == 0)
    def _():
        s_c = lax.dot_general(qbd16, kc_ref[0].astype(BF16), NT_DIMS, preferred_element_type=F32) + bcmp_ref[...]
        cid = lax.broadcasted_iota(jnp.int32, s_c.shape, 1)
        (p_c,) = _masked_softmax_parts([s_c], [cid < n_cmp], axis=1)
        oc_ref[...] = _group_lanes(jnp.dot(p_c.astype(BF16), vc_ref[0].astype(BF16), preferred_element_type=F32))
        gr = lax.broadcasted_iota(jnp.int32, (SUBLANES, nh), 0)
        gh = lax.broadcasted_iota(jnp.int32, (SUBLANES, nh), 1) // NSA_R
        pg = jnp.dot((gr == gh).astype(F32), p_c, precision=HIGHEST, preferred_element_type=F32)
        imp = jnp.dot(pg, ovp_ref[...], precision=HIGHEST, preferred_element_type=F32)
        sid = lax.broadcasted_iota(jnp.int32, imp.shape, 1)
        forced = (sid == 0) | (sid == cur) | (sid == cur - 1)
        imp = jnp.where(forced, FORCE_SCORE, imp)
        imp = jnp.where(sid * SEL_BLOCK <= past_len, imp, -FORCE_SCORE)
        er = lax.broadcasted_iota(jnp.int32, (n_sel_pad, n_sel_pad), 0)
        ec = lax.broadcasted_iota(jnp.int32, (n_sel_pad, n_sel_pad), 1)
        rows = []
        for g in range(NSA_N_KV):
            row = imp[g:g + 1, :]
            col = jnp.sum(jnp.where(er == ec, row, 0.0), axis=1, keepdims=True)
            ahead = (col > row) | ((col == row) & (er < ec))
            cnt = jnp.sum(ahead.astype(jnp.int32), axis=0, keepdims=True)
            sel = ((cnt < SEL_TOPK) & (row > -0.5 * FORCE_SCORE)).astype(F32)
            rows.extend([sel] * NSA_R)
        sel_ref[...] = jnp.concatenate(rows, axis=0)
        m_ref[...] = jnp.full(m_ref.shape, NEG, F32)
        l_ref[...] = jnp.zeros(l_ref.shape, F32)
        acc_ref[...] = jnp.zeros(acc_ref.shape, F32)

    def online(s):
        m_old = m_ref[...]
        m_new = jnp.maximum(m_old, jnp.max(s, axis=-1, keepdims=True))
        alpha = jnp.exp(m_old - m_new)
        pr = jnp.exp(s - m_new)
        l_ref[...] = l_ref[...] * alpha + jnp.sum(pr, axis=-1, keepdims=True)
        m_ref[...] = m_new
        return alpha, pr

    k_all = jnp.concatenate([r[0:NSA_KV_W, :] for r in page_refs], axis=1).astype(BF16)
    v_all = jnp.concatenate([r[NSA_KV_W:2 * NSA_KV_W, :] for r in page_refs], axis=1).astype(BF16)
    n_tok = n_pp * PAGE_SIZE
    s = jnp.dot(qbd16, k_all, preferred_element_type=F32) + bsel_ref[0]
    eb = lax.broadcasted_iota(jnp.int32, (n_sel_pad, n_tok), 0)
    ej = lax.broadcasted_iota(jnp.int32, (n_sel_pad, n_tok), 1)
    expand = (eb == step * (n_tok // SEL_BLOCK) + ej // SEL_BLOCK).astype(BF16)
    chosen = jnp.dot(sel_ref[...].astype(BF16), expand, preferred_element_type=F32)
    s = jnp.where(chosen > 0.5, s, NEG)
    alpha, pr = online(s)
    acc_ref[...] = acc_ref[...] * alpha + lax.dot_general(pr.astype(BF16), v_all, NT_DIMS, preferred_element_type=F32)

    @pl.when(step == pl.num_programs(1) - 1)
    def _():
        b0 = b0_ref[:, 0:1]
        kn = kvn_ref[0, :, 2 * NSA_KV_W:3 * NSA_KV_W]
        vn = kvn_ref[0, :, 3 * NSA_KV_W:4 * NSA_KV_W]
        s_n = jnp.sum(qbd * kn, axis=-1, keepdims=True) + b0
        s_n = jnp.where(sel_ref[:, cur:cur + 1] > 0.5, s_n, NEG)
        alpha, pr = online(s_n)
        acc = acc_ref[...] * alpha + pr * vn
        o_s = _group_lanes(acc / l_ref[...])
        wk = win_ref[0, 0:NSA_KV_W, :].astype(BF16)
        wv = win_ref[0, NSA_KV_W:2 * NSA_KV_W, :].astype(BF16)
        n_win = win_ref.shape[2]
        s_w = jnp.dot(qbd16, wk, preferred_element_type=F32) + bwin_ref[...]
        wi = lax.broadcasted_iota(jnp.int32, s_w.shape, 1)
        s_wn = jnp.sum(qbd * kwn_ref[0, :, 0:NSA_KV_W], axis=-1, keepdims=True) + b0
        p_w, p_wn = _masked_softmax_parts([s_w, s_wn], [n_win - wi <= WINDOW, jnp.full(s_wn.shape, True)], axis=1)
        o_w = _group_lanes(lax.dot_general(p_w.astype(BF16), wv, NT_DIMS, preferred_element_type=F32)
                           + p_wn * kwn_ref[0, :, NSA_KV_W:2 * NSA_KV_W])
        gate = jax.nn.sigmoid(gate_ref[0])
        mixed = (oc_ref[...] * _silu(z_ref[0, 0]) * gate[:, 0:1] + o_s * _silu(z_ref[0, 1]) * gate[:, 1:2]
                 + o_w * _silu(z_ref[0, 2]) * gate[:, 2:3])
        o_ref[0] = mixed


def _nsa_layer_sample(xs_in, cache_t, win_t, page_table, prm, layer, n_layers):
    norm_w, w_in, pe, w1, w2, w_out, rel_bias = prm
    bsz, t, d = xs_in.shape
    assert t == 1
    n_pages = page_table.shape[1]
    past_len = n_pages * PAGE_SIZE
    n_win = win_t.shape[2]
    nh, dh = NSA_N_HEADS, NSA_HEAD_DIM
    x2 = xs_in.reshape(bsz, d)
    u = _norm_matmul(x2, norm_w, _nsa_in_weights(w_in).astype(BF16), _pick_tn(NSA_N_PAD, 1152))
    ids = (page_table * n_layers + layer).astype(jnp.int32)
    kc, vc = _compress(cache_t, lambda b, p, ids: (ids[b, p], 0, 0), bsz, n_pages, ids, pe, w1, w2)
    n_cmp_pad = kc.shape[1]
    n_cmp = n_cmp_pad - 1
    ns = past_len // SEL_BLOCK + 1
    n_sel_pad = _round_up(ns, LANES)
    ov = np.zeros((n_cmp_pad, n_sel_pad), np.float32)
    ov[:n_cmp, :ns] = _overlap_matrix(n_cmp, ns)
    n_pp = math.gcd(PAGES_PER_STEP, n_pages)
    n_steps = n_pages // n_pp
    n_tok = n_pp * PAGE_SIZE
    kpos = jnp.arange(past_len).reshape(n_steps, n_tok)
    bsel = jnp.moveaxis(_bias_of_dist(rel_bias, past_len - kpos), 0, 1)
    bcmp = _bias_of_dist(rel_bias, past_len - (jnp.arange(n_cmp_pad) * CMP_STRIDE + CMP_BLOCK - 1))
    bwin = _bias_of_dist(rel_bias, n_win - jnp.arange(n_win))
    b0 = jnp.broadcast_to(_bias_of_dist(rel_bias, jnp.zeros((1,), jnp.int32)), (nh, LANES))
    q3 = u[:, COL_Q:COL_Q + NSA_Q_W].reshape(bsz, nh, dh)
    kv_new = u[:, COL_KV:COL_KV + 4 * NSA_KV_W]
    kw_new = u[:, COL_KVW:COL_KVW + 2 * NSA_KV_W]
    gate = u[:, COL_GATE:COL_GATE + nh * N_BRANCH].reshape(bsz, nh, N_BRANCH)
    z4 = u[:, COL_Z:COL_Z + N_BRANCH * NSA_Q_W].reshape(bsz, N_BRANCH, nh, dh)
    per_b3 = lambda b, s, ids: (b, 0, 0)
    const2 = lambda b, s, ids: (0, 0)
    page_specs = [pl.BlockSpec((None, 2 * NSA_KV_W, PAGE_SIZE),
                               functools.partial(lambda b, s, ids, k: (ids[b, s * n_pp + k], 1, 0), k=k))
                  for k in range(n_pp)]
    grid_spec = pltpu.PrefetchScalarGridSpec(
        num_scalar_prefetch=1,
        grid=(bsz, n_steps),
        in_specs=page_specs + [
            pl.BlockSpec((1, nh, dh), per_b3),
            pl.BlockSpec((1, 1, 4 * NSA_KV_W), per_b3),
            pl.BlockSpec((1, 1, 2 * NSA_KV_W), per_b3),
            pl.BlockSpec((1, nh, N_BRANCH), per_b3),
            pl.BlockSpec((1, N_BRANCH, nh, dh), lambda b, s, ids: (b, 0, 0, 0)),
            pl.BlockSpec((1, n_cmp_pad, NSA_KV_W), per_b3),
            pl.BlockSpec((1, n_cmp_pad, NSA_KV_W), per_b3),
            pl.BlockSpec((1, 2 * NSA_KV_W, n_win), per_b3),
            pl.BlockSpec((n_cmp_pad, n_sel_pad), const2),
            pl.BlockSpec((1, nh, n_tok), lambda b, s, ids: (s, 0, 0)),
            pl.BlockSpec((nh, n_cmp_pad), const2),
            pl.BlockSpec((nh, n_win), const2),
            pl.BlockSpec((nh, LANES), const2),
        ],
        out_specs=pl.BlockSpec((1, nh, dh), per_b3),
        scratch_shapes=[
            pltpu.VMEM((nh, n_sel_pad), F32),
            pltpu.VMEM((nh, 1), F32),
            pltpu.VMEM((nh, 1), F32),
            pltpu.VMEM((nh, NSA_KV_W), F32),
            pltpu.VMEM((nh, dh), F32),
        ],
    )
    mixed = pl.pallas_call(
        functools.partial(_nsa_step_kernel, n_pp=n_pp, past_len=past_len, n_cmp=n_cmp, n_sel_pad=n_sel_pad),
        out_shape=jax.ShapeDtypeStruct((bsz, nh, dh), F32),
        grid_spec=grid_spec,
        compiler_params=_cparams(("parallel", "arbitrary")),
        name="nsa_step",
    )(ids, *([cache_t] * n_pp), q3, kv_new.reshape(bsz, 1, -1), kw_new.reshape(bsz, 1, -1), gate, z4, kc, vc, win_t,
      jnp.asarray(ov), bsel, bcmp, bwin, b0)
    out = _matmul_res(mixed.reshape(bsz, NSA_Q_W).astype(BF16), w_out.astype(BF16), x2).reshape(bsz, 1, d)
    return out, kv_new, kw_new


def kernel(x_prompt, x_sample, state_ssm, state_conv, cache_kv, cache_win, page_table, rel_bias, final_norm,
           ssd_norm, ssd_w_in, ssd_conv_w, ssd_conv_b, ssd_dt_bias, ssd_a_log, ssd_d, ssd_gnorm, ssd_w_out,
           nsa_norm, nsa_w_in, nsa_cmp_pe, nsa_cmp_w1, nsa_cmp_w2, nsa_w_out):
    depth = ssd_norm.shape[0] + nsa_norm.shape[0]
    g, dh = NSA_N_KV, NSA_HEAD_DIM
    xp, xs = x_prompt, x_sample
    bp, tp, d = xp.shape
    bs = xs.shape[0]
    n_phys, n_nsa = cache_kv.shape[:2]
    win_len_s = cache_win.shape[2]
    win_len_p = min(WINDOW, tp)
    cache_t = cache_kv.transpose(0, 1, 3, 4, 5, 2).reshape(n_phys * n_nsa, 4 * NSA_KV_W, PAGE_SIZE)
    cwin_t = cache_win.transpose(0, 1, 3, 4, 5, 2).reshape(n_nsa, bs, 2 * NSA_KV_W, win_len_s)
    ssm_p, conv_p, kv_p, win_p = [], [], [], []
    ssm_s, conv_s, kv_s, win_s = [], [], [], []
    for layer in range(depth):
        j = layer // 2
        if layer % 2 == 0:
            prm = (ssd_norm[j], ssd_w_in[j], ssd_conv_w[j], ssd_conv_b[j], ssd_dt_bias[j], ssd_a_log[j],
                   ssd_d[j], ssd_gnorm[j], ssd_w_out[j])
            xp, c, s = _ssd_layer_prompt(xp, prm)
            conv_p.append(c)
            ssm_p.append(s)
            xs, c, s = _ssd_layer_sample(xs, state_conv[j], state_ssm[j], prm)
            conv_s.append(c)
            ssm_s.append(s)
        else:
            prm = (nsa_norm[j], nsa_w_in[j], nsa_cmp_pe[j], nsa_cmp_w1[j], nsa_cmp_w2[j], nsa_w_out[j], rel_bias)
            xp, ut = _nsa_layer_prompt(xp, prm)
            kv_p.append(ut[:, COL_KV:COL_KV + 4 * NSA_KV_W, :])
            win_p.append(ut[:, COL_KVW:COL_KVW + 2 * NSA_KV_W, tp - win_len_p:])
            xs, kv_new, kw_new = _nsa_layer_sample(xs, cache_t, cwin_t[j], page_table, prm, j, n_nsa)
            kv_s.append(kv_new.reshape(bs, 1, 4, g, dh))
            win_s.append(jnp.concatenate([cwin_t[j][:, :, 1:], kw_new[:, :, None]], axis=2))
    y_prompt = _rmsnorm(xp.reshape(bp * tp, d), final_norm).reshape(bp, tp, d)
    y_sample = _rmsnorm(xs.reshape(bs, d), final_norm).reshape(bs, 1, d)
    kv_prompt = jnp.stack(kv_p, axis=1).reshape(bp, n_nsa, 4, g, dh, tp).transpose(0, 5, 1, 2, 3, 4)
    win_prompt = jnp.stack(win_p).reshape(n_nsa, bp, 2, g, dh, win_len_p).transpose(0, 1, 5, 2, 3, 4)
    win_sample = jnp.stack(win_s).reshape(n_nsa, bs, 2, g, dh, win_len_s).transpose(0, 1, 5, 2, 3, 4)
    return (y_prompt, y_sample, jnp.stack(ssm_p), jnp.stack(conv_p), kv_prompt, win_prompt,
            jnp.stack(ssm_s), jnp.stack(conv_s), jnp.stack(kv_s, axis=2), win_sample)
```

```python
import functools
import math

import jax
import jax.numpy as jnp
import numpy as np
from jax import lax
from jax.experimental import pallas as pl
from jax.experimental.pallas import tpu as pltpu

F32 = jnp.float32
BF16 = jnp.bfloat16
HIGHEST = lax.Precision.HIGHEST

NORM_EPS = 1e-6
SSD_HEAD_DIM = 64
SSD_N_GROUPS = 8
SSD_D_STATE = 128
SSD_CONV_W = 4
SSD_CHUNK = 128
NSA_N_HEADS = 16
NSA_HEAD_DIM = 64
NSA_N_KV = 4
NSA_R = NSA_N_HEADS // NSA_N_KV
N_BRANCH = 3
CMP_BLOCK = 32
CMP_STRIDE = 16
CMP_HIDDEN = 128
SEL_BLOCK = 64
SEL_TOPK = 16
WINDOW = 512
FORCE_SCORE = 1e4
REL_BUCKETS = 32
REL_MAX_DIST = 1024
PAGE_SIZE = 128

LANES = 128
SUBLANES = 8
VMEM_LIMIT = 56 * 1024 * 1024

NEG = -1e30
NT_DIMS = (((1,), (1,)), ((), ()))
TN_DIMS = (((0,), (0,)), ((), ()))


def _cparams(sem):
    return pltpu.CompilerParams(dimension_semantics=sem, vmem_limit_bytes=VMEM_LIMIT)


def _round_up(x, m):
    return (x + m - 1) // m * m


def _silu(x):
    return x * jax.nn.sigmoid(x)


def _pick_tm(m, cap, quantum=SUBLANES):
    if m <= cap:
        return m
    for tm in range(cap // quantum * quantum, 0, -quantum):
        if m % tm == 0:
            return tm
    raise ValueError(f"no row tile for {m}")


def _pick_tn(n, cap=1024):
    best = LANES
    for tn in range(LANES, cap + 1, LANES):
        if n % tn == 0:
            best = tn
    return best


def _norm_matmul_kernel(x_ref, nw_ref, w_ref, o_ref, xn_ref, *, transpose_out):
    @pl.when(pl.program_id(1) == 0)
    def _():
        x = x_ref[...]
        ms = jnp.mean(x * x, axis=-1, keepdims=True)
        xn = x * lax.rsqrt(ms + NORM_EPS) * nw_ref[...]
        if transpose_out:
            xn_ref[...] = xn.T.astype(BF16)
        else:
            xn_ref[...] = xn.astype(BF16)

    if transpose_out:
        o_ref[...] = jnp.dot(w_ref[...], xn_ref[...], preferred_element_type=F32).astype(o_ref.dtype)
    else:
        o_ref[...] = jnp.dot(xn_ref[...], w_ref[...], preferred_element_type=F32).astype(o_ref.dtype)


def _norm_matmul(x, nw, w, tn, out_dtype=F32):
    m, k = x.shape
    n = w.shape[1]
    tm = _pick_tm(m, 1024)
    assert n % tn == 0
    return pl.pallas_call(
        functools.partial(_norm_matmul_kernel, transpose_out=False),
        out_shape=jax.ShapeDtypeStruct((m, n), out_dtype),
        grid=(m // tm, n // tn),
        in_specs=[
            pl.BlockSpec((tm, k), lambda i, j: (i, 0)),
            pl.BlockSpec((1, k), lambda i, j: (0, 0)),
            pl.BlockSpec((k, tn), lambda i, j: (0, j)),
        ],
        out_specs=pl.BlockSpec((tm, tn), lambda i, j: (i, j)),
        scratch_shapes=[pltpu.VMEM((tm, k), BF16)],
        compiler_params=_cparams(("parallel", "arbitrary")),
        name="norm_matmul",
    )(x, nw.reshape(1, k), w)


def _norm_matmul_t(x, nw, wt, bsz, tn):
    m, k = x.shape
    n = wt.shape[0]
    t = m // bsz
    tm = _pick_tm(t, 1024, LANES)
    assert n % tn == 0
    tpb = t // tm
    return pl.pallas_call(
        functools.partial(_norm_matmul_kernel, transpose_out=True),
        out_shape=jax.ShapeDtypeStruct((bsz, n, t), F32),
        grid=(m // tm, n // tn),
        in_specs=[
            pl.BlockSpec((tm, k), lambda i, j: (i, 0)),
            pl.BlockSpec((1, k), lambda i, j: (0, 0)),
            pl.BlockSpec((tn, k), lambda i, j: (j, 0)),
        ],
        out_specs=pl.BlockSpec((None, tn, tm), lambda i, j: (i // tpb, j, i % tpb)),
        scratch_shapes=[pltpu.VMEM((k, tm), BF16)],
        compiler_params=_cparams(("parallel", "arbitrary")),
        name="norm_matmul_t",
    )(x, nw.reshape(1, k), wt)


def _matmul_res_kernel(y_ref, w_ref, x_ref, o_ref, *, y_transposed):
    dims = TN_DIMS if y_transposed else (((1,), (0,)), ((), ()))
    o_ref[...] = x_ref[...] + lax.dot_general(y_ref[...], w_ref[...], dims, preferred_element_type=F32)


def _matmul_res(y, w, x):
    m, k = y.shape
    n = w.shape[1]
    tm = min(m, 512)
    assert m % tm == 0
    return pl.pallas_call(
        functools.partial(_matmul_res_kernel, y_transposed=False),
        out_shape=jax.ShapeDtypeStruct((m, n), F32),
        grid=(m // tm,),
        in_specs=[
            pl.BlockSpec((tm, k), lambda i: (i, 0)),
            pl.BlockSpec((k, n), lambda i: (0, 0)),
            pl.BlockSpec((tm, n), lambda i: (i, 0)),
        ],
        out_specs=pl.BlockSpec((tm, n), lambda i: (i, 0)),
        compiler_params=_cparams(("parallel",)),
        name="matmul_res",
    )(y, w, x)


def _matmul_res_t(yt, w, x):
    bsz, k, t = yt.shape
    n = w.shape[1]
    tm = min(t, 512)
    assert t % tm == 0
    tpb = t // tm
    return pl.pallas_call(
        functools.partial(_matmul_res_kernel, y_transposed=True),
        out_shape=jax.ShapeDtypeStruct((bsz * t, n), F32),
        grid=(bsz * tpb,),
        in_specs=[
            pl.BlockSpec((None, k, tm), lambda i: (i // tpb, 0, i % tpb)),
            pl.BlockSpec((k, n), lambda i: (0, 0)),
            pl.BlockSpec((tm, n), lambda i: (i, 0)),
        ],
        out_specs=pl.BlockSpec((tm, n), lambda i: (i, 0)),
        compiler_params=_cparams(("parallel",)),
        name="matmul_res_t",
    )(yt, w, x)


def _rmsnorm_kernel(x_ref, nw_ref, o_ref):
    x = x_ref[...]
    ms = jnp.mean(x * x, axis=-1, keepdims=True)
    o_ref[...] = x * lax.rsqrt(ms + NORM_EPS) * nw_ref[...]


def _rmsnorm(x, nw):
    m, k = x.shape
    tm = min(m, 512)
    assert m % tm == 0
    return pl.pallas_call(
        _rmsnorm_kernel,
        out_shape=jax.ShapeDtypeStruct((m, k), F32),
        grid=(m // tm,),
        in_specs=[pl.BlockSpec((tm, k), lambda i: (i, 0)), pl.BlockSpec((1, k), lambda i: (0, 0))],
        out_specs=pl.BlockSpec((tm, k), lambda i: (i, 0)),
        compiler_params=_cparams(("parallel",)),
        name="final_rmsnorm",
    )(x, nw.reshape(1, k))


def _ssd_chunk_kernel(z_ref, x_ref, bc_ref, dt_ref, cprev_ref, sprev_ref, cw_ref, cb_ref, dtb_ref, alog_ref,
                      dskip_ref, gn_ref, y_ref, hlast_ref, xpad_ref, state_ref, *, d_inner, n_heads):
    lc = SSD_CHUNK
    g_n, n, p = SSD_N_GROUPS, SSD_D_STATE, SSD_HEAD_DIM
    r_n = n_heads // g_n
    c = pl.program_id(1)

    @pl.when(c == 0)
    def _():
        xpad_ref[0:SUBLANES, :] = cprev_ref[0]
        state_ref[...] = sprev_ref[0]

    @pl.when(c > 0)
    def _():
        xpad_ref[0:SUBLANES, :] = xpad_ref[lc:lc + SUBLANES, :]

    xpad_ref[SUBLANES:SUBLANES + lc, 0:d_inner] = x_ref[...]
    xpad_ref[SUBLANES:SUBLANES + lc, d_inner:] = bc_ref[...]

    conv = cb_ref[...]
    for k in range(SSD_CONV_W):
        off = SUBLANES - (SSD_CONV_W - 1) + k
        conv = conv + xpad_ref[off:off + lc, :] * cw_ref[k:k + 1, :]
    act = _silu(conv)
    xs = act[:, :d_inner]
    bm = act[:, d_inner:d_inner + g_n * n]
    cm = act[:, d_inner + g_n * n:]

    dt = jax.nn.softplus(dt_ref[...] + dtb_ref[...])
    a = -jnp.exp(alog_ref[...])
    dta = dt * a
    row = lax.broadcasted_iota(jnp.int32, (lc, lc), 0)
    col = lax.broadcasted_iota(jnp.int32, (lc, lc), 1)
    causal = row >= col
    tril = causal.astype(F32)
    a_cs = jnp.dot(tril, dta, precision=HIGHEST, preferred_element_type=F32)
    a_cs_t = a_cs.T
    a_end = a_cs[lc - 1:lc, :]
    to_end = jnp.exp(a_end - a_cs)
    e_cs = jnp.exp(a_cs)
    e_end = jnp.exp(a_end)

    assert 2 * p == LANES and r_n % 2 == 0
    lane_lo = lax.broadcasted_iota(jnp.int32, (1, LANES), 1) < p

    def pair_lanes(v, k):
        return jnp.where(lane_lo, v[:, 2 * k:2 * k + 1], v[:, 2 * k + 1:2 * k + 2])

    zs = _silu(z_ref[...])
    gw = r_n * p
    ys = []
    for g in range(g_n):
        bg16 = bm[:, g * n:(g + 1) * n].astype(BF16)
        cg16 = cm[:, g * n:(g + 1) * n].astype(BF16)
        cb = lax.dot_general(cg16, bg16, NT_DIMS, preferred_element_type=F32)
        pairs = range(g * r_n // 2, (g + 1) * r_n // 2)
        gl = slice(g * gw, (g + 1) * gw)
        xs_g = xs[:, gl]
        xdt_g = xs_g * jnp.concatenate([pair_lanes(dt, k) for k in pairs], axis=1)
        y_diag = []
        for i, k in enumerate(pairs):
            xdt_k = xdt_g[:, i * LANES:(i + 1) * LANES]
            acc = None
            for j, h in enumerate((2 * k, 2 * k + 1)):
                seg = a_cs[:, h:h + 1] - a_cs_t[h:h + 1, :]
                decay = jnp.exp(jnp.where(causal, seg, -jnp.inf))
                m_h = (cb * decay).astype(BF16)
                x_h = jnp.where(lane_lo if j == 0 else jnp.logical_not(lane_lo), xdt_k, 0.0).astype(BF16)
                d = jnp.dot(m_h, x_h, preferred_element_type=F32)
                acc = d if acc is None else acc + d
            y_diag.append(acc)
        y_diag = jnp.concatenate(y_diag, axis=1)
        sprev = state_ref[:, gl]
        y_off = (jnp.dot(cg16, sprev.astype(BF16), preferred_element_type=F32)
                 * jnp.concatenate([pair_lanes(e_cs, k) for k in pairs], axis=1))
        xw = (xdt_g * jnp.concatenate([pair_lanes(to_end, k) for k in pairs], axis=1)).astype(BF16)
        st = lax.dot_general(bg16, xw, TN_DIMS, preferred_element_type=F32)
        state_ref[:, gl] = sprev * jnp.concatenate([pair_lanes(e_end, k) for k in pairs], axis=1) + st
        yg = (y_diag + y_off + xs_g * dskip_ref[:, gl]) * zs[:, gl]
        ms = jnp.mean(yg * yg, axis=-1, keepdims=True)
        ys.append(yg * lax.rsqrt(ms + NORM_EPS))
    y = jnp.concatenate(ys, axis=1) * gn_ref[...]
    y_ref[...] = y.astype(y_ref.dtype)

    @pl.when(c == pl.num_programs(1) - 1)
    def _():
        hlast_ref[0] = state_ref[...]


def _pad_lanes(v, width=LANES):
    return jnp.pad(v.astype(F32), (0, width - v.shape[0])).reshape(1, width)


def _ssd_prompt_core(u, bsz, t, conv_prev8, ssm_prev, conv_w, conv_b, dt_bias, a_log, d_skip, gnorm):
    n_heads = dt_bias.shape[0]
    d_inner = n_heads * SSD_HEAD_DIM
    conv_dim = conv_w.shape[1]
    assert conv_dim == 2 * d_inner and t % SSD_CHUNK == 0
    lc = SSD_CHUNK
    nc = t // lc
    dt_blk = (d_inner + conv_dim) // LANES
    kern = functools.partial(_ssd_chunk_kernel, d_inner=d_inner, n_heads=n_heads)
    const = lambda b, c: (0, 0)
    state_t = ssm_prev.transpose(0, 3, 1, 2).reshape(bsz, SSD_D_STATE, d_inner)
    y, hlast_t = pl.pallas_call(
        kern,
        out_shape=(jax.ShapeDtypeStruct((bsz * t, d_inner), BF16),
                   jax.ShapeDtypeStruct((bsz, SSD_D_STATE, d_inner), F32)),
        grid=(bsz, nc),
        in_specs=[
            pl.BlockSpec((lc, d_inner), lambda b, c: (b * nc + c, 0)),
            pl.BlockSpec((lc, d_inner), lambda b, c: (b * nc + c, 1)),
            pl.BlockSpec((lc, d_inner), lambda b, c: (b * nc + c, 2)),
            pl.BlockSpec((lc, LANES), lambda b, c: (b * nc + c, dt_blk)),
            pl.BlockSpec((1, SUBLANES, conv_dim), lambda b, c: (b, 0, 0)),
            pl.BlockSpec((1, SSD_D_STATE, d_inner), lambda b, c: (b, 0, 0)),
            pl.BlockSpec((SSD_CONV_W, conv_dim), const),
            pl.BlockSpec((1, conv_dim), const),
            pl.BlockSpec((1, LANES), const),
            pl.BlockSpec((1, LANES), const),
            pl.BlockSpec((1, d_inner), const),
            pl.BlockSpec((1, d_inner), const),
        ],
        out_specs=(pl.BlockSpec((lc, d_inner), lambda b, c: (b * nc + c, 0)),
                   pl.BlockSpec((1, SSD_D_STATE, d_inner), lambda b, c: (b, 0, 0))),
        scratch_shapes=[pltpu.VMEM((lc + SUBLANES, conv_dim), F32),
                        pltpu.VMEM((SSD_D_STATE, d_inner), F32)],
        compiler_params=_cparams(("parallel", "arbitrary")),
        name="ssd_chunk",
    )(u, u, u, u, conv_prev8, state_t, conv_w, conv_b.reshape(1, conv_dim), _pad_lanes(dt_bias),
      _pad_lanes(a_log), jnp.repeat(d_skip.astype(F32), SSD_HEAD_DIM).reshape(1, d_inner), gnorm.reshape(1, d_inner))
    hlast = hlast_t.reshape(bsz, SSD_D_STATE, n_heads, SSD_HEAD_DIM).transpose(0, 2, 3, 1)
    return y, hlast


def _ssd_in_weights(w_in, d_inner, conv_dim):
    main = d_inner + conv_dim
    n_dt = w_in.shape[1] - main
    w = jnp.concatenate([w_in[:, :main], jnp.pad(w_in[:, main:], ((0, 0), (0, LANES - n_dt)))], axis=1)
    return w.astype(BF16)


def _ssd_layer_prompt(xp, prm):
    norm_w, w_in, conv_w, conv_b, dt_bias, a_log, d_skip, gnorm, w_out = prm
    bsz, t, d = xp.shape
    assert t >= SSD_CONV_W - 1
    n_heads = dt_bias.shape[0]
    d_inner = n_heads * SSD_HEAD_DIM
    conv_dim = conv_w.shape[1]
    w = _ssd_in_weights(w_in, d_inner, conv_dim)
    x2 = xp.reshape(bsz * t, d)
    u = _norm_matmul(x2, norm_w, w, _pick_tn(w.shape[1]))
    conv_prev8 = jnp.zeros((bsz, SUBLANES, conv_dim), F32)
    ssm_prev = jnp.zeros((bsz, n_heads, SSD_HEAD_DIM, SSD_D_STATE), F32)
    y, hlast = _ssd_prompt_core(u, bsz, t, conv_prev8, ssm_prev, conv_w, conv_b, dt_bias, a_log, d_skip, gnorm)
    out = _matmul_res(y, w_out.astype(BF16), x2).reshape(bsz, t, d)
    u3 = u.reshape(bsz, t, -1)
    new_conv = u3[:, t - (SSD_CONV_W - 1):, d_inner:d_inner + conv_dim]
    return out, new_conv, hlast


def _ssd_step_pre_kernel(u_ref, cst_ref, cw_ref, cb_ref, dtb_ref, alog_ref,
                         xs_ref, xdt_ref, dec_ref, bm_ref, cm_ref, ncst_ref, *, d_inner, n_heads):
    conv_dim = cw_ref.shape[1]
    gn = SSD_N_GROUPS * SSD_D_STATE
    xbc = u_ref[:, d_inner:d_inner + conv_dim]
    conv = cb_ref[...]
    for k in range(SSD_CONV_W - 1):
        conv = conv + cst_ref[k] * cw_ref[k:k + 1, :]
    conv = conv + xbc * cw_ref[SSD_CONV_W - 1:SSD_CONV_W, :]
    for k in range(SSD_CONV_W - 2):
        ncst_ref[k] = cst_ref[k + 1]
    ncst_ref[SSD_CONV_W - 2] = xbc
    act = _silu(conv)
    xs = act[:, :d_inner]
    dt = jax.nn.softplus(u_ref[:, d_inner + conv_dim:d_inner + conv_dim + LANES] + dtb_ref[...])
    dec = jnp.exp(dt * (-jnp.exp(alog_ref[...])))
    hrow = lax.broadcasted_iota(jnp.int32, (LANES, d_inner), 0)
    hcol = lax.broadcasted_iota(jnp.int32, (LANES, d_inner), 1) // SSD_HEAD_DIM
    expand = (hrow == hcol).astype(F32)
    dt_e = jnp.dot(dt, expand, precision=HIGHEST, preferred_element_type=F32)
    dec_e = jnp.dot(dec, expand, precision=HIGHEST, preferred_element_type=F32)
    xs_ref[...] = xs
    xdt_ref[...] = xs * dt_e
    dec_ref[...] = dec_e
    bm_ref[...] = act[:, d_inner:d_inner + gn]
    cm_ref[...] = act[:, d_inner + gn:]


def _ssd_step_state_kernel(s_ref, xdt_ref, dec_ref, b_ref, c_ref, snew_ref, y_ref, *, rows_per_group):
    for g in range(SSD_N_GROUPS):
        sl = slice(g * rows_per_group, (g + 1) * rows_per_group)
        s = s_ref[0, sl, :]
        bg = b_ref[0, g:g + 1, :]
        cg = c_ref[0, g:g + 1, :]
        xdt = xdt_ref[0, sl, :]
        dec = dec_ref[0, sl, :]
        cb = jnp.sum(bg * cg, axis=1, keepdims=True)
        y_off = jnp.sum(s * cg, axis=1, keepdims=True)
        snew_ref[0, sl, :] = s * dec + xdt * bg
        y_ref[0, sl, :] = cb * xdt + dec * y_off


def _ssd_step_post_kernel(y_ref, xs_ref, u_ref, dsk_ref, gn_ref, o_ref, *, d_inner, group_width):
    y = (y_ref[...] + xs_ref[...] * dsk_ref[...]) * _silu(u_ref[:, :d_inner])
    outs = []
    for g in range(SSD_N_GROUPS):
        yg = y[:, g * group_width:(g + 1) * group_width]
        ms = jnp.mean(yg * yg, axis=-1, keepdims=True)
        outs.append(yg * lax.rsqrt(ms + NORM_EPS))
    o_ref[...] = (jnp.concatenate(outs, axis=1) * gn_ref[...]).astype(o_ref.dtype)


def _ssd_layer_sample(xs_in, conv_state, ssm_state, prm):
    norm_w, w_in, conv_w, conv_b, dt_bias, a_log, d_skip, gnorm, w_out = prm
    bsz, t, d = xs_in.shape
    assert t == 1
    n_heads = dt_bias.shape[0]
    d_inner = n_heads * SSD_HEAD_DIM
    conv_dim = conv_w.shape[1]
    gn = SSD_N_GROUPS * SSD_D_STATE
    w = _ssd_in_weights(w_in, d_inner, conv_dim)
    x2 = xs_in.reshape(bsz, d)
    u = _norm_matmul(x2, norm_w, w, _pick_tn(w.shape[1]))
    cst = conv_state.transpose(1, 0, 2)
    pre = functools.partial(_ssd_step_pre_kernel, d_inner=d_inner, n_heads=n_heads)
    xs, xdt, dec, bm, cm, ncst = pl.pallas_call(
        pre,
        out_shape=(jax.ShapeDtypeStruct((bsz, d_inner), F32), jax.ShapeDtypeStruct((bsz, d_inner), F32),
                   jax.ShapeDtypeStruct((bsz, d_inner), F32), jax.ShapeDtypeStruct((bsz, gn), F32),
                   jax.ShapeDtypeStruct((bsz, gn), F32), jax.ShapeDtypeStruct(cst.shape, F32)),
        compiler_params=pltpu.CompilerParams(vmem_limit_bytes=VMEM_LIMIT),
        name="ssd_step_pre",
    )(u, cst, conv_w, conv_b.reshape(1, conv_dim), _pad_lanes(dt_bias), _pad_lanes(a_log))
    rows = n_heads * SSD_HEAD_DIM
    rpg = rows // SSD_N_GROUPS
    st = functools.partial(_ssd_step_state_kernel, rows_per_group=rpg)
    snew, ycol = pl.pallas_call(
        st,
        out_shape=(jax.ShapeDtypeStruct((bsz, rows, SSD_D_STATE), F32), jax.ShapeDtypeStruct((bsz, rows, 1), F32)),
        grid=(bsz,),
        in_specs=[
            pl.BlockSpec((1, rows, SSD_D_STATE), lambda b: (b, 0, 0)),
            pl.BlockSpec((1, rows, 1), lambda b: (b, 0, 0)),
            pl.BlockSpec((1, rows, 1), lambda b: (b, 0, 0)),
            pl.BlockSpec((1, SSD_N_GROUPS, SSD_D_STATE), lambda b: (b, 0, 0)),
            pl.BlockSpec((1, SSD_N_GROUPS, SSD_D_STATE), lambda b: (b, 0, 0)),
        ],
        out_specs=(pl.BlockSpec((1, rows, SSD_D_STATE), lambda b: (b, 0, 0)),
                   pl.BlockSpec((1, rows, 1), lambda b: (b, 0, 0))),
        compiler_params=_cparams(("parallel",)),
        name="ssd_step_state",
    )(ssm_state.reshape(bsz, rows, SSD_D_STATE), xdt.reshape(bsz, rows, 1), dec.reshape(bsz, rows, 1),
      bm.reshape(bsz, SSD_N_GROUPS, SSD_D_STATE), cm.reshape(bsz, SSD_N_GROUPS, SSD_D_STATE))
    post = functools.partial(_ssd_step_post_kernel, d_inner=d_inner, group_width=d_inner // SSD_N_GROUPS)
    y = pl.pallas_call(
        post,
        out_shape=jax.ShapeDtypeStruct((bsz, d_inner), BF16),
        compiler_params=pltpu.CompilerParams(vmem_limit_bytes=VMEM_LIMIT),
        name="ssd_step_post",
    )(ycol.reshape(bsz, rows), xs, u, jnp.repeat(d_skip.astype(F32), SSD_HEAD_DIM).reshape(1, d_inner),
      gnorm.reshape(1, d_inner))
    out = _matmul_res(y, w_out.astype(BF16), x2).reshape(bsz, 1, d)
    return out, ncst.transpose(1, 0, 2), snew.reshape(ssm_state.shape)


NSA_Q_W = NSA_N_HEADS * NSA_HEAD_DIM
NSA_KV_W = NSA_N_KV * NSA_HEAD_DIM
COL_Q = 0
COL_KV = NSA_Q_W
COL_Z = COL_KV + 4 * NSA_KV_W
COL_KVW = COL_Z + N_BRANCH * NSA_Q_W
COL_GATE = COL_KVW + 2 * NSA_KV_W
NSA_N_PAD = COL_GATE + LANES
TQ = 128
KT = 128
N_REL_TILES = -(-(REL_MAX_DIST + TQ - 1) // KT)
CMP_NEAR = 128
CMP_FRONT = CMP_NEAR - TQ // CMP_STRIDE
PAGES_PER_STEP = 16


def _nsa_in_weights(w_in):
    cuts = np.cumsum([NSA_Q_W, 4 * NSA_KV_W, 2 * NSA_KV_W, NSA_N_HEADS * N_BRANCH]).tolist()
    q, kv, kvw, gate, z = (w_in[:, a:b] for a, b in zip([0] + cuts, cuts + [w_in.shape[1]]))
    gate = jnp.pad(gate, ((0, 0), (0, LANES - gate.shape[1])))
    return jnp.concatenate([q, kv, z, kvw, gate], axis=1)


def _rel_bucket(dist):
    n = jnp.maximum(dist, 0)
    exact = REL_BUCKETS // 2
    logv = jnp.log(jnp.maximum(n, 1).astype(F32) / exact) / math.log(REL_MAX_DIST / exact)
    large = jnp.minimum(exact + (logv * (REL_BUCKETS - exact)).astype(jnp.int32), REL_BUCKETS - 1)
    return jnp.where(n < exact, n, large)


def _bias_of_dist(rel_bias, dist):
    return jnp.moveaxis(rel_bias.astype(F32)[_rel_bucket(dist)], -1, 0)


def _lanes_by_group(t):
    h, n, q = t.shape
    return t.reshape(NSA_N_KV, NSA_R, n, q).transpose(0, 2, 1, 3).reshape(NSA_N_KV, n, NSA_R * q)


def _overlap_matrix(nc, ns):
    c_start = np.arange(nc) * CMP_STRIDE
    c_end = c_start + CMP_BLOCK - 1
    s_start = np.arange(ns) * SEL_BLOCK
    ov = np.clip(np.minimum(c_end[:, None], s_start[None, :] + SEL_BLOCK - 1)
                 - np.maximum(c_start[:, None], s_start[None, :]) + 1, 0, None).astype(np.float32) / CMP_STRIDE
    return ov


def _masked_softmax_parts(parts, valids, axis):
    m = None
    for s, v in zip(parts, valids):
        pm = jnp.max(jnp.where(v, s, -jnp.inf), axis=axis, keepdims=True)
        m = pm if m is None else jnp.maximum(m, pm)
    m = jnp.where(m > -jnp.inf, m, 0.0)
    es = [jnp.where(v, jnp.exp(s - m), 0.0) for s, v in zip(parts, valids)]
    den = None
    for e in es:
        d = jnp.sum(e, axis=axis, keepdims=True)
        den = d if den is None else den + d
    inv = 1.0 / jnp.where(den > 0, den, 1.0)
    return [e * inv for e in es]


def _compress_kernel(*refs, n_pp):
    page_refs = refs[1:1 + n_pp]
    pe_ref, w1_ref, w2_ref, kc_ref, vc_ref, x_ref = refs[1 + n_pp:]
    step = pl.program_id(1)
    hb = PAGE_SIZE // CMP_STRIDE
    dh = NSA_HEAD_DIM
    for k in range(n_pp):
        pt = page_refs[k][...].T.reshape(hb, CMP_STRIDE, 2 * NSA_KV_W)
        row0 = pl.multiple_of((step * n_pp + k) * hb, hb)
        for l in range(CMP_STRIDE):
            row = pt[:, l, :]
            for kv in range(2):
                for g in range(NSA_N_KV):
                    c0 = kv * NSA_KV_W + g * dh
                    x_ref[kv, g, pl.ds(row0, hb), l * dh:(l + 1) * dh] = row[:, c0:c0 + dh]

    @pl.when(step == pl.num_programs(1) - 1)
    def _():
        half = CMP_STRIDE * dh
        for kv, out_ref in ((0, kc_ref), (1, vc_ref)):
            w_top = w1_ref[kv, 0:half, :]
            w_bot = w1_ref[kv, half:2 * half, :]
            w2 = w2_ref[kv]
            for g in range(NSA_N_KV):
                x = x_ref[kv, g]
                top = jnp.dot((x + pe_ref[kv, 0:1, :]).astype(BF16), w_top, preferred_element_type=F32)
                bot = jnp.dot((x + pe_ref[kv, 1:2, :]).astype(BF16), w_bot, preferred_element_type=F32)
                bot = jnp.concatenate([bot[1:], jnp.zeros((1, bot.shape[1]), F32)], axis=0)
                hid = _silu(top + bot).astype(BF16)
                out_ref[0, :, g * dh:(g + 1) * dh] = jnp.dot(hid, w2, preferred_element_type=F32)


def _compress(src, page_index, bsz, n_pages, ids, pe, w1, w2):
    n_pp = math.gcd(PAGES_PER_STEP, n_pages)
    hb = PAGE_SIZE // CMP_STRIDE
    nh = n_pages * hb
    half = CMP_STRIDE * NSA_HEAD_DIM
    blk = (None,) * (src.ndim - 2) + (2 * NSA_KV_W, PAGE_SIZE)
    page_specs = [pl.BlockSpec(blk, functools.partial(lambda b, s, ids, k: page_index(b, s * n_pp + k, ids), k=k))
                  for k in range(n_pp)]
    grid_spec = pltpu.PrefetchScalarGridSpec(
        num_scalar_prefetch=1,
        grid=(bsz, n_pages // n_pp),
        in_specs=page_specs + [
            pl.BlockSpec((2, 2, half), lambda b, s, ids: (0, 0, 0)),
            pl.BlockSpec((2, 2 * half, CMP_HIDDEN), lambda b, s, ids: (0, 0, 0)),
            pl.BlockSpec((2, CMP_HIDDEN, NSA_HEAD_DIM), lambda b, s, ids: (0, 0, 0)),
        ],
        out_specs=(pl.BlockSpec((1, nh, NSA_KV_W), lambda b, s, ids: (b, 0, 0)),
                   pl.BlockSpec((1, nh, NSA_KV_W), lambda b, s, ids: (b, 0, 0))),
        scratch_shapes=[pltpu.VMEM((2, NSA_N_KV, nh, half), F32)],
    )
    return pl.pallas_call(
        functools.partial(_compress_kernel, n_pp=n_pp),
        out_shape=(jax.ShapeDtypeStruct((bsz, nh, NSA_KV_W), F32), jax.ShapeDtypeStruct((bsz, nh, NSA_KV_W), F32)),
        grid_spec=grid_spec,
        compiler_params=_cparams(("parallel", "arbitrary")),
        name="nsa_compress",
    )(ids, *([src] * n_pp), pe.reshape(2, 2, half).astype(F32), w1.astype(BF16), w2.astype(BF16))


def _nsa_attn_kernel(qt_ref, kv_ref, kc_ref, vc_ref, ov_ref, trel_ref, tcn_ref, gate_ref, z0_ref, z1_ref, z2_ref,
                     o_ref, sel_ref, imp_ref, m_ref, l_ref, acc_ref, ob_ref, *, n_sel, n_cmp_rows):
    t = pl.program_id(1)
    dh = NSA_HEAD_DIM
    rq = NSA_R * TQ
    scale = dh ** -0.5
    far_idx = N_REL_TILES
    ns8 = sel_ref.shape[1]
    near0 = pl.multiple_of(t * (TQ // CMP_STRIDE), SUBLANES)
    far_rows = slice(CMP_FRONT, CMP_FRONT + n_cmp_rows)

    def cmp_branch(g, qt, with_far):
        gs = slice(g * dh, (g + 1) * dh)
        kc_near = kc_ref[0, pl.ds(near0, CMP_NEAR), gs].astype(BF16)
        vc_near = vc_ref[0, pl.ds(near0, CMP_NEAR), gs].astype(BF16)
        cn = lax.broadcasted_iota(jnp.int32, (CMP_NEAR, rq), 0)
        qn = lax.broadcasted_iota(jnp.int32, (CMP_NEAR, rq), 1) % TQ
        dist_n = qn - CMP_STRIDE * cn + (CMP_STRIDE * CMP_FRONT - (CMP_BLOCK - 1))
        parts = [jnp.dot(kc_near, qt, preferred_element_type=F32) + tcn_ref[g]]
        valids = [(dist_n >= 0) & (cn + near0 >= CMP_FRONT)]
        if with_far:
            kc_far = kc_ref[0, far_rows, gs].astype(BF16)
            parts.append(jnp.dot(kc_far, qt, preferred_element_type=F32)
                         + trel_ref[g, far_idx, 0:1, :].astype(F32))
            valids.append(lax.broadcasted_iota(jnp.int32, (n_cmp_rows, rq), 0) + CMP_FRONT < near0)
        ps = _masked_softmax_parts(parts, valids, axis=0)
        o_c = lax.dot_general(vc_near, ps[0].astype(BF16), TN_DIMS, preferred_element_type=F32)
        pg = sum(ps[0][:, r * TQ:(r + 1) * TQ] for r in range(NSA_R))
        imp = lax.dot_general(ov_ref[pl.ds(near0, CMP_NEAR), :], pg, TN_DIMS, precision=HIGHEST,
                              preferred_element_type=F32)
        if with_far:
            vc_far = vc_ref[0, far_rows, gs].astype(BF16)
            o_c = o_c + lax.dot_general(vc_far, ps[1].astype(BF16), TN_DIMS, preferred_element_type=F32)
            pg = sum(ps[1][:, r * TQ:(r + 1) * TQ] for r in range(NSA_R))
            imp = imp + lax.dot_general(ov_ref[far_rows, :], pg, TN_DIMS, precision=HIGHEST,
                                        preferred_element_type=F32)
        ob_ref[0, g] = o_c
        imp_ref[...] = imp

    qts = []
    sid = lax.broadcasted_iota(jnp.int32, (ns8, TQ), 0)
    qpos = t * TQ + lax.broadcasted_iota(jnp.int32, (ns8, TQ), 1)
    cur = qpos // SEL_BLOCK
    forced = (sid == 0) | (sid == cur) | (sid == cur - 1)
    reachable = sid * SEL_BLOCK <= qpos
    sub = lax.broadcasted_iota(jnp.int32, (SUBLANES, TQ), 0)
    for g in range(NSA_N_KV):
        qt = jnp.concatenate([qt_ref[(g * NSA_R + r) * dh:(g * NSA_R + r + 1) * dh, :] for r in range(NSA_R)], axis=1)
        qt = (qt * scale).astype(BF16)
        qts.append(qt)
        has_far = near0 > CMP_FRONT

        @pl.when(has_far)
        def _():
            cmp_branch(g, qt, True)

        @pl.when(jnp.logical_not(has_far))
        def _():
            cmp_branch(g, qt, False)

        imp = jnp.where(forced, FORCE_SCORE, imp_ref[0:ns8, :])
        imp = jnp.where(reachable, imp, -FORCE_SCORE)
        blocks = [imp[SUBLANES * kb:SUBLANES * (kb + 1)] for kb in range(ns8 // SUBLANES)]
        cnts = [jnp.zeros((SUBLANES, TQ), F32) for _ in blocks]
        for sp in range(n_sel):
            other = imp[sp:sp + 1, :]
            for kb, blk in enumerate(blocks):
                if kb > sp // SUBLANES:
                    ahead = other >= blk
                elif kb < sp // SUBLANES:
                    ahead = other > blk
                else:
                    ahead = (other > blk) | ((other == blk) & (sub > sp % SUBLANES))
                cnts[kb] = cnts[kb] + jnp.where(ahead, 1.0, 0.0)
        cnt = jnp.concatenate(cnts, axis=0)
        sel_ref[g] = jnp.where((cnt < SEL_TOPK) & (imp > -0.5 * FORCE_SCORE), 1.0, 0.0).astype(BF16)

    def reset():
        m_ref[...] = jnp.full(m_ref.shape, NEG, F32)
        l_ref[...] = jnp.zeros(l_ref.shape, F32)
        acc_ref[...] = jnp.zeros(acc_ref.shape, F32)

    def attend(g, k0, n_keys, k_col, v_col, bias, ok):
        k_t = kv_ref[pl.ds(k0, n_keys), k_col + g * dh:k_col + (g + 1) * dh]
        v_t = kv_ref[pl.ds(k0, n_keys), v_col + g * dh:v_col + (g + 1) * dh]
        s = jnp.dot(k_t, qts[g], preferred_element_type=F32) + bias
        s = jnp.concatenate([jnp.where(ok, s[:, r * TQ:(r + 1) * TQ], NEG) for r in range(NSA_R)], axis=1)
        m_old = m_ref[g]
        m_new = jnp.maximum(m_old, jnp.max(s, axis=0, keepdims=True))
        alpha = jnp.exp(m_old - m_new)
        p = jnp.exp(s - m_new)
        l_ref[g] = l_ref[g] * alpha + jnp.sum(p, axis=0, keepdims=True)
        acc_ref[g] = acc_ref[g] * alpha + lax.dot_general(v_t, p.astype(BF16), TN_DIMS, preferred_element_type=F32)
        m_ref[g] = m_new

    reset()
    kt2 = 2 * KT
    kj2 = lax.broadcasted_iota(jnp.int32, (kt2, TQ), 0)
    qi2 = lax.broadcasted_iota(jnp.int32, (kt2, TQ), 1)
    ej = lax.broadcasted_iota(jnp.int32, (kt2, ns8), 0) // SEL_BLOCK
    eb = lax.broadcasted_iota(jnp.int32, (kt2, ns8), 1)

    def sel_body(kt, carry):
        k0 = pl.multiple_of(kt * kt2, kt2)
        delta = t - 2 * kt
        i_lo = jnp.clip(delta, 0, far_idx)
        i_hi = jnp.clip(delta - 1, 0, far_idx)
        expand = (eb == kt * (kt2 // SEL_BLOCK) + ej).astype(BF16)
        causal_ok = t * TQ + qi2 >= k0 + kj2
        for g in range(NSA_N_KV):
            chosen = jnp.dot(expand, sel_ref[g], preferred_element_type=F32)
            bias = jnp.concatenate([trel_ref[g, i_lo], trel_ref[g, i_hi]], axis=0).astype(F32)
            attend(g, WINDOW + k0, kt2, 0, NSA_KV_W, bias, (chosen > 0.5) & causal_ok)
        return carry

    lax.fori_loop(0, t // 2 + 1, sel_body, 0)
    for g in range(NSA_N_KV):
        ob_ref[1, g] = acc_ref[g] / l_ref[g]

    reset()
    row0 = pl.multiple_of(t * TQ, TQ)
    for deltas in ((0,), (2, 1), (4, 3)):
        n_keys = KT * len(deltas)
        kjw = lax.broadcasted_iota(jnp.int32, (n_keys, TQ), 0)
        qiw = lax.broadcasted_iota(jnp.int32, (n_keys, TQ), 1)
        dist = deltas[0] * KT + qiw - kjw
        ok = (dist >= 0) & (dist <= WINDOW) & (row0 + kjw >= deltas[0] * KT)
        for g in range(NSA_N_KV):
            bias = jnp.concatenate([trel_ref[g, d] for d in deltas], axis=0).astype(F32)
            attend(g, row0 + (WINDOW - deltas[0] * KT), n_keys, 2 * NSA_KV_W, 3 * NSA_KV_W, bias, ok)
    for g in range(NSA_N_KV):
        ob_ref[2, g] = acc_ref[g] / l_ref[g]

    gate = jax.nn.sigmoid(gate_ref[...])
    z_refs = (z0_ref, z1_ref, z2_ref)
    for h in range(NSA_N_HEADS):
        g, r = divmod(h, NSA_R)
        mixed = None
        for c in range(N_BRANCH):
            o_hc = ob_ref[c, g, :, r * TQ:(r + 1) * TQ]
            term = o_hc * _silu(z_refs[c][h * dh:(h + 1) * dh, :]) * gate[h * N_BRANCH + c:h * N_BRANCH + c + 1, :]
            mixed = term if mixed is None else mixed + term
        o_ref[h * dh:(h + 1) * dh, :] = mixed.astype(o_ref.dtype)


def _nsa_prompt_tables(rel_bias):
    kj = jnp.arange(KT)[:, None]
    qi = jnp.arange(TQ)[None, :]
    tiles = [_lanes_by_group(_bias_of_dist(rel_bias, delta * KT + qi - kj)) for delta in range(N_REL_TILES)]
    far = jnp.broadcast_to(rel_bias.astype(F32)[REL_BUCKETS - 1].reshape(NSA_N_HEADS, 1, 1), (NSA_N_HEADS, KT, TQ))
    tiles.append(_lanes_by_group(far))
    trel = jnp.stack(tiles, axis=1).astype(BF16)
    cn = jnp.arange(CMP_NEAR)[:, None]
    dist_n = qi - CMP_STRIDE * cn + (CMP_STRIDE * CMP_FRONT - (CMP_BLOCK - 1))
    tcn = _lanes_by_group(_bias_of_dist(rel_bias, dist_n))
    return trel, tcn


def _nsa_layer_prompt(xp, prm):
    norm_w, w_in, pe, w1, w2, w_out, rel_bias = prm
    bsz, t, d = xp.shape
    assert t % (2 * KT) == 0 and t % PAGE_SIZE == 0 and t >= WINDOW
    n_tiles = t // TQ
    ns = t // SEL_BLOCK
    assert ns <= LANES
    x2 = xp.reshape(bsz * t, d)
    w = _nsa_in_weights(w_in)
    ut = _norm_matmul_t(x2, norm_w, w.T.astype(BF16), bsz, _pick_tn(NSA_N_PAD, 1152))
    w_kv = jnp.concatenate([w[:, COL_KV + 2 * NSA_KV_W:COL_KV + 4 * NSA_KV_W], w[:, COL_KVW:COL_KVW + 2 * NSA_KV_W]],
                           axis=1).astype(BF16)
    kvn = _norm_matmul(x2, norm_w, w_kv, 4 * NSA_KV_W, out_dtype=BF16)
    t_pad = WINDOW + t
    kvn = jnp.pad(kvn.reshape(bsz, t, 4 * NSA_KV_W), ((0, 0), (WINDOW, 0), (0, 0))).reshape(bsz * t_pad, 4 * NSA_KV_W)
    n_pages = t // PAGE_SIZE
    cmp_blk = COL_KV // (2 * NSA_KV_W)
    kc, vc = _compress(ut, lambda b, p, ids: (b, cmp_blk, p), bsz, n_pages, jnp.zeros((1, 1), jnp.int32), pe, w1, w2)
    nh = kc.shape[1]
    n_far = _round_up(CMP_FRONT + nh, LANES)
    padc = ((0, 0), (CMP_FRONT, n_far - CMP_FRONT - nh), (0, 0))
    kcp = jnp.pad(kc, padc)
    vcp = jnp.pad(vc, padc)
    ov = np.zeros((n_far, LANES), np.float32)
    ov[CMP_FRONT:CMP_FRONT + nh - 1, :ns] = _overlap_matrix(nh - 1, ns)
    trel, tcn = _nsa_prompt_tables(rel_bias)
    rq = NSA_R * TQ
    const2 = lambda b, i: (0, 0)
    const3 = lambda b, i: (0, 0, 0)
    const4 = lambda b, i: (0, 0, 0, 0)
    zb = COL_Z // NSA_Q_W
    mixed_t = pl.pallas_call(
        functools.partial(_nsa_attn_kernel, n_sel=ns, n_cmp_rows=nh),
        out_shape=jax.ShapeDtypeStruct((bsz, NSA_Q_W, t), BF16),
        grid=(bsz, n_tiles),
        in_specs=[
            pl.BlockSpec((None, NSA_Q_W, TQ), lambda b, i: (b, 0, i)),
            pl.BlockSpec((t_pad, 4 * NSA_KV_W), lambda b, i: (b, 0)),
            pl.BlockSpec((1, n_far, NSA_KV_W), lambda b, i: (b, 0, 0)),
            pl.BlockSpec((1, n_far, NSA_KV_W), lambda b, i: (b, 0, 0)),
            pl.BlockSpec((n_far, LANES), const2),
            pl.BlockSpec((NSA_N_KV, N_REL_TILES + 1, KT, rq), const4),
            pl.BlockSpec((NSA_N_KV, CMP_NEAR, rq), const3),
            pl.BlockSpec((None, LANES, TQ), lambda b, i: (b, COL_GATE // LANES, i)),
            pl.BlockSpec((None, NSA_Q_W, TQ), lambda b, i: (b, zb, i)),
            pl.BlockSpec((None, NSA_Q_W, TQ), lambda b, i: (b, zb + 1, i)),
            pl.BlockSpec((None, NSA_Q_W, TQ), lambda b, i: (b, zb + 2, i)),
        ],
        out_specs=pl.BlockSpec((None, NSA_Q_W, TQ), lambda b, i: (b, 0, i)),
        scratch_shapes=[
            pltpu.VMEM((NSA_N_KV, _round_up(ns, 2 * SUBLANES), TQ), BF16),
            pltpu.VMEM((LANES, TQ), F32),
            pltpu.VMEM((NSA_N_KV, 1, rq), F32),
            pltpu.VMEM((NSA_N_KV, 1, rq), F32),
            pltpu.VMEM((NSA_N_KV, NSA_HEAD_DIM, rq), F32),
            pltpu.VMEM((N_BRANCH, NSA_N_KV, NSA_HEAD_DIM, rq), F32),
        ],
        compiler_params=_cparams(("parallel", "arbitrary")),
        name="nsa_attn",
    )(ut, kvn, kcp, vcp, jnp.asarray(ov), trel, tcn, ut, ut, ut, ut)
    out = _matmul_res_t(mixed_t, w_out.astype(BF16), x2).reshape(bsz, t, d)
    return out, ut


def _block_diag_q(q2):
    hg = lax.broadcasted_iota(jnp.int32, q2.shape, 0) // NSA_R
    return jnp.concatenate([jnp.where(hg == g, q2, 0.0) for g in range(NSA_N_KV)], axis=1)


def _group_lanes(o):
    dh = NSA_HEAD_DIM
    hg = lax.broadcasted_iota(jnp.int32, (o.shape[0], dh), 0) // NSA_R
    return sum(jnp.where(hg == g, o[:, g * dh:(g + 1) * dh], 0.0) for g in range(NSA_N_KV))


def _nsa_step_kernel(*refs, n_pp, past_len, n_cmp, n_sel_pad):
    page_refs = refs[1:1 + n_pp]
    (q_ref, kvn_ref, kwn_ref, gate_ref, z_ref, kc_ref, vc_ref, win_ref, ovp_ref, bsel_ref, bcmp_ref, bwin_ref, b0_ref,
     o_ref, sel_ref, m_ref, l_ref, acc_ref, oc_ref) = refs[1 + n_pp:]
    step = pl.program_id(1)
    nh = NSA_N_HEADS
    dh = NSA_HEAD_DIM
    qbd = _block_diag_q(q_ref[0] * (dh ** -0.5))
    qbd16 = qbd.astype(BF16)
    cur = past_len // SEL_BLOCK

    @pl.when(step == 0)
    def _():
        s_c = lax.dot_general(qbd16, kc_ref[0].astype(BF16), NT_DIMS, preferred_element_type=F32) + bcmp_ref[...]
        cid = lax.broadcasted_iota(jnp.int32, s_c.shape, 1)
        (p_c,) = _masked_softmax_parts([s_c], [cid < n_cmp], axis=1)
        oc_ref[...] = _group_lanes(jnp.dot(p_c.astype(BF16), vc_ref[0].astype(BF16), preferred_element_type=F32))
        gr = lax.broadcasted_iota(jnp.int32, (SUBLANES, nh), 0)
        gh = lax.broadcasted_iota(jnp.int32, (SUBLANES, nh), 1) // NSA_R
        pg = jnp.dot((gr == gh).astype(F32), p_c, precision=HIGHEST, preferred_element_type=F32)
        imp = jnp.dot(pg, ovp_ref[...], precision=HIGHEST, preferred_element_type=F32)
        sid = lax.broadcasted_iota(jnp.int32, imp.shape, 1)
        forced = (sid == 0) | (sid == cur) | (sid == cur - 1)
        imp = jnp.where(forced, FORCE_SCORE, imp)
        imp = jnp.where(sid * SEL_BLOCK <= past_len, imp, -FORCE_SCORE)
        er = lax.broadcasted_iota(jnp.int32, (n_sel_pad, n_sel_pad), 0)
        ec = lax.broadcasted_iota(jnp.int32, (n_sel_pad, n_sel_pad), 1)
        rows = []
        for g in range(NSA_N_KV):
            row = imp[g:g + 1, :]
            col = jnp.sum(jnp.where(er == ec, row, 0.0), axis=1, keepdims=True)
            ahead = (col > row) | ((col == row) & (er < ec))
            cnt = jnp.sum(ahead.astype(jnp.int32), axis=0, keepdims=True)
            sel = ((cnt < SEL_TOPK) & (row > -0.5 * FORCE_SCORE)).astype(F32)
            rows.extend([sel] * NSA_R)
        sel_ref[...] = jnp.concatenate(rows, axis=0)
        m_ref[...] = jnp.full(m_ref.shape, NEG, F32)
        l_ref[...] = jnp.zeros(l_ref.shape, F32)
        acc_ref[...] = jnp.zeros(acc_ref.shape, F32)

    def online(s):
        m_old = m_ref[...]
        m_new = jnp.maximum(m_old, jnp.max(s, axis=-1, keepdims=True))
        alpha = jnp.exp(m_old - m_new)
        pr = jnp.exp(s - m_new)
        l_ref[...] = l_ref[...] * alpha + jnp.sum(pr, axis=-1, keepdims=True)
        m_ref[...] = m_new
        return alpha, pr

    k_all = jnp.concatenate([r[0:NSA_KV_W, :] for r in page_refs], axis=1).astype(BF16)
    v_all = jnp.concatenate([r[NSA_KV_W:2 * NSA_KV_W, :] for r in page_refs], axis=1).astype(BF16)
    n_tok = n_pp * PAGE_SIZE
    s = jnp.dot(qbd16, k_all, preferred_element_type=F32) + bsel_ref[0]
    eb = lax.broadcasted_iota(jnp.int32, (n_sel_pad, n_tok), 0)
    ej = lax.broadcasted_iota(jnp.int32, (n_sel_pad, n_tok), 1)
    expand = (eb == step * (n_tok // SEL_BLOCK) + ej // SEL_BLOCK).astype(BF16)
    chosen = jnp.dot(sel_ref[...].astype(BF16), expand, preferred_element_type=F32)
    s = jnp.where(chosen > 0.5, s, NEG)
    alpha, pr = online(s)
    acc_ref[...] = acc_ref[...] * alpha + lax.dot_general(pr.astype(BF16), v_all, NT_DIMS, preferred_element_type=F32)

    @pl.when(step == pl.num_programs(1) - 1)
    def _():
        b0 = b0_ref[:, 0:1]
        kn = kvn_ref[0, :, 2 * NSA_KV_W:3 * NSA_KV_W]
        vn = kvn_ref[0, :, 3 * NSA_KV_W:4 * NSA_KV_W]
        s_n = jnp.sum(qbd * kn, axis=-1, keepdims=True) + b0
        s_n = jnp.where(sel_ref[:, cur:cur + 1] > 0.5, s_n, NEG)
        alpha, pr = online(s_n)
        acc = acc_ref[...] * alpha + pr * vn
        o_s = _group_lanes(acc / l_ref[...])
        wk = win_ref[0, 0:NSA_KV_W, :].astype(BF16)
        wv = win_ref[0, NSA_KV_W:2 * NSA_KV_W, :].astype(BF16)
        n_win = win_ref.shape[2]
        s_w = jnp.dot(qbd16, wk, preferred_element_type=F32) + bwin_ref[...]
        wi = lax.broadcasted_iota(jnp.int32, s_w.shape, 1)
        s_wn = jnp.sum(qbd * kwn_ref[0, :, 0:NSA_KV_W], axis=-1, keepdims=True) + b0
        p_w, p_wn = _masked_softmax_parts([s_w, s_wn], [n_win - wi <= WINDOW, jnp.full(s_wn.shape, True)], axis=1)
        o_w = _group_lanes(lax.dot_general(p_w.astype(BF16), wv, NT_DIMS, preferred_element_type=F32)
                           + p_wn * kwn_ref[0, :, NSA_KV_W:2 * NSA_KV_W])
        gate = jax.nn.sigmoid(gate_ref[0])
        mixed = (oc_ref[...] * _silu(z_ref[0, 0]) * gate[:, 0:1] + o_s * _silu(z_ref[0, 1]) * gate[:, 1:2]
                 + o_w * _silu(z_ref[0, 2]) * gate[:, 2:3])
        o_ref[0] = mixed


def _nsa_layer_sample(xs_in, cache_t, win_t, page_table, prm, layer, n_layers):
    norm_w, w_in, pe, w1, w2, w_out, rel_bias = prm
    bsz, t, d = xs_in.shape
    assert t == 1
    n_pages = page_table.shape[1]
    past_len = n_pages * PAGE_SIZE
    n_win = win_t.shape[2]
    nh, dh = NSA_N_HEADS, NSA_HEAD_DIM
    x2 = xs_in.reshape(bsz, d)
    u = _norm_matmul(x2, norm_w, _nsa_in_weights(w_in).astype(BF16), _pick_tn(NSA_N_PAD, 1152))
    ids = (page_table * n_layers + layer).astype(jnp.int32)
    kc, vc = _compress(cache_t, lambda b, p, ids: (ids[b, p], 0, 0), bsz, n_pages, ids, pe, w1, w2)
    n_cmp_pad = kc.shape[1]
    n_cmp = n_cmp_pad - 1
    ns = past_len // SEL_BLOCK + 1
    n_sel_pad = _round_up(ns, LANES)
    ov = np.zeros((n_cmp_pad, n_sel_pad), np.float32)
    ov[:n_cmp, :ns] = _overlap_matrix(n_cmp, ns)
    n_pp = math.gcd(PAGES_PER_STEP, n_pages)
    n_steps = n_pages // n_pp
    n_tok = n_pp * PAGE_SIZE
    kpos = jnp.arange(past_len).reshape(n_steps, n_tok)
    bsel = jnp.moveaxis(_bias_of_dist(rel_bias, past_len - kpos), 0, 1)
    bcmp = _bias_of_dist(rel_bias, past_len - (jnp.arange(n_cmp_pad) * CMP_STRIDE + CMP_BLOCK - 1))
    bwin = _bias_of_dist(rel_bias, n_win - jnp.arange(n_win))
    b0 = jnp.broadcast_to(_bias_of_dist(rel_bias, jnp.zeros((1,), jnp.int32)), (nh, LANES))
    q3 = u[:, COL_Q:COL_Q + NSA_Q_W].reshape(bsz, nh, dh)
    kv_new = u[:, COL_KV:COL_KV + 4 * NSA_KV_W]
    kw_new = u[:, COL_KVW:COL_KVW + 2 * NSA_KV_W]
    gate = u[:, COL_GATE:COL_GATE + nh * N_BRANCH].reshape(bsz, nh, N_BRANCH)
    z4 = u[:, COL_Z:COL_Z + N_BRANCH * NSA_Q_W].reshape(bsz, N_BRANCH, nh, dh)
    per_b3 = lambda b, s, ids: (b, 0, 0)
    const2 = lambda b, s, ids: (0, 0)
    page_specs = [pl.BlockSpec((None, 2 * NSA_KV_W, PAGE_SIZE),
                               functools.partial(lambda b, s, ids, k: (ids[b, s * n_pp + k], 1, 0), k=k))
                  for k in range(n_pp)]
    grid_spec = pltpu.PrefetchScalarGridSpec(
        num_scalar_prefetch=1,
        grid=(bsz, n_steps),
        in_specs=page_specs + [
            pl.BlockSpec((1, nh, dh), per_b3),
            pl.BlockSpec((1, 1, 4 * NSA_KV_W), per_b3),
            pl.BlockSpec((1, 1, 2 * NSA_KV_W), per_b3),
            pl.BlockSpec((1, nh, N_BRANCH), per_b3),
            pl.BlockSpec((1, N_BRANCH, nh, dh), lambda b, s, ids: (b, 0, 0, 0)),
            pl.BlockSpec((1, n_cmp_pad, NSA_KV_W), per_b3),
            pl.BlockSpec((1, n_cmp_pad, NSA_KV_W), per_b3),
            pl.BlockSpec((1, 2 * NSA_KV_W, n_win), per_b3),
            pl.BlockSpec((n_cmp_pad, n_sel_pad), const2),
            pl.BlockSpec((1, nh, n_tok), lambda b, s, ids: (s, 0, 0)),
            pl.BlockSpec((nh, n_cmp_pad), const2),
            pl.BlockSpec((nh, n_win), const2),
            pl.BlockSpec((nh, LANES), const2),
        ],
        out_specs=pl.BlockSpec((1, nh, dh), per_b3),
        scratch_shapes=[
            pltpu.VMEM((nh, n_sel_pad), F32),
            pltpu.VMEM((nh, 1), F32),
            pltpu.VMEM((nh, 1), F32),
            pltpu.VMEM((nh, NSA_KV_W), F32),
            pltpu.VMEM((nh, dh), F32),
        ],
    )
    mixed = pl.pallas_call(
        functools.partial(_nsa_step_kernel, n_pp=n_pp, past_len=past_len, n_cmp=n_cmp, n_sel_pad=n_sel_pad),
        out_shape=jax.ShapeDtypeStruct((bsz, nh, dh), F32),
        grid_spec=grid_spec,
        compiler_params=_cparams(("parallel", "arbitrary")),
        name="nsa_step",
    )(ids, *([cache_t] * n_pp), q3, kv_new.reshape(bsz, 1, -1), kw_new.reshape(bsz, 1, -1), gate, z4, kc, vc, win_t,
      jnp.asarray(ov), bsel, bcmp, bwin, b0)
    out = _matmul_res(mixed.reshape(bsz, NSA_Q_W).astype(BF16), w_out.astype(BF16), x2).reshape(bsz, 1, d)
    return out, kv_new, kw_new


def kernel(x_prompt, x_sample, state_ssm, state_conv, cache_kv, cache_win, page_table, rel_bias, final_norm,
           ssd_norm, ssd_w_in, ssd_conv_w, ssd_conv_b, ssd_dt_bias, ssd_a_log, ssd_d, ssd_gnorm, ssd_w_out,
           nsa_norm, nsa_w_in, nsa_cmp_pe, nsa_cmp_w1, nsa_cmp_w2, nsa_w_out):
    depth = ssd_norm.shape[0] + nsa_norm.shape[0]
    g, dh = NSA_N_KV, NSA_HEAD_DIM
    xp, xs = x_prompt, x_sample
    bp, tp, d = xp.shape
    bs = xs.shape[0]
    n_phys, n_nsa = cache_kv.shape[:2]
    win_len_s = cache_win.shape[2]
    win_len_p = min(WINDOW, tp)
    cache_t = cache_kv.transpose(0, 1, 3, 4, 5, 2).reshape(n_phys * n_nsa, 4 * NSA_KV_W, PAGE_SIZE)
    cwin_t = cache_win.transpose(0, 1, 3, 4, 5, 2).reshape(n_nsa, bs, 2 * NSA_KV_W, win_len_s)
    ssm_p, conv_p, kv_p, win_p = [], [], [], []
    ssm_s, conv_s, kv_s, win_s = [], [], [], []
    for layer in range(depth):
        j = layer // 2
        if layer % 2 == 0:
            prm = (ssd_norm[j], ssd_w_in[j], ssd_conv_w[j], ssd_conv_b[j], ssd_dt_bias[j], ssd_a_log[j],
                   ssd_d[j], ssd_gnorm[j], ssd_w_out[j])
            xp, c, s = _ssd_layer_prompt(xp, prm)
            conv_p.append(c)
            ssm_p.append(s)
            xs, c, s = _ssd_layer_sample(xs, state_conv[j], state_ssm[j], prm)
            conv_s.append(c)
            ssm_s.append(s)
        else:
            prm = (nsa_norm[j], nsa_w_in[j], nsa_cmp_pe[j], nsa_cmp_w1[j], nsa_cmp_w2[j], nsa_w_out[j], rel_bias)
            xp, ut = _nsa_layer_prompt(xp, prm)
            kv_p.append(ut[:, COL_KV:COL_KV + 4 * NSA_KV_W, :])
            win_p.append(ut[:, COL_KVW:COL_KVW + 2 * NSA_KV_W, tp - win_len_p:])
            xs, kv_new, kw_new = _nsa_layer_sample(xs, cache_t, cwin_t[j], page_table, prm, j, n_nsa)
            kv_s.append(kv_new.reshape(bs, 1, 4, g, dh))
            win_s.append(jnp.concatenate([cwin_t[j][:, :, 1:], kw_new[:, :, None]], axis=2))
    y_prompt = _rmsnorm(xp.reshape(bp * tp, d), final_norm).reshape(bp, tp, d)
    y_sample = _rmsnorm(xs.reshape(bs, d), final_norm).reshape(bs, 1, d)
    kv_prompt = jnp.stack(kv_p, axis=1).reshape(bp, n_nsa, 4, g, dh, tp).transpose(0, 5, 1, 2, 3, 4)
    win_prompt = jnp.stack(win_p).reshape(n_nsa, bp, 2, g, dh, win_len_p).transpose(0, 1, 5, 2, 3, 4)
    win_sample = jnp.stack(win_s).reshape(n_nsa, bs, 2, g, dh, win_len_s).transpose(0, 1, 5, 2, 3, 4)
    return (y_prompt, y_sample, jnp.stack(ssm_p), jnp.stack(conv_p), kv_prompt, win_prompt,
            jnp.stack(ssm_s), jnp.stack(conv_s), jnp.stack(kv_s, axis=2), win_sample)
```

```python
import functools
import math

import jax
import jax.numpy as jnp
import numpy as np
from jax import lax
from jax.experimental import pallas as pl
from jax.experimental.pallas import tpu as pltpu

F32 = jnp.float32
BF16 = jnp.bfloat16
HIGHEST = lax.Precision.HIGHEST

NORM_EPS = 1e-6
SSD_HEAD_DIM = 64
SSD_N_GROUPS = 8
SSD_D_STATE = 128
SSD_CONV_W = 4
SSD_CHUNK = 128
NSA_N_HEADS = 16
NSA_HEAD_DIM = 64
NSA_N_KV = 4
NSA_R = NSA_N_HEADS // NSA_N_KV
N_BRANCH = 3
CMP_BLOCK = 32
CMP_STRIDE = 16
CMP_HIDDEN = 128
SEL_BLOCK = 64
SEL_TOPK = 16
WINDOW = 512
FORCE_SCORE = 1e4
REL_BUCKETS = 32
REL_MAX_DIST = 1024
PAGE_SIZE = 128

LANES = 128
SUBLANES = 8
VMEM_LIMIT = 56 * 1024 * 1024

NEG = -1e30
LOG2E = math.log2(math.e)
NT_DIMS = (((1,), (1,)), ((), ()))
TN_DIMS = (((0,), (0,)), ((), ()))


def _cparams(sem):
    return pltpu.CompilerParams(dimension_semantics=sem, vmem_limit_bytes=VMEM_LIMIT)


def _round_up(x, m):
    return (x + m - 1) // m * m


def _silu(x):
    return x * jax.nn.sigmoid(x)


def _pick_tm(m, cap, quantum=SUBLANES):
    if m <= cap:
        return m
    for tm in range(cap // quantum * quantum, 0, -quantum):
        if m % tm == 0:
            return tm
    raise ValueError(f"no row tile for {m}")


def _pick_tn(n, cap=1024):
    best = LANES
    for tn in range(LANES, cap + 1, LANES):
        if n % tn == 0:
            best = tn
    return best


def _norm_matmul_kernel(x_ref, nw_ref, w_ref, o_ref, xn_ref, *, transpose_out):
    @pl.when(pl.program_id(1) == 0)
    def _():
        x = x_ref[...]
        ms = jnp.mean(x * x, axis=-1, keepdims=True)
        xn = x * lax.rsqrt(ms + NORM_EPS) * nw_ref[...]
        if transpose_out:
            xn_ref[...] = xn.T.astype(BF16)
        else:
            xn_ref[...] = xn.astype(BF16)

    if transpose_out:
        o_ref[...] = jnp.dot(w_ref[...], xn_ref[...], preferred_element_type=F32).astype(o_ref.dtype)
    else:
        o_ref[...] = jnp.dot(xn_ref[...], w_ref[...], preferred_element_type=F32).astype(o_ref.dtype)


def _norm_matmul(x, nw, w, tn, out_dtype=F32):
    m, k = x.shape
    n = w.shape[1]
    tm = _pick_tm(m, 1024)
    assert n % tn == 0
    return pl.pallas_call(
        functools.partial(_norm_matmul_kernel, transpose_out=False),
        out_shape=jax.ShapeDtypeStruct((m, n), out_dtype),
        grid=(m // tm, n // tn),
        in_specs=[
            pl.BlockSpec((tm, k), lambda i, j: (i, 0)),
            pl.BlockSpec((1, k), lambda i, j: (0, 0)),
            pl.BlockSpec((k, tn), lambda i, j: (0, j)),
        ],
        out_specs=pl.BlockSpec((tm, tn), lambda i, j: (i, j)),
        scratch_shapes=[pltpu.VMEM((tm, k), BF16)],
        compiler_params=_cparams(("parallel", "arbitrary")),
        name="norm_matmul",
    )(x, nw.reshape(1, k), w)


def _norm_matmul_t(x, nw, wt, bsz, tn):
    m, k = x.shape
    n = wt.shape[0]
    t = m // bsz
    tm = _pick_tm(t, 1024, LANES)
    assert n % tn == 0
    tpb = t // tm
    return pl.pallas_call(
        functools.partial(_norm_matmul_kernel, transpose_out=True),
        out_shape=jax.ShapeDtypeStruct((bsz, n, t), F32),
        grid=(m // tm, n // tn),
        in_specs=[
            pl.BlockSpec((tm, k), lambda i, j: (i, 0)),
            pl.BlockSpec((1, k), lambda i, j: (0, 0)),
            pl.BlockSpec((tn, k), lambda i, j: (j, 0)),
        ],
        out_specs=pl.BlockSpec((None, tn, tm), lambda i, j: (i // tpb, j, i % tpb)),
        scratch_shapes=[pltpu.VMEM((k, tm), BF16)],
        compiler_params=_cparams(("parallel", "arbitrary")),
        name="norm_matmul_t",
    )(x, nw.reshape(1, k), wt)


def _matmul_res_kernel(y_ref, w_ref, x_ref, o_ref, *, y_transposed):
    dims = TN_DIMS if y_transposed else (((1,), (0,)), ((), ()))
    o_ref[...] = x_ref[...] + lax.dot_general(y_ref[...], w_ref[...], dims, preferred_element_type=F32)


def _matmul_res(y, w, x):
    m, k = y.shape
    n = w.shape[1]
    tm = min(m, 512)
    assert m % tm == 0
    return pl.pallas_call(
        functools.partial(_matmul_res_kernel, y_transposed=False),
        out_shape=jax.ShapeDtypeStruct((m, n), F32),
        grid=(m // tm,),
        in_specs=[
            pl.BlockSpec((tm, k), lambda i: (i, 0)),
            pl.BlockSpec((k, n), lambda i: (0, 0)),
            pl.BlockSpec((tm, n), lambda i: (i, 0)),
        ],
        out_specs=pl.BlockSpec((tm, n), lambda i: (i, 0)),
        compiler_params=_cparams(("parallel",)),
        name="matmul_res",
    )(y, w, x)


def _matmul_res_t(yt, w, x):
    bsz, k, t = yt.shape
    n = w.shape[1]
    tm = min(t, 512)
    assert t % tm == 0
    tpb = t // tm
    return pl.pallas_call(
        functools.partial(_matmul_res_kernel, y_transposed=True),
        out_shape=jax.ShapeDtypeStruct((bsz * t, n), F32),
        grid=(bsz * tpb,),
        in_specs=[
            pl.BlockSpec((None, k, tm), lambda i: (i // tpb, 0, i % tpb)),
            pl.BlockSpec((k, n), lambda i: (0, 0)),
            pl.BlockSpec((tm, n), lambda i: (i, 0)),
        ],
        out_specs=pl.BlockSpec((tm, n), lambda i: (i, 0)),
        compiler_params=_cparams(("parallel",)),
        name="matmul_res_t",
    )(yt, w, x)


def _rmsnorm_kernel(x_ref, nw_ref, o_ref):
    x = x_ref[...]
    ms = jnp.mean(x * x, axis=-1, keepdims=True)
    o_ref[...] = x * lax.rsqrt(ms + NORM_EPS) * nw_ref[...]


def _rmsnorm(x, nw):
    m, k = x.shape
    tm = min(m, 512)
    assert m % tm == 0
    return pl.pallas_call(
        _rmsnorm_kernel,
        out_shape=jax.ShapeDtypeStruct((m, k), F32),
        grid=(m // tm,),
        in_specs=[pl.BlockSpec((tm, k), lambda i: (i, 0)), pl.BlockSpec((1, k), lambda i: (0, 0))],
        out_specs=pl.BlockSpec((tm, k), lambda i: (i, 0)),
        compiler_params=_cparams(("parallel",)),
        name="final_rmsnorm",
    )(x, nw.reshape(1, k))


def _ssd_chunk_kernel(z_ref, x_ref, bc_ref, dt_ref, cprev_ref, sprev_ref, cw_ref, cb_ref, dtb_ref, alog_ref,
                      dskip_ref, gn_ref, y_ref, hlast_ref, xpad_ref, state_ref, *, d_inner, n_heads):
    lc = SSD_CHUNK
    g_n, n, p = SSD_N_GROUPS, SSD_D_STATE, SSD_HEAD_DIM
    r_n = n_heads // g_n
    c = pl.program_id(1)

    @pl.when(c == 0)
    def _():
        xpad_ref[0:SUBLANES, :] = cprev_ref[0]
        state_ref[...] = sprev_ref[0]

    @pl.when(c > 0)
    def _():
        xpad_ref[0:SUBLANES, :] = xpad_ref[lc:lc + SUBLANES, :]

    xpad_ref[SUBLANES:SUBLANES + lc, 0:d_inner] = x_ref[...]
    xpad_ref[SUBLANES:SUBLANES + lc, d_inner:] = bc_ref[...]

    conv = cb_ref[...]
    for k in range(SSD_CONV_W):
        off = SUBLANES - (SSD_CONV_W - 1) + k
        conv = conv + xpad_ref[off:off + lc, :] * cw_ref[k:k + 1, :]
    act = _silu(conv)
    xs = act[:, :d_inner]
    bm = act[:, d_inner:d_inner + g_n * n]
    cm = act[:, d_inner + g_n * n:]

    dt = jax.nn.softplus(dt_ref[...] + dtb_ref[...])
    a = -jnp.exp(alog_ref[...])
    dta = dt * a
    row = lax.broadcasted_iota(jnp.int32, (lc, lc), 0)
    col = lax.broadcasted_iota(jnp.int32, (lc, lc), 1)
    causal = row >= col
    tril = causal.astype(F32)
    a_cs = jnp.dot(tril, dta, precision=HIGHEST, preferred_element_type=F32)
    a_cs_t = a_cs.T
    a_end = a_cs[lc - 1:lc, :]
    to_end = jnp.exp(a_end - a_cs)
    e_cs = jnp.exp(a_cs)
    e_end = jnp.exp(a_end)

    assert 2 * p == LANES and r_n % 2 == 0
    lane_lo = lax.broadcasted_iota(jnp.int32, (1, LANES), 1) < p

    def pair_lanes(v, k):
        return jnp.where(lane_lo, v[:, 2 * k:2 * k + 1], v[:, 2 * k + 1:2 * k + 2])

    zs = _silu(z_ref[...])
    gw = r_n * p
    ys = []
    for g in range(g_n):
        bg16 = bm[:, g * n:(g + 1) * n].astype(BF16)
        cg16 = cm[:, g * n:(g + 1) * n].astype(BF16)
        cb = lax.dot_general(cg16, bg16, NT_DIMS, preferred_element_type=F32)
        pairs = range(g * r_n // 2, (g + 1) * r_n // 2)
        gl = slice(g * gw, (g + 1) * gw)
        xs_g = xs[:, gl]
        xdt_g = xs_g * jnp.concatenate([pair_lanes(dt, k) for k in pairs], axis=1)
        y_diag = []
        for i, k in enumerate(pairs):
            xdt_k = xdt_g[:, i * LANES:(i + 1) * LANES]
            acc = None
            for j, h in enumerate((2 * k, 2 * k + 1)):
                seg = a_cs[:, h:h + 1] - a_cs_t[h:h + 1, :]
                decay = jnp.exp(jnp.where(causal, seg, -jnp.inf))
                m_h = (cb * decay).astype(BF16)
                x_h = jnp.where(lane_lo if j == 0 else jnp.logical_not(lane_lo), xdt_k, 0.0).astype(BF16)
                d = jnp.dot(m_h, x_h, preferred_element_type=F32)
                acc = d if acc is None else acc + d
            y_diag.append(acc)
        y_diag = jnp.concatenate(y_diag, axis=1)
        sprev = state_ref[:, gl]
        y_off = (jnp.dot(cg16, sprev.astype(BF16), preferred_element_type=F32)
                 * jnp.concatenate([pair_lanes(e_cs, k) for k in pairs], axis=1))
        xw = (xdt_g * jnp.concatenate([pair_lanes(to_end, k) for k in pairs], axis=1)).astype(BF16)
        st = lax.dot_general(bg16, xw, TN_DIMS, preferred_element_type=F32)
        state_ref[:, gl] = sprev * jnp.concatenate([pair_lanes(e_end, k) for k in pairs], axis=1) + st
        yg = (y_diag + y_off + xs_g * dskip_ref[:, gl]) * zs[:, gl]
        ms = jnp.mean(yg * yg, axis=-1, keepdims=True)
        ys.append(yg * lax.rsqrt(ms + NORM_EPS))
    y = jnp.concatenate(ys, axis=1) * gn_ref[...]
    y_ref[...] = y.astype(y_ref.dtype)

    @pl.when(c == pl.num_programs(1) - 1)
    def _():
        hlast_ref[0] = state_ref[...]


def _pad_lanes(v, width=LANES):
    return jnp.pad(v.astype(F32), (0, width - v.shape[0])).reshape(1, width)


def _ssd_prompt_core(u, bsz, t, conv_prev8, ssm_prev, conv_w, conv_b, dt_bias, a_log, d_skip, gnorm):
    n_heads = dt_bias.shape[0]
    d_inner = n_heads * SSD_HEAD_DIM
    conv_dim = conv_w.shape[1]
    assert conv_dim == 2 * d_inner and t % SSD_CHUNK == 0
    lc = SSD_CHUNK
    nc = t // lc
    dt_blk = (d_inner + conv_dim) // LANES
    kern = functools.partial(_ssd_chunk_kernel, d_inner=d_inner, n_heads=n_heads)
    const = lambda b, c: (0, 0)
    state_t = ssm_prev.transpose(0, 3, 1, 2).reshape(bsz, SSD_D_STATE, d_inner)
    y, hlast_t = pl.pallas_call(
        kern,
        out_shape=(jax.ShapeDtypeStruct((bsz * t, d_inner), BF16),
                   jax.ShapeDtypeStruct((bsz, SSD_D_STATE, d_inner), F32)),
        grid=(bsz, nc),
        in_specs=[
            pl.BlockSpec((lc, d_inner), lambda b, c: (b * nc + c, 0)),
            pl.BlockSpec((lc, d_inner), lambda b, c: (b * nc + c, 1)),
            pl.BlockSpec((lc, d_inner), lambda b, c: (b * nc + c, 2)),
            pl.BlockSpec((lc, LANES), lambda b, c: (b * nc + c, dt_blk)),
            pl.BlockSpec((1, SUBLANES, conv_dim), lambda b, c: (b, 0, 0)),
            pl.BlockSpec((1, SSD_D_STATE, d_inner), lambda b, c: (b, 0, 0)),
            pl.BlockSpec((SSD_CONV_W, conv_dim), const),
            pl.BlockSpec((1, conv_dim), const),
            pl.BlockSpec((1, LANES), const),
            pl.BlockSpec((1, LANES), const),
            pl.BlockSpec((1, d_inner), const),
            pl.BlockSpec((1, d_inner), const),
        ],
        out_specs=(pl.BlockSpec((lc, d_inner), lambda b, c: (b * nc + c, 0)),
                   pl.BlockSpec((1, SSD_D_STATE, d_inner), lambda b, c: (b, 0, 0))),
        scratch_shapes=[pltpu.VMEM((lc + SUBLANES, conv_dim), F32),
                        pltpu.VMEM((SSD_D_STATE, d_inner), F32)],
        compiler_params=_cparams(("parallel", "arbitrary")),
        name="ssd_chunk",
    )(u, u, u, u, conv_prev8, state_t, conv_w, conv_b.reshape(1, conv_dim), _pad_lanes(dt_bias),
      _pad_lanes(a_log), jnp.repeat(d_skip.astype(F32), SSD_HEAD_DIM).reshape(1, d_inner), gnorm.reshape(1, d_inner))
    hlast = hlast_t.reshape(bsz, SSD_D_STATE, n_heads, SSD_HEAD_DIM).transpose(0, 2, 3, 1)
    return y, hlast


def _ssd_in_weights(w_in, d_inner, conv_dim):
    main = d_inner + conv_dim
    n_dt = w_in.shape[1] - main
    w = jnp.concatenate([w_in[:, :main], jnp.pad(w_in[:, main:], ((0, 0), (0, LANES - n_dt)))], axis=1)
    return w.astype(BF16)


def _ssd_layer_prompt(xp, prm):
    norm_w, w_in, conv_w, conv_b, dt_bias, a_log, d_skip, gnorm, w_out = prm
    bsz, t, d = xp.shape
    assert t >= SSD_CONV_W - 1
    n_heads = dt_bias.shape[0]
    d_inner = n_heads * SSD_HEAD_DIM
    conv_dim = conv_w.shape[1]
    w = _ssd_in_weights(w_in, d_inner, conv_dim)
    x2 = xp.reshape(bsz * t, d)
    u = _norm_matmul(x2, norm_w, w, _pick_tn(w.shape[1]))
    conv_prev8 = jnp.zeros((bsz, SUBLANES, conv_dim), F32)
    ssm_prev = jnp.zeros((bsz, n_heads, SSD_HEAD_DIM, SSD_D_STATE), F32)
    y, hlast = _ssd_prompt_core(u, bsz, t, conv_prev8, ssm_prev, conv_w, conv_b, dt_bias, a_log, d_skip, gnorm)
    out = _matmul_res(y, w_out.astype(BF16), x2).reshape(bsz, t, d)
    u3 = u.reshape(bsz, t, -1)
    new_conv = u3[:, t - (SSD_CONV_W - 1):, d_inner:d_inner + conv_dim]
    return out, new_conv, hlast


def _ssd_step_pre_kernel(u_ref, cst_ref, cw_ref, cb_ref, dtb_ref, alog_ref,
                         xs_ref, xdt_ref, dec_ref, bm_ref, cm_ref, ncst_ref, *, d_inner, n_heads):
    conv_dim = cw_ref.shape[1]
    gn = SSD_N_GROUPS * SSD_D_STATE
    xbc = u_ref[:, d_inner:d_inner + conv_dim]
    conv = cb_ref[...]
    for k in range(SSD_CONV_W - 1):
        conv = conv + cst_ref[k] * cw_ref[k:k + 1, :]
    conv = conv + xbc * cw_ref[SSD_CONV_W - 1:SSD_CONV_W, :]
    for k in range(SSD_CONV_W - 2):
        ncst_ref[k] = cst_ref[k + 1]
    ncst_ref[SSD_CONV_W - 2] = xbc
    act = _silu(conv)
    xs = act[:, :d_inner]
    dt = jax.nn.softplus(u_ref[:, d_inner + conv_dim:d_inner + conv_dim + LANES] + dtb_ref[...])
    dec = jnp.exp(dt * (-jnp.exp(alog_ref[...])))
    hrow = lax.broadcasted_iota(jnp.int32, (LANES, d_inner), 0)
    hcol = lax.broadcasted_iota(jnp.int32, (LANES, d_inner), 1) // SSD_HEAD_DIM
    expand = (hrow == hcol).astype(F32)
    dt_e = jnp.dot(dt, expand, precision=HIGHEST, preferred_element_type=F32)
    dec_e = jnp.dot(dec, expand, precision=HIGHEST, preferred_element_type=F32)
    xs_ref[...] = xs
    xdt_ref[...] = xs * dt_e
    dec_ref[...] = dec_e
    bm_ref[...] = act[:, d_inner:d_inner + gn]
    cm_ref[...] = act[:, d_inner + gn:]


def _ssd_step_state_kernel(s_ref, xdt_ref, dec_ref, b_ref, c_ref, snew_ref, y_ref, *, rows_per_group):
    for g in range(SSD_N_GROUPS):
        sl = slice(g * rows_per_group, (g + 1) * rows_per_group)
        s = s_ref[0, sl, :]
        bg = b_ref[0, g:g + 1, :]
        cg = c_ref[0, g:g + 1, :]
        xdt = xdt_ref[0, sl, :]
        dec = dec_ref[0, sl, :]
        cb = jnp.sum(bg * cg, axis=1, keepdims=True)
        y_off = jnp.sum(s * cg, axis=1, keepdims=True)
        snew_ref[0, sl, :] = s * dec + xdt * bg
        y_ref[0, sl, :] = cb * xdt + dec * y_off


def _ssd_step_post_kernel(y_ref, xs_ref, u_ref, dsk_ref, gn_ref, o_ref, *, d_inner, group_width):
    y = (y_ref[...] + xs_ref[...] * dsk_ref[...]) * _silu(u_ref[:, :d_inner])
    outs = []
    for g in range(SSD_N_GROUPS):
        yg = y[:, g * group_width:(g + 1) * group_width]
        ms = jnp.mean(yg * yg, axis=-1, keepdims=True)
        outs.append(yg * lax.rsqrt(ms + NORM_EPS))
    o_ref[...] = (jnp.concatenate(outs, axis=1) * gn_ref[...]).astype(o_ref.dtype)


def _ssd_layer_sample(xs_in, conv_state, ssm_state, prm):
    norm_w, w_in, conv_w, conv_b, dt_bias, a_log, d_skip, gnorm, w_out = prm
    bsz, t, d = xs_in.shape
    assert t == 1
    n_heads = dt_bias.shape[0]
    d_inner = n_heads * SSD_HEAD_DIM
    conv_dim = conv_w.shape[1]
    gn = SSD_N_GROUPS * SSD_D_STATE
    w = _ssd_in_weights(w_in, d_inner, conv_dim)
    x2 = xs_in.reshape(bsz, d)
    u = _norm_matmul(x2, norm_w, w, _pick_tn(w.shape[1]))
    cst = conv_state.transpose(1, 0, 2)
    pre = functools.partial(_ssd_step_pre_kernel, d_inner=d_inner, n_heads=n_heads)
    xs, xdt, dec, bm, cm, ncst = pl.pallas_call(
        pre,
        out_shape=(jax.ShapeDtypeStruct((bsz, d_inner), F32), jax.ShapeDtypeStruct((bsz, d_inner), F32),
                   jax.ShapeDtypeStruct((bsz, d_inner), F32), jax.ShapeDtypeStruct((bsz, gn), F32),
                   jax.ShapeDtypeStruct((bsz, gn), F32), jax.ShapeDtypeStruct(cst.shape, F32)),
        compiler_params=pltpu.CompilerParams(vmem_limit_bytes=VMEM_LIMIT),
        name="ssd_step_pre",
    )(u, cst, conv_w, conv_b.reshape(1, conv_dim), _pad_lanes(dt_bias), _pad_lanes(a_log))
    rows = n_heads * SSD_HEAD_DIM
    rpg = rows // SSD_N_GROUPS
    st = functools.partial(_ssd_step_state_kernel, rows_per_group=rpg)
    snew, ycol = pl.pallas_call(
        st,
        out_shape=(jax.ShapeDtypeStruct((bsz, rows, SSD_D_STATE), F32), jax.ShapeDtypeStruct((bsz, rows, 1), F32)),
        grid=(bsz,),
        in_specs=[
            pl.BlockSpec((1, rows, SSD_D_STATE), lambda b: (b, 0, 0)),
            pl.BlockSpec((1, rows, 1), lambda b: (b, 0, 0)),
            pl.BlockSpec((1, rows, 1), lambda b: (b, 0, 0)),
            pl.BlockSpec((1, SSD_N_GROUPS, SSD_D_STATE), lambda b: (b, 0, 0)),
            pl.BlockSpec((1, SSD_N_GROUPS, SSD_D_STATE), lambda b: (b, 0, 0)),
        ],
        out_specs=(pl.BlockSpec((1, rows, SSD_D_STATE), lambda b: (b, 0, 0)),
                   pl.BlockSpec((1, rows, 1), lambda b: (b, 0, 0))),
        compiler_params=_cparams(("parallel",)),
        name="ssd_step_state",
    )(ssm_state.reshape(bsz, rows, SSD_D_STATE), xdt.reshape(bsz, rows, 1), dec.reshape(bsz, rows, 1),
      bm.reshape(bsz, SSD_N_GROUPS, SSD_D_STATE), cm.reshape(bsz, SSD_N_GROUPS, SSD_D_STATE))
    post = functools.partial(_ssd_step_post_kernel, d_inner=d_inner, group_width=d_inner // SSD_N_GROUPS)
    y = pl.pallas_call(
        post,
        out_shape=jax.ShapeDtypeStruct((bsz, d_inner), BF16),
        compiler_params=pltpu.CompilerParams(vmem_limit_bytes=VMEM_LIMIT),
        name="ssd_step_post",
    )(ycol.reshape(bsz, rows), xs, u, jnp.repeat(d_skip.astype(F32), SSD_HEAD_DIM).reshape(1, d_inner),
      gnorm.reshape(1, d_inner))
    out = _matmul_res(y, w_out.astype(BF16), x2).reshape(bsz, 1, d)
    return out, ncst.transpose(1, 0, 2), snew.reshape(ssm_state.shape)


NSA_Q_W = NSA_N_HEADS * NSA_HEAD_DIM
NSA_KV_W = NSA_N_KV * NSA_HEAD_DIM
COL_Q = 0
COL_KV = NSA_Q_W
COL_Z = COL_KV + 4 * NSA_KV_W
COL_KVW = COL_Z + N_BRANCH * NSA_Q_W
COL_GATE = COL_KVW + 2 * NSA_KV_W
NSA_N_PAD = COL_GATE + LANES
TQ = 128
KT = 128
N_REL_TILES = -(-(REL_MAX_DIST + TQ - 1) // KT)
CMP_NEAR = 128
CMP_FRONT = CMP_NEAR - TQ // CMP_STRIDE
PAGES_PER_STEP = 16
SEL_STEP_TILES = 4


def _nsa_in_weights(w_in):
    cuts = np.cumsum([NSA_Q_W, 4 * NSA_KV_W, 2 * NSA_KV_W, NSA_N_HEADS * N_BRANCH]).tolist()
    q, kv, kvw, gate, z = (w_in[:, a:b] for a, b in zip([0] + cuts, cuts + [w_in.shape[1]]))
    gate = jnp.pad(gate, ((0, 0), (0, LANES - gate.shape[1])))
    return jnp.concatenate([q, kv, z, kvw, gate], axis=1)


def _rel_bucket(dist):
    n = jnp.maximum(dist, 0)
    exact = REL_BUCKETS // 2
    logv = jnp.log(jnp.maximum(n, 1).astype(F32) / exact) / math.log(REL_MAX_DIST / exact)
    large = jnp.minimum(exact + (logv * (REL_BUCKETS - exact)).astype(jnp.int32), REL_BUCKETS - 1)
    return jnp.where(n < exact, n, large)


def _bias_by_dist(rel_bias, n):
    near = jnp.moveaxis(rel_bias.astype(F32)[_rel_bucket(jnp.arange(min(n, REL_MAX_DIST)))], -1, 0)
    if n <= REL_MAX_DIST:
        return near
    far = jnp.broadcast_to(rel_bias.astype(F32)[REL_BUCKETS - 1][:, None], (rel_bias.shape[1], n - REL_MAX_DIST))
    return jnp.concatenate([near, far], axis=1)


def _toeplitz(vec, first, n_rows, n_cols, row_step):
    lead = vec.shape[:-1]
    lo = first - row_step * (n_rows - 1)
    front = max(-lo, 0)
    if front:
        vec = jnp.concatenate([jnp.broadcast_to(vec[..., :1], lead + (front,)), vec], axis=-1)
    span = row_step * (n_rows - 1) + n_cols
    w = vec[..., lo + front:lo + front + span]
    pitch = span + row_step
    flat = jnp.tile(w, n_rows + 1)[..., :n_rows * pitch]
    s = flat.reshape(lead + (n_rows, pitch))[..., :n_cols]
    return jnp.flip(s, axis=-2)


def _lanes_by_group(t):
    h, n, q = t.shape
    return t.reshape(NSA_N_KV, NSA_R, n, q).transpose(0, 2, 1, 3).reshape(NSA_N_KV, n, NSA_R * q)


def _overlap_matrix(nc, ns):
    c_start = np.arange(nc) * CMP_STRIDE
    c_end = c_start + CMP_BLOCK - 1
    s_start = np.arange(ns) * SEL_BLOCK
    ov = np.clip(np.minimum(c_end[:, None], s_start[None, :] + SEL_BLOCK - 1)
                 - np.maximum(c_start[:, None], s_start[None, :]) + 1, 0, None).astype(np.float32) / CMP_STRIDE
    return ov


def _masked_softmax_parts(parts, valids, axis):
    m = None
    for s, v in zip(parts, valids):
        pm = jnp.max(jnp.where(v, s, -jnp.inf), axis=axis, keepdims=True)
        m = pm if m is None else jnp.maximum(m, pm)
    m = jnp.where(m > -jnp.inf, m, 0.0)
    es = [jnp.where(v, jnp.exp(s - m), 0.0) for s, v in zip(parts, valids)]
    den = None
    for e in es:
        d = jnp.sum(e, axis=axis, keepdims=True)
        den = d if den is None else den + d
    inv = 1.0 / jnp.where(den > 0, den, 1.0)
    return [e * inv for e in es]


def _compress_kernel(*refs, n_pp):
    page_refs = refs[1:1 + n_pp]
    pe_ref, w1_ref, w2_ref, kc_ref, vc_ref, x_ref = refs[1 + n_pp:]
    step = pl.program_id(1)
    hb = PAGE_SIZE // CMP_STRIDE
    dh = NSA_HEAD_DIM
    for k in range(n_pp):
        pt = page_refs[k][...].T.reshape(hb, CMP_STRIDE, 2 * NSA_KV_W)
        row0 = pl.multiple_of((step * n_pp + k) * hb, hb)
        for l in range(CMP_STRIDE):
            row = pt[:, l, :]
            for kv in range(2):
                for g in range(NSA_N_KV):
                    c0 = kv * NSA_KV_W + g * dh
                    x_ref[kv, g, pl.ds(row0, hb), l * dh:(l + 1) * dh] = row[:, c0:c0 + dh]

    @pl.when(step == pl.num_programs(1) - 1)
    def _():
        half = CMP_STRIDE * dh
        for kv, out_ref in ((0, kc_ref), (1, vc_ref)):
            w_top = w1_ref[kv, 0:half, :]
            w_bot = w1_ref[kv, half:2 * half, :]
            w2 = w2_ref[kv]
            for g in range(NSA_N_KV):
                x = x_ref[kv, g]
                top = jnp.dot((x + pe_ref[kv, 0:1, :]).astype(BF16), w_top, preferred_element_type=F32)
                bot = jnp.dot((x + pe_ref[kv, 1:2, :]).astype(BF16), w_bot, preferred_element_type=F32)
                bot = jnp.concatenate([bot[1:], jnp.zeros((1, bot.shape[1]), F32)], axis=0)
                hid = _silu(top + bot).astype(BF16)
                out_ref[0, :, g * dh:(g + 1) * dh] = jnp.dot(hid, w2, preferred_element_type=F32)


def _compress(src, page_index, bsz, n_pages, ids, pe, w1, w2):
    n_pp = math.gcd(PAGES_PER_STEP, n_pages)
    hb = PAGE_SIZE // CMP_STRIDE
    nh = n_pages * hb
    half = CMP_STRIDE * NSA_HEAD_DIM
    blk = (None,) * (src.ndim - 2) + (2 * NSA_KV_W, PAGE_SIZE)
    page_specs = [pl.BlockSpec(blk, functools.partial(lambda b, s, ids, k: page_index(b, s * n_pp + k, ids), k=k))
                  for k in range(n_pp)]
    grid_spec = pltpu.PrefetchScalarGridSpec(
        num_scalar_prefetch=1,
        grid=(bsz, n_pages // n_pp),
        in_specs=page_specs + [
            pl.BlockSpec((2, 2, half), lambda b, s, ids: (0, 0, 0)),
            pl.BlockSpec((2, 2 * half, CMP_HIDDEN), lambda b, s, ids: (0, 0, 0)),
            pl.BlockSpec((2, CMP_HIDDEN, NSA_HEAD_DIM), lambda b, s, ids: (0, 0, 0)),
        ],
        out_specs=(pl.BlockSpec((1, nh, NSA_KV_W), lambda b, s, ids: (b, 0, 0)),
                   pl.BlockSpec((1, nh, NSA_KV_W), lambda b, s, ids: (b, 0, 0))),
        scratch_shapes=[pltpu.VMEM((2, NSA_N_KV, nh, half), F32)],
    )
    return pl.pallas_call(
        functools.partial(_compress_kernel, n_pp=n_pp),
        out_shape=(jax.ShapeDtypeStruct((bsz, nh, NSA_KV_W), F32), jax.ShapeDtypeStruct((bsz, nh, NSA_KV_W), F32)),
        grid_spec=grid_spec,
        compiler_params=_cparams(("parallel", "arbitrary")),
        name="nsa_compress",
    )(ids, *([src] * n_pp), pe.reshape(2, 2, half).astype(F32), w1.astype(BF16), w2.astype(BF16))


def _nsa_attn_kernel(qt_ref, kv_ref, kc_ref, vc_ref, ov_ref, trel_ref, tcn_ref, cfar_ref, gate_ref, z0_ref, z1_ref,
                     z2_ref, o_ref, qaug_ref, imp_ref, m_ref, l_ref, acc_ref, ob_ref, s_ref, *, n_sel, n_cmp_rows):
    t = pl.program_id(1)
    dh = NSA_HEAD_DIM
    rq = NSA_R * TQ
    scale = dh ** -0.5
    far_idx = N_REL_TILES
    ns8 = _round_up(n_sel, 2 * SUBLANES)
    assert ns8 < LANES
    near0 = pl.multiple_of(t * (TQ // CMP_STRIDE), SUBLANES)
    far_rows = slice(CMP_FRONT, CMP_FRONT + n_cmp_rows)

    def cmp_branch(g, qt, with_far):
        gs = slice(g * dh, (g + 1) * dh)
        kc_near = kc_ref[0, pl.ds(near0, CMP_NEAR), gs].astype(BF16)
        vc_near = vc_ref[0, pl.ds(near0, CMP_NEAR), gs].astype(BF16)
        cn = lax.broadcasted_iota(jnp.int32, (CMP_NEAR, rq), 0)
        qn = lax.broadcasted_iota(jnp.int32, (CMP_NEAR, rq), 1) % TQ
        dist_n = qn - CMP_STRIDE * cn + (CMP_STRIDE * CMP_FRONT - (CMP_BLOCK - 1))
        parts = [jnp.dot(kc_near, qt, preferred_element_type=F32) + tcn_ref[g]]
        valids = [(dist_n >= 0) & (cn + near0 >= CMP_FRONT)]
        if with_far:
            kc_far = kc_ref[0, far_rows, gs].astype(BF16)
            parts.append(jnp.dot(kc_far, qt, preferred_element_type=F32) + cfar_ref[g, 0:1, :])
            valids.append(lax.broadcasted_iota(jnp.int32, (n_cmp_rows, rq), 0) + CMP_FRONT < near0)
        ps = _masked_softmax_parts(parts, valids, axis=0)

        def importance(ov, p):
            pg = sum(p[:, r * TQ:(r + 1) * TQ] for r in range(NSA_R))
            hi = pg.astype(BF16)
            lo = (pg - hi.astype(F32)).astype(BF16)
            ov16 = ov.astype(BF16)
            return (lax.dot_general(ov16, hi, TN_DIMS, preferred_element_type=F32)
                    + lax.dot_general(ov16, lo, TN_DIMS, preferred_element_type=F32))

        o_c = lax.dot_general(vc_near, ps[0].astype(BF16), TN_DIMS, preferred_element_type=F32)
        imp = importance(ov_ref[pl.ds(near0, CMP_NEAR), :], ps[0])
        if with_far:
            vc_far = vc_ref[0, far_rows, gs].astype(BF16)
            o_c = o_c + lax.dot_general(vc_far, ps[1].astype(BF16), TN_DIMS, preferred_element_type=F32)
            imp = imp + importance(ov_ref[far_rows, :], ps[1])
        ob_ref[0, g] = o_c
        imp_ref[...] = imp

    qts = []
    sid = lax.broadcasted_iota(jnp.int32, (ns8, TQ), 0)
    qpos = t * TQ + lax.broadcasted_iota(jnp.int32, (ns8, TQ), 1)
    cur = qpos // SEL_BLOCK
    forced = (sid == 0) | (sid == cur) | (sid == cur - 1)
    reachable = sid * SEL_BLOCK <= qpos
    sub = lax.broadcasted_iota(jnp.int32, (SUBLANES, TQ), 0)
    tail_row = lax.broadcasted_iota(jnp.int32, (LANES - ns8, rq), 0)
    for g in range(NSA_N_KV):
        q32 = jnp.concatenate([qt_ref[(g * NSA_R + r) * dh:(g * NSA_R + r + 1) * dh, :] for r in range(NSA_R)], axis=1)
        qt = (q32 * scale).astype(BF16)
        qts.append((q32 * (scale * LOG2E)).astype(BF16))
        has_far = near0 > CMP_FRONT

        @pl.when(has_far)
        def _():
            cmp_branch(g, qt, True)

        @pl.when(jnp.logical_not(has_far))
        def _():
            cmp_branch(g, qt, False)

        imp = jnp.where(forced, FORCE_SCORE, imp_ref[0:ns8, :])
        imp = jnp.where(reachable, imp, -FORCE_SCORE)
        blocks = [imp[SUBLANES * kb:SUBLANES * (kb + 1)] for kb in range(ns8 // SUBLANES)]
        cnts = [jnp.zeros((SUBLANES, TQ), F32) for _ in blocks]
        for sp in range(n_sel):
            other = imp[sp:sp + 1, :]
            for kb, blk in enumerate(blocks):
                if kb > sp // SUBLANES:
                    ahead = other >= blk
                elif kb < sp // SUBLANES:
                    ahead = other > blk
                else:
                    ahead = (other > blk) | ((other == blk) & (sub > sp % SUBLANES))
                cnts[kb] = cnts[kb] + jnp.where(ahead, 1.0, 0.0)
        cnt = jnp.concatenate(cnts, axis=0)
        unsel = jnp.where((cnt < SEL_TOPK) & (imp > -0.5 * FORCE_SCORE), 0.0, NEG)
        zero = jnp.zeros((dh, rq), BF16)
        q_rows = [qts[g], zero] if g % 2 == 0 else [zero, qts[g]]
        tail = jnp.where(tail_row == 0, trel_ref[g, far_idx, 0:1, :], 0.0)
        qaug_ref[g] = jnp.concatenate(q_rows + [jnp.concatenate([unsel] * NSA_R, axis=1).astype(BF16),
                                                tail.astype(BF16)], axis=0)

    def reset():
        m_ref[...] = jnp.full(m_ref.shape, NEG, F32)
        l_ref[...] = jnp.zeros(l_ref.shape, F32)
        acc_ref[...] = jnp.zeros(acc_ref.shape, F32)

    def online_step(g, s, v_t):
        m_old = m_ref[g]
        m_new = jnp.maximum(m_old, jnp.max(s, axis=0, keepdims=True))
        alpha = jnp.exp2(m_old - m_new)
        p = jnp.exp2(s - m_new)
        l_ref[g] = l_ref[g] * alpha + jnp.sum(p, axis=0, keepdims=True)
        acc_ref[g] = acc_ref[g] * alpha + lax.dot_general(v_t, p.astype(BF16), TN_DIMS, preferred_element_type=F32)
        m_ref[g] = m_new

    def mask_heads(s, ok):
        return jnp.concatenate([jnp.where(ok, s[:, r * TQ:(r + 1) * TQ], NEG) for r in range(NSA_R)], axis=1)

    reset()
    kt2 = SEL_STEP_TILES * KT
    kj2 = lax.broadcasted_iota(jnp.int32, (kt2, TQ), 0)
    qi2 = lax.broadcasted_iota(jnp.int32, (kt2, TQ), 1)
    e_blk = lax.broadcasted_iota(jnp.int32, (kt2, LANES), 0) // SEL_BLOCK
    e_lane = lax.broadcasted_iota(jnp.int32, (kt2, LANES), 1)

    def sel_step(kt, mode):
        k0 = pl.multiple_of(kt * kt2, kt2)
        onehot = e_lane == kt * (kt2 // SEL_BLOCK) + e_blk
        if mode == "far":
            onehot = onehot | (e_lane == ns8)
        e_aug = jnp.where(onehot, 1.0, 0.0).astype(BF16)
        delta = t - SEL_STEP_TILES * kt
        idx = [jnp.clip(delta - i, 0, far_idx) for i in range(SEL_STEP_TILES)]
        for pair in range(NSA_N_KV // 2):
            k_aug = jnp.concatenate([kv_ref[pl.ds(WINDOW + k0, kt2), pair * LANES:(pair + 1) * LANES], e_aug], axis=1)
            for g in (2 * pair, 2 * pair + 1):
                s = jnp.dot(k_aug, qaug_ref[g], preferred_element_type=F32)
                if mode != "far":
                    s = s + jnp.concatenate([trel_ref[g, i] for i in idx], axis=0)
                if mode == "last":
                    s = mask_heads(s, t * TQ + qi2 >= k0 + kj2)
                s_ref[g] = s
        for g in range(NSA_N_KV):
            online_step(g, s_ref[g], kv_ref[pl.ds(WINDOW + k0, kt2), NSA_KV_W + g * dh:NSA_KV_W + (g + 1) * dh])

    n_steps = t // SEL_STEP_TILES + 1
    n_far_steps = jnp.maximum(t - far_idx + 1, 0) // SEL_STEP_TILES

    def far_body(kt, carry):
        sel_step(kt, "far")
        return carry

    def near_body(kt, carry):
        sel_step(kt, "near")
        return carry

    lax.fori_loop(0, n_far_steps, far_body, 0)
    lax.fori_loop(n_far_steps, n_steps - 1, near_body, 0)
    sel_step(n_steps - 1, "last")
    for g in range(NSA_N_KV):
        ob_ref[1, g] = acc_ref[g] / l_ref[g]

    assert SEL_STEP_TILES >= 2

    reset()
    row0 = pl.multiple_of(t * TQ, TQ)
    for deltas in ((0,), (2, 1), (4, 3)):
        n_keys = KT * len(deltas)
        kjw = lax.broadcasted_iota(jnp.int32, (n_keys, TQ), 0)
        qiw = lax.broadcasted_iota(jnp.int32, (n_keys, TQ), 1)
        dist = deltas[0] * KT + qiw - kjw
        ok = (dist >= 0) & (dist <= WINDOW) & (row0 + kjw >= deltas[0] * KT)
        k0 = row0 + (WINDOW - deltas[0] * KT)
        for g in range(NSA_N_KV):
            k_t = kv_ref[pl.ds(k0, n_keys), 2 * NSA_KV_W + g * dh:2 * NSA_KV_W + (g + 1) * dh]
            s = jnp.dot(k_t, qts[g], preferred_element_type=F32)
            s = s + jnp.concatenate([trel_ref[g, d] for d in deltas], axis=0)
            s_ref[g, 0:n_keys, :] = mask_heads(s, ok)
        for g in range(NSA_N_KV):
            online_step(g, s_ref[g, 0:n_keys, :],
                        kv_ref[pl.ds(k0, n_keys), 3 * NSA_KV_W + g * dh:3 * NSA_KV_W + (g + 1) * dh])
    for g in range(NSA_N_KV):
        ob_ref[2, g] = acc_ref[g] / l_ref[g]

    gate = jax.nn.sigmoid(gate_ref[...])
    z_refs = (z0_ref, z1_ref, z2_ref)
    for h in range(NSA_N_HEADS):
        g, r = divmod(h, NSA_R)
        mixed = None
        for c in range(N_BRANCH):
            o_hc = ob_ref[c, g, :, r * TQ:(r + 1) * TQ]
            term = o_hc * _silu(z_refs[c][h * dh:(h + 1) * dh, :]) * gate[h * N_BRANCH + c:h * N_BRANCH + c + 1, :]
            mixed = term if mixed is None else mixed + term
        o_ref[h * dh:(h + 1) * dh, :] = mixed.astype(o_ref.dtype)


def _nsa_prompt_tables(rel_bias):
    n_dist = max((N_REL_TILES + 1) * KT, CMP_STRIDE * CMP_FRONT + TQ)
    vec = _bias_by_dist(rel_bias, n_dist)
    tiles = [_lanes_by_group(_toeplitz(vec, delta * KT, KT, TQ, 1)) for delta in range(N_REL_TILES)]
    far = jnp.broadcast_to(rel_bias.astype(F32)[REL_BUCKETS - 1].reshape(NSA_N_HEADS, 1, 1), (NSA_N_HEADS, KT, TQ))
    far = _lanes_by_group(far)
    trel = jnp.stack(tiles + [far], axis=1) * LOG2E
    tcn = _lanes_by_group(_toeplitz(vec, CMP_STRIDE * CMP_FRONT - (CMP_BLOCK - 1), CMP_NEAR, TQ, CMP_STRIDE))
    return trel, tcn, far[:, :SUBLANES, :]


def _nsa_layer_prompt(xp, prm):
    norm_w, w_in, pe, w1, w2, w_out, rel_bias = prm
    bsz, t, d = xp.shape
    assert t % (SEL_STEP_TILES * KT) == 0 and t % PAGE_SIZE == 0 and t >= WINDOW
    n_tiles = t // TQ
    ns = t // SEL_BLOCK
    assert ns <= LANES
    x2 = xp.reshape(bsz * t, d)
    w = _nsa_in_weights(w_in)
    ut = _norm_matmul_t(x2, norm_w, w.T.astype(BF16), bsz, _pick_tn(NSA_N_PAD, 1152))
    w_kv = jnp.concatenate([w[:, COL_KV + 2 * NSA_KV_W:COL_KV + 4 * NSA_KV_W], w[:, COL_KVW:COL_KVW + 2 * NSA_KV_W]],
                           axis=1).astype(BF16)
    kvn = _norm_matmul(x2, norm_w, w_kv, 4 * NSA_KV_W, out_dtype=BF16)
    t_pad = WINDOW + t
    kvn = jnp.pad(kvn.reshape(bsz, t, 4 * NSA_KV_W), ((0, 0), (WINDOW, 0), (0, 0))).reshape(bsz * t_pad, 4 * NSA_KV_W)
    n_pages = t // PAGE_SIZE
    cmp_blk = COL_KV // (2 * NSA_KV_W)
    kc, vc = _compress(ut, lambda b, p, ids: (b, cmp_blk, p), bsz, n_pages, jnp.zeros((1, 1), jnp.int32), pe, w1, w2)
    nh = kc.shape[1]
    n_far = _round_up(CMP_FRONT + nh, LANES)
    padc = ((0, 0), (CMP_FRONT, n_far - CMP_FRONT - nh), (0, 0))
    kcp = jnp.pad(kc, padc)
    vcp = jnp.pad(vc, padc)
    ov = np.zeros((n_far, LANES), np.float32)
    ov[CMP_FRONT:CMP_FRONT + nh - 1, :ns] = _overlap_matrix(nh - 1, ns)
    trel, tcn, cfar = _nsa_prompt_tables(rel_bias)
    rq = NSA_R * TQ
    const2 = lambda b, i: (0, 0)
    const3 = lambda b, i: (0, 0, 0)
    const4 = lambda b, i: (0, 0, 0, 0)
    zb = COL_Z // NSA_Q_W
    once = dict(pipeline_mode=pl.Buffered(1))
    mixed_t = pl.pallas_call(
        functools.partial(_nsa_attn_kernel, n_sel=ns, n_cmp_rows=nh),
        out_shape=jax.ShapeDtypeStruct((bsz, NSA_Q_W, t), BF16),
        grid=(bsz, n_tiles),
        in_specs=[
            pl.BlockSpec((None, NSA_Q_W, TQ), lambda b, i: (b, 0, i)),
            pl.BlockSpec((t_pad, 4 * NSA_KV_W), lambda b, i: (b, 0), **once),
            pl.BlockSpec((1, n_far, NSA_KV_W), lambda b, i: (b, 0, 0), **once),
            pl.BlockSpec((1, n_far, NSA_KV_W), lambda b, i: (b, 0, 0), **once),
            pl.BlockSpec((n_far, LANES), const2, **once),
            pl.BlockSpec((NSA_N_KV, N_REL_TILES + 1, KT, rq), const4, **once),
            pl.BlockSpec((NSA_N_KV, CMP_NEAR, rq), const3, **once),
            pl.BlockSpec((NSA_N_KV, SUBLANES, rq), const3, **once),
            pl.BlockSpec((None, LANES, TQ), lambda b, i: (b, COL_GATE // LANES, i)),
            pl.BlockSpec((None, NSA_Q_W, TQ), lambda b, i: (b, zb, i)),
            pl.BlockSpec((None, NSA_Q_W, TQ), lambda b, i: (b, zb + 1, i)),
            pl.BlockSpec((None, NSA_Q_W, TQ), lambda b, i: (b, zb + 2, i)),
        ],
        out_specs=pl.BlockSpec((None, NSA_Q_W, TQ), lambda b, i: (b, 0, i)),
        scratch_shapes=[
            pltpu.VMEM((NSA_N_KV, 2 * LANES, rq), BF16),
            pltpu.VMEM((LANES, TQ), F32),
            pltpu.VMEM((NSA_N_KV, 1, rq), F32),
            pltpu.VMEM((NSA_N_KV, 1, rq), F32),
            pltpu.VMEM((NSA_N_KV, NSA_HEAD_DIM, rq), F32),
            pltpu.VMEM((N_BRANCH, NSA_N_KV, NSA_HEAD_DIM, rq), F32),
            pltpu.VMEM((NSA_N_KV, SEL_STEP_TILES * KT, rq), F32),
        ],
        compiler_params=_cparams(("parallel", "arbitrary")),
        name="nsa_attn",
    )(ut, kvn, kcp, vcp, jnp.asarray(ov), trel, tcn, cfar, ut, ut, ut, ut)
    out = _matmul_res_t(mixed_t, w_out.astype(BF16), x2).reshape(bsz, t, d)
    return out, ut


def _block_diag_q(q2):
    hg = lax.broadcasted_iota(jnp.int32, q2.shape, 0) // NSA_R
    return jnp.concatenate([jnp.where(hg == g, q2, 0.0) for g in range(NSA_N_KV)], axis=1)


def _group_lanes(o):
    dh = NSA_HEAD_DIM
    hg = lax.broadcasted_iota(jnp.int32, (o.shape[0], dh), 0) // NSA_R
    return sum(jnp.where(hg == g, o[:, g * dh:(g + 1) * dh], 0.0) for g in range(NSA_N_KV))


def _nsa_step_kernel(*refs, n_pp, past_len, n_cmp, n_sel_pad):
    page_refs = refs[1:1 + n_pp]
    (q_ref, kvn_ref, kwn_ref, gate_ref, z_ref, kc_ref, vc_ref, win_ref, ovp_ref, bsel_ref, bcmp_ref, bwin_ref, b0_ref,
     o_ref, sel_ref, m_ref, l_ref, acc_ref, oc_ref) = refs[1 + n_pp:]
    step = pl.program_id(1)
    nh = NSA_N_HEADS
    dh = NSA_HEAD_DIM
    qbd = _block_diag_q(q_ref[0] * (dh ** -0.5))
    qbd16 = qbd.astype(BF16)
    cur = past_len // SEL_BLOCK

    @pl.when(step == 0)
    def _():
        s_c = lax.dot_general(qbd16, kc_ref[0].astype(BF16), NT_DIMS, preferred_element_type=F32) + bcmp_ref[...]
        cid = lax.broadcasted_iota(jnp.int32, s_c.shape, 1)
        (p_c,) = _masked_softmax_parts([s_c], [cid < n_cmp], axis=1)
        oc_ref[...] = _group_lanes(jnp.dot(p_c.astype(BF16), vc_ref[0].astype(BF16), preferred_element_type=F32))
        gr = lax.broadcasted_iota(jnp.int32, (SUBLANES, nh), 0)
        gh = lax.broadcasted_iota(jnp.int32, (SUBLANES, nh), 1) // NSA_R
        pg = jnp.dot((gr == gh).astype(F32), p_c, precision=HIGHEST, preferred_element_type=F32)
        imp = jnp.dot(pg, ovp_ref[...], precision=HIGHEST, preferred_element_type=F32)
        sid = lax.broadcasted_iota(jnp.int32, imp.shape, 1)
        forced = (sid == 0) | (sid == cur) | (sid == cur - 1)
        imp = jnp.where(forced, FORCE_SCORE, imp)
        imp = jnp.where(sid * SEL_BLOCK <= past_len, imp, -FORCE_SCORE)
        er = lax.broadcasted_iota(jnp.int32, (n_sel_pad, n_sel_pad), 0)
        ec = lax.broadcasted_iota(jnp.int32, (n_sel_pad, n_sel_pad), 1)
        rows = []
        for g in range(NSA_N_KV):
            row = imp[g:g + 1, :]
            col = jnp.sum(jnp.where(er == ec, row, 0.0), axis=1, keepdims=True)
            ahead = (col > row) | ((col == row) & (er < ec))
            cnt = jnp.sum(ahead.astype(jnp.int32), axis=0, keepdims=True)
            sel = ((cnt < SEL_TOPK) & (row > -0.5 * FORCE_SCORE)).astype(F32)
            rows.extend([sel] * NSA_R)
        sel_ref[...] = jnp.concatenate(rows, axis=0)
        m_ref[...] = jnp.full(m_ref.shape, NEG, F32)
        l_ref[...] = jnp.zeros(l_ref.shape, F32)
        acc_ref[...] = jnp.zeros(acc_ref.shape, F32)

    def online(s):
        m_old = m_ref[...]
        m_new = jnp.maximum(m_old, jnp.max(s, axis=-1, keepdims=True))
        alpha = jnp.exp(m_old - m_new)
        pr = jnp.exp(s - m_new)
        l_ref[...] = l_ref[...] * alpha + jnp.sum(pr, axis=-1, keepdims=True)
        m_ref[...] = m_new
        return alpha, pr

    k_all = jnp.concatenate([r[0:NSA_KV_W, :] for r in page_refs], axis=1).astype(BF16)
    v_all = jnp.concatenate([r[NSA_KV_W:2 * NSA_KV_W, :] for r in page_refs], axis=1).astype(BF16)
    n_tok = n_pp * PAGE_SIZE
    s = jnp.dot(qbd16, k_all, preferred_element_type=F32) + bsel_ref[0]
    eb = lax.broadcasted_iota(jnp.int32, (n_sel_pad, n_tok), 0)
    ej = lax.broadcasted_iota(jnp.int32, (n_sel_pad, n_tok), 1)
    expand = (eb == step * (n_tok // SEL_BLOCK) + ej // SEL_BLOCK).astype(BF16)
    chosen = jnp.dot(sel_ref[...].astype(BF16), expand, preferred_element_type=F32)
    s = jnp.where(chosen > 0.5, s, NEG)
    alpha, pr = online(s)
    acc_ref[...] = acc_ref[...] * alpha + lax.dot_general(pr.astype(BF16), v_all, NT_DIMS, preferred_element_type=F32)

    @pl.when(step == pl.num_programs(1) - 1)
    def _():
        b0 = b0_ref[:, 0:1]
        kn = kvn_ref[0, :, 2 * NSA_KV_W:3 * NSA_KV_W]
        vn = kvn_ref[0, :, 3 * NSA_KV_W:4 * NSA_KV_W]
        s_n = jnp.sum(qbd * kn, axis=-1, keepdims=True) + b0
        s_n = jnp.where(sel_ref[:, cur:cur + 1] > 0.5, s_n, NEG)
        alpha, pr = online(s_n)
        acc = acc_ref[...] * alpha + pr * vn
        o_s = _group_lanes(acc / l_ref[...])
        wk = win_ref[0, 0:NSA_KV_W, :].astype(BF16)
        wv = win_ref[0, NSA_KV_W:2 * NSA_KV_W, :].astype(BF16)
        n_win = win_ref.shape[2]
        s_w = jnp.dot(qbd16, wk, preferred_element_type=F32) + bwin_ref[...]
        wi = lax.broadcasted_iota(jnp.int32, s_w.shape, 1)
        s_wn = jnp.sum(qbd * kwn_ref[0, :, 0:NSA_KV_W], axis=-1, keepdims=True) + b0
        p_w, p_wn = _masked_softmax_parts([s_w, s_wn], [n_win - wi <= WINDOW, jnp.full(s_wn.shape, True)], axis=1)
        o_w = _group_lanes(lax.dot_general(p_w.astype(BF16), wv, NT_DIMS, preferred_element_type=F32)
                           + p_wn * kwn_ref[0, :, NSA_KV_W:2 * NSA_KV_W])
        gate = jax.nn.sigmoid(gate_ref[0])
        mixed = (oc_ref[...] * _silu(z_ref[0, 0]) * gate[:, 0:1] + o_s * _silu(z_ref[0, 1]) * gate[:, 1:2]
                 + o_w * _silu(z_ref[0, 2]) * gate[:, 2:3])
        o_ref[0] = mixed


def _nsa_layer_sample(xs_in, cache_t, win_t, page_table, prm, layer, n_layers):
    norm_w, w_in, pe, w1, w2, w_out, rel_bias = prm
    bsz, t, d = xs_in.shape
    assert t == 1
    n_pages = page_table.shape[1]
    past_len = n_pages * PAGE_SIZE
    n_win = win_t.shape[2]
    nh, dh = NSA_N_HEADS, NSA_HEAD_DIM
    x2 = xs_in.reshape(bsz, d)
    u = _norm_matmul(x2, norm_w, _nsa_in_weights(w_in).astype(BF16), _pick_tn(NSA_N_PAD, 1152))
    ids = (page_table * n_layers + layer).astype(jnp.int32)
    kc, vc = _compress(cache_t, lambda b, p, ids: (ids[b, p], 0, 0), bsz, n_pages, ids, pe, w1, w2)
    n_cmp_pad = kc.shape[1]
    n_cmp = n_cmp_pad - 1
    ns = past_len // SEL_BLOCK + 1
    n_sel_pad = _round_up(ns, LANES)
    ov = np.zeros((n_cmp_pad, n_sel_pad), np.float32)
    ov[:n_cmp, :ns] = _overlap_matrix(n_cmp, ns)
    n_pp = math.gcd(PAGES_PER_STEP, n_pages)
    n_steps = n_pages // n_pp
    n_tok = n_pp * PAGE_SIZE
    vec = _bias_by_dist(rel_bias, past_len + 1)
    bsel = jnp.moveaxis(jnp.flip(vec[:, 1:], axis=1).reshape(nh, n_steps, n_tok), 0, 1)
    bcmp = jnp.flip(vec[:, :past_len - (CMP_BLOCK - 1) + 1], axis=1)[:, ::CMP_STRIDE]
    bcmp = jnp.pad(bcmp, ((0, 0), (0, n_cmp_pad - bcmp.shape[1])))
    bwin = jnp.flip(vec[:, 1:n_win + 1], axis=1)
    b0 = jnp.broadcast_to(vec[:, 0:1], (nh, LANES))
    q3 = u[:, COL_Q:COL_Q + NSA_Q_W].reshape(bsz, nh, dh)
    kv_new = u[:, COL_KV:COL_KV + 4 * NSA_KV_W]
    kw_new = u[:, COL_KVW:COL_KVW + 2 * NSA_KV_W]
    gate = u[:, COL_GATE:COL_GATE + nh * N_BRANCH].reshape(bsz, nh, N_BRANCH)
    z4 = u[:, COL_Z:COL_Z + N_BRANCH * NSA_Q_W].reshape(bsz, N_BRANCH, nh, dh)
    per_b3 = lambda b, s, ids: (b, 0, 0)
    const2 = lambda b, s, ids: (0, 0)
    page_specs = [pl.BlockSpec((None, 2 * NSA_KV_W, PAGE_SIZE),
                               functools.partial(lambda b, s, ids, k: (ids[b, s * n_pp + k], 1, 0), k=k))
                  for k in range(n_pp)]
    grid_spec = pltpu.PrefetchScalarGridSpec(
        num_scalar_prefetch=1,
        grid=(bsz, n_steps),
        in_specs=page_specs + [
            pl.BlockSpec((1, nh, dh), per_b3),
            pl.BlockSpec((1, 1, 4 * NSA_KV_W), per_b3),
            pl.BlockSpec((1, 1, 2 * NSA_KV_W), per_b3),
            pl.BlockSpec((1, nh, N_BRANCH), per_b3),
            pl.BlockSpec((1, N_BRANCH, nh, dh), lambda b, s, ids: (b, 0, 0, 0)),
            pl.BlockSpec((1, n_cmp_pad, NSA_KV_W), per_b3),
            pl.BlockSpec((1, n_cmp_pad, NSA_KV_W), per_b3),
            pl.BlockSpec((1, 2 * NSA_KV_W, n_win), per_b3),
            pl.BlockSpec((n_cmp_pad, n_sel_pad), const2),
            pl.BlockSpec((1, nh, n_tok), lambda b, s, ids: (s, 0, 0)),
            pl.BlockSpec((nh, n_cmp_pad), const2),
            pl.BlockSpec((nh, n_win), const2),
            pl.BlockSpec((nh, LANES), const2),
        ],
        out_specs=pl.BlockSpec((1, nh, dh), per_b3),
        scratch_shapes=[
            pltpu.VMEM((nh, n_sel_pad), F32),
            pltpu.VMEM((nh, 1), F32),
            pltpu.VMEM((nh, 1), F32),
            pltpu.VMEM((nh, NSA_KV_W), F32),
            pltpu.VMEM((nh, dh), F32),
        ],
    )
    mixed = pl.pallas_call(
        functools.partial(_nsa_step_kernel, n_pp=n_pp, past_len=past_len, n_cmp=n_cmp, n_sel_pad=n_sel_pad),
        out_shape=jax.ShapeDtypeStruct((bsz, nh, dh), F32),
        grid_spec=grid_spec,
        compiler_params=_cparams(("parallel", "arbitrary")),
        name="nsa_step",
    )(ids, *([cache_t] * n_pp), q3, kv_new.reshape(bsz, 1, -1), kw_new.reshape(bsz, 1, -1), gate, z4, kc, vc, win_t,
      jnp.asarray(ov), bsel, bcmp, bwin, b0)
    out = _matmul_res(mixed.reshape(bsz, NSA_Q_W).astype(BF16), w_out.astype(BF16), x2).reshape(bsz, 1, d)
    return out, kv_new, kw_new


def kernel(x_prompt, x_sample, state_ssm, state_conv, cache_kv, cache_win, page_table, rel_bias, final_norm,
           ssd_norm, ssd_w_in, ssd_conv_w, ssd_conv_b, ssd_dt_bias, ssd_a_log, ssd_d, ssd_gnorm, ssd_w_out,
           nsa_norm, nsa_w_in, nsa_cmp_pe, nsa_cmp_w1, nsa_cmp_w2, nsa_w_out):
    depth = ssd_norm.shape[0] + nsa_norm.shape[0]
    g, dh = NSA_N_KV, NSA_HEAD_DIM
    xp, xs = x_prompt, x_sample
    bp, tp, d = xp.shape
    bs = xs.shape[0]
    n_phys, n_nsa = cache_kv.shape[:2]
    win_len_s = cache_win.shape[2]
    win_len_p = min(WINDOW, tp)
    cache_t = cache_kv.transpose(0, 1, 3, 4, 5, 2).reshape(n_phys * n_nsa, 4 * NSA_KV_W, PAGE_SIZE)
    cwin_t = cache_win.transpose(0, 1, 3, 4, 5, 2).reshape(n_nsa, bs, 2 * NSA_KV_W, win_len_s)
    ssm_p, conv_p, kv_p, win_p = [], [], [], []
    ssm_s, conv_s, kv_s, win_s = [], [], [], []
    for layer in range(depth):
        j = layer // 2
        if layer % 2 == 0:
            prm = (ssd_norm[j], ssd_w_in[j], ssd_conv_w[j], ssd_conv_b[j], ssd_dt_bias[j], ssd_a_log[j],
                   ssd_d[j], ssd_gnorm[j], ssd_w_out[j])
            xp, c, s = _ssd_layer_prompt(xp, prm)
            conv_p.append(c)
            ssm_p.append(s)
            xs, c, s = _ssd_layer_sample(xs, state_conv[j], state_ssm[j], prm)
            conv_s.append(c)
            ssm_s.append(s)
        else:
            prm = (nsa_norm[j], nsa_w_in[j], nsa_cmp_pe[j], nsa_cmp_w1[j], nsa_cmp_w2[j], nsa_w_out[j], rel_bias)
            xp, ut = _nsa_layer_prompt(xp, prm)
            kv_p.append(ut[:, COL_KV:COL_KV + 4 * NSA_KV_W, :])
            win_p.append(ut[:, COL_KVW:COL_KVW + 2 * NSA_KV_W, tp - win_len_p:])
            xs, kv_new, kw_new = _nsa_layer_sample(xs, cache_t, cwin_t[j], page_table, prm, j, n_nsa)
            kv_s.append(kv_new.reshape(bs, 1, 4, g, dh))
            win_s.append(jnp.concatenate([cwin_t[j][:, :, 1:], kw_new[:, :, None]], axis=2))
    y_prompt = _rmsnorm(xp.reshape(bp * tp, d), final_norm).reshape(bp, tp, d)
    y_sample = _rmsnorm(xs.reshape(bs, d), final_norm).reshape(bs, 1, d)
    kv_prompt = jnp.stack(kv_p, axis=1).reshape(bp, n_nsa, 4, g, dh, tp).transpose(0, 5, 1, 2, 3, 4)
    win_prompt = jnp.stack(win_p).reshape(n_nsa, bp, 2, g, dh, win_len_p).transpose(0, 1, 5, 2, 3, 4)
    win_sample = jnp.stack(win_s).reshape(n_nsa, bs, 2, g, dh, win_len_s).transpose(0, 1, 5, 2, 3, 4)
    return (y_prompt, y_sample, jnp.stack(ssm_p), jnp.stack(conv_p), kv_prompt, win_prompt,
            jnp.stack(ssm_s), jnp.stack(conv_s), jnp.stack(kv_s, axis=2), win_sample)
```

```python
import functools
import math

import jax
import jax.numpy as jnp
import numpy as np
from jax import lax
from jax.experimental import pallas as pl
from jax.experimental.pallas import tpu as pltpu

F32 = jnp.float32
BF16 = jnp.bfloat16
HIGHEST = lax.Precision.HIGHEST

NORM_EPS = 1e-6
SSD_HEAD_DIM = 64
SSD_N_GROUPS = 8
SSD_D_STATE = 128
SSD_CONV_W = 4
SSD_CHUNK = 128
NSA_N_HEADS = 16
NSA_HEAD_DIM = 64
NSA_N_KV = 4
NSA_R = NSA_N_HEADS // NSA_N_KV
N_BRANCH = 3
CMP_BLOCK = 32
CMP_STRIDE = 16
CMP_HIDDEN = 128
SEL_BLOCK = 64
SEL_TOPK = 16
WINDOW = 512
FORCE_SCORE = 1e4
REL_BUCKETS = 32
REL_MAX_DIST = 1024
PAGE_SIZE = 128

LANES = 128
SUBLANES = 8
VMEM_LIMIT = 56 * 1024 * 1024

NEG = -1e30
LOG2E = math.log2(math.e)
NT_DIMS = (((1,), (1,)), ((), ()))
TN_DIMS = (((0,), (0,)), ((), ()))


def _cparams(sem):
    return pltpu.CompilerParams(dimension_semantics=sem, vmem_limit_bytes=VMEM_LIMIT)


def _round_up(x, m):
    return (x + m - 1) // m * m


def _silu(x):
    return x * jax.nn.sigmoid(x)


def _pick_tm(m, cap, quantum=SUBLANES):
    if m <= cap:
        return m
    for tm in range(cap // quantum * quantum, 0, -quantum):
        if m % tm == 0:
            return tm
    raise ValueError(f"no row tile for {m}")


def _pick_tn(n, cap=1024):
    best = LANES
    for tn in range(LANES, cap + 1, LANES):
        if n % tn == 0:
            best = tn
    return best


def _norm_matmul_kernel(x_ref, nw_ref, w_ref, o_ref, xn_ref, *, transpose_out):
    @pl.when(pl.program_id(1) == 0)
    def _():
        x = x_ref[...]
        ms = jnp.mean(x * x, axis=-1, keepdims=True)
        xn = x * lax.rsqrt(ms + NORM_EPS) * nw_ref[...]
        if transpose_out:
            xn_ref[...] = xn.T.astype(BF16)
        else:
            xn_ref[...] = xn.astype(BF16)

    if transpose_out:
        o_ref[...] = jnp.dot(w_ref[...], xn_ref[...], preferred_element_type=F32).astype(o_ref.dtype)
    else:
        o_ref[...] = jnp.dot(xn_ref[...], w_ref[...], preferred_element_type=F32).astype(o_ref.dtype)


def _norm_matmul(x, nw, w, tn, out_dtype=F32):
    m, k = x.shape
    n = w.shape[1]
    tm = _pick_tm(m, 1024)
    assert n % tn == 0
    return pl.pallas_call(
        functools.partial(_norm_matmul_kernel, transpose_out=False),
        out_shape=jax.ShapeDtypeStruct((m, n), out_dtype),
        grid=(m // tm, n // tn),
        in_specs=[
            pl.BlockSpec((tm, k), lambda i, j: (i, 0)),
            pl.BlockSpec((1, k), lambda i, j: (0, 0)),
            pl.BlockSpec((k, tn), lambda i, j: (0, j)),
        ],
        out_specs=pl.BlockSpec((tm, tn), lambda i, j: (i, j)),
        scratch_shapes=[pltpu.VMEM((tm, k), BF16)],
        compiler_params=_cparams(("parallel", "arbitrary")),
        name="norm_matmul",
    )(x, nw.reshape(1, k), w)


def _norm_matmul_t(x, nw, wt, bsz, tn):
    m, k = x.shape
    n = wt.shape[0]
    t = m // bsz
    tm = _pick_tm(t, 1024, LANES)
    assert n % tn == 0
    tpb = t // tm
    return pl.pallas_call(
        functools.partial(_norm_matmul_kernel, transpose_out=True),
        out_shape=jax.ShapeDtypeStruct((bsz, n, t), F32),
        grid=(m // tm, n // tn),
        in_specs=[
            pl.BlockSpec((tm, k), lambda i, j: (i, 0)),
            pl.BlockSpec((1, k), lambda i, j: (0, 0)),
            pl.BlockSpec((tn, k), lambda i, j: (j, 0)),
        ],
        out_specs=pl.BlockSpec((None, tn, tm), lambda i, j: (i // tpb, j, i % tpb)),
        scratch_shapes=[pltpu.VMEM((k, tm), BF16)],
        compiler_params=_cparams(("parallel", "arbitrary")),
        name="norm_matmul_t",
    )(x, nw.reshape(1, k), wt)


def _matmul_res_kernel(y_ref, w_ref, x_ref, o_ref, *, y_transposed):
    dims = TN_DIMS if y_transposed else (((1,), (0,)), ((), ()))
    o_ref[...] = x_ref[...] + lax.dot_general(y_ref[...], w_ref[...], dims, preferred_element_type=F32)


def _matmul_res(y, w, x):
    m, k = y.shape
    n = w.shape[1]
    tm = min(m, 512)
    assert m % tm == 0
    return pl.pallas_call(
        functools.partial(_matmul_res_kernel, y_transposed=False),
        out_shape=jax.ShapeDtypeStruct((m, n), F32),
        grid=(m // tm,),
        in_specs=[
            pl.BlockSpec((tm, k), lambda i: (i, 0)),
            pl.BlockSpec((k, n), lambda i: (0, 0)),
            pl.BlockSpec((tm, n), lambda i: (i, 0)),
        ],
        out_specs=pl.BlockSpec((tm, n), lambda i: (i, 0)),
        compiler_params=_cparams(("parallel",)),
        name="matmul_res",
    )(y, w, x)


def _matmul_res_t(yt, w, x):
    bsz, k, t = yt.shape
    n = w.shape[1]
    tm = min(t, 512)
    assert t % tm == 0
    tpb = t // tm
    return pl.pallas_call(
        functools.partial(_matmul_res_kernel, y_transposed=True),
        out_shape=jax.ShapeDtypeStruct((bsz * t, n), F32),
        grid=(bsz * tpb,),
        in_specs=[
            pl.BlockSpec((None, k, tm), lambda i: (i // tpb, 0, i % tpb)),
            pl.BlockSpec((k, n), lambda i: (0, 0)),
            pl.BlockSpec((tm, n), lambda i: (i, 0)),
        ],
        out_specs=pl.BlockSpec((tm, n), lambda i: (i, 0)),
        compiler_params=_cparams(("parallel",)),
        name="matmul_res_t",
    )(yt, w, x)


def _rmsnorm_kernel(x_ref, nw_ref, o_ref):
    x = x_ref[...]
    ms = jnp.mean(x * x, axis=-1, keepdims=True)
    o_ref[...] = x * lax.rsqrt(ms + NORM_EPS) * nw_ref[...]


def _rmsnorm(x, nw):
    m, k = x.shape
    tm = min(m, 512)
    assert m % tm == 0
    return pl.pallas_call(
        _rmsnorm_kernel,
        out_shape=jax.ShapeDtypeStruct((m, k), F32),
        grid=(m // tm,),
        in_specs=[pl.BlockSpec((tm, k), lambda i: (i, 0)), pl.BlockSpec((1, k), lambda i: (0, 0))],
        out_specs=pl.BlockSpec((tm, k), lambda i: (i, 0)),
        compiler_params=_cparams(("parallel",)),
        name="final_rmsnorm",
    )(x, nw.reshape(1, k))


def _ssd_chunk_kernel(z_ref, x_ref, bc_ref, dt_ref, cprev_ref, sprev_ref, cw_ref, cb_ref, dtb_ref, alog_ref,
                      dskip_ref, gn_ref, y_ref, hlast_ref, xpad_ref, state_ref, *, d_inner, n_heads):
    lc = SSD_CHUNK
    g_n, n, p = SSD_N_GROUPS, SSD_D_STATE, SSD_HEAD_DIM
    r_n = n_heads // g_n
    c = pl.program_id(1)

    @pl.when(c == 0)
    def _():
        xpad_ref[0:SUBLANES, :] = cprev_ref[0]
        state_ref[...] = sprev_ref[0]

    @pl.when(c > 0)
    def _():
        xpad_ref[0:SUBLANES, :] = xpad_ref[lc:lc + SUBLANES, :]

    xpad_ref[SUBLANES:SUBLANES + lc, 0:d_inner] = x_ref[...]
    xpad_ref[SUBLANES:SUBLANES + lc, d_inner:] = bc_ref[...]

    conv = cb_ref[...]
    for k in range(SSD_CONV_W):
        off = SUBLANES - (SSD_CONV_W - 1) + k
        conv = conv + xpad_ref[off:off + lc, :] * cw_ref[k:k + 1, :]
    act = _silu(conv)
    xs = act[:, :d_inner]
    bm = act[:, d_inner:d_inner + g_n * n]
    cm = act[:, d_inner + g_n * n:]

    dt = jax.nn.softplus(dt_ref[...] + dtb_ref[...])
    a = -jnp.exp(alog_ref[...])
    dta = dt * a
    row = lax.broadcasted_iota(jnp.int32, (lc, lc), 0)
    col = lax.broadcasted_iota(jnp.int32, (lc, lc), 1)
    causal = row >= col
    tril = causal.astype(F32)
    a_cs = jnp.dot(tril, dta, precision=HIGHEST, preferred_element_type=F32)
    a_cs_t = a_cs.T
    a_end = a_cs[lc - 1:lc, :]
    to_end = jnp.exp(a_end - a_cs)
    e_cs = jnp.exp(a_cs)
    e_end = jnp.exp(a_end)

    assert 2 * p == LANES and r_n % 2 == 0
    lane_lo = lax.broadcasted_iota(jnp.int32, (1, LANES), 1) < p

    def pair_lanes(v, k):
        return jnp.where(lane_lo, v[:, 2 * k:2 * k + 1], v[:, 2 * k + 1:2 * k + 2])

    zs = _silu(z_ref[...])
    gw = r_n * p
    ys = []
    for g in range(g_n):
        bg16 = bm[:, g * n:(g + 1) * n].astype(BF16)
        cg16 = cm[:, g * n:(g + 1) * n].astype(BF16)
        cb = lax.dot_general(cg16, bg16, NT_DIMS, preferred_element_type=F32)
        pairs = range(g * r_n // 2, (g + 1) * r_n // 2)
        gl = slice(g * gw, (g + 1) * gw)
        xs_g = xs[:, gl]
        xdt_g = xs_g * jnp.concatenate([pair_lanes(dt, k) for k in pairs], axis=1)
        y_diag = []
        for i, k in enumerate(pairs):
            xdt_k = xdt_g[:, i * LANES:(i + 1) * LANES]
            acc = None
            for j, h in enumerate((2 * k, 2 * k + 1)):
                seg = a_cs[:, h:h + 1] - a_cs_t[h:h + 1, :]
                decay = jnp.exp(jnp.where(causal, seg, -jnp.inf))
                m_h = (cb * decay).astype(BF16)
                x_h = jnp.where(lane_lo if j == 0 else jnp.logical_not(lane_lo), xdt_k, 0.0).astype(BF16)
                d = jnp.dot(m_h, x_h, preferred_element_type=F32)
                acc = d if acc is None else acc + d
            y_diag.append(acc)
        y_diag = jnp.concatenate(y_diag, axis=1)
        sprev = state_ref[:, gl]
        y_off = (jnp.dot(cg16, sprev.astype(BF16), preferred_element_type=F32)
                 * jnp.concatenate([pair_lanes(e_cs, k) for k in pairs], axis=1))
        xw = (xdt_g * jnp.concatenate([pair_lanes(to_end, k) for k in pairs], axis=1)).astype(BF16)
        st = lax.dot_general(bg16, xw, TN_DIMS, preferred_element_type=F32)
        state_ref[:, gl] = sprev * jnp.concatenate([pair_lanes(e_end, k) for k in pairs], axis=1) + st
        yg = (y_diag + y_off + xs_g * dskip_ref[:, gl]) * zs[:, gl]
        ms = jnp.mean(yg * yg, axis=-1, keepdims=True)
        ys.append(yg * lax.rsqrt(ms + NORM_EPS))
    y = jnp.concatenate(ys, axis=1) * gn_ref[...]
    y_ref[...] = y.astype(y_ref.dtype)

    @pl.when(c == pl.num_programs(1) - 1)
    def _():
        hlast_ref[0] = state_ref[...]


def _pad_lanes(v, width=LANES):
    return jnp.pad(v.astype(F32), (0, width - v.shape[0])).reshape(1, width)


def _ssd_prompt_core(u, bsz, t, conv_prev8, ssm_prev, conv_w, conv_b, dt_bias, a_log, d_skip, gnorm):
    n_heads = dt_bias.shape[0]
    d_inner = n_heads * SSD_HEAD_DIM
    conv_dim = conv_w.shape[1]
    assert conv_dim == 2 * d_inner and t % SSD_CHUNK == 0
    lc = SSD_CHUNK
    nc = t // lc
    dt_blk = (d_inner + conv_dim) // LANES
    kern = functools.partial(_ssd_chunk_kernel, d_inner=d_inner, n_heads=n_heads)
    const = lambda b, c: (0, 0)
    state_t = ssm_prev.transpose(0, 3, 1, 2).reshape(bsz, SSD_D_STATE, d_inner)
    y, hlast_t = pl.pallas_call(
        kern,
        out_shape=(jax.ShapeDtypeStruct((bsz * t, d_inner), BF16),
                   jax.ShapeDtypeStruct((bsz, SSD_D_STATE, d_inner), F32)),
        grid=(bsz, nc),
        in_specs=[
            pl.BlockSpec((lc, d_inner), lambda b, c: (b * nc + c, 0)),
            pl.BlockSpec((lc, d_inner), lambda b, c: (b * nc + c, 1)),
            pl.BlockSpec((lc, d_inner), lambda b, c: (b * nc + c, 2)),
            pl.BlockSpec((lc, LANES), lambda b, c: (b * nc + c, dt_blk)),
            pl.BlockSpec((1, SUBLANES, conv_dim), lambda b, c: (b, 0, 0)),
            pl.BlockSpec((1, SSD_D_STATE, d_inner), lambda b, c: (b, 0, 0)),
            pl.BlockSpec((SSD_CONV_W, conv_dim), const),
            pl.BlockSpec((1, conv_dim), const),
            pl.BlockSpec((1, LANES), const),
            pl.BlockSpec((1, LANES), const),
            pl.BlockSpec((1, d_inner), const),
            pl.BlockSpec((1, d_inner), const),
        ],
        out_specs=(pl.BlockSpec((lc, d_inner), lambda b, c: (b * nc + c, 0)),
                   pl.BlockSpec((1, SSD_D_STATE, d_inner), lambda b, c: (b, 0, 0))),
        scratch_shapes=[pltpu.VMEM((lc + SUBLANES, conv_dim), F32),
                        pltpu.VMEM((SSD_D_STATE, d_inner), F32)],
        compiler_params=_cparams(("parallel", "arbitrary")),
        name="ssd_chunk",
    )(u, u, u, u, conv_prev8, state_t, conv_w, conv_b.reshape(1, conv_dim), _pad_lanes(dt_bias),
      _pad_lanes(a_log), jnp.repeat(d_skip.astype(F32), SSD_HEAD_DIM).reshape(1, d_inner), gnorm.reshape(1, d_inner))
    hlast = hlast_t.reshape(bsz, SSD_D_STATE, n_heads, SSD_HEAD_DIM).transpose(0, 2, 3, 1)
    return y, hlast


def _ssd_in_weights(w_in, d_inner, conv_dim):
    main = d_inner + conv_dim
    n_dt = w_in.shape[1] - main
    w = jnp.concatenate([w_in[:, :main], jnp.pad(w_in[:, main:], ((0, 0), (0, LANES - n_dt)))], axis=1)
    return w.astype(BF16)


def _ssd_layer_prompt(xp, prm):
    norm_w, w_in, conv_w, conv_b, dt_bias, a_log, d_skip, gnorm, w_out = prm
    bsz, t, d = xp.shape
    assert t >= SSD_CONV_W - 1
    n_heads = dt_bias.shape[0]
    d_inner = n_heads * SSD_HEAD_DIM
    conv_dim = conv_w.shape[1]
    w = _ssd_in_weights(w_in, d_inner, conv_dim)
    x2 = xp.reshape(bsz * t, d)
    u = _norm_matmul(x2, norm_w, w, _pick_tn(w.shape[1]))
    conv_prev8 = jnp.zeros((bsz, SUBLANES, conv_dim), F32)
    ssm_prev = jnp.zeros((bsz, n_heads, SSD_HEAD_DIM, SSD_D_STATE), F32)
    y, hlast = _ssd_prompt_core(u, bsz, t, conv_prev8, ssm_prev, conv_w, conv_b, dt_bias, a_log, d_skip, gnorm)
    out = _matmul_res(y, w_out.astype(BF16), x2).reshape(bsz, t, d)
    u3 = u.reshape(bsz, t, -1)
    new_conv = u3[:, t - (SSD_CONV_W - 1):, d_inner:d_inner + conv_dim]
    return out, new_conv, hlast


def _ssd_step_pre_kernel(u_ref, cst_ref, cw_ref, cb_ref, dtb_ref, alog_ref,
                         xs_ref, xdt_ref, dec_ref, bm_ref, cm_ref, ncst_ref, *, d_inner, n_heads):
    conv_dim = cw_ref.shape[1]
    gn = SSD_N_GROUPS * SSD_D_STATE
    xbc = u_ref[:, d_inner:d_inner + conv_dim]
    conv = cb_ref[...]
    for k in range(SSD_CONV_W - 1):
        conv = conv + cst_ref[k] * cw_ref[k:k + 1, :]
    conv = conv + xbc * cw_ref[SSD_CONV_W - 1:SSD_CONV_W, :]
    for k in range(SSD_CONV_W - 2):
        ncst_ref[k] = cst_ref[k + 1]
    ncst_ref[SSD_CONV_W - 2] = xbc
    act = _silu(conv)
    xs = act[:, :d_inner]
    dt = jax.nn.softplus(u_ref[:, d_inner + conv_dim:d_inner + conv_dim + LANES] + dtb_ref[...])
    dec = jnp.exp(dt * (-jnp.exp(alog_ref[...])))
    hrow = lax.broadcasted_iota(jnp.int32, (LANES, d_inner), 0)
    hcol = lax.broadcasted_iota(jnp.int32, (LANES, d_inner), 1) // SSD_HEAD_DIM
    expand = (hrow == hcol).astype(F32)
    dt_e = jnp.dot(dt, expand, precision=HIGHEST, preferred_element_type=F32)
    dec_e = jnp.dot(dec, expand, precision=HIGHEST, preferred_element_type=F32)
    xs_ref[...] = xs
    xdt_ref[...] = xs * dt_e
    dec_ref[...] = dec_e
    bm_ref[...] = act[:, d_inner:d_inner + gn]
    cm_ref[...] = act[:, d_inner + gn:]


def _ssd_step_state_kernel(s_ref, xdt_ref, dec_ref, b_ref, c_ref, snew_ref, y_ref, *, rows_per_group):
    for g in range(SSD_N_GROUPS):
        sl = slice(g * rows_per_group, (g + 1) * rows_per_group)
        s = s_ref[0, sl, :]
        bg = b_ref[0, g:g + 1, :]
        cg = c_ref[0, g:g + 1, :]
        xdt = xdt_ref[0, sl, :]
        dec = dec_ref[0, sl, :]
        cb = jnp.sum(bg * cg, axis=1, keepdims=True)
        y_off = jnp.sum(s * cg, axis=1, keepdims=True)
        snew_ref[0, sl, :] = s * dec + xdt * bg
        y_ref[0, sl, :] = cb * xdt + dec * y_off


def _ssd_step_post_kernel(y_ref, xs_ref, u_ref, dsk_ref, gn_ref, o_ref, *, d_inner, group_width):
    y = (y_ref[...] + xs_ref[...] * dsk_ref[...]) * _silu(u_ref[:, :d_inner])
    outs = []
    for g in range(SSD_N_GROUPS):
        yg = y[:, g * group_width:(g + 1) * group_width]
        ms = jnp.mean(yg * yg, axis=-1, keepdims=True)
        outs.append(yg * lax.rsqrt(ms + NORM_EPS))
    o_ref[...] = (jnp.concatenate(outs, axis=1) * gn_ref[...]).astype(o_ref.dtype)


def _ssd_layer_sample(xs_in, conv_state, ssm_state, prm):
    norm_w, w_in, conv_w, conv_b, dt_bias, a_log, d_skip, gnorm, w_out = prm
    bsz, t, d = xs_in.shape
    assert t == 1
    n_heads = dt_bias.shape[0]
    d_inner = n_heads * SSD_HEAD_DIM
    conv_dim = conv_w.shape[1]
    gn = SSD_N_GROUPS * SSD_D_STATE
    w = _ssd_in_weights(w_in, d_inner, conv_dim)
    x2 = xs_in.reshape(bsz, d)
    u = _norm_matmul(x2, norm_w, w, _pick_tn(w.shape[1]))
    cst = conv_state.transpose(1, 0, 2)
    pre = functools.partial(_ssd_step_pre_kernel, d_inner=d_inner, n_heads=n_heads)
    xs, xdt, dec, bm, cm, ncst = pl.pallas_call(
        pre,
        out_shape=(jax.ShapeDtypeStruct((bsz, d_inner), F32), jax.ShapeDtypeStruct((bsz, d_inner), F32),
                   jax.ShapeDtypeStruct((bsz, d_inner), F32), jax.ShapeDtypeStruct((bsz, gn), F32),
                   jax.ShapeDtypeStruct((bsz, gn), F32), jax.ShapeDtypeStruct(cst.shape, F32)),
        compiler_params=pltpu.CompilerParams(vmem_limit_bytes=VMEM_LIMIT),
        name="ssd_step_pre",
    )(u, cst, conv_w, conv_b.reshape(1, conv_dim), _pad_lanes(dt_bias), _pad_lanes(a_log))
    rows = n_heads * SSD_HEAD_DIM
    rpg = rows // SSD_N_GROUPS
    st = functools.partial(_ssd_step_state_kernel, rows_per_group=rpg)
    snew, ycol = pl.pallas_call(
        st,
        out_shape=(jax.ShapeDtypeStruct((bsz, rows, SSD_D_STATE), F32), jax.ShapeDtypeStruct((bsz, rows, 1), F32)),
        grid=(bsz,),
        in_specs=[
            pl.BlockSpec((1, rows, SSD_D_STATE), lambda b: (b, 0, 0)),
            pl.BlockSpec((1, rows, 1), lambda b: (b, 0, 0)),
            pl.BlockSpec((1, rows, 1), lambda b: (b, 0, 0)),
            pl.BlockSpec((1, SSD_N_GROUPS, SSD_D_STATE), lambda b: (b, 0, 0)),
            pl.BlockSpec((1, SSD_N_GROUPS, SSD_D_STATE), lambda b: (b, 0, 0)),
        ],
        out_specs=(pl.BlockSpec((1, rows, SSD_D_STATE), lambda b: (b, 0, 0)),
                   pl.BlockSpec((1, rows, 1), lambda b: (b, 0, 0))),
        compiler_params=_cparams(("parallel",)),
        name="ssd_step_state",
    )(ssm_state.reshape(bsz, rows, SSD_D_STATE), xdt.reshape(bsz, rows, 1), dec.reshape(bsz, rows, 1),
      bm.reshape(bsz, SSD_N_GROUPS, SSD_D_STATE), cm.reshape(bsz, SSD_N_GROUPS, SSD_D_STATE))
    post = functools.partial(_ssd_step_post_kernel, d_inner=d_inner, group_width=d_inner // SSD_N_GROUPS)
    y = pl.pallas_call(
        post,
        out_shape=jax.ShapeDtypeStruct((bsz, d_inner), BF16),
        compiler_params=pltpu.CompilerParams(vmem_limit_bytes=VMEM_LIMIT),
        name="ssd_step_post",
    )(ycol.reshape(bsz, rows), xs, u, jnp.repeat(d_skip.astype(F32), SSD_HEAD_DIM).reshape(1, d_inner),
      gnorm.reshape(1, d_inner))
    out = _matmul_res(y, w_out.astype(BF16), x2).reshape(bsz, 1, d)
    return out, ncst.transpose(1, 0, 2), snew.reshape(ssm_state.shape)


NSA_Q_W = NSA_N_HEADS * NSA_HEAD_DIM
NSA_KV_W = NSA_N_KV * NSA_HEAD_DIM
COL_Q = 0
COL_KV = NSA_Q_W
COL_Z = COL_KV + 4 * NSA_KV_W
COL_KVW = COL_Z + N_BRANCH * NSA_Q_W
COL_GATE = COL_KVW + 2 * NSA_KV_W
NSA_N_PAD = COL_GATE + LANES
TQ = 128
KT = 128
N_REL_TILES = -(-(REL_MAX_DIST + TQ - 1) // KT)
CMP_NEAR = 128
CMP_FRONT = CMP_NEAR - TQ // CMP_STRIDE
PAGES_PER_STEP = 16
SEL_STEP_TILES = 4


def _nsa_in_weights(w_in):
    cuts = np.cumsum([NSA_Q_W, 4 * NSA_KV_W, 2 * NSA_KV_W, NSA_N_HEADS * N_BRANCH]).tolist()
    q, kv, kvw, gate, z = (w_in[:, a:b] for a, b in zip([0] + cuts, cuts + [w_in.shape[1]]))
    gate = jnp.pad(gate, ((0, 0), (0, LANES - gate.shape[1])))
    return jnp.concatenate([q, kv, z, kvw, gate], axis=1)


def _rel_bucket(dist):
    n = jnp.maximum(dist, 0)
    exact = REL_BUCKETS // 2
    logv = jnp.log(jnp.maximum(n, 1).astype(F32) / exact) / math.log(REL_MAX_DIST / exact)
    large = jnp.minimum(exact + (logv * (REL_BUCKETS - exact)).astype(jnp.int32), REL_BUCKETS - 1)
    return jnp.where(n < exact, n, large)


def _bias_by_dist(rel_bias, n):
    near = jnp.moveaxis(rel_bias.astype(F32)[_rel_bucket(jnp.arange(min(n, REL_MAX_DIST)))], -1, 0)
    if n <= REL_MAX_DIST:
        return near
    far = jnp.broadcast_to(rel_bias.astype(F32)[REL_BUCKETS - 1][:, None], (rel_bias.shape[1], n - REL_MAX_DIST))
    return jnp.concatenate([near, far], axis=1)


def _toeplitz(vec, first, n_rows, n_cols, row_step):
    lead = vec.shape[:-1]
    lo = first - row_step * (n_rows - 1)
    front = max(-lo, 0)
    if front:
        vec = jnp.concatenate([jnp.broadcast_to(vec[..., :1], lead + (front,)), vec], axis=-1)
    span = row_step * (n_rows - 1) + n_cols
    w = vec[..., lo + front:lo + front + span]
    pitch = span + row_step
    flat = jnp.tile(w, n_rows + 1)[..., :n_rows * pitch]
    s = flat.reshape(lead + (n_rows, pitch))[..., :n_cols]
    return jnp.flip(s, axis=-2)


def _lanes_by_group(t):
    h, n, q = t.shape
    return t.reshape(NSA_N_KV, NSA_R, n, q).transpose(0, 2, 1, 3).reshape(NSA_N_KV, n, NSA_R * q)


def _overlap_matrix(nc, ns):
    c_start = np.arange(nc) * CMP_STRIDE
    c_end = c_start + CMP_BLOCK - 1
    s_start = np.arange(ns) * SEL_BLOCK
    ov = np.clip(np.minimum(c_end[:, None], s_start[None, :] + SEL_BLOCK - 1)
                 - np.maximum(c_start[:, None], s_start[None, :]) + 1, 0, None).astype(np.float32) / CMP_STRIDE
    return ov


def _masked_softmax_parts(parts, valids, axis):
    m = None
    for s, v in zip(parts, valids):
        pm = jnp.max(jnp.where(v, s, -jnp.inf), axis=axis, keepdims=True)
        m = pm if m is None else jnp.maximum(m, pm)
    m = jnp.where(m > -jnp.inf, m, 0.0)
    es = [jnp.where(v, jnp.exp(s - m), 0.0) for s, v in zip(parts, valids)]
    den = None
    for e in es:
        d = jnp.sum(e, axis=axis, keepdims=True)
        den = d if den is None else den + d
    inv = 1.0 / jnp.where(den > 0, den, 1.0)
    return [e * inv for e in es]


def _compress_kernel(*refs, n_pp):
    page_refs = refs[1:1 + n_pp]
    perm_ref, pe_ref, w1_ref, w2_ref, kc_ref, vc_ref, x_ref = refs[1 + n_pp:]
    step = pl.program_id(1)
    hb = PAGE_SIZE // CMP_STRIDE
    dh = NSA_HEAD_DIM
    for k in range(n_pp):
        pt = lax.dot_general(perm_ref[...], page_refs[k][...].astype(BF16), NT_DIMS, preferred_element_type=F32)
        row0 = pl.multiple_of((step * n_pp + k) * hb, hb)
        for l in range(CMP_STRIDE):
            row = pt[l * hb:(l + 1) * hb, :]
            for kv in range(2):
                for g in range(NSA_N_KV):
                    c0 = kv * NSA_KV_W + g * dh
                    x_ref[kv, g, pl.ds(row0, hb), l * dh:(l + 1) * dh] = row[:, c0:c0 + dh]

    @pl.when(step == pl.num_programs(1) - 1)
    def _():
        hid_w = CMP_HIDDEN
        for kv, out_ref in ((0, kc_ref), (1, vc_ref)):
            w1 = w1_ref[kv]
            w2 = w2_ref[kv]
            pe_w = jnp.dot(pe_ref[kv].astype(BF16), w1, preferred_element_type=F32)
            pe_term = pe_w[0:1, :hid_w] + pe_w[1:2, hid_w:]
            for g in range(NSA_N_KV):
                both = jnp.dot(x_ref[kv, g].astype(BF16), w1, preferred_element_type=F32)
                bot = both[:, hid_w:]
                bot = jnp.concatenate([bot[1:], jnp.zeros((1, hid_w), F32)], axis=0)
                hid = _silu(both[:, :hid_w] + bot + pe_term).astype(BF16)
                out_ref[0, :, g * dh:(g + 1) * dh] = jnp.dot(hid, w2, preferred_element_type=F32)


def _compress(src, page_index, bsz, n_pages, ids, pe, w1, w2):
    n_pp = math.gcd(PAGES_PER_STEP, n_pages)
    hb = PAGE_SIZE // CMP_STRIDE
    nh = n_pages * hb
    half = CMP_STRIDE * NSA_HEAD_DIM
    blk = (None,) * (src.ndim - 2) + (2 * NSA_KV_W, PAGE_SIZE)
    page_specs = [pl.BlockSpec(blk, functools.partial(lambda b, s, ids, k: page_index(b, s * n_pp + k, ids), k=k))
                  for k in range(n_pp)]
    perm = np.zeros((PAGE_SIZE, PAGE_SIZE), np.float32)
    for l in range(CMP_STRIDE):
        for i in range(hb):
            perm[l * hb + i, CMP_STRIDE * i + l] = 1.0
    w1cat = jnp.concatenate([w1[:, :half], w1[:, half:]], axis=-1).astype(BF16)
    grid_spec = pltpu.PrefetchScalarGridSpec(
        num_scalar_prefetch=1,
        grid=(bsz, n_pages // n_pp),
        in_specs=page_specs + [
            pl.BlockSpec((PAGE_SIZE, PAGE_SIZE), lambda b, s, ids: (0, 0)),
            pl.BlockSpec((2, 2, half), lambda b, s, ids: (0, 0, 0)),
            pl.BlockSpec((2, half, 2 * CMP_HIDDEN), lambda b, s, ids: (0, 0, 0)),
            pl.BlockSpec((2, CMP_HIDDEN, NSA_HEAD_DIM), lambda b, s, ids: (0, 0, 0)),
        ],
        out_specs=(pl.BlockSpec((1, nh, NSA_KV_W), lambda b, s, ids: (b, 0, 0)),
                   pl.BlockSpec((1, nh, NSA_KV_W), lambda b, s, ids: (b, 0, 0))),
        scratch_shapes=[pltpu.VMEM((2, NSA_N_KV, nh, half), F32)],
    )
    return pl.pallas_call(
        functools.partial(_compress_kernel, n_pp=n_pp),
        out_shape=(jax.ShapeDtypeStruct((bsz, nh, NSA_KV_W), F32), jax.ShapeDtypeStruct((bsz, nh, NSA_KV_W), F32)),
        grid_spec=grid_spec,
        compiler_params=_cparams(("parallel", "arbitrary")),
        name="nsa_compress",
    )(ids, *([src] * n_pp), jnp.asarray(perm, BF16), pe.reshape(2, 2, half).astype(F32), w1cat, w2.astype(BF16))


def _nsa_attn_kernel(qt_ref, kv_ref, kc_ref, vc_ref, ov_ref, trel_ref, tcn_ref, cfar_ref, gate_ref, z0_ref, z1_ref,
                     z2_ref, o_ref, qaug_ref, imp_ref, m_ref, l_ref, acc_ref, ob_ref, s_ref, *, n_sel, n_cmp_rows):
    t = pl.program_id(1)
    dh = NSA_HEAD_DIM
    rq = NSA_R * TQ
    scale = dh ** -0.5
    far_idx = N_REL_TILES
    ns8 = _round_up(n_sel, 2 * SUBLANES)
    assert ns8 < LANES
    near0 = pl.multiple_of(t * (TQ // CMP_STRIDE), SUBLANES)
    far_rows = slice(CMP_FRONT, CMP_FRONT + n_cmp_rows)

    def cmp_branch(g, qt, with_far):
        gs = slice(g * dh, (g + 1) * dh)
        kc_near = kc_ref[0, pl.ds(near0, CMP_NEAR), gs].astype(BF16)
        vc_near = vc_ref[0, pl.ds(near0, CMP_NEAR), gs].astype(BF16)
        cn = lax.broadcasted_iota(jnp.int32, (CMP_NEAR, rq), 0)
        qn = lax.broadcasted_iota(jnp.int32, (CMP_NEAR, rq), 1) % TQ
        dist_n = qn - CMP_STRIDE * cn + (CMP_STRIDE * CMP_FRONT - (CMP_BLOCK - 1))
        parts = [jnp.dot(kc_near, qt, preferred_element_type=F32) + tcn_ref[g]]
        valids = [(dist_n >= 0) & (cn + near0 >= CMP_FRONT)]
        if with_far:
            kc_far = kc_ref[0, far_rows, gs].astype(BF16)
            parts.append(jnp.dot(kc_far, qt, preferred_element_type=F32) + cfar_ref[g, 0:1, :])
            valids.append(lax.broadcasted_iota(jnp.int32, (n_cmp_rows, rq), 0) + CMP_FRONT < near0)
        ps = _masked_softmax_parts(parts, valids, axis=0)

        def importance(ov, p):
            pg = sum(p[:, r * TQ:(r + 1) * TQ] for r in range(NSA_R))
            hi = pg.astype(BF16)
            lo = (pg - hi.astype(F32)).astype(BF16)
            ov16 = ov.astype(BF16)
            return (lax.dot_general(ov16, hi, TN_DIMS, preferred_element_type=F32)
                    + lax.dot_general(ov16, lo, TN_DIMS, preferred_element_type=F32))

        o_c = lax.dot_general(vc_near, ps[0].astype(BF16), TN_DIMS, preferred_element_type=F32)
        imp = importance(ov_ref[pl.ds(near0, CMP_NEAR), :], ps[0])
        if with_far:
            vc_far = vc_ref[0, far_rows, gs].astype(BF16)
            o_c = o_c + lax.dot_general(vc_far, ps[1].astype(BF16), TN_DIMS, preferred_element_type=F32)
            imp = imp + importance(ov_ref[far_rows, :], ps[1])
        ob_ref[0, g] = o_c
        imp_ref[...] = imp

    qts = []
    sid = lax.broadcasted_iota(jnp.int32, (ns8, TQ), 0)
    qpos = t * TQ + lax.broadcasted_iota(jnp.int32, (ns8, TQ), 1)
    cur = qpos // SEL_BLOCK
    forced = (sid == 0) | (sid == cur) | (sid == cur - 1)
    reachable = sid * SEL_BLOCK <= qpos
    sub = lax.broadcasted_iota(jnp.int32, (SUBLANES, TQ), 0)
    tail_row = lax.broadcasted_iota(jnp.int32, (LANES - ns8, rq), 0)
    for g in range(NSA_N_KV):
        q32 = jnp.concatenate([qt_ref[(g * NSA_R + r) * dh:(g * NSA_R + r + 1) * dh, :] for r in range(NSA_R)], axis=1)
        qt = (q32 * scale).astype(BF16)
        qts.append((q32 * (scale * LOG2E)).astype(BF16))
        has_far = near0 > CMP_FRONT

        @pl.when(has_far)
        def _():
            cmp_branch(g, qt, True)

        @pl.when(jnp.logical_not(has_far))
        def _():
            cmp_branch(g, qt, False)

        imp = jnp.where(forced, FORCE_SCORE, imp_ref[0:ns8, :])
        imp = jnp.where(reachable, imp, -FORCE_SCORE)
        blocks = [imp[SUBLANES * kb:SUBLANES * (kb + 1)] for kb in range(ns8 // SUBLANES)]
        cnts = [jnp.zeros((SUBLANES, TQ), F32) for _ in blocks]
        for sp in range(n_sel):
            other = imp[sp:sp + 1, :]
            for kb, blk in enumerate(blocks):
                if kb > sp // SUBLANES:
                    ahead = other >= blk
                elif kb < sp // SUBLANES:
                    ahead = other > blk
                else:
                    ahead = (other > blk) | ((other == blk) & (sub > sp % SUBLANES))
                cnts[kb] = cnts[kb] + jnp.where(ahead, 1.0, 0.0)
        cnt = jnp.concatenate(cnts, axis=0)
        unsel = jnp.where((cnt < SEL_TOPK) & (imp > -0.5 * FORCE_SCORE), 0.0, NEG)
        zero = jnp.zeros((dh, rq), BF16)
        q_rows = [qts[g], zero] if g % 2 == 0 else [zero, qts[g]]
        tail = jnp.where(tail_row == 0, trel_ref[g, far_idx, 0:1, :], 0.0)
        qaug_ref[g] = jnp.concatenate(q_rows + [jnp.concatenate([unsel] * NSA_R, axis=1).astype(BF16),
                                                tail.astype(BF16)], axis=0)

    def reset():
        m_ref[...] = jnp.full(m_ref.shape, NEG, F32)
        l_ref[...] = jnp.zeros(l_ref.shape, F32)
        acc_ref[...] = jnp.zeros(acc_ref.shape, F32)

    def online_step(g, s, v_t):
        m_old = m_ref[g]
        m_new = jnp.maximum(m_old, jnp.max(s, axis=0, keepdims=True))
        alpha = jnp.exp2(m_old - m_new)
        p = jnp.exp2(s - m_new)
        l_ref[g] = l_ref[g] * alpha + jnp.sum(p, axis=0, keepdims=True)
        acc_ref[g] = acc_ref[g] * alpha + lax.dot_general(v_t, p.astype(BF16), TN_DIMS, preferred_element_type=F32)
        m_ref[g] = m_new

    def mask_heads(s, ok):
        return jnp.concatenate([jnp.where(ok, s[:, r * TQ:(r + 1) * TQ], NEG) for r in range(NSA_R)], axis=1)

    reset()
    kt2 = SEL_STEP_TILES * KT
    kj2 = lax.broadcasted_iota(jnp.int32, (kt2, TQ), 0)
    qi2 = lax.broadcasted_iota(jnp.int32, (kt2, TQ), 1)
    e_blk = lax.broadcasted_iota(jnp.int32, (kt2, LANES), 0) // SEL_BLOCK
    e_lane = lax.broadcasted_iota(jnp.int32, (kt2, LANES), 1)

    def sel_step(kt, mode):
        k0 = pl.multiple_of(kt * kt2, kt2)
        onehot = e_lane == kt * (kt2 // SEL_BLOCK) + e_blk
        if mode == "far":
            onehot = onehot | (e_lane == ns8)
        e_aug = jnp.where(onehot, 1.0, 0.0).astype(BF16)
        delta = t - SEL_STEP_TILES * kt
        idx = [jnp.clip(delta - i, 0, far_idx) for i in range(SEL_STEP_TILES)]
        for pair in range(NSA_N_KV // 2):
            k_aug = jnp.concatenate([kv_ref[pl.ds(WINDOW + k0, kt2), pair * LANES:(pair + 1) * LANES], e_aug], axis=1)
            for g in (2 * pair, 2 * pair + 1):
                s = jnp.dot(k_aug, qaug_ref[g], preferred_element_type=F32)
                if mode != "far":
                    s = s + jnp.concatenate([trel_ref[g, i] for i in idx], axis=0)
                if mode == "last":
                    s = mask_heads(s, t * TQ + qi2 >= k0 + kj2)
                s_ref[g] = s
        for g in range(NSA_N_KV):
            online_step(g, s_ref[g], kv_ref[pl.ds(WINDOW + k0, kt2), NSA_KV_W + g * dh:NSA_KV_W + (g + 1) * dh])

    n_steps = t // SEL_STEP_TILES + 1
    n_far_steps = jnp.maximum(t - far_idx + 1, 0) // SEL_STEP_TILES

    def far_body(kt, carry):
        sel_step(kt, "far")
        return carry

    def near_body(kt, carry):
        sel_step(kt, "near")
        return carry

    lax.fori_loop(0, n_far_steps, far_body, 0)
    lax.fori_loop(n_far_steps, n_steps - 1, near_body, 0)
    sel_step(n_steps - 1, "last")
    for g in range(NSA_N_KV):
        ob_ref[1, g] = acc_ref[g] / l_ref[g]

    assert SEL_STEP_TILES >= 2

    reset()
    row0 = pl.multiple_of(t * TQ, TQ)
    for deltas in ((0,), (2, 1), (4, 3)):
        n_keys = KT * len(deltas)
        kjw = lax.broadcasted_iota(jnp.int32, (n_keys, TQ), 0)
        qiw = lax.broadcasted_iota(jnp.int32, (n_keys, TQ), 1)
        dist = deltas[0] * KT + qiw - kjw
        ok = (dist >= 0) & (dist <= WINDOW) & (row0 + kjw >= deltas[0] * KT)
        k0 = row0 + (WINDOW - deltas[0] * KT)
        for g in range(NSA_N_KV):
            k_t = kv_ref[pl.ds(k0, n_keys), 2 * NSA_KV_W + g * dh:2 * NSA_KV_W + (g + 1) * dh]
            s = jnp.dot(k_t, qts[g], preferred_element_type=F32)
            s = s + jnp.concatenate([trel_ref[g, d] for d in deltas], axis=0)
            s_ref[g, 0:n_keys, :] = mask_heads(s, ok)
        for g in range(NSA_N_KV):
            online_step(g, s_ref[g, 0:n_keys, :],
                        kv_ref[pl.ds(k0, n_keys), 3 * NSA_KV_W + g * dh:3 * NSA_KV_W + (g + 1) * dh])
    for g in range(NSA_N_KV):
        ob_ref[2, g] = acc_ref[g] / l_ref[g]

    gate = jax.nn.sigmoid(gate_ref[...])
    z_refs = (z0_ref, z1_ref, z2_ref)
    for h in range(NSA_N_HEADS):
        g, r = divmod(h, NSA_R)
        mixed = None
        for c in range(N_BRANCH):
            o_hc = ob_ref[c, g, :, r * TQ:(r + 1) * TQ]
            term = o_hc * _silu(z_refs[c][h * dh:(h + 1) * dh, :]) * gate[h * N_BRANCH + c:h * N_BRANCH + c + 1, :]
            mixed = term if mixed is None else mixed + term
        o_ref[h * dh:(h + 1) * dh, :] = mixed.astype(o_ref.dtype)


def _nsa_prompt_tables(rel_bias):
    n_dist = max((N_REL_TILES + 1) * KT, CMP_STRIDE * CMP_FRONT + TQ)
    vec = _bias_by_dist(rel_bias, n_dist)
    tiles = [_lanes_by_group(_toeplitz(vec, delta * KT, KT, TQ, 1)) for delta in range(N_REL_TILES)]
    far = jnp.broadcast_to(rel_bias.astype(F32)[REL_BUCKETS - 1].reshape(NSA_N_HEADS, 1, 1), (NSA_N_HEADS, KT, TQ))
    far = _lanes_by_group(far)
    trel = jnp.stack(tiles + [far], axis=1) * LOG2E
    tcn = _lanes_by_group(_toeplitz(vec, CMP_STRIDE * CMP_FRONT - (CMP_BLOCK - 1), CMP_NEAR, TQ, CMP_STRIDE))
    return trel, tcn, far[:, :SUBLANES, :]


def _nsa_layer_prompt(xp, prm):
    norm_w, w_in, pe, w1, w2, w_out, rel_bias = prm
    bsz, t, d = xp.shape
    assert t % (SEL_STEP_TILES * KT) == 0 and t % PAGE_SIZE == 0 and t >= WINDOW
    n_tiles = t // TQ
    ns = t // SEL_BLOCK
    assert ns <= LANES
    x2 = xp.reshape(bsz * t, d)
    w = _nsa_in_weights(w_in)
    ut = _norm_matmul_t(x2, norm_w, w.T.astype(BF16), bsz, _pick_tn(NSA_N_PAD, 1152))
    w_kv = jnp.concatenate([w[:, COL_KV + 2 * NSA_KV_W:COL_KV + 4 * NSA_KV_W], w[:, COL_KVW:COL_KVW + 2 * NSA_KV_W]],
                           axis=1).astype(BF16)
    kvn = _norm_matmul(x2, norm_w, w_kv, 4 * NSA_KV_W, out_dtype=BF16)
    t_pad = WINDOW + t
    kvn = jnp.pad(kvn.reshape(bsz, t, 4 * NSA_KV_W), ((0, 0), (WINDOW, 0), (0, 0))).reshape(bsz * t_pad, 4 * NSA_KV_W)
    n_pages = t // PAGE_SIZE
    cmp_blk = COL_KV // (2 * NSA_KV_W)
    kc, vc = _compress(ut, lambda b, p, ids: (b, cmp_blk, p), bsz, n_pages, jnp.zeros((1, 1), jnp.int32), pe, w1, w2)
    nh = kc.shape[1]
    n_far = _round_up(CMP_FRONT + nh, LANES)
    padc = ((0, 0), (CMP_FRONT, n_far - CMP_FRONT - nh), (0, 0))
    kcp = jnp.pad(kc, padc)
    vcp = jnp.pad(vc, padc)
    ov = np.zeros((n_far, LANES), np.float32)
    ov[CMP_FRONT:CMP_FRONT + nh - 1, :ns] = _overlap_matrix(nh - 1, ns)
    trel, tcn, cfar = _nsa_prompt_tables(rel_bias)
    rq = NSA_R * TQ
    const2 = lambda b, i: (0, 0)
    const3 = lambda b, i: (0, 0, 0)
    const4 = lambda b, i: (0, 0, 0, 0)
    zb = COL_Z // NSA_Q_W
    once = dict(pipeline_mode=pl.Buffered(1))
    mixed_t = pl.pallas_call(
        functools.partial(_nsa_attn_kernel, n_sel=ns, n_cmp_rows=max(nh - CMP_NEAR, SUBLANES)),
        out_shape=jax.ShapeDtypeStruct((bsz, NSA_Q_W, t), BF16),
        grid=(bsz, n_tiles),
        in_specs=[
            pl.BlockSpec((None, NSA_Q_W, TQ), lambda b, i: (b, 0, i)),
            pl.BlockSpec((t_pad, 4 * NSA_KV_W), lambda b, i: (b, 0), **once),
            pl.BlockSpec((1, n_far, NSA_KV_W), lambda b, i: (b, 0, 0), **once),
            pl.BlockSpec((1, n_far, NSA_KV_W), lambda b, i: (b, 0, 0), **once),
            pl.BlockSpec((n_far, LANES), const2, **once),
            pl.BlockSpec((NSA_N_KV, N_REL_TILES + 1, KT, rq), const4, **once),
            pl.BlockSpec((NSA_N_KV, CMP_NEAR, rq), const3, **once),
            pl.BlockSpec((NSA_N_KV, SUBLANES, rq), const3, **once),
            pl.BlockSpec((None, LANES, TQ), lambda b, i: (b, COL_GATE // LANES, i)),
            pl.BlockSpec((None, NSA_Q_W, TQ), lambda b, i: (b, zb, i)),
            pl.BlockSpec((None, NSA_Q_W, TQ), lambda b, i: (b, zb + 1, i)),
            pl.BlockSpec((None, NSA_Q_W, TQ), lambda b, i: (b, zb + 2, i)),
        ],
        out_specs=pl.BlockSpec((None, NSA_Q_W, TQ), lambda b, i: (b, 0, i)),
        scratch_shapes=[
            pltpu.VMEM((NSA_N_KV, 2 * LANES, rq), BF16),
            pltpu.VMEM((LANES, TQ), F32),
            pltpu.VMEM((NSA_N_KV, 1, rq), F32),
            pltpu.VMEM((NSA_N_KV, 1, rq), F32),
            pltpu.VMEM((NSA_N_KV, NSA_HEAD_DIM, rq), F32),
            pltpu.VMEM((N_BRANCH, NSA_N_KV, NSA_HEAD_DIM, rq), F32),
            pltpu.VMEM((NSA_N_KV, SEL_STEP_TILES * KT, rq), F32),
        ],
        compiler_params=_cparams(("parallel", "arbitrary")),
        name="nsa_attn",
    )(ut, kvn, kcp, vcp, jnp.asarray(ov), trel, tcn, cfar, ut, ut, ut, ut)
    out = _matmul_res_t(mixed_t, w_out.astype(BF16), x2).reshape(bsz, t, d)
    return out, ut


def _block_diag_q(q2):
    hg = lax.broadcasted_iota(jnp.int32, q2.shape, 0) // NSA_R
    return jnp.concatenate([jnp.where(hg == g, q2, 0.0) for g in range(NSA_N_KV)], axis=1)


def _group_lanes(o):
    dh = NSA_HEAD_DIM
    hg = lax.broadcasted_iota(jnp.int32, (o.shape[0], dh), 0) // NSA_R
    return sum(jnp.where(hg == g, o[:, g * dh:(g + 1) * dh], 0.0) for g in range(NSA_N_KV))


def _nsa_step_kernel(*refs, n_pp, past_len, n_cmp, n_sel_pad):
    page_refs = refs[1:1 + n_pp]
    (q_ref, kvn_ref, kwn_ref, gate_ref, z_ref, kc_ref, vc_ref, win_ref, ovp_ref, bsel_ref, bcmp_ref, bwin_ref, b0_ref,
     o_ref, sel_ref, m_ref, l_ref, acc_ref, oc_ref) = refs[1 + n_pp:]
    step = pl.program_id(1)
    nh = NSA_N_HEADS
    dh = NSA_HEAD_DIM
    qbd = _block_diag_q(q_ref[0] * (dh ** -0.5))
    qbd16 = qbd.astype(BF16)
    cur = past_len // SEL_BLOCK

    @pl.when(step == 0)
    def _():
        s_c = lax.dot_general(qbd16, kc_ref[0].astype(BF16), NT_DIMS, preferred_element_type=F32) + bcmp_ref[...]
        cid = lax.broadcasted_iota(jnp.int32, s_c.shape, 1)
        (p_c,) = _masked_softmax_parts([s_c], [cid < n_cmp], axis=1)
        oc_ref[...] = _group_lanes(jnp.dot(p_c.astype(BF16), vc_ref[0].astype(BF16), preferred_element_type=F32))
        gr = lax.broadcasted_iota(jnp.int32, (SUBLANES, nh), 0)
        gh = lax.broadcasted_iota(jnp.int32, (SUBLANES, nh), 1) // NSA_R
        pg = jnp.dot((gr == gh).astype(F32), p_c, precision=HIGHEST, preferred_element_type=F32)
        imp = jnp.dot(pg, ovp_ref[...], precision=HIGHEST, preferred_element_type=F32)
        sid = lax.broadcasted_iota(jnp.int32, imp.shape, 1)
        forced = (sid == 0) | (sid == cur) | (sid == cur - 1)
        imp = jnp.where(forced, FORCE_SCORE, imp)
        imp = jnp.where(sid * SEL_BLOCK <= past_len, imp, -FORCE_SCORE)
        er = lax.broadcasted_iota(jnp.int32, (n_sel_pad, n_sel_pad), 0)
        ec = lax.broadcasted_iota(jnp.int32, (n_sel_pad, n_sel_pad), 1)
        rows = []
        for g in range(NSA_N_KV):
            row = imp[g:g + 1, :]
            col = jnp.sum(jnp.where(er == ec, row, 0.0), axis=1, keepdims=True)
            ahead = (col > row) | ((col == row) & (er < ec))
            cnt = jnp.sum(ahead.astype(jnp.int32), axis=0, keepdims=True)
            sel = ((cnt < SEL_TOPK) & (row > -0.5 * FORCE_SCORE)).astype(F32)
            rows.extend([sel] * NSA_R)
        sel_ref[...] = jnp.concatenate(rows, axis=0)
        m_ref[...] = jnp.full(m_ref.shape, NEG, F32)
        l_ref[...] = jnp.zeros(l_ref.shape, F32)
        acc_ref[...] = jnp.zeros(acc_ref.shape, F32)

    def online(s):
        m_old = m_ref[...]
        m_new = jnp.maximum(m_old, jnp.max(s, axis=-1, keepdims=True))
        alpha = jnp.exp(m_old - m_new)
        pr = jnp.exp(s - m_new)
        l_ref[...] = l_ref[...] * alpha + jnp.sum(pr, axis=-1, keepdims=True)
        m_ref[...] = m_new
        return alpha, pr

    k_all = jnp.concatenate([r[0:NSA_KV_W, :] for r in page_refs], axis=1).astype(BF16)
    v_all = jnp.concatenate([r[NSA_KV_W:2 * NSA_KV_W, :] for r in page_refs], axis=1).astype(BF16)
    n_tok = n_pp * PAGE_SIZE
    s = jnp.dot(qbd16, k_all, preferred_element_type=F32) + bsel_ref[0]
    eb = lax.broadcasted_iota(jnp.int32, (n_sel_pad, n_tok), 0)
    ej = lax.broadcasted_iota(jnp.int32, (n_sel_pad, n_tok), 1)
    expand = (eb == step * (n_tok // SEL_BLOCK) + ej // SEL_BLOCK).astype(BF16)
    chosen = jnp.dot(sel_ref[...].astype(BF16), expand, preferred_element_type=F32)
    s = jnp.where(chosen > 0.5, s, NEG)
    alpha, pr = online(s)
    acc_ref[...] = acc_ref[...] * alpha + lax.dot_general(pr.astype(BF16), v_all, NT_DIMS, preferred_element_type=F32)

    @pl.when(step == pl.num_programs(1) - 1)
    def _():
        b0 = b0_ref[:, 0:1]
        kn = kvn_ref[0, :, 2 * NSA_KV_W:3 * NSA_KV_W]
        vn = kvn_ref[0, :, 3 * NSA_KV_W:4 * NSA_KV_W]
        s_n = jnp.sum(qbd * kn, axis=-1, keepdims=True) + b0
        s_n = jnp.where(sel_ref[:, cur:cur + 1] > 0.5, s_n, NEG)
        alpha, pr = online(s_n)
        acc = acc_ref[...] * alpha + pr * vn
        o_s = _group_lanes(acc / l_ref[...])
        wk = win_ref[0, 0:NSA_KV_W, :].astype(BF16)
        wv = win_ref[0, NSA_KV_W:2 * NSA_KV_W, :].astype(BF16)
        n_win = win_ref.shape[2]
        s_w = jnp.dot(qbd16, wk, preferred_element_type=F32) + bwin_ref[...]
        wi = lax.broadcasted_iota(jnp.int32, s_w.shape, 1)
        s_wn = jnp.sum(qbd * kwn_ref[0, :, 0:NSA_KV_W], axis=-1, keepdims=True) + b0
        p_w, p_wn = _masked_softmax_parts([s_w, s_wn], [n_win - wi <= WINDOW, jnp.full(s_wn.shape, True)], axis=1)
        o_w = _group_lanes(lax.dot_general(p_w.astype(BF16), wv, NT_DIMS, preferred_element_type=F32)
                           + p_wn * kwn_ref[0, :, NSA_KV_W:2 * NSA_KV_W])
        gate = jax.nn.sigmoid(gate_ref[0])
        mixed = (oc_ref[...] * _silu(z_ref[0, 0]) * gate[:, 0:1] + o_s * _silu(z_ref[0, 1]) * gate[:, 1:2]
                 + o_w * _silu(z_ref[0, 2]) * gate[:, 2:3])
        o_ref[0] = mixed


def _nsa_layer_sample(xs_in, cache_t, win_t, page_table, prm, layer, n_layers):
    norm_w, w_in, pe, w1, w2, w_out, rel_bias = prm
    bsz, t, d = xs_in.shape
    assert t == 1
    n_pages = page_table.shape[1]
    past_len = n_pages * PAGE_SIZE
    n_win = win_t.shape[2]
    nh, dh = NSA_N_HEADS, NSA_HEAD_DIM
    x2 = xs_in.reshape(bsz, d)
    u = _norm_matmul(x2, norm_w, _nsa_in_weights(w_in).astype(BF16), _pick_tn(NSA_N_PAD, 1152))
    ids = (page_table * n_layers + layer).astype(jnp.int32)
    kc, vc = _compress(cache_t, lambda b, p, ids: (ids[b, p], 0, 0), bsz, n_pages, ids, pe, w1, w2)
    n_cmp_pad = kc.shape[1]
    n_cmp = n_cmp_pad - 1
    ns = past_len // SEL_BLOCK + 1
    n_sel_pad = _round_up(ns, LANES)
    ov = np.zeros((n_cmp_pad, n_sel_pad), np.float32)
    ov[:n_cmp, :ns] = _overlap_matrix(n_cmp, ns)
    n_pp = math.gcd(PAGES_PER_STEP, n_pages)
    n_steps = n_pages // n_pp
    n_tok = n_pp * PAGE_SIZE
    vec = _bias_by_dist(rel_bias, past_len + 1)
    bsel = jnp.moveaxis(jnp.flip(vec[:, 1:], axis=1).reshape(nh, n_steps, n_tok), 0, 1)
    bcmp = jnp.flip(vec[:, :past_len - (CMP_BLOCK - 1) + 1], axis=1)[:, ::CMP_STRIDE]
    bcmp = jnp.pad(bcmp, ((0, 0), (0, n_cmp_pad - bcmp.shape[1])))
    bwin = jnp.flip(vec[:, 1:n_win + 1], axis=1)
    b0 = jnp.broadcast_to(vec[:, 0:1], (nh, LANES))
    q3 = u[:, COL_Q:COL_Q + NSA_Q_W].reshape(bsz, nh, dh)
    kv_new = u[:, COL_KV:COL_KV + 4 * NSA_KV_W]
    kw_new = u[:, COL_KVW:COL_KVW + 2 * NSA_KV_W]
    gate = u[:, COL_GATE:COL_GATE + nh * N_BRANCH].reshape(bsz, nh, N_BRANCH)
    z4 = u[:, COL_Z:COL_Z + N_BRANCH * NSA_Q_W].reshape(bsz, N_BRANCH, nh, dh)
    per_b3 = lambda b, s, ids: (b, 0, 0)
    const2 = lambda b, s, ids: (0, 0)
    page_specs = [pl.BlockSpec((None, 2 * NSA_KV_W, PAGE_SIZE),
                               functools.partial(lambda b, s, ids, k: (ids[b, s * n_pp + k], 1, 0), k=k))
                  for k in range(n_pp)]
    grid_spec = pltpu.PrefetchScalarGridSpec(
        num_scalar_prefetch=1,
        grid=(bsz, n_steps),
        in_specs=page_specs + [
            pl.BlockSpec((1, nh, dh), per_b3),
            pl.BlockSpec((1, 1, 4 * NSA_KV_W), per_b3),
            pl.BlockSpec((1, 1, 2 * NSA_KV_W), per_b3),
            pl.BlockSpec((1, nh, N_BRANCH), per_b3),
            pl.BlockSpec((1, N_BRANCH, nh, dh), lambda b, s, ids: (b, 0, 0, 0)),
            pl.BlockSpec((1, n_cmp_pad, NSA_KV_W), per_b3),
            pl.BlockSpec((1, n_cmp_pad, NSA_KV_W), per_b3),
            pl.BlockSpec((1, 2 * NSA_KV_W, n_win), per_b3),
            pl.BlockSpec((n_cmp_pad, n_sel_pad), const2),
            pl.BlockSpec((1, nh, n_tok), lambda b, s, ids: (s, 0, 0)),
            pl.BlockSpec((nh, n_cmp_pad), const2),
            pl.BlockSpec((nh, n_win), const2),
            pl.BlockSpec((nh, LANES), const2),
        ],
        out_specs=pl.BlockSpec((1, nh, dh), per_b3),
        scratch_shapes=[
            pltpu.VMEM((nh, n_sel_pad), F32),
            pltpu.VMEM((nh, 1), F32),
            pltpu.VMEM((nh, 1), F32),
            pltpu.VMEM((nh, NSA_KV_W), F32),
            pltpu.VMEM((nh, dh), F32),
        ],
    )
    mixed = pl.pallas_call(
        functools.partial(_nsa_step_kernel, n_pp=n_pp, past_len=past_len, n_cmp=n_cmp, n_sel_pad=n_sel_pad),
        out_shape=jax.ShapeDtypeStruct((bsz, nh, dh), F32),
        grid_spec=grid_spec,
        compiler_params=_cparams(("parallel", "arbitrary")),
        name="nsa_step",
    )(ids, *([cache_t] * n_pp), q3, kv_new.reshape(bsz, 1, -1), kw_new.reshape(bsz, 1, -1), gate, z4, kc, vc, win_t,
      jnp.asarray(ov), bsel, bcmp, bwin, b0)
    out = _matmul_res(mixed.reshape(bsz, NSA_Q_W).astype(BF16), w_out.astype(BF16), x2).reshape(bsz, 1, d)
    return out, kv_new, kw_new


def kernel(x_prompt, x_sample, state_ssm, state_conv, cache_kv, cache_win, page_table, rel_bias, final_norm,
           ssd_norm, ssd_w_in, ssd_conv_w, ssd_conv_b, ssd_dt_bias, ssd_a_log, ssd_d, ssd_gnorm, ssd_w_out,
           nsa_norm, nsa_w_in, nsa_cmp_pe, nsa_cmp_w1, nsa_cmp_w2, nsa_w_out):
    depth = ssd_norm.shape[0] + nsa_norm.shape[0]
    g, dh = NSA_N_KV, NSA_HEAD_DIM
    xp, xs = x_prompt, x_sample
    bp, tp, d = xp.shape
    bs = xs.shape[0]
    n_phys, n_nsa = cache_kv.shape[:2]
    win_len_s = cache_win.shape[2]
    win_len_p = min(WINDOW, tp)
    cache_t = cache_kv.transpose(0, 1, 3, 4, 5, 2).reshape(n_phys * n_nsa, 4 * NSA_KV_W, PAGE_SIZE)
    cwin_t = cache_win.transpose(0, 1, 3, 4, 5, 2).reshape(n_nsa, bs, 2 * NSA_KV_W, win_len_s)
    ssm_p, conv_p, kv_p, win_p = [], [], [], []
    ssm_s, conv_s, kv_s, win_s = [], [], [], []
    for layer in range(depth):
        j = layer // 2
        if layer % 2 == 0:
            prm = (ssd_norm[j], ssd_w_in[j], ssd_conv_w[j], ssd_conv_b[j], ssd_dt_bias[j], ssd_a_log[j],
                   ssd_d[j], ssd_gnorm[j], ssd_w_out[j])
            xp, c, s = _ssd_layer_prompt(xp, prm)
            conv_p.append(c)
            ssm_p.append(s)
            xs, c, s = _ssd_layer_sample(xs, state_conv[j], state_ssm[j], prm)
            conv_s.append(c)
            ssm_s.append(s)
        else:
            prm = (nsa_norm[j], nsa_w_in[j], nsa_cmp_pe[j], nsa_cmp_w1[j], nsa_cmp_w2[j], nsa_w_out[j], rel_bias)
            xp, ut = _nsa_layer_prompt(xp, prm)
            kv_p.append(ut[:, COL_KV:COL_KV + 4 * NSA_KV_W, :])
            win_p.append(ut[:, COL_KVW:COL_KVW + 2 * NSA_KV_W, tp - win_len_p:])
            xs, kv_new, kw_new = _nsa_layer_sample(xs, cache_t, cwin_t[j], page_table, prm, j, n_nsa)
            kv_s.append(kv_new.reshape(bs, 1, 4, g, dh))
            win_s.append(jnp.concatenate([cwin_t[j][:, :, 1:], kw_new[:, :, None]], axis=2))
    y_prompt = _rmsnorm(xp.reshape(bp * tp, d), final_norm).reshape(bp, tp, d)
    y_sample = _rmsnorm(xs.reshape(bs, d), final_norm).reshape(bs, 1, d)
    kv_prompt = jnp.stack(kv_p, axis=1).reshape(bp, n_nsa, 4, g, dh, tp).transpose(0, 5, 1, 2, 3, 4)
    win_prompt = jnp.stack(win_p).reshape(n_nsa, bp, 2, g, dh, win_len_p).transpose(0, 1, 5, 2, 3, 4)
    win_sample = jnp.stack(win_s).reshape(n_nsa, bs, 2, g, dh, win_len_s).transpose(0, 1, 5, 2, 3, 4)
    return (y_prompt, y_sample, jnp.stack(ssm_p), jnp.stack(conv_p), kv_prompt, win_prompt,
            jnp.stack(ssm_s), jnp.stack(conv_s), jnp.stack(kv_s, axis=2), win_sample)
```

```python
import functools
import math

import jax
import jax.numpy as jnp
import numpy as np
from jax import lax
from jax.experimental import pallas as pl
from jax.experimental.pallas import tpu as pltpu

F32 = jnp.float32
BF16 = jnp.bfloat16
HIGHEST = lax.Precision.HIGHEST

NORM_EPS = 1e-6
SSD_HEAD_DIM = 64
SSD_N_GROUPS = 8
SSD_D_STATE = 128
SSD_CONV_W = 4
SSD_CHUNK = 128
NSA_N_HEADS = 16
NSA_HEAD_DIM = 64
NSA_N_KV = 4
NSA_R = NSA_N_HEADS // NSA_N_KV
N_BRANCH = 3
CMP_BLOCK = 32
CMP_STRIDE = 16
CMP_HIDDEN = 128
SEL_BLOCK = 64
SEL_TOPK = 16
WINDOW = 512
FORCE_SCORE = 1e4
REL_BUCKETS = 32
REL_MAX_DIST = 1024
PAGE_SIZE = 128

LANES = 128
SUBLANES = 8
VMEM_LIMIT = 56 * 1024 * 1024

NEG = -1e30
LOG2E = math.log2(math.e)
NT_DIMS = (((1,), (1,)), ((), ()))
TN_DIMS = (((0,), (0,)), ((), ()))


def _cparams(sem):
    return pltpu.CompilerParams(dimension_semantics=sem, vmem_limit_bytes=VMEM_LIMIT)


def _round_up(x, m):
    return (x + m - 1) // m * m


def _silu(x):
    return x * jax.nn.sigmoid(x)


def _pick_tm(m, cap, quantum=SUBLANES):
    if m <= cap:
        return m
    for tm in range(cap // quantum * quantum, 0, -quantum):
        if m % tm == 0:
            return tm
    raise ValueError(f"no row tile for {m}")


def _pick_tn(n, cap=1024):
    best = LANES
    for tn in range(LANES, cap + 1, LANES):
        if n % tn == 0:
            best = tn
    return best


def _norm_matmul_kernel(x_ref, nw_ref, w_ref, o_ref, xn_ref, *, transpose_out):
    @pl.when(pl.program_id(1) == 0)
    def _():
        x = x_ref[...]
        ms = jnp.mean(x * x, axis=-1, keepdims=True)
        xn = x * lax.rsqrt(ms + NORM_EPS) * nw_ref[...]
        if transpose_out:
            xn_ref[...] = xn.T.astype(BF16)
        else:
            xn_ref[...] = xn.astype(BF16)

    if transpose_out:
        o_ref[...] = jnp.dot(w_ref[...], xn_ref[...], preferred_element_type=F32).astype(o_ref.dtype)
    else:
        o_ref[...] = jnp.dot(xn_ref[...], w_ref[...], preferred_element_type=F32).astype(o_ref.dtype)


def _norm_matmul(x, nw, w, tn, out_dtype=F32):
    m, k = x.shape
    n = w.shape[1]
    tm = _pick_tm(m, 1024)
    assert n % tn == 0
    return pl.pallas_call(
        functools.partial(_norm_matmul_kernel, transpose_out=False),
        out_shape=jax.ShapeDtypeStruct((m, n), out_dtype),
        grid=(m // tm, n // tn),
        in_specs=[
            pl.BlockSpec((tm, k), lambda i, j: (i, 0)),
            pl.BlockSpec((1, k), lambda i, j: (0, 0)),
            pl.BlockSpec((k, tn), lambda i, j: (0, j)),
        ],
        out_specs=pl.BlockSpec((tm, tn), lambda i, j: (i, j)),
        scratch_shapes=[pltpu.VMEM((tm, k), BF16)],
        compiler_params=_cparams(("parallel", "arbitrary")),
        name="norm_matmul",
    )(x, nw.reshape(1, k), w)


def _norm_matmul_t(x, nw, wt, bsz, tn):
    m, k = x.shape
    n = wt.shape[0]
    t = m // bsz
    tm = _pick_tm(t, 1024, LANES)
    assert n % tn == 0
    tpb = t // tm
    return pl.pallas_call(
        functools.partial(_norm_matmul_kernel, transpose_out=True),
        out_shape=jax.ShapeDtypeStruct((bsz, n, t), F32),
        grid=(m // tm, n // tn),
        in_specs=[
            pl.BlockSpec((tm, k), lambda i, j: (i, 0)),
            pl.BlockSpec((1, k), lambda i, j: (0, 0)),
            pl.BlockSpec((tn, k), lambda i, j: (j, 0)),
        ],
        out_specs=pl.BlockSpec((None, tn, tm), lambda i, j: (i // tpb, j, i % tpb)),
        scratch_shapes=[pltpu.VMEM((k, tm), BF16)],
        compiler_params=_cparams(("parallel", "arbitrary")),
        name="norm_matmul_t",
    )(x, nw.reshape(1, k), wt)


def _matmul_res_kernel(y_ref, w_ref, x_ref, o_ref, *, y_transposed):
    dims = TN_DIMS if y_transposed else (((1,), (0,)), ((), ()))
    o_ref[...] = x_ref[...] + lax.dot_general(y_ref[...], w_ref[...], dims, preferred_element_type=F32)


def _matmul_res(y, w, x):
    m, k = y.shape
    n = w.shape[1]
    tm = min(m, 512)
    assert m % tm == 0
    return pl.pallas_call(
        functools.partial(_matmul_res_kernel, y_transposed=False),
        out_shape=jax.ShapeDtypeStruct((m, n), F32),
        grid=(m // tm,),
        in_specs=[
            pl.BlockSpec((tm, k), lambda i: (i, 0)),
            pl.BlockSpec((k, n), lambda i: (0, 0)),
            pl.BlockSpec((tm, n), lambda i: (i, 0)),
        ],
        out_specs=pl.BlockSpec((tm, n), lambda i: (i, 0)),
        compiler_params=_cparams(("parallel",)),
        name="matmul_res",
    )(y, w, x)


def _matmul_res_t(yt, w, x):
    bsz, k, t = yt.shape
    n = w.shape[1]
    tm = min(t, 512)
    assert t % tm == 0
    tpb = t // tm
    return pl.pallas_call(
        functools.partial(_matmul_res_kernel, y_transposed=True),
        out_shape=jax.ShapeDtypeStruct((bsz * t, n), F32),
        grid=(bsz * tpb,),
        in_specs=[
            pl.BlockSpec((None, k, tm), lambda i: (i // tpb, 0, i % tpb)),
            pl.BlockSpec((k, n), lambda i: (0, 0)),
            pl.BlockSpec((tm, n), lambda i: (i, 0)),
        ],
        out_specs=pl.BlockSpec((tm, n), lambda i: (i, 0)),
        compiler_params=_cparams(("parallel",)),
        name="matmul_res_t",
    )(yt, w, x)


def _rmsnorm_kernel(x_ref, nw_ref, o_ref):
    x = x_ref[...]
    ms = jnp.mean(x * x, axis=-1, keepdims=True)
    o_ref[...] = x * lax.rsqrt(ms + NORM_EPS) * nw_ref[...]


def _rmsnorm(x, nw):
    m, k = x.shape
    tm = min(m, 512)
    assert m % tm == 0
    return pl.pallas_call(
        _rmsnorm_kernel,
        out_shape=jax.ShapeDtypeStruct((m, k), F32),
        grid=(m // tm,),
        in_specs=[pl.BlockSpec((tm, k), lambda i: (i, 0)), pl.BlockSpec((1, k), lambda i: (0, 0))],
        out_specs=pl.BlockSpec((tm, k), lambda i: (i, 0)),
        compiler_params=_cparams(("parallel",)),
        name="final_rmsnorm",
    )(x, nw.reshape(1, k))


def _ssd_chunk_kernel(z_ref, x_ref, bc_ref, dt_ref, cprev_ref, sprev_ref, cw_ref, cb_ref, dtb_ref, alog_ref,
                      dskip_ref, gn_ref, y_ref, hlast_ref, xpad_ref, state_ref, *, d_inner, n_heads):
    lc = SSD_CHUNK
    g_n, n, p = SSD_N_GROUPS, SSD_D_STATE, SSD_HEAD_DIM
    r_n = n_heads // g_n
    c = pl.program_id(1)

    @pl.when(c == 0)
    def _():
        xpad_ref[0:SUBLANES, :] = cprev_ref[0]
        state_ref[...] = sprev_ref[0]

    @pl.when(c > 0)
    def _():
        xpad_ref[0:SUBLANES, :] = xpad_ref[lc:lc + SUBLANES, :]

    xpad_ref[SUBLANES:SUBLANES + lc, 0:d_inner] = x_ref[...]
    xpad_ref[SUBLANES:SUBLANES + lc, d_inner:] = bc_ref[...]

    conv = cb_ref[...]
    for k in range(SSD_CONV_W):
        off = SUBLANES - (SSD_CONV_W - 1) + k
        conv = conv + xpad_ref[off:off + lc, :] * cw_ref[k:k + 1, :]
    act = _silu(conv)
    xs = act[:, :d_inner]
    bm = act[:, d_inner:d_inner + g_n * n]
    cm = act[:, d_inner + g_n * n:]

    dt = jax.nn.softplus(dt_ref[...] + dtb_ref[...])
    a = -jnp.exp(alog_ref[...])
    dta = dt * a
    row = lax.broadcasted_iota(jnp.int32, (lc, lc), 0)
    col = lax.broadcasted_iota(jnp.int32, (lc, lc), 1)
    causal = row >= col
    tril = causal.astype(F32)
    a_cs = jnp.dot(tril, dta, precision=HIGHEST, preferred_element_type=F32)
    a_cs_t = a_cs.T
    a_end = a_cs[lc - 1:lc, :]
    to_end = jnp.exp(a_end - a_cs)
    e_cs = jnp.exp(a_cs)
    e_end = jnp.exp(a_end)

    assert 2 * p == LANES and r_n % 2 == 0
    lane_lo = lax.broadcasted_iota(jnp.int32, (1, LANES), 1) < p

    def pair_lanes(v, k):
        return jnp.where(lane_lo, v[:, 2 * k:2 * k + 1], v[:, 2 * k + 1:2 * k + 2])

    zs = _silu(z_ref[...])
    gw = r_n * p
    ys = []
    for g in range(g_n):
        bg16 = bm[:, g * n:(g + 1) * n].astype(BF16)
        cg16 = cm[:, g * n:(g + 1) * n].astype(BF16)
        cb = lax.dot_general(cg16, bg16, NT_DIMS, preferred_element_type=F32)
        pairs = range(g * r_n // 2, (g + 1) * r_n // 2)
        gl = slice(g * gw, (g + 1) * gw)
        xs_g = xs[:, gl]
        xdt_g = xs_g * jnp.concatenate([pair_lanes(dt, k) for k in pairs], axis=1)
        y_diag = []
        for i, k in enumerate(pairs):
            xdt_k = xdt_g[:, i * LANES:(i + 1) * LANES]
            acc = None
            for j, h in enumerate((2 * k, 2 * k + 1)):
                seg = a_cs[:, h:h + 1] - a_cs_t[h:h + 1, :]
                decay = jnp.exp(jnp.where(causal, seg, -jnp.inf))
                m_h = (cb * decay).astype(BF16)
                x_h = jnp.where(lane_lo if j == 0 else jnp.logical_not(lane_lo), xdt_k, 0.0).astype(BF16)
                d = jnp.dot(m_h, x_h, preferred_element_type=F32)
                acc = d if acc is None else acc + d
            y_diag.append(acc)
        y_diag = jnp.concatenate(y_diag, axis=1)
        sprev = state_ref[:, gl]
        y_off = (jnp.dot(cg16, sprev.astype(BF16), preferred_element_type=F32)
                 * jnp.concatenate([pair_lanes(e_cs, k) for k in pairs], axis=1))
        xw = (xdt_g * jnp.concatenate([pair_lanes(to_end, k) for k in pairs], axis=1)).astype(BF16)
        st = lax.dot_general(bg16, xw, TN_DIMS, preferred_element_type=F32)
        state_ref[:, gl] = sprev * jnp.concatenate([pair_lanes(e_end, k) for k in pairs], axis=1) + st
        yg = (y_diag + y_off + xs_g * dskip_ref[:, gl]) * zs[:, gl]
        ms = jnp.mean(yg * yg, axis=-1, keepdims=True)
        ys.append(yg * lax.rsqrt(ms + NORM_EPS))
    y = jnp.concatenate(ys, axis=1) * gn_ref[...]
    y_ref[...] = y.astype(y_ref.dtype)

    @pl.when(c == pl.num_programs(1) - 1)
    def _():
        hlast_ref[0] = state_ref[...]


def _pad_lanes(v, width=LANES):
    return jnp.pad(v.astype(F32), (0, width - v.shape[0])).reshape(1, width)


def _ssd_prompt_core(u, bsz, t, conv_prev8, ssm_prev, conv_w, conv_b, dt_bias, a_log, d_skip, gnorm):
    n_heads = dt_bias.shape[0]
    d_inner = n_heads * SSD_HEAD_DIM
    conv_dim = conv_w.shape[1]
    assert conv_dim == 2 * d_inner and t % SSD_CHUNK == 0
    lc = SSD_CHUNK
    nc = t // lc
    dt_blk = (d_inner + conv_dim) // LANES
    kern = functools.partial(_ssd_chunk_kernel, d_inner=d_inner, n_heads=n_heads)
    const = lambda b, c: (0, 0)
    state_t = ssm_prev.transpose(0, 3, 1, 2).reshape(bsz, SSD_D_STATE, d_inner)
    y, hlast_t = pl.pallas_call(
        kern,
        out_shape=(jax.ShapeDtypeStruct((bsz * t, d_inner), BF16),
                   jax.ShapeDtypeStruct((bsz, SSD_D_STATE, d_inner), F32)),
        grid=(bsz, nc),
        in_specs=[
            pl.BlockSpec((lc, d_inner), lambda b, c: (b * nc + c, 0)),
            pl.BlockSpec((lc, d_inner), lambda b, c: (b * nc + c, 1)),
            pl.BlockSpec((lc, d_inner), lambda b, c: (b * nc + c, 2)),
            pl.BlockSpec((lc, LANES), lambda b, c: (b * nc + c, dt_blk)),
            pl.BlockSpec((1, SUBLANES, conv_dim), lambda b, c: (b, 0, 0)),
            pl.BlockSpec((1, SSD_D_STATE, d_inner), lambda b, c: (b, 0, 0)),
            pl.BlockSpec((SSD_CONV_W, conv_dim), const),
            pl.BlockSpec((1, conv_dim), const),
            pl.BlockSpec((1, LANES), const),
            pl.BlockSpec((1, LANES), const),
            pl.BlockSpec((1, d_inner), const),
            pl.BlockSpec((1, d_inner), const),
        ],
        out_specs=(pl.BlockSpec((lc, d_inner), lambda b, c: (b * nc + c, 0)),
                   pl.BlockSpec((1, SSD_D_STATE, d_inner), lambda b, c: (b, 0, 0))),
        scratch_shapes=[pltpu.VMEM((lc + SUBLANES, conv_dim), F32),
                        pltpu.VMEM((SSD_D_STATE, d_inner), F32)],
        compiler_params=_cparams(("parallel", "arbitrary")),
        name="ssd_chunk",
    )(u, u, u, u, conv_prev8, state_t, conv_w, conv_b.reshape(1, conv_dim), _pad_lanes(dt_bias),
      _pad_lanes(a_log), jnp.repeat(d_skip.astype(F32), SSD_HEAD_DIM).reshape(1, d_inner), gnorm.reshape(1, d_inner))
    hlast = hlast_t.reshape(bsz, SSD_D_STATE, n_heads, SSD_HEAD_DIM).transpose(0, 2, 3, 1)
    return y, hlast


def _ssd_in_weights(w_in, d_inner, conv_dim):
    main = d_inner + conv_dim
    n_dt = w_in.shape[1] - main
    w = jnp.concatenate([w_in[:, :main], jnp.pad(w_in[:, main:], ((0, 0), (0, LANES - n_dt)))], axis=1)
    return w.astype(BF16)


def _ssd_layer_prompt(xp, prm):
    norm_w, w_in, conv_w, conv_b, dt_bias, a_log, d_skip, gnorm, w_out = prm
    bsz, t, d = xp.shape
    assert t >= SSD_CONV_W - 1
    n_heads = dt_bias.shape[0]
    d_inner = n_heads * SSD_HEAD_DIM
    conv_dim = conv_w.shape[1]
    w = _ssd_in_weights(w_in, d_inner, conv_dim)
    x2 = xp.reshape(bsz * t, d)
    u = _norm_matmul(x2, norm_w, w, _pick_tn(w.shape[1]))
    conv_prev8 = jnp.zeros((bsz, SUBLANES, conv_dim), F32)
    ssm_prev = jnp.zeros((bsz, n_heads, SSD_HEAD_DIM, SSD_D_STATE), F32)
    y, hlast = _ssd_prompt_core(u, bsz, t, conv_prev8, ssm_prev, conv_w, conv_b, dt_bias, a_log, d_skip, gnorm)
    out = _matmul_res(y, w_out.astype(BF16), x2).reshape(bsz, t, d)
    u3 = u.reshape(bsz, t, -1)
    new_conv = u3[:, t - (SSD_CONV_W - 1):, d_inner:d_inner + conv_dim]
    return out, new_conv, hlast


def _ssd_step_pre_kernel(u_ref, cst_ref, cw_ref, cb_ref, dtb_ref, alog_ref,
                         xs_ref, xdt_ref, dec_ref, bm_ref, cm_ref, ncst_ref, *, d_inner, n_heads):
    conv_dim = cw_ref.shape[1]
    gn = SSD_N_GROUPS * SSD_D_STATE
    xbc = u_ref[:, d_inner:d_inner + conv_dim]
    conv = cb_ref[...]
    for k in range(SSD_CONV_W - 1):
        conv = conv + cst_ref[k] * cw_ref[k:k + 1, :]
    conv = conv + xbc * cw_ref[SSD_CONV_W - 1:SSD_CONV_W, :]
    for k in range(SSD_CONV_W - 2):
        ncst_ref[k] = cst_ref[k + 1]
    ncst_ref[SSD_CONV_W - 2] = xbc
    act = _silu(conv)
    xs = act[:, :d_inner]
    dt = jax.nn.softplus(u_ref[:, d_inner + conv_dim:d_inner + conv_dim + LANES] + dtb_ref[...])
    dec = jnp.exp(dt * (-jnp.exp(alog_ref[...])))
    hrow = lax.broadcasted_iota(jnp.int32, (LANES, d_inner), 0)
    hcol = lax.broadcasted_iota(jnp.int32, (LANES, d_inner), 1) // SSD_HEAD_DIM
    expand = (hrow == hcol).astype(F32)
    dt_e = jnp.dot(dt, expand, precision=HIGHEST, preferred_element_type=F32)
    dec_e = jnp.dot(dec, expand, precision=HIGHEST, preferred_element_type=F32)
    xs_ref[...] = xs
    xdt_ref[...] = xs * dt_e
    dec_ref[...] = dec_e
    bm_ref[...] = act[:, d_inner:d_inner + gn]
    cm_ref[...] = act[:, d_inner + gn:]


def _ssd_step_state_kernel(s_ref, xdt_ref, dec_ref, b_ref, c_ref, snew_ref, y_ref, *, rows_per_group):
    for g in range(SSD_N_GROUPS):
        sl = slice(g * rows_per_group, (g + 1) * rows_per_group)
        s = s_ref[0, sl, :]
        bg = b_ref[0, g:g + 1, :]
        cg = c_ref[0, g:g + 1, :]
        xdt = xdt_ref[0, sl, :]
        dec = dec_ref[0, sl, :]
        cb = jnp.sum(bg * cg, axis=1, keepdims=True)
        y_off = jnp.sum(s * cg, axis=1, keepdims=True)
        snew_ref[0, sl, :] = s * dec + xdt * bg
        y_ref[0, sl, :] = cb * xdt + dec * y_off


def _ssd_step_post_kernel(y_ref, xs_ref, u_ref, dsk_ref, gn_ref, o_ref, *, d_inner, group_width):
    y = (y_ref[...] + xs_ref[...] * dsk_ref[...]) * _silu(u_ref[:, :d_inner])
    outs = []
    for g in range(SSD_N_GROUPS):
        yg = y[:, g * group_width:(g + 1) * group_width]
        ms = jnp.mean(yg * yg, axis=-1, keepdims=True)
        outs.append(yg * lax.rsqrt(ms + NORM_EPS))
    o_ref[...] = (jnp.concatenate(outs, axis=1) * gn_ref[...]).astype(o_ref.dtype)


def _ssd_layer_sample(xs_in, conv_state, ssm_state, prm):
    norm_w, w_in, conv_w, conv_b, dt_bias, a_log, d_skip, gnorm, w_out = prm
    bsz, t, d = xs_in.shape
    assert t == 1
    n_heads = dt_bias.shape[0]
    d_inner = n_heads * SSD_HEAD_DIM
    conv_dim = conv_w.shape[1]
    gn = SSD_N_GROUPS * SSD_D_STATE
    w = _ssd_in_weights(w_in, d_inner, conv_dim)
    x2 = xs_in.reshape(bsz, d)
    u = _norm_matmul(x2, norm_w, w, _pick_tn(w.shape[1]))
    cst = conv_state.transpose(1, 0, 2)
    pre = functools.partial(_ssd_step_pre_kernel, d_inner=d_inner, n_heads=n_heads)
    xs, xdt, dec, bm, cm, ncst = pl.pallas_call(
        pre,
        out_shape=(jax.ShapeDtypeStruct((bsz, d_inner), F32), jax.ShapeDtypeStruct((bsz, d_inner), F32),
                   jax.ShapeDtypeStruct((bsz, d_inner), F32), jax.ShapeDtypeStruct((bsz, gn), F32),
                   jax.ShapeDtypeStruct((bsz, gn), F32), jax.ShapeDtypeStruct(cst.shape, F32)),
        compiler_params=pltpu.CompilerParams(vmem_limit_bytes=VMEM_LIMIT),
        name="ssd_step_pre",
    )(u, cst, conv_w, conv_b.reshape(1, conv_dim), _pad_lanes(dt_bias), _pad_lanes(a_log))
    rows = n_heads * SSD_HEAD_DIM
    rpg = rows // SSD_N_GROUPS
    st = functools.partial(_ssd_step_state_kernel, rows_per_group=rpg)
    snew, ycol = pl.pallas_call(
        st,
        out_shape=(jax.ShapeDtypeStruct((bsz, rows, SSD_D_STATE), F32), jax.ShapeDtypeStruct((bsz, rows, 1), F32)),
        grid=(bsz,),
        in_specs=[
            pl.BlockSpec((1, rows, SSD_D_STATE), lambda b: (b, 0, 0)),
            pl.BlockSpec((1, rows, 1), lambda b: (b, 0, 0)),
            pl.BlockSpec((1, rows, 1), lambda b: (b, 0, 0)),
            pl.BlockSpec((1, SSD_N_GROUPS, SSD_D_STATE), lambda b: (b, 0, 0)),
            pl.BlockSpec((1, SSD_N_GROUPS, SSD_D_STATE), lambda b: (b, 0, 0)),
        ],
        out_specs=(pl.BlockSpec((1, rows, SSD_D_STATE), lambda b: (b, 0, 0)),
                   pl.BlockSpec((1, rows, 1), lambda b: (b, 0, 0))),
        compiler_params=_cparams(("parallel",)),
        name="ssd_step_state",
    )(ssm_state.reshape(bsz, rows, SSD_D_STATE), xdt.reshape(bsz, rows, 1), dec.reshape(bsz, rows, 1),
      bm.reshape(bsz, SSD_N_GROUPS, SSD_D_STATE), cm.reshape(bsz, SSD_N_GROUPS, SSD_D_STATE))
    post = functools.partial(_ssd_step_post_kernel, d_inner=d_inner, group_width=d_inner // SSD_N_GROUPS)
    y = pl.pallas_call(
        post,
        out_shape=jax.ShapeDtypeStruct((bsz, d_inner), BF16),
        compiler_params=pltpu.CompilerParams(vmem_limit_bytes=VMEM_LIMIT),
        name="ssd_step_post",
    )(ycol.reshape(bsz, rows), xs, u, jnp.repeat(d_skip.astype(F32), SSD_HEAD_DIM).reshape(1, d_inner),
      gnorm.reshape(1, d_inner))
    out = _matmul_res(y, w_out.astype(BF16), x2).reshape(bsz, 1, d)
    return out, ncst.transpose(1, 0, 2), snew.reshape(ssm_state.shape)


NSA_Q_W = NSA_N_HEADS * NSA_HEAD_DIM
NSA_KV_W = NSA_N_KV * NSA_HEAD_DIM
COL_Q = 0
COL_KV = NSA_Q_W
COL_Z = COL_KV + 4 * NSA_KV_W
COL_KVW = COL_Z + N_BRANCH * NSA_Q_W
COL_GATE = COL_KVW + 2 * NSA_KV_W
NSA_N_PAD = COL_GATE + LANES
TQ = 128
KT = 128
N_REL_TILES = -(-(REL_MAX_DIST + TQ - 1) // KT)
CMP_NEAR = 128
CMP_FRONT = CMP_NEAR - TQ // CMP_STRIDE
PAGES_PER_STEP = 32
SEL_STEP_TILES = 4


def _nsa_in_weights(w_in):
    cuts = np.cumsum([NSA_Q_W, 4 * NSA_KV_W, 2 * NSA_KV_W, NSA_N_HEADS * N_BRANCH]).tolist()
    q, kv, kvw, gate, z = (w_in[:, a:b] for a, b in zip([0] + cuts, cuts + [w_in.shape[1]]))
    gate = jnp.pad(gate, ((0, 0), (0, LANES - gate.shape[1])))
    return jnp.concatenate([q, kv, z, kvw, gate], axis=1)


def _rel_bucket(dist):
    n = jnp.maximum(dist, 0)
    exact = REL_BUCKETS // 2
    logv = jnp.log(jnp.maximum(n, 1).astype(F32) / exact) / math.log(REL_MAX_DIST / exact)
    large = jnp.minimum(exact + (logv * (REL_BUCKETS - exact)).astype(jnp.int32), REL_BUCKETS - 1)
    return jnp.where(n < exact, n, large)


def _bias_by_dist(rel_bias, n):
    near = jnp.moveaxis(rel_bias.astype(F32)[_rel_bucket(jnp.arange(min(n, REL_MAX_DIST)))], -1, 0)
    if n <= REL_MAX_DIST:
        return near
    far = jnp.broadcast_to(rel_bias.astype(F32)[REL_BUCKETS - 1][:, None], (rel_bias.shape[1], n - REL_MAX_DIST))
    return jnp.concatenate([near, far], axis=1)


def _toeplitz(vec, first, n_rows, n_cols, row_step):
    lead = vec.shape[:-1]
    lo = first - row_step * (n_rows - 1)
    front = max(-lo, 0)
    if front:
        vec = jnp.concatenate([jnp.broadcast_to(vec[..., :1], lead + (front,)), vec], axis=-1)
    span = row_step * (n_rows - 1) + n_cols
    w = vec[..., lo + front:lo + front + span]
    pitch = span + row_step
    flat = jnp.tile(w, n_rows + 1)[..., :n_rows * pitch]
    s = flat.reshape(lead + (n_rows, pitch))[..., :n_cols]
    return jnp.flip(s, axis=-2)


def _lanes_by_group(t):
    h, n, q = t.shape
    return t.reshape(NSA_N_KV, NSA_R, n, q).transpose(0, 2, 1, 3).reshape(NSA_N_KV, n, NSA_R * q)


def _overlap_matrix(nc, ns):
    c_start = np.arange(nc) * CMP_STRIDE
    c_end = c_start + CMP_BLOCK - 1
    s_start = np.arange(ns) * SEL_BLOCK
    ov = np.clip(np.minimum(c_end[:, None], s_start[None, :] + SEL_BLOCK - 1)
                 - np.maximum(c_start[:, None], s_start[None, :]) + 1, 0, None).astype(np.float32) / CMP_STRIDE
    return ov


def _masked_softmax_parts(parts, valids, axis):
    m = None
    for s, v in zip(parts, valids):
        pm = jnp.max(jnp.where(v, s, -jnp.inf), axis=axis, keepdims=True)
        m = pm if m is None else jnp.maximum(m, pm)
    m = jnp.where(m > -jnp.inf, m, 0.0)
    es = [jnp.where(v, jnp.exp(s - m), 0.0) for s, v in zip(parts, valids)]
    den = None
    for e in es:
        d = jnp.sum(e, axis=axis, keepdims=True)
        den = d if den is None else den + d
    inv = 1.0 / jnp.where(den > 0, den, 1.0)
    return [e * inv for e in es]


def _compress_kernel(*refs, n_pp):
    page_refs = refs[1:1 + n_pp]
    perm_ref, pe_ref, w1_ref, w2_ref, kc_ref, vc_ref, x_ref = refs[1 + n_pp:]
    step = pl.program_id(1)
    hb = PAGE_SIZE // CMP_STRIDE
    dh = NSA_HEAD_DIM
    for k in range(n_pp):
        pt = lax.dot_general(perm_ref[...], page_refs[k][...].astype(BF16), NT_DIMS, preferred_element_type=F32)
        row0 = pl.multiple_of((step * n_pp + k) * hb, hb)
        for l in range(CMP_STRIDE):
            row = pt[l * hb:(l + 1) * hb, :]
            for kv in range(2):
                for g in range(NSA_N_KV):
                    c0 = kv * NSA_KV_W + g * dh
                    x_ref[kv, g, pl.ds(row0, hb), l * dh:(l + 1) * dh] = row[:, c0:c0 + dh]

    @pl.when(step == pl.num_programs(1) - 1)
    def _():
        hid_w = CMP_HIDDEN
        for kv, out_ref in ((0, kc_ref), (1, vc_ref)):
            w1 = w1_ref[kv]
            w2 = w2_ref[kv]
            pe_w = jnp.dot(pe_ref[kv].astype(BF16), w1, preferred_element_type=F32)
            pe_term = pe_w[0:1, :hid_w] + pe_w[1:2, hid_w:]
            for g in range(NSA_N_KV):
                both = jnp.dot(x_ref[kv, g].astype(BF16), w1, preferred_element_type=F32)
                bot = both[:, hid_w:]
                bot = jnp.concatenate([bot[1:], jnp.zeros((1, hid_w), F32)], axis=0)
                hid = _silu(both[:, :hid_w] + bot + pe_term).astype(BF16)
                out_ref[0, :, g * dh:(g + 1) * dh] = jnp.dot(hid, w2, preferred_element_type=F32)


def _compress(src, page_index, bsz, n_pages, ids, pe, w1, w2):
    n_pp = math.gcd(PAGES_PER_STEP, n_pages)
    hb = PAGE_SIZE // CMP_STRIDE
    nh = n_pages * hb
    half = CMP_STRIDE * NSA_HEAD_DIM
    blk = (None,) * (src.ndim - 2) + (2 * NSA_KV_W, PAGE_SIZE)
    page_specs = [pl.BlockSpec(blk, functools.partial(lambda b, s, ids, k: page_index(b, s * n_pp + k, ids), k=k))
                  for k in range(n_pp)]
    perm = np.zeros((PAGE_SIZE, PAGE_SIZE), np.float32)
    for l in range(CMP_STRIDE):
        for i in range(hb):
            perm[l * hb + i, CMP_STRIDE * i + l] = 1.0
    w1cat = jnp.concatenate([w1[:, :half], w1[:, half:]], axis=-1).astype(BF16)
    grid_spec = pltpu.PrefetchScalarGridSpec(
        num_scalar_prefetch=1,
        grid=(bsz, n_pages // n_pp),
        in_specs=page_specs + [
            pl.BlockSpec((PAGE_SIZE, PAGE_SIZE), lambda b, s, ids: (0, 0)),
            pl.BlockSpec((2, 2, half), lambda b, s, ids: (0, 0, 0)),
            pl.BlockSpec((2, half, 2 * CMP_HIDDEN), lambda b, s, ids: (0, 0, 0)),
            pl.BlockSpec((2, CMP_HIDDEN, NSA_HEAD_DIM), lambda b, s, ids: (0, 0, 0)),
        ],
        out_specs=(pl.BlockSpec((1, nh, NSA_KV_W), lambda b, s, ids: (b, 0, 0)),
                   pl.BlockSpec((1, nh, NSA_KV_W), lambda b, s, ids: (b, 0, 0))),
        scratch_shapes=[pltpu.VMEM((2, NSA_N_KV, nh, half), F32)],
    )
    return pl.pallas_call(
        functools.partial(_compress_kernel, n_pp=n_pp),
        out_shape=(jax.ShapeDtypeStruct((bsz, nh, NSA_KV_W), F32), jax.ShapeDtypeStruct((bsz, nh, NSA_KV_W), F32)),
        grid_spec=grid_spec,
        compiler_params=_cparams(("parallel", "arbitrary")),
        name="nsa_compress",
    )(ids, *([src] * n_pp), jnp.asarray(perm, BF16), pe.reshape(2, 2, half).astype(F32), w1cat, w2.astype(BF16))


def _nsa_attn_kernel(qt_ref, kv_ref, kc_ref, vc_ref, ov_ref, trel_ref, tcn_ref, cfar_ref, gate_ref, z0_ref, z1_ref,
                     z2_ref, o_ref, qaug_ref, imp_ref, m_ref, l_ref, acc_ref, ob_ref, s_ref, *, n_sel, n_cmp_rows):
    t = pl.program_id(1)
    dh = NSA_HEAD_DIM
    rq = NSA_R * TQ
    scale = dh ** -0.5
    far_idx = N_REL_TILES
    ns8 = _round_up(n_sel, 2 * SUBLANES)
    assert ns8 < LANES
    near0 = pl.multiple_of(t * (TQ // CMP_STRIDE), SUBLANES)
    far_rows = slice(CMP_FRONT, CMP_FRONT + n_cmp_rows)

    def cmp_branch(g, qt, with_far):
        gs = slice(g * dh, (g + 1) * dh)
        kc_near = kc_ref[0, pl.ds(near0, CMP_NEAR), gs].astype(BF16)
        vc_near = vc_ref[0, pl.ds(near0, CMP_NEAR), gs].astype(BF16)
        cn = lax.broadcasted_iota(jnp.int32, (CMP_NEAR, rq), 0)
        qn = lax.broadcasted_iota(jnp.int32, (CMP_NEAR, rq), 1) % TQ
        dist_n = qn - CMP_STRIDE * cn + (CMP_STRIDE * CMP_FRONT - (CMP_BLOCK - 1))
        parts = [jnp.dot(kc_near, qt, preferred_element_type=F32) + tcn_ref[g]]
        valids = [(dist_n >= 0) & (cn + near0 >= CMP_FRONT)]
        if with_far:
            kc_far = kc_ref[0, far_rows, gs].astype(BF16)
            parts.append(jnp.dot(kc_far, qt, preferred_element_type=F32) + cfar_ref[g, 0:1, :])
            valids.append(lax.broadcasted_iota(jnp.int32, (n_cmp_rows, rq), 0) + CMP_FRONT < near0)
        ps = _masked_softmax_parts(parts, valids, axis=0)

        def importance(ov, p):
            pg = sum(p[:, r * TQ:(r + 1) * TQ] for r in range(NSA_R))
            hi = pg.astype(BF16)
            lo = (pg - hi.astype(F32)).astype(BF16)
            ov16 = ov.astype(BF16)
            return (lax.dot_general(ov16, hi, TN_DIMS, preferred_element_type=F32)
                    + lax.dot_general(ov16, lo, TN_DIMS, preferred_element_type=F32))

        o_c = lax.dot_general(vc_near, ps[0].astype(BF16), TN_DIMS, preferred_element_type=F32)
        imp = importance(ov_ref[pl.ds(near0, CMP_NEAR), :], ps[0])
        if with_far:
            vc_far = vc_ref[0, far_rows, gs].astype(BF16)
            o_c = o_c + lax.dot_general(vc_far, ps[1].astype(BF16), TN_DIMS, preferred_element_type=F32)
            imp = imp + importance(ov_ref[far_rows, :], ps[1])
        ob_ref[0, g] = o_c
        imp_ref[...] = imp

    qts = []
    sid = lax.broadcasted_iota(jnp.int32, (ns8, TQ), 0)
    qpos = t * TQ + lax.broadcasted_iota(jnp.int32, (ns8, TQ), 1)
    cur = qpos // SEL_BLOCK
    forced = (sid == 0) | (sid == cur) | (sid == cur - 1)
    reachable = sid * SEL_BLOCK <= qpos
    sub = lax.broadcasted_iota(jnp.int32, (SUBLANES, TQ), 0)
    tail_row = lax.broadcasted_iota(jnp.int32, (LANES - ns8, rq), 0)
    for g in range(NSA_N_KV):
        q32 = jnp.concatenate([qt_ref[(g * NSA_R + r) * dh:(g * NSA_R + r + 1) * dh, :] for r in range(NSA_R)], axis=1)
        qt = (q32 * scale).astype(BF16)
        qts.append((q32 * (scale * LOG2E)).astype(BF16))
        has_far = near0 > CMP_FRONT

        @pl.when(has_far)
        def _():
            cmp_branch(g, qt, True)

        @pl.when(jnp.logical_not(has_far))
        def _():
            cmp_branch(g, qt, False)

        imp = jnp.where(forced, FORCE_SCORE, imp_ref[0:ns8, :])
        imp = jnp.where(reachable, imp, -FORCE_SCORE)
        blocks = [imp[SUBLANES * kb:SUBLANES * (kb + 1)] for kb in range(ns8 // SUBLANES)]
        cnts = [jnp.zeros((SUBLANES, TQ), F32) for _ in blocks]
        for sp in range(n_sel):
            other = imp[sp:sp + 1, :]
            for kb, blk in enumerate(blocks):
                if kb > sp // SUBLANES:
                    ahead = other >= blk
                elif kb < sp // SUBLANES:
                    ahead = other > blk
                else:
                    ahead = (other > blk) | ((other == blk) & (sub > sp % SUBLANES))
                cnts[kb] = cnts[kb] + jnp.where(ahead, 1.0, 0.0)
        cnt = jnp.concatenate(cnts, axis=0)
        unsel = jnp.where((cnt < SEL_TOPK) & (imp > -0.5 * FORCE_SCORE), 0.0, NEG)
        zero = jnp.zeros((dh, rq), BF16)
        q_rows = [qts[g], zero] if g % 2 == 0 else [zero, qts[g]]
        tail = jnp.where(tail_row == 0, trel_ref[g, far_idx, 0:1, :], 0.0)
        qaug_ref[g] = jnp.concatenate(q_rows + [jnp.concatenate([unsel] * NSA_R, axis=1).astype(BF16),
                                                tail.astype(BF16)], axis=0)

    def reset():
        m_ref[...] = jnp.full(m_ref.shape, NEG, F32)
        l_ref[...] = jnp.zeros(l_ref.shape, F32)
        acc_ref[...] = jnp.zeros(acc_ref.shape, F32)

    def online_step(g, s, v_t):
        m_old = m_ref[g]
        m_new = jnp.maximum(m_old, jnp.max(s, axis=0, keepdims=True))
        alpha = jnp.exp2(m_old - m_new)
        p = jnp.exp2(s - m_new)
        l_ref[g] = l_ref[g] * alpha + jnp.sum(p, axis=0, keepdims=True)
        acc_ref[g] = acc_ref[g] * alpha + lax.dot_general(v_t, p.astype(BF16), TN_DIMS, preferred_element_type=F32)
        m_ref[g] = m_new

    def mask_heads(s, ok):
        return jnp.concatenate([jnp.where(ok, s[:, r * TQ:(r + 1) * TQ], NEG) for r in range(NSA_R)], axis=1)

    reset()
    kt2 = SEL_STEP_TILES * KT

    def sel_step(kt, mode, n_tiles=SEL_STEP_TILES):
        nk = n_tiles * KT
        k0 = pl.multiple_of(kt * kt2, kt2)
        e_blk = lax.broadcasted_iota(jnp.int32, (nk, LANES), 0) // SEL_BLOCK
        e_lane = lax.broadcasted_iota(jnp.int32, (nk, LANES), 1)
        onehot = e_lane == kt * (kt2 // SEL_BLOCK) + e_blk
        if mode == "far":
            onehot = onehot | (e_lane == ns8)
        e_aug = jnp.where(onehot, 1.0, 0.0).astype(BF16)
        delta = t - SEL_STEP_TILES * kt
        idx = [jnp.clip(delta - i, 0, far_idx) for i in range(n_tiles)]
        for pair in range(NSA_N_KV // 2):
            k_aug = jnp.concatenate([kv_ref[pl.ds(WINDOW + k0, nk), pair * LANES:(pair + 1) * LANES], e_aug], axis=1)
            for g in (2 * pair, 2 * pair + 1):
                s = jnp.dot(k_aug, qaug_ref[g], preferred_element_type=F32)
                if mode != "far":
                    s = s + jnp.concatenate([trel_ref[g, i] for i in idx], axis=0)
                if mode == "last":
                    kj = lax.broadcasted_iota(jnp.int32, (nk, TQ), 0)
                    qi = lax.broadcasted_iota(jnp.int32, (nk, TQ), 1)
                    s = mask_heads(s, t * TQ + qi >= k0 + kj)
                s_ref[g, 0:nk, :] = s
        for g in range(NSA_N_KV):
            online_step(g, s_ref[g, 0:nk, :],
                        kv_ref[pl.ds(WINDOW + k0, nk), NSA_KV_W + g * dh:NSA_KV_W + (g + 1) * dh])

    n_steps = t // SEL_STEP_TILES + 1
    n_far_steps = jnp.maximum(t - far_idx + 1, 0) // SEL_STEP_TILES

    def far_body(kt, carry):
        sel_step(kt, "far")
        return carry

    def near_body(kt, carry):
        sel_step(kt, "near")
        return carry

    lax.fori_loop(0, n_far_steps, far_body, 0)
    lax.fori_loop(n_far_steps, n_steps - 1, near_body, 0)
    short_last = t % SEL_STEP_TILES < SEL_STEP_TILES // 2

    @pl.when(short_last)
    def _():
        sel_step(n_steps - 1, "last", SEL_STEP_TILES // 2)

    @pl.when(jnp.logical_not(short_last))
    def _():
        sel_step(n_steps - 1, "last")
    for g in range(NSA_N_KV):
        ob_ref[1, g] = acc_ref[g] / l_ref[g]

    assert SEL_STEP_TILES >= 2

    reset()
    row0 = pl.multiple_of(t * TQ, TQ)
    for deltas in ((0,), (2, 1), (4, 3)):
        n_keys = KT * len(deltas)
        kjw = lax.broadcasted_iota(jnp.int32, (n_keys, TQ), 0)
        qiw = lax.broadcasted_iota(jnp.int32, (n_keys, TQ), 1)
        dist = deltas[0] * KT + qiw - kjw
        ok = (dist >= 0) & (dist <= WINDOW) & (row0 + kjw >= deltas[0] * KT)
        k0 = row0 + (WINDOW - deltas[0] * KT)
        for g in range(NSA_N_KV):
            k_t = kv_ref[pl.ds(k0, n_keys), 2 * NSA_KV_W + g * dh:2 * NSA_KV_W + (g + 1) * dh]
            s = jnp.dot(k_t, qts[g], preferred_element_type=F32)
            s = s + jnp.concatenate([trel_ref[g, d] for d in deltas], axis=0)
            s_ref[g, 0:n_keys, :] = mask_heads(s, ok)
        for g in range(NSA_N_KV):
            online_step(g, s_ref[g, 0:n_keys, :],
                        kv_ref[pl.ds(k0, n_keys), 3 * NSA_KV_W + g * dh:3 * NSA_KV_W + (g + 1) * dh])
    for g in range(NSA_N_KV):
        ob_ref[2, g] = acc_ref[g] / l_ref[g]

    gate = jax.nn.sigmoid(gate_ref[...])
    z_refs = (z0_ref, z1_ref, z2_ref)
    for h in range(NSA_N_HEADS):
        g, r = divmod(h, NSA_R)
        mixed = None
        for c in range(N_BRANCH):
            o_hc = ob_ref[c, g, :, r * TQ:(r + 1) * TQ]
            term = o_hc * _silu(z_refs[c][h * dh:(h + 1) * dh, :]) * gate[h * N_BRANCH + c:h * N_BRANCH + c + 1, :]
            mixed = term if mixed is None else mixed + term
        o_ref[h * dh:(h + 1) * dh, :] = mixed.astype(o_ref.dtype)


def _nsa_prompt_tables(rel_bias):
    n_dist = max((N_REL_TILES + 1) * KT, CMP_STRIDE * CMP_FRONT + TQ)
    vec = _bias_by_dist(rel_bias, n_dist)
    near = _toeplitz(vec, 0, KT, N_REL_TILES * TQ, 1).reshape(NSA_N_HEADS, KT, N_REL_TILES, TQ)
    near = _lanes_by_group(near.transpose(0, 2, 1, 3).reshape(NSA_N_HEADS, N_REL_TILES * KT, TQ))
    near = near.reshape(NSA_N_KV, N_REL_TILES, KT, NSA_R * TQ)
    far = jnp.broadcast_to(rel_bias.astype(F32)[REL_BUCKETS - 1].reshape(NSA_N_HEADS, 1, 1), (NSA_N_HEADS, KT, TQ))
    far = _lanes_by_group(far)
    trel = jnp.concatenate([near, far[:, None]], axis=1) * LOG2E
    tcn = _lanes_by_group(_toeplitz(vec, CMP_STRIDE * CMP_FRONT - (CMP_BLOCK - 1), CMP_NEAR, TQ, CMP_STRIDE))
    return trel, tcn, far[:, :SUBLANES, :]


def _nsa_layer_prompt(xp, prm):
    norm_w, w_in, pe, w1, w2, w_out, rel_bias = prm
    bsz, t, d = xp.shape
    assert t % (SEL_STEP_TILES * KT) == 0 and t % PAGE_SIZE == 0 and t >= WINDOW
    n_tiles = t // TQ
    ns = t // SEL_BLOCK
    assert ns <= LANES
    x2 = xp.reshape(bsz * t, d)
    w = _nsa_in_weights(w_in)
    ut = _norm_matmul_t(x2, norm_w, w.T.astype(BF16), bsz, _pick_tn(NSA_N_PAD, 1152))
    w_kv = jnp.concatenate([w[:, COL_KV + 2 * NSA_KV_W:COL_KV + 4 * NSA_KV_W], w[:, COL_KVW:COL_KVW + 2 * NSA_KV_W]],
                           axis=1).astype(BF16)
    kvn = _norm_matmul(x2, norm_w, w_kv, 4 * NSA_KV_W, out_dtype=BF16)
    t_pad = WINDOW + t
    kvn = jnp.pad(kvn.reshape(bsz, t, 4 * NSA_KV_W), ((0, 0), (WINDOW, 0), (0, 0))).reshape(bsz * t_pad, 4 * NSA_KV_W)
    n_pages = t // PAGE_SIZE
    cmp_blk = COL_KV // (2 * NSA_KV_W)
    kc, vc = _compress(ut, lambda b, p, ids: (b, cmp_blk, p), bsz, n_pages, jnp.zeros((1, 1), jnp.int32), pe, w1, w2)
    nh = kc.shape[1]
    n_far = _round_up(CMP_FRONT + nh, LANES)
    padc = ((0, 0), (CMP_FRONT, n_far - CMP_FRONT - nh), (0, 0))
    kcp = jnp.pad(kc, padc)
    vcp = jnp.pad(vc, padc)
    ov = np.zeros((n_far, LANES), np.float32)
    ov[CMP_FRONT:CMP_FRONT + nh - 1, :ns] = _overlap_matrix(nh - 1, ns)
    trel, tcn, cfar = _nsa_prompt_tables(rel_bias)
    rq = NSA_R * TQ
    const2 = lambda b, i: (0, 0)
    const3 = lambda b, i: (0, 0, 0)
    const4 = lambda b, i: (0, 0, 0, 0)
    zb = COL_Z // NSA_Q_W
    once = dict(pipeline_mode=pl.Buffered(1))
    mixed_t = pl.pallas_call(
        functools.partial(_nsa_attn_kernel, n_sel=ns, n_cmp_rows=max(nh - CMP_NEAR, SUBLANES)),
        out_shape=jax.ShapeDtypeStruct((bsz, NSA_Q_W, t), BF16),
        grid=(bsz, n_tiles),
        in_specs=[
            pl.BlockSpec((None, NSA_Q_W, TQ), lambda b, i: (b, 0, i)),
            pl.BlockSpec((t_pad, 4 * NSA_KV_W), lambda b, i: (b, 0), **once),
            pl.BlockSpec((1, n_far, NSA_KV_W), lambda b, i: (b, 0, 0), **once),
            pl.BlockSpec((1, n_far, NSA_KV_W), lambda b, i: (b, 0, 0), **once),
            pl.BlockSpec((n_far, LANES), const2, **once),
            pl.BlockSpec((NSA_N_KV, N_REL_TILES + 1, KT, rq), const4, **once),
            pl.BlockSpec((NSA_N_KV, CMP_NEAR, rq), const3, **once),
            pl.BlockSpec((NSA_N_KV, SUBLANES, rq), const3, **once),
            pl.BlockSpec((None, LANES, TQ), lambda b, i: (b, COL_GATE // LANES, i)),
            pl.BlockSpec((None, NSA_Q_W, TQ), lambda b, i: (b, zb, i)),
            pl.BlockSpec((None, NSA_Q_W, TQ), lambda b, i: (b, zb + 1, i)),
            pl.BlockSpec((None, NSA_Q_W, TQ), lambda b, i: (b, zb + 2, i)),
        ],
        out_specs=pl.BlockSpec((None, NSA_Q_W, TQ), lambda b, i: (b, 0, i)),
        scratch_shapes=[
            pltpu.VMEM((NSA_N_KV, 2 * LANES, rq), BF16),
            pltpu.VMEM((LANES, TQ), F32),
            pltpu.VMEM((NSA_N_KV, 1, rq), F32),
            pltpu.VMEM((NSA_N_KV, 1, rq), F32),
            pltpu.VMEM((NSA_N_KV, NSA_HEAD_DIM, rq), F32),
            pltpu.VMEM((N_BRANCH, NSA_N_KV, NSA_HEAD_DIM, rq), F32),
            pltpu.VMEM((NSA_N_KV, SEL_STEP_TILES * KT, rq), F32),
        ],
        compiler_params=_cparams(("parallel", "arbitrary")),
        name="nsa_attn",
    )(ut, kvn, kcp, vcp, jnp.asarray(ov), trel, tcn, cfar, ut, ut, ut, ut)
    out = _matmul_res_t(mixed_t, w_out.astype(BF16), x2).reshape(bsz, t, d)
    return out, ut


def _block_diag_q(q2):
    hg = lax.broadcasted_iota(jnp.int32, q2.shape, 0) // NSA_R
    return jnp.concatenate([jnp.where(hg == g, q2, 0.0) for g in range(NSA_N_KV)], axis=1)


def _group_lanes(o):
    dh = NSA_HEAD_DIM
    hg = lax.broadcasted_iota(jnp.int32, (o.shape[0], dh), 0) // NSA_R
    return sum(jnp.where(hg == g, o[:, g * dh:(g + 1) * dh], 0.0) for g in range(NSA_N_KV))


def _nsa_step_kernel(*refs, n_pp, past_len, n_cmp, n_sel_pad):
    page_refs = refs[1:1 + n_pp]
    (q_ref, kvn_ref, kwn_ref, gate_ref, z_ref, kc_ref, vc_ref, win_ref, ovp_ref, bsel_ref, bcmp_ref, bwin_ref, b0_ref,
     o_ref, sel_ref, m_ref, l_ref, acc_ref, oc_ref) = refs[1 + n_pp:]
    step = pl.program_id(1)
    nh = NSA_N_HEADS
    dh = NSA_HEAD_DIM
    qbd = _block_diag_q(q_ref[0] * (dh ** -0.5))
    qbd16 = qbd.astype(BF16)
    cur = past_len // SEL_BLOCK

    @pl.when(step == 0)
    def _():
        s_c = lax.dot_general(qbd16, kc_ref[0].astype(BF16), NT_DIMS, preferred_element_type=F32) + bcmp_ref[...]
        cid = lax.broadcasted_iota(jnp.int32, s_c.shape, 1)
        (p_c,) = _masked_softmax_parts([s_c], [cid < n_cmp], axis=1)
        oc_ref[...] = _group_lanes(jnp.dot(p_c.astype(BF16), vc_ref[0].astype(BF16), preferred_element_type=F32))
        gr = lax.broadcasted_iota(jnp.int32, (SUBLANES, nh), 0)
        gh = lax.broadcasted_iota(jnp.int32, (SUBLANES, nh), 1) // NSA_R
        pg = jnp.dot((gr == gh).astype(F32), p_c, precision=HIGHEST, preferred_element_type=F32)
        imp = jnp.dot(pg, ovp_ref[...], precision=HIGHEST, preferred_element_type=F32)
        sid = lax.broadcasted_iota(jnp.int32, imp.shape, 1)
        forced = (sid == 0) | (sid == cur) | (sid == cur - 1)
        imp = jnp.where(forced, FORCE_SCORE, imp)
        imp = jnp.where(sid * SEL_BLOCK <= past_len, imp, -FORCE_SCORE)
        er = lax.broadcasted_iota(jnp.int32, (n_sel_pad, n_sel_pad), 0)
        ec = lax.broadcasted_iota(jnp.int32, (n_sel_pad, n_sel_pad), 1)
        rows = []
        for g in range(NSA_N_KV):
            row = imp[g:g + 1, :]
            col = jnp.sum(jnp.where(er == ec, row, 0.0), axis=1, keepdims=True)
            ahead = (col > row) | ((col == row) & (er < ec))
            cnt = jnp.sum(ahead.astype(jnp.int32), axis=0, keepdims=True)
            sel = ((cnt < SEL_TOPK) & (row > -0.5 * FORCE_SCORE)).astype(F32)
            rows.extend([sel] * NSA_R)
        sel_ref[...] = jnp.concatenate(rows, axis=0)
        m_ref[...] = jnp.full(m_ref.shape, NEG, F32)
        l_ref[...] = jnp.zeros(l_ref.shape, F32)
        acc_ref[...] = jnp.zeros(acc_ref.shape, F32)

    def online(s):
        m_old = m_ref[...]
        m_new = jnp.maximum(m_old, jnp.max(s, axis=-1, keepdims=True))
        alpha = jnp.exp(m_old - m_new)
        pr = jnp.exp(s - m_new)
        l_ref[...] = l_ref[...] * alpha + jnp.sum(pr, axis=-1, keepdims=True)
        m_ref[...] = m_new
        return alpha, pr

    k_all = jnp.concatenate([r[0:NSA_KV_W, :] for r in page_refs], axis=1).astype(BF16)
    v_all = jnp.concatenate([r[NSA_KV_W:2 * NSA_KV_W, :] for r in page_refs], axis=1).astype(BF16)
    n_tok = n_pp * PAGE_SIZE
    s = jnp.dot(qbd16, k_all, preferred_element_type=F32) + bsel_ref[0]
    eb = lax.broadcasted_iota(jnp.int32, (n_sel_pad, n_tok), 0)
    ej = lax.broadcasted_iota(jnp.int32, (n_sel_pad, n_tok), 1)
    expand = (eb == step * (n_tok // SEL_BLOCK) + ej // SEL_BLOCK).astype(BF16)
    chosen = jnp.dot(sel_ref[...].astype(BF16), expand, preferred_element_type=F32)
    s = jnp.where(chosen > 0.5, s, NEG)
    alpha, pr = online(s)
    acc_ref[...] = acc_ref[...] * alpha + lax.dot_general(pr.astype(BF16), v_all, NT_DIMS, preferred_element_type=F32)

    @pl.when(step == pl.num_programs(1) - 1)
    def _():
        b0 = b0_ref[:, 0:1]
        kn = kvn_ref[0, :, 2 * NSA_KV_W:3 * NSA_KV_W]
        vn = kvn_ref[0, :, 3 * NSA_KV_W:4 * NSA_KV_W]
        s_n = jnp.sum(qbd * kn, axis=-1, keepdims=True) + b0
        s_n = jnp.where(sel_ref[:, cur:cur + 1] > 0.5, s_n, NEG)
        alpha, pr = online(s_n)
        acc = acc_ref[...] * alpha + pr * vn
        o_s = _group_lanes(acc / l_ref[...])
        wk = win_ref[0, 0:NSA_KV_W, :].astype(BF16)
        wv = win_ref[0, NSA_KV_W:2 * NSA_KV_W, :].astype(BF16)
        n_win = win_ref.shape[2]
        s_w = jnp.dot(qbd16, wk, preferred_element_type=F32) + bwin_ref[...]
        wi = lax.broadcasted_iota(jnp.int32, s_w.shape, 1)
        s_wn = jnp.sum(qbd * kwn_ref[0, :, 0:NSA_KV_W], axis=-1, keepdims=True) + b0
        p_w, p_wn = _masked_softmax_parts([s_w, s_wn], [n_win - wi <= WINDOW, jnp.full(s_wn.shape, True)], axis=1)
        o_w = _group_lanes(lax.dot_general(p_w.astype(BF16), wv, NT_DIMS, preferred_element_type=F32)
                           + p_wn * kwn_ref[0, :, NSA_KV_W:2 * NSA_KV_W])
        gate = jax.nn.sigmoid(gate_ref[0])
        mixed = (oc_ref[...] * _silu(z_ref[0, 0]) * gate[:, 0:1] + o_s * _silu(z_ref[0, 1]) * gate[:, 1:2]
                 + o_w * _silu(z_ref[0, 2]) * gate[:, 2:3])
        o_ref[0] = mixed


def _nsa_layer_sample(xs_in, cache_t, win_t, page_table, prm, layer, n_layers):
    norm_w, w_in, pe, w1, w2, w_out, rel_bias = prm
    bsz, t, d = xs_in.shape
    assert t == 1
    n_pages = page_table.shape[1]
    past_len = n_pages * PAGE_SIZE
    n_win = win_t.shape[2]
    nh, dh = NSA_N_HEADS, NSA_HEAD_DIM
    x2 = xs_in.reshape(bsz, d)
    u = _norm_matmul(x2, norm_w, _nsa_in_weights(w_in).astype(BF16), _pick_tn(NSA_N_PAD, 1152))
    ids = (page_table * n_layers + layer).astype(jnp.int32)
    kc, vc = _compress(cache_t, lambda b, p, ids: (ids[b, p], 0, 0), bsz, n_pages, ids, pe, w1, w2)
    n_cmp_pad = kc.shape[1]
    n_cmp = n_cmp_pad - 1
    ns = past_len // SEL_BLOCK + 1
    n_sel_pad = _round_up(ns, LANES)
    ov = np.zeros((n_cmp_pad, n_sel_pad), np.float32)
    ov[:n_cmp, :ns] = _overlap_matrix(n_cmp, ns)
    n_pp = math.gcd(PAGES_PER_STEP, n_pages)
    n_steps = n_pages // n_pp
    n_tok = n_pp * PAGE_SIZE
    vec = _bias_by_dist(rel_bias, past_len + 1)
    bsel = jnp.moveaxis(jnp.flip(vec[:, 1:], axis=1).reshape(nh, n_steps, n_tok), 0, 1)
    bcmp = jnp.flip(vec[:, :past_len - (CMP_BLOCK - 1) + 1], axis=1)[:, ::CMP_STRIDE]
    bcmp = jnp.pad(bcmp, ((0, 0), (0, n_cmp_pad - bcmp.shape[1])))
    bwin = jnp.flip(vec[:, 1:n_win + 1], axis=1)
    b0 = jnp.broadcast_to(vec[:, 0:1], (nh, LANES))
    q3 = u[:, COL_Q:COL_Q + NSA_Q_W].reshape(bsz, nh, dh)
    kv_new = u[:, COL_KV:COL_KV + 4 * NSA_KV_W]
    kw_new = u[:, COL_KVW:COL_KVW + 2 * NSA_KV_W]
    gate = u[:, COL_GATE:COL_GATE + nh * N_BRANCH].reshape(bsz, nh, N_BRANCH)
    z4 = u[:, COL_Z:COL_Z + N_BRANCH * NSA_Q_W].reshape(bsz, N_BRANCH, nh, dh)
    per_b3 = lambda b, s, ids: (b, 0, 0)
    const2 = lambda b, s, ids: (0, 0)
    page_specs = [pl.BlockSpec((None, 2 * NSA_KV_W, PAGE_SIZE),
                               functools.partial(lambda b, s, ids, k: (ids[b, s * n_pp + k], 1, 0), k=k))
                  for k in range(n_pp)]
    grid_spec = pltpu.PrefetchScalarGridSpec(
        num_scalar_prefetch=1,
        grid=(bsz, n_steps),
        in_specs=page_specs + [
            pl.BlockSpec((1, nh, dh), per_b3),
            pl.BlockSpec((1, 1, 4 * NSA_KV_W), per_b3),
            pl.BlockSpec((1, 1, 2 * NSA_KV_W), per_b3),
            pl.BlockSpec((1, nh, N_BRANCH), per_b3),
            pl.BlockSpec((1, N_BRANCH, nh, dh), lambda b, s, ids: (b, 0, 0, 0)),
            pl.BlockSpec((1, n_cmp_pad, NSA_KV_W), per_b3),
            pl.BlockSpec((1, n_cmp_pad, NSA_KV_W), per_b3),
            pl.BlockSpec((1, 2 * NSA_KV_W, n_win), per_b3),
            pl.BlockSpec((n_cmp_pad, n_sel_pad), const2),
            pl.BlockSpec((1, nh, n_tok), lambda b, s, ids: (s, 0, 0)),
            pl.BlockSpec((nh, n_cmp_pad), const2),
            pl.BlockSpec((nh, n_win), const2),
            pl.BlockSpec((nh, LANES), const2),
        ],
        out_specs=pl.BlockSpec((1, nh, dh), per_b3),
        scratch_shapes=[
            pltpu.VMEM((nh, n_sel_pad), F32),
            pltpu.VMEM((nh, 1), F32),
            pltpu.VMEM((nh, 1), F32),
            pltpu.VMEM((nh, NSA_KV_W), F32),
            pltpu.VMEM((nh, dh), F32),
        ],
    )
    mixed = pl.pallas_call(
        functools.partial(_nsa_step_kernel, n_pp=n_pp, past_len=past_len, n_cmp=n_cmp, n_sel_pad=n_sel_pad),
        out_shape=jax.ShapeDtypeStruct((bsz, nh, dh), F32),
        grid_spec=grid_spec,
        compiler_params=_cparams(("parallel", "arbitrary")),
        name="nsa_step",
    )(ids, *([cache_t] * n_pp), q3, kv_new.reshape(bsz, 1, -1), kw_new.reshape(bsz, 1, -1), gate, z4, kc, vc, win_t,
      jnp.asarray(ov), bsel, bcmp, bwin, b0)
    out = _matmul_res(mixed.reshape(bsz, NSA_Q_W).astype(BF16), w_out.astype(BF16), x2).reshape(bsz, 1, d)
    return out, kv_new, kw_new


def kernel(x_prompt, x_sample, state_ssm, state_conv, cache_kv, cache_win, page_table, rel_bias, final_norm,
           ssd_norm, ssd_w_in, ssd_conv_w, ssd_conv_b, ssd_dt_bias, ssd_a_log, ssd_d, ssd_gnorm, ssd_w_out,
           nsa_norm, nsa_w_in, nsa_cmp_pe, nsa_cmp_w1, nsa_cmp_w2, nsa_w_out):
    depth = ssd_norm.shape[0] + nsa_norm.shape[0]
    g, dh = NSA_N_KV, NSA_HEAD_DIM
    xp, xs = x_prompt, x_sample
    bp, tp, d = xp.shape
    bs = xs.shape[0]
    n_phys, n_nsa = cache_kv.shape[:2]
    win_len_s = cache_win.shape[2]
    win_len_p = min(WINDOW, tp)
    cache_t = cache_kv.transpose(0, 1, 3, 4, 5, 2).reshape(n_phys * n_nsa, 4 * NSA_KV_W, PAGE_SIZE)
    cwin_t = cache_win.transpose(0, 1, 3, 4, 5, 2).reshape(n_nsa, bs, 2 * NSA_KV_W, win_len_s)
    ssm_p, conv_p, kv_p, win_p = [], [], [], []
    ssm_s, conv_s, kv_s, win_s = [], [], [], []
    for layer in range(depth):
        j = layer // 2
        if layer % 2 == 0:
            prm = (ssd_norm[j], ssd_w_in[j], ssd_conv_w[j], ssd_conv_b[j], ssd_dt_bias[j], ssd_a_log[j],
                   ssd_d[j], ssd_gnorm[j], ssd_w_out[j])
            xp, c, s = _ssd_layer_prompt(xp, prm)
            conv_p.append(c)
            ssm_p.append(s)
            xs, c, s = _ssd_layer_sample(xs, state_conv[j], state_ssm[j], prm)
            conv_s.append(c)
            ssm_s.append(s)
        else:
            prm = (nsa_norm[j], nsa_w_in[j], nsa_cmp_pe[j], nsa_cmp_w1[j], nsa_cmp_w2[j], nsa_w_out[j], rel_bias)
            xp, ut = _nsa_layer_prompt(xp, prm)
            kv_p.append(ut[:, COL_KV:COL_KV + 4 * NSA_KV_W, :])
            win_p.append(ut[:, COL_KVW:COL_KVW + 2 * NSA_KV_W, tp - win_len_p:])
            xs, kv_new, kw_new = _nsa_layer_sample(xs, cache_t, cwin_t[j], page_table, prm, j, n_nsa)
            kv_s.append(kv_new.reshape(bs, 1, 4, g, dh))
            win_s.append(jnp.concatenate([cwin_t[j][:, :, 1:], kw_new[:, :, None]], axis=2))
    y_prompt = _rmsnorm(xp.reshape(bp * tp, d), final_norm).reshape(bp, tp, d)
    y_sample = _rmsnorm(xs.reshape(bs, d), final_norm).reshape(bs, 1, d)
    kv_prompt = jnp.stack(kv_p, axis=1).reshape(bp, n_nsa, 4, g, dh, tp).transpose(0, 5, 1, 2, 3, 4)
    win_prompt = jnp.stack(win_p).reshape(n_nsa, bp, 2, g, dh, win_len_p).transpose(0, 1, 5, 2, 3, 4)
    win_sample = jnp.stack(win_s).reshape(n_nsa, bs, 2, g, dh, win_len_s).transpose(0, 1, 5, 2, 3, 4)
    return (y_prompt, y_sample, jnp.stack(ssm_p), jnp.stack(conv_p), kv_prompt, win_prompt,
            jnp.stack(ssm_s), jnp.stack(conv_s), jnp.stack(kv_s, axis=2), win_sample)
```

```python
import functools
import math

import jax
import jax.numpy as jnp
import numpy as np
from jax import lax
from jax.experimental import pallas as pl
from jax.experimental.pallas import tpu as pltpu

F32 = jnp.float32
BF16 = jnp.bfloat16
HIGHEST = lax.Precision.HIGHEST

NORM_EPS = 1e-6
SSD_HEAD_DIM = 64
SSD_N_GROUPS = 8
SSD_D_STATE = 128
SSD_CONV_W = 4
SSD_CHUNK = 128
NSA_N_HEADS = 16
NSA_HEAD_DIM = 64
NSA_N_KV = 4
NSA_R = NSA_N_HEADS // NSA_N_KV
N_BRANCH = 3
CMP_BLOCK = 32
CMP_STRIDE = 16
CMP_HIDDEN = 128
SEL_BLOCK = 64
SEL_TOPK = 16
WINDOW = 512
FORCE_SCORE = 1e4
REL_BUCKETS = 32
REL_MAX_DIST = 1024
PAGE_SIZE = 128

LANES = 128
SUBLANES = 8
VMEM_LIMIT = 56 * 1024 * 1024

NEG = -1e30
LOG2E = math.log2(math.e)
NT_DIMS = (((1,), (1,)), ((), ()))
TN_DIMS = (((0,), (0,)), ((), ()))


def _cparams(sem):
    return pltpu.CompilerParams(dimension_semantics=sem, vmem_limit_bytes=VMEM_LIMIT)


def _round_up(x, m):
    return (x + m - 1) // m * m


def _silu(x):
    return x * jax.nn.sigmoid(x)


def _pick_tm(m, cap, quantum=SUBLANES):
    if m <= cap:
        return m
    for tm in range(cap // quantum * quantum, 0, -quantum):
        if m % tm == 0:
            return tm
    raise ValueError(f"no row tile for {m}")


def _pick_tn(n, cap=1024):
    best = LANES
    for tn in range(LANES, cap + 1, LANES):
        if n % tn == 0:
            best = tn
    return best


def _norm_matmul_kernel(x_ref, nw_ref, w_ref, o_ref, xn_ref, *, transpose_out):
    @pl.when(pl.program_id(1) == 0)
    def _():
        x = x_ref[...]
        ms = jnp.mean(x * x, axis=-1, keepdims=True)
        xn = x * lax.rsqrt(ms + NORM_EPS) * nw_ref[...]
        if transpose_out:
            xn_ref[...] = xn.T.astype(BF16)
        else:
            xn_ref[...] = xn.astype(BF16)

    if transpose_out:
        o_ref[...] = jnp.dot(w_ref[...], xn_ref[...], preferred_element_type=F32).astype(o_ref.dtype)
    else:
        o_ref[...] = jnp.dot(xn_ref[...], w_ref[...], preferred_element_type=F32).astype(o_ref.dtype)


def _norm_matmul(x, nw, w, tn, out_dtype=F32):
    m, k = x.shape
    n = w.shape[1]
    tm = _pick_tm(m, 1024)
    assert n % tn == 0
    return pl.pallas_call(
        functools.partial(_norm_matmul_kernel, transpose_out=False),
        out_shape=jax.ShapeDtypeStruct((m, n), out_dtype),
        grid=(m // tm, n // tn),
        in_specs=[
            pl.BlockSpec((tm, k), lambda i, j: (i, 0)),
            pl.BlockSpec((1, k), lambda i, j: (0, 0)),
            pl.BlockSpec((k, tn), lambda i, j: (0, j)),
        ],
        out_specs=pl.BlockSpec((tm, tn), lambda i, j: (i, j)),
        scratch_shapes=[pltpu.VMEM((tm, k), BF16)],
        compiler_params=_cparams(("parallel", "arbitrary")),
        name="norm_matmul",
    )(x, nw.reshape(1, k), w)


def _norm_matmul_t(x, nw, wt, bsz, tn):
    m, k = x.shape
    n = wt.shape[0]
    t = m // bsz
    tm = _pick_tm(t, 1024, LANES)
    assert n % tn == 0
    tpb = t // tm
    return pl.pallas_call(
        functools.partial(_norm_matmul_kernel, transpose_out=True),
        out_shape=jax.ShapeDtypeStruct((bsz, n, t), F32),
        grid=(m // tm, n // tn),
        in_specs=[
            pl.BlockSpec((tm, k), lambda i, j: (i, 0)),
            pl.BlockSpec((1, k), lambda i, j: (0, 0)),
            pl.BlockSpec((tn, k), lambda i, j: (j, 0)),
        ],
        out_specs=pl.BlockSpec((None, tn, tm), lambda i, j: (i // tpb, j, i % tpb)),
        scratch_shapes=[pltpu.VMEM((k, tm), BF16)],
        compiler_params=_cparams(("parallel", "arbitrary")),
        name="norm_matmul_t",
    )(x, nw.reshape(1, k), wt)


def _kv_proj_kernel(x_ref, nw_ref, w_ref, *rest, n_kv_tiles):
    okv_ref, okvw_ref, xn_ref = rest[-3:]
    j = pl.program_id(1)

    @pl.when(j == 0)
    def _():
        x = x_ref[...]
        ms = jnp.mean(x * x, axis=-1, keepdims=True)
        xn_ref[...] = (x * lax.rsqrt(ms + NORM_EPS) * nw_ref[...]).T.astype(BF16)

    res = jnp.dot(w_ref[...], xn_ref[...], preferred_element_type=F32)

    @pl.when(j < n_kv_tiles)
    def _():
        okv_ref[...] = res

    @pl.when(j >= n_kv_tiles)
    def _():
        okvw_ref[...] = res


def _kv_proj_t(x, nw, wt, bsz, layer, n_layers, kv_rows, bufs):
    m, k = x.shape
    n = wt.shape[0]
    t = m // bsz
    tn = n - kv_rows
    assert kv_rows % tn == 0 and tn % SUBLANES == 0
    n_kv_tiles = kv_rows // tn
    tm = _pick_tm(t, 1024, LANES)
    tpb = t // tm
    in_specs = [
        pl.BlockSpec((tm, k), lambda i, j: (i, 0)),
        pl.BlockSpec((1, k), lambda i, j: (0, 0)),
        pl.BlockSpec((tn, k), lambda i, j: (j, 0)),
    ]
    args = [x, nw.reshape(1, k), wt]
    aliases = {}
    if bufs is not None:
        in_specs += [pl.BlockSpec(memory_space=pl.ANY), pl.BlockSpec(memory_space=pl.ANY)]
        aliases = {len(args): 0, len(args) + 1: 1}
        args += list(bufs)
    return pl.pallas_call(
        functools.partial(_kv_proj_kernel, n_kv_tiles=n_kv_tiles),
        out_shape=(jax.ShapeDtypeStruct((bsz, n_layers, kv_rows, t), F32),
                   jax.ShapeDtypeStruct((bsz, n_layers, tn, t), F32)),
        grid=(m // tm, n // tn),
        in_specs=in_specs,
        out_specs=(pl.BlockSpec((None, None, tn, tm),
                                lambda i, j: (i // tpb, layer, jnp.minimum(j, n_kv_tiles - 1), i % tpb)),
                   pl.BlockSpec((None, None, tn, tm), lambda i, j: (i // tpb, layer, 0, i % tpb))),
        scratch_shapes=[pltpu.VMEM((k, tm), BF16)],
        input_output_aliases=aliases,
        compiler_params=_cparams(("parallel", "arbitrary")),
        name="kv_proj_t",
    )(*args)


def _matmul_res_kernel(y_ref, w_ref, x_ref, o_ref, *, y_transposed):
    dims = TN_DIMS if y_transposed else (((1,), (0,)), ((), ()))
    o_ref[...] = x_ref[...] + lax.dot_general(y_ref[...], w_ref[...], dims, preferred_element_type=F32)


def _matmul_res(y, w, x):
    m, k = y.shape
    n = w.shape[1]
    tm = min(m, 512)
    assert m % tm == 0
    return pl.pallas_call(
        functools.partial(_matmul_res_kernel, y_transposed=False),
        out_shape=jax.ShapeDtypeStruct((m, n), F32),
        grid=(m // tm,),
        in_specs=[
            pl.BlockSpec((tm, k), lambda i: (i, 0)),
            pl.BlockSpec((k, n), lambda i: (0, 0)),
            pl.BlockSpec((tm, n), lambda i: (i, 0)),
        ],
        out_specs=pl.BlockSpec((tm, n), lambda i: (i, 0)),
        compiler_params=_cparams(("parallel",)),
        name="matmul_res",
    )(y, w, x)


def _matmul_res_t(yt, w, x):
    bsz, k, t = yt.shape
    n = w.shape[1]
    tm = min(t, 512)
    assert t % tm == 0
    tpb = t // tm
    return pl.pallas_call(
        functools.partial(_matmul_res_kernel, y_transposed=True),
        out_shape=jax.ShapeDtypeStruct((bsz * t, n), F32),
        grid=(bsz * tpb,),
        in_specs=[
            pl.BlockSpec((None, k, tm), lambda i: (i // tpb, 0, i % tpb)),
            pl.BlockSpec((k, n), lambda i: (0, 0)),
            pl.BlockSpec((tm, n), lambda i: (i, 0)),
        ],
        out_specs=pl.BlockSpec((tm, n), lambda i: (i, 0)),
        compiler_params=_cparams(("parallel",)),
        name="matmul_res_t",
    )(yt, w, x)


def _rmsnorm_kernel(x_ref, nw_ref, o_ref):
    x = x_ref[...]
    ms = jnp.mean(x * x, axis=-1, keepdims=True)
    o_ref[...] = x * lax.rsqrt(ms + NORM_EPS) * nw_ref[...]


def _rmsnorm(x, nw):
    m, k = x.shape
    tm = min(m, 512)
    assert m % tm == 0
    return pl.pallas_call(
        _rmsnorm_kernel,
        out_shape=jax.ShapeDtypeStruct((m, k), F32),
        grid=(m // tm,),
        in_specs=[pl.BlockSpec((tm, k), lambda i: (i, 0)), pl.BlockSpec((1, k), lambda i: (0, 0))],
        out_specs=pl.BlockSpec((tm, k), lambda i: (i, 0)),
        compiler_params=_cparams(("parallel",)),
        name="final_rmsnorm",
    )(x, nw.reshape(1, k))


def _ssd_chunk_kernel(z_ref, x_ref, bc_ref, dt_ref, cprev_ref, sprev_ref, cw_ref, cb_ref, dtb_ref, alog_ref,
                      dskip_ref, gn_ref, y_ref, hlast_ref, xpad_ref, state_ref, *, d_inner, n_heads):
    lc = SSD_CHUNK
    g_n, n, p = SSD_N_GROUPS, SSD_D_STATE, SSD_HEAD_DIM
    r_n = n_heads // g_n
    c = pl.program_id(1)

    @pl.when(c == 0)
    def _():
        xpad_ref[0:SUBLANES, :] = cprev_ref[0]
        state_ref[...] = sprev_ref[0]

    @pl.when(c > 0)
    def _():
        xpad_ref[0:SUBLANES, :] = xpad_ref[lc:lc + SUBLANES, :]

    xpad_ref[SUBLANES:SUBLANES + lc, 0:d_inner] = x_ref[...]
    xpad_ref[SUBLANES:SUBLANES + lc, d_inner:] = bc_ref[...]

    conv = cb_ref[...]
    for k in range(SSD_CONV_W):
        off = SUBLANES - (SSD_CONV_W - 1) + k
        conv = conv + xpad_ref[off:off + lc, :] * cw_ref[k:k + 1, :]
    act = _silu(conv)
    xs = act[:, :d_inner]
    bm = act[:, d_inner:d_inner + g_n * n]
    cm = act[:, d_inner + g_n * n:]

    dt = jax.nn.softplus(dt_ref[...] + dtb_ref[...])
    a = -jnp.exp(alog_ref[...])
    dta = dt * a
    row = lax.broadcasted_iota(jnp.int32, (lc, lc), 0)
    col = lax.broadcasted_iota(jnp.int32, (lc, lc), 1)
    causal = row >= col
    tril = causal.astype(F32)
    a_cs = jnp.dot(tril, dta, precision=HIGHEST, preferred_element_type=F32)
    a_cs_t = a_cs.T
    a_end = a_cs[lc - 1:lc, :]
    to_end = jnp.exp(a_end - a_cs)
    e_cs = jnp.exp(a_cs)
    e_end = jnp.exp(a_end)

    assert 2 * p == LANES and r_n % 2 == 0
    lane_lo = lax.broadcasted_iota(jnp.int32, (1, LANES), 1) < p

    def pair_lanes(v, k):
        return jnp.where(lane_lo, v[:, 2 * k:2 * k + 1], v[:, 2 * k + 1:2 * k + 2])

    zs = _silu(z_ref[...])
    gw = r_n * p
    ys = []
    for g in range(g_n):
        bg16 = bm[:, g * n:(g + 1) * n].astype(BF16)
        cg16 = cm[:, g * n:(g + 1) * n].astype(BF16)
        cb = lax.dot_general(cg16, bg16, NT_DIMS, preferred_element_type=F32)
        pairs = range(g * r_n // 2, (g + 1) * r_n // 2)
        gl = slice(g * gw, (g + 1) * gw)
        xs_g = xs[:, gl]
        xdt_g = xs_g * jnp.concatenate([pair_lanes(dt, k) for k in pairs], axis=1)
        y_diag = []
        for i, k in enumerate(pairs):
            xdt_k = xdt_g[:, i * LANES:(i + 1) * LANES]
            acc = None
            for j, h in enumerate((2 * k, 2 * k + 1)):
                seg = a_cs[:, h:h + 1] - a_cs_t[h:h + 1, :]
                decay = jnp.exp(jnp.where(causal, seg, -jnp.inf))
                m_h = (cb * decay).astype(BF16)
                x_h = jnp.where(lane_lo if j == 0 else jnp.logical_not(lane_lo), xdt_k, 0.0).astype(BF16)
                d = jnp.dot(m_h, x_h, preferred_element_type=F32)
                acc = d if acc is None else acc + d
            y_diag.append(acc)
        y_diag = jnp.concatenate(y_diag, axis=1)
        sprev = state_ref[:, gl]
        y_off = (jnp.dot(cg16, sprev.astype(BF16), preferred_element_type=F32)
                 * jnp.concatenate([pair_lanes(e_cs, k) for k in pairs], axis=1))
        xw = (xdt_g * jnp.concatenate([pair_lanes(to_end, k) for k in pairs], axis=1)).astype(BF16)
        st = lax.dot_general(bg16, xw, TN_DIMS, preferred_element_type=F32)
        state_ref[:, gl] = sprev * jnp.concatenate([pair_lanes(e_end, k) for k in pairs], axis=1) + st
        yg = (y_diag + y_off + xs_g * dskip_ref[:, gl]) * zs[:, gl]
        ms = jnp.mean(yg * yg, axis=-1, keepdims=True)
        ys.append(yg * lax.rsqrt(ms + NORM_EPS))
    y = jnp.concatenate(ys, axis=1) * gn_ref[...]
    y_ref[...] = y.astype(y_ref.dtype)

    @pl.when(c == pl.num_programs(1) - 1)
    def _():
        hlast_ref[0] = state_ref[...]


def _pad_lanes(v, width=LANES):
    return jnp.pad(v.astype(F32), (0, width - v.shape[0])).reshape(1, width)


def _ssd_prompt_core(u, bsz, t, conv_prev8, ssm_prev, conv_w, conv_b, dt_bias, a_log, d_skip, gnorm):
    n_heads = dt_bias.shape[0]
    d_inner = n_heads * SSD_HEAD_DIM
    conv_dim = conv_w.shape[1]
    assert conv_dim == 2 * d_inner and t % SSD_CHUNK == 0
    lc = SSD_CHUNK
    nc = t // lc
    dt_blk = (d_inner + conv_dim) // LANES
    kern = functools.partial(_ssd_chunk_kernel, d_inner=d_inner, n_heads=n_heads)
    const = lambda b, c: (0, 0)
    state_t = ssm_prev.transpose(0, 3, 1, 2).reshape(bsz, SSD_D_STATE, d_inner)
    y, hlast_t = pl.pallas_call(
        kern,
        out_shape=(jax.ShapeDtypeStruct((bsz * t, d_inner), BF16),
                   jax.ShapeDtypeStruct((bsz, SSD_D_STATE, d_inner), F32)),
        grid=(bsz, nc),
        in_specs=[
            pl.BlockSpec((lc, d_inner), lambda b, c: (b * nc + c, 0)),
            pl.BlockSpec((lc, d_inner), lambda b, c: (b * nc + c, 1)),
            pl.BlockSpec((lc, d_inner), lambda b, c: (b * nc + c, 2)),
            pl.BlockSpec((lc, LANES), lambda b, c: (b * nc + c, dt_blk)),
            pl.BlockSpec((1, SUBLANES, conv_dim), lambda b, c: (b, 0, 0)),
            pl.BlockSpec((1, SSD_D_STATE, d_inner), lambda b, c: (b, 0, 0)),
            pl.BlockSpec((SSD_CONV_W, conv_dim), const),
            pl.BlockSpec((1, conv_dim), const),
            pl.BlockSpec((1, LANES), const),
            pl.BlockSpec((1, LANES), const),
            pl.BlockSpec((1, d_inner), const),
            pl.BlockSpec((1, d_inner), const),
        ],
        out_specs=(pl.BlockSpec((lc, d_inner), lambda b, c: (b * nc + c, 0)),
                   pl.BlockSpec((1, SSD_D_STATE, d_inner), lambda b, c: (b, 0, 0))),
        scratch_shapes=[pltpu.VMEM((lc + SUBLANES, conv_dim), F32),
                        pltpu.VMEM((SSD_D_STATE, d_inner), F32)],
        compiler_params=_cparams(("parallel", "arbitrary")),
        name="ssd_chunk",
    )(u, u, u, u, conv_prev8, state_t, conv_w, conv_b.reshape(1, conv_dim), _pad_lanes(dt_bias),
      _pad_lanes(a_log), jnp.repeat(d_skip.astype(F32), SSD_HEAD_DIM).reshape(1, d_inner), gnorm.reshape(1, d_inner))
    hlast = hlast_t.reshape(bsz, SSD_D_STATE, n_heads, SSD_HEAD_DIM).transpose(0, 2, 3, 1)
    return y, hlast


def _ssd_in_weights(w_in, d_inner, conv_dim):
    main = d_inner + conv_dim
    n_dt = w_in.shape[1] - main
    w = jnp.concatenate([w_in[:, :main], jnp.pad(w_in[:, main:], ((0, 0), (0, LANES - n_dt)))], axis=1)
    return w.astype(BF16)


def _ssd_layer_prompt(xp, prm):
    norm_w, w_in, conv_w, conv_b, dt_bias, a_log, d_skip, gnorm, w_out = prm
    bsz, t, d = xp.shape
    assert t >= SSD_CONV_W - 1
    n_heads = dt_bias.shape[0]
    d_inner = n_heads * SSD_HEAD_DIM
    conv_dim = conv_w.shape[1]
    w = _ssd_in_weights(w_in, d_inner, conv_dim)
    x2 = xp.reshape(bsz * t, d)
    u = _norm_matmul(x2, norm_w, w, _pick_tn(w.shape[1]))
    conv_prev8 = jnp.zeros((bsz, SUBLANES, conv_dim), F32)
    ssm_prev = jnp.zeros((bsz, n_heads, SSD_HEAD_DIM, SSD_D_STATE), F32)
    y, hlast = _ssd_prompt_core(u, bsz, t, conv_prev8, ssm_prev, conv_w, conv_b, dt_bias, a_log, d_skip, gnorm)
    out = _matmul_res(y, w_out.astype(BF16), x2).reshape(bsz, t, d)
    u3 = u.reshape(bsz, t, -1)
    new_conv = u3[:, t - (SSD_CONV_W - 1):, d_inner:d_inner + conv_dim]
    return out, new_conv, hlast


def _ssd_step_pre_kernel(u_ref, cst_ref, cw_ref, cb_ref, dtb_ref, alog_ref,
                         xs_ref, xdt_ref, dec_ref, bm_ref, cm_ref, ncst_ref, *, d_inner, n_heads):
    conv_dim = cw_ref.shape[1]
    gn = SSD_N_GROUPS * SSD_D_STATE
    xbc = u_ref[:, d_inner:d_inner + conv_dim]
    conv = cb_ref[...]
    for k in range(SSD_CONV_W - 1):
        conv = conv + cst_ref[k] * cw_ref[k:k + 1, :]
    conv = conv + xbc * cw_ref[SSD_CONV_W - 1:SSD_CONV_W, :]
    for k in range(SSD_CONV_W - 2):
        ncst_ref[k] = cst_ref[k + 1]
    ncst_ref[SSD_CONV_W - 2] = xbc
    act = _silu(conv)
    xs = act[:, :d_inner]
    dt = jax.nn.softplus(u_ref[:, d_inner + conv_dim:d_inner + conv_dim + LANES] + dtb_ref[...])
    dec = jnp.exp(dt * (-jnp.exp(alog_ref[...])))
    hrow = lax.broadcasted_iota(jnp.int32, (LANES, d_inner), 0)
    hcol = lax.broadcasted_iota(jnp.int32, (LANES, d_inner), 1) // SSD_HEAD_DIM
    expand = (hrow == hcol).astype(F32)
    dt_e = jnp.dot(dt, expand, precision=HIGHEST, preferred_element_type=F32)
    dec_e = jnp.dot(dec, expand, precision=HIGHEST, preferred_element_type=F32)
    xs_ref[...] = xs
    xdt_ref[...] = xs * dt_e
    dec_ref[...] = dec_e
    bm_ref[...] = act[:, d_inner:d_inner + gn]
    cm_ref[...] = act[:, d_inner + gn:]


def _ssd_step_state_kernel(s_ref, xdt_ref, dec_ref, b_ref, c_ref, snew_ref, y_ref, *, rows_per_group):
    for g in range(SSD_N_GROUPS):
        sl = slice(g * rows_per_group, (g + 1) * rows_per_group)
        s = s_ref[0, sl, :]
        bg = b_ref[0, g:g + 1, :]
        cg = c_ref[0, g:g + 1, :]
        xdt = xdt_ref[0, sl, :]
        dec = dec_ref[0, sl, :]
        cb = jnp.sum(bg * cg, axis=1, keepdims=True)
        y_off = jnp.sum(s * cg, axis=1, keepdims=True)
        snew_ref[0, sl, :] = s * dec + xdt * bg
        y_ref[0, sl, :] = cb * xdt + dec * y_off


def _ssd_step_post_kernel(y_ref, xs_ref, u_ref, dsk_ref, gn_ref, o_ref, *, d_inner, group_width):
    y = (y_ref[...] + xs_ref[...] * dsk_ref[...]) * _silu(u_ref[:, :d_inner])
    outs = []
    for g in range(SSD_N_GROUPS):
        yg = y[:, g * group_width:(g + 1) * group_width]
        ms = jnp.mean(yg * yg, axis=-1, keepdims=True)
        outs.append(yg * lax.rsqrt(ms + NORM_EPS))
    o_ref[...] = (jnp.concatenate(outs, axis=1) * gn_ref[...]).astype(o_ref.dtype)


def _ssd_layer_sample(xs_in, conv_state, ssm_states, layer, prm):
    norm_w, w_in, conv_w, conv_b, dt_bias, a_log, d_skip, gnorm, w_out = prm
    bsz, t, d = xs_in.shape
    assert t == 1
    n_heads = dt_bias.shape[0]
    d_inner = n_heads * SSD_HEAD_DIM
    conv_dim = conv_w.shape[1]
    gn = SSD_N_GROUPS * SSD_D_STATE
    w = _ssd_in_weights(w_in, d_inner, conv_dim)
    x2 = xs_in.reshape(bsz, d)
    u = _norm_matmul(x2, norm_w, w, _pick_tn(w.shape[1]))
    cst = conv_state.transpose(1, 0, 2)
    pre = functools.partial(_ssd_step_pre_kernel, d_inner=d_inner, n_heads=n_heads)
    xs, xdt, dec, bm, cm, ncst = pl.pallas_call(
        pre,
        out_shape=(jax.ShapeDtypeStruct((bsz, d_inner), F32), jax.ShapeDtypeStruct((bsz, d_inner), F32),
                   jax.ShapeDtypeStruct((bsz, d_inner), F32), jax.ShapeDtypeStruct((bsz, gn), F32),
                   jax.ShapeDtypeStruct((bsz, gn), F32), jax.ShapeDtypeStruct(cst.shape, F32)),
        compiler_params=pltpu.CompilerParams(vmem_limit_bytes=VMEM_LIMIT),
        name="ssd_step_pre",
    )(u, cst, conv_w, conv_b.reshape(1, conv_dim), _pad_lanes(dt_bias), _pad_lanes(a_log))
    rows = n_heads * SSD_HEAD_DIM
    rpg = rows // SSD_N_GROUPS
    st = functools.partial(_ssd_step_state_kernel, rows_per_group=rpg)
    snew, ycol = pl.pallas_call(
        st,
        out_shape=(jax.ShapeDtypeStruct((bsz, rows, SSD_D_STATE), F32), jax.ShapeDtypeStruct((bsz, rows, 1), F32)),
        grid=(bsz,),
        in_specs=[
            pl.BlockSpec((None, 1, rows, SSD_D_STATE), lambda b: (layer, b, 0, 0)),
            pl.BlockSpec((1, rows, 1), lambda b: (b, 0, 0)),
            pl.BlockSpec((1, rows, 1), lambda b: (b, 0, 0)),
            pl.BlockSpec((1, SSD_N_GROUPS, SSD_D_STATE), lambda b: (b, 0, 0)),
            pl.BlockSpec((1, SSD_N_GROUPS, SSD_D_STATE), lambda b: (b, 0, 0)),
        ],
        out_specs=(pl.BlockSpec((1, rows, SSD_D_STATE), lambda b: (b, 0, 0)),
                   pl.BlockSpec((1, rows, 1), lambda b: (b, 0, 0))),
        compiler_params=_cparams(("parallel",)),
        name="ssd_step_state",
    )(ssm_states.reshape(ssm_states.shape[0], bsz, rows, SSD_D_STATE), xdt.reshape(bsz, rows, 1),
      dec.reshape(bsz, rows, 1), bm.reshape(bsz, SSD_N_GROUPS, SSD_D_STATE),
      cm.reshape(bsz, SSD_N_GROUPS, SSD_D_STATE))
    post = functools.partial(_ssd_step_post_kernel, d_inner=d_inner, group_width=d_inner // SSD_N_GROUPS)
    y = pl.pallas_call(
        post,
        out_shape=jax.ShapeDtypeStruct((bsz, d_inner), BF16),
        compiler_params=pltpu.CompilerParams(vmem_limit_bytes=VMEM_LIMIT),
        name="ssd_step_post",
    )(ycol.reshape(bsz, rows), xs, u, jnp.repeat(d_skip.astype(F32), SSD_HEAD_DIM).reshape(1, d_inner),
      gnorm.reshape(1, d_inner))
    out = _matmul_res(y, w_out.astype(BF16), x2).reshape(bsz, 1, d)
    return out, ncst.transpose(1, 0, 2), snew.reshape(ssm_states.shape[1:])


NSA_Q_W = NSA_N_HEADS * NSA_HEAD_DIM
NSA_KV_W = NSA_N_KV * NSA_HEAD_DIM
COL_Q = 0
COL_KV = NSA_Q_W
COL_Z = COL_KV + 4 * NSA_KV_W
COL_KVW = COL_Z + N_BRANCH * NSA_Q_W
COL_GATE = COL_KVW + 2 * NSA_KV_W
NSA_N_PAD = COL_GATE + LANES
TQ = 128
KT = 128
N_REL_TILES = -(-(REL_MAX_DIST + TQ - 1) // KT)
CMP_NEAR = 128
CMP_FRONT = CMP_NEAR - TQ // CMP_STRIDE
PAGES_PER_STEP = 32
SEL_STEP_TILES = 4


def _nsa_in_weights(w_in):
    cuts = np.cumsum([NSA_Q_W, 4 * NSA_KV_W, 2 * NSA_KV_W, NSA_N_HEADS * N_BRANCH]).tolist()
    q, kv, kvw, gate, z = (w_in[:, a:b] for a, b in zip([0] + cuts, cuts + [w_in.shape[1]]))
    gate = jnp.pad(gate, ((0, 0), (0, LANES - gate.shape[1])))
    return jnp.concatenate([q, kv, z, kvw, gate], axis=1)


def _rel_bucket(dist):
    n = jnp.maximum(dist, 0)
    exact = REL_BUCKETS // 2
    logv = jnp.log(jnp.maximum(n, 1).astype(F32) / exact) / math.log(REL_MAX_DIST / exact)
    large = jnp.minimum(exact + (logv * (REL_BUCKETS - exact)).astype(jnp.int32), REL_BUCKETS - 1)
    return jnp.where(n < exact, n, large)


def _bias_by_dist(rel_bias, n):
    near = jnp.moveaxis(rel_bias.astype(F32)[_rel_bucket(jnp.arange(min(n, REL_MAX_DIST)))], -1, 0)
    if n <= REL_MAX_DIST:
        return near
    far = jnp.broadcast_to(rel_bias.astype(F32)[REL_BUCKETS - 1][:, None], (rel_bias.shape[1], n - REL_MAX_DIST))
    return jnp.concatenate([near, far], axis=1)


def _toeplitz(vec, first, n_rows, n_cols, row_step):
    lead = vec.shape[:-1]
    lo = first - row_step * (n_rows - 1)
    front = max(-lo, 0)
    if front:
        vec = jnp.concatenate([jnp.broadcast_to(vec[..., :1], lead + (front,)), vec], axis=-1)
    span = row_step * (n_rows - 1) + n_cols
    w = vec[..., lo + front:lo + front + span]
    pitch = span + row_step
    flat = jnp.tile(w, n_rows + 1)[..., :n_rows * pitch]
    s = flat.reshape(lead + (n_rows, pitch))[..., :n_cols]
    return jnp.flip(s, axis=-2)


def _lanes_by_group(t):
    h, n, q = t.shape
    return t.reshape(NSA_N_KV, NSA_R, n, q).transpose(0, 2, 1, 3).reshape(NSA_N_KV, n, NSA_R * q)


def _overlap_matrix(nc, ns):
    c_start = np.arange(nc) * CMP_STRIDE
    c_end = c_start + CMP_BLOCK - 1
    s_start = np.arange(ns) * SEL_BLOCK
    ov = np.clip(np.minimum(c_end[:, None], s_start[None, :] + SEL_BLOCK - 1)
                 - np.maximum(c_start[:, None], s_start[None, :]) + 1, 0, None).astype(np.float32) / CMP_STRIDE
    return ov


def _masked_softmax_parts(parts, valids, axis):
    m = None
    for s, v in zip(parts, valids):
        pm = jnp.max(jnp.where(v, s, -jnp.inf), axis=axis, keepdims=True)
        m = pm if m is None else jnp.maximum(m, pm)
    m = jnp.where(m > -jnp.inf, m, 0.0)
    es = [jnp.where(v, jnp.exp(s - m), 0.0) for s, v in zip(parts, valids)]
    den = None
    for e in es:
        d = jnp.sum(e, axis=axis, keepdims=True)
        den = d if den is None else den + d
    inv = 1.0 / jnp.where(den > 0, den, 1.0)
    return [e * inv for e in es]


def _compress_kernel(*refs, n_pp):
    page_refs = refs[1:1 + n_pp]
    perm_ref, pe_ref, w1_ref, w2_ref, kc_ref, vc_ref, x_ref = refs[1 + n_pp:]
    step = pl.program_id(1)
    hb = PAGE_SIZE // CMP_STRIDE
    dh = NSA_HEAD_DIM
    for k in range(n_pp):
        pt = lax.dot_general(perm_ref[...], page_refs[k][...].astype(BF16), NT_DIMS, preferred_element_type=F32)
        row0 = pl.multiple_of((step * n_pp + k) * hb, hb)
        for l in range(CMP_STRIDE):
            row = pt[l * hb:(l + 1) * hb, :]
            for kv in range(2):
                for g in range(NSA_N_KV):
                    c0 = kv * NSA_KV_W + g * dh
                    x_ref[kv, g, pl.ds(row0, hb), l * dh:(l + 1) * dh] = row[:, c0:c0 + dh]

    @pl.when(step == pl.num_programs(1) - 1)
    def _():
        hid_w = CMP_HIDDEN
        for kv, out_ref in ((0, kc_ref), (1, vc_ref)):
            w1 = w1_ref[kv]
            w2 = w2_ref[kv]
            pe_w = jnp.dot(pe_ref[kv].astype(BF16), w1, preferred_element_type=F32)
            pe_term = pe_w[0:1, :hid_w] + pe_w[1:2, hid_w:]
            for g in range(NSA_N_KV):
                both = jnp.dot(x_ref[kv, g].astype(BF16), w1, preferred_element_type=F32)
                bot = both[:, hid_w:]
                bot = jnp.concatenate([bot[1:], jnp.zeros((1, hid_w), F32)], axis=0)
                hid = _silu(both[:, :hid_w] + bot + pe_term).astype(BF16)
                out_ref[0, :, g * dh:(g + 1) * dh] = jnp.dot(hid, w2, preferred_element_type=F32)


def _compress(src, page_index, bsz, n_pages, ids, pe, w1, w2):
    n_pp = math.gcd(PAGES_PER_STEP, n_pages)
    hb = PAGE_SIZE // CMP_STRIDE
    nh = n_pages * hb
    half = CMP_STRIDE * NSA_HEAD_DIM
    blk = (None,) * (src.ndim - 2) + (2 * NSA_KV_W, PAGE_SIZE)
    page_specs = [pl.BlockSpec(blk, functools.partial(lambda b, s, ids, k: page_index(b, s * n_pp + k, ids), k=k))
                  for k in range(n_pp)]
    perm = np.zeros((PAGE_SIZE, PAGE_SIZE), np.float32)
    for l in range(CMP_STRIDE):
        for i in range(hb):
            perm[l * hb + i, CMP_STRIDE * i + l] = 1.0
    w1cat = jnp.concatenate([w1[:, :half], w1[:, half:]], axis=-1).astype(BF16)
    grid_spec = pltpu.PrefetchScalarGridSpec(
        num_scalar_prefetch=1,
        grid=(bsz, n_pages // n_pp),
        in_specs=page_specs + [
            pl.BlockSpec((PAGE_SIZE, PAGE_SIZE), lambda b, s, ids: (0, 0)),
            pl.BlockSpec((2, 2, half), lambda b, s, ids: (0, 0, 0)),
            pl.BlockSpec((2, half, 2 * CMP_HIDDEN), lambda b, s, ids: (0, 0, 0)),
            pl.BlockSpec((2, CMP_HIDDEN, NSA_HEAD_DIM), lambda b, s, ids: (0, 0, 0)),
        ],
        out_specs=(pl.BlockSpec((1, nh, NSA_KV_W), lambda b, s, ids: (b, 0, 0)),
                   pl.BlockSpec((1, nh, NSA_KV_W), lambda b, s, ids: (b, 0, 0))),
        scratch_shapes=[pltpu.VMEM((2, NSA_N_KV, nh, half), F32)],
    )
    return pl.pallas_call(
        functools.partial(_compress_kernel, n_pp=n_pp),
        out_shape=(jax.ShapeDtypeStruct((bsz, nh, NSA_KV_W), F32), jax.ShapeDtypeStruct((bsz, nh, NSA_KV_W), F32)),
        grid_spec=grid_spec,
        compiler_params=_cparams(("parallel", "arbitrary")),
        name="nsa_compress",
    )(ids, *([src] * n_pp), jnp.asarray(perm, BF16), pe.reshape(2, 2, half).astype(F32), w1cat, w2.astype(BF16))


def _nsa_attn_kernel(qt_ref, kv_ref, kc_ref, vc_ref, ov_ref, trel_ref, tcn_ref, cfar_ref, gate_ref, z0_ref, z1_ref,
                     z2_ref, o_ref, qaug_ref, imp_ref, m_ref, l_ref, acc_ref, ob_ref, s_ref, *, n_sel, n_cmp_rows):
    t = pl.program_id(1)
    dh = NSA_HEAD_DIM
    rq = NSA_R * TQ
    scale = dh ** -0.5
    far_idx = N_REL_TILES
    ns8 = _round_up(n_sel, 2 * SUBLANES)
    assert ns8 < LANES
    near0 = pl.multiple_of(t * (TQ // CMP_STRIDE), SUBLANES)
    far_rows = slice(CMP_FRONT, CMP_FRONT + n_cmp_rows)

    def cmp_branch(g, qt, with_far):
        gs = slice(g * dh, (g + 1) * dh)
        kc_near = kc_ref[0, pl.ds(near0, CMP_NEAR), gs].astype(BF16)
        vc_near = vc_ref[0, pl.ds(near0, CMP_NEAR), gs].astype(BF16)
        cn = lax.broadcasted_iota(jnp.int32, (CMP_NEAR, rq), 0)
        qn = lax.broadcasted_iota(jnp.int32, (CMP_NEAR, rq), 1) % TQ
        dist_n = qn - CMP_STRIDE * cn + (CMP_STRIDE * CMP_FRONT - (CMP_BLOCK - 1))
        parts = [jnp.dot(kc_near, qt, preferred_element_type=F32) + tcn_ref[g]]
        valids = [(dist_n >= 0) & (cn + near0 >= CMP_FRONT)]
        if with_far:
            kc_far = kc_ref[0, far_rows, gs].astype(BF16)
            parts.append(jnp.dot(kc_far, qt, preferred_element_type=F32) + cfar_ref[g, 0:1, :])
            valids.append(lax.broadcasted_iota(jnp.int32, (n_cmp_rows, rq), 0) + CMP_FRONT < near0)
        ps = _masked_softmax_parts(parts, valids, axis=0)

        def importance(ov, p):
            pg = sum(p[:, r * TQ:(r + 1) * TQ] for r in range(NSA_R))
            hi = pg.astype(BF16)
            lo = (pg - hi.astype(F32)).astype(BF16)
            ov16 = ov.astype(BF16)
            return (lax.dot_general(ov16, hi, TN_DIMS, preferred_element_type=F32)
                    + lax.dot_general(ov16, lo, TN_DIMS, preferred_element_type=F32))

        o_c = lax.dot_general(vc_near, ps[0].astype(BF16), TN_DIMS, preferred_element_type=F32)
        imp = importance(ov_ref[pl.ds(near0, CMP_NEAR), :], ps[0])
        if with_far:
            vc_far = vc_ref[0, far_rows, gs].astype(BF16)
            o_c = o_c + lax.dot_general(vc_far, ps[1].astype(BF16), TN_DIMS, preferred_element_type=F32)
            imp = imp + importance(ov_ref[far_rows, :], ps[1])
        ob_ref[0, g] = o_c
        imp_ref[...] = imp

    qts = []
    sid = lax.broadcasted_iota(jnp.int32, (ns8, TQ), 0)
    qpos = t * TQ + lax.broadcasted_iota(jnp.int32, (ns8, TQ), 1)
    cur = qpos // SEL_BLOCK
    forced = (sid == 0) | (sid == cur) | (sid == cur - 1)
    reachable = sid * SEL_BLOCK <= qpos
    sub = lax.broadcasted_iota(jnp.int32, (SUBLANES, TQ), 0)
    tail_row = lax.broadcasted_iota(jnp.int32, (LANES - ns8, rq), 0)
    for g in range(NSA_N_KV):
        q32 = jnp.concatenate([qt_ref[(g * NSA_R + r) * dh:(g * NSA_R + r + 1) * dh, :] for r in range(NSA_R)], axis=1)
        qt = (q32 * scale).astype(BF16)
        qts.append((q32 * (scale * LOG2E)).astype(BF16))
        has_far = near0 > CMP_FRONT

        @pl.when(has_far)
        def _():
            cmp_branch(g, qt, True)

        @pl.when(jnp.logical_not(has_far))
        def _():
            cmp_branch(g, qt, False)

        imp = jnp.where(forced, FORCE_SCORE, imp_ref[0:ns8, :])
        imp = jnp.where(reachable, imp, -FORCE_SCORE)
        blocks = [imp[SUBLANES * kb:SUBLANES * (kb + 1)] for kb in range(ns8 // SUBLANES)]
        cnts = [jnp.zeros((SUBLANES, TQ), F32) for _ in blocks]
        for sp in range(n_sel):
            other = imp[sp:sp + 1, :]
            for kb, blk in enumerate(blocks):
                if kb > sp // SUBLANES:
                    ahead = other >= blk
                elif kb < sp // SUBLANES:
                    ahead = other > blk
                else:
                    ahead = (other > blk) | ((other == blk) & (sub > sp % SUBLANES))
                cnts[kb] = cnts[kb] + jnp.where(ahead, 1.0, 0.0)
        cnt = jnp.concatenate(cnts, axis=0)
        unsel = jnp.where((cnt < SEL_TOPK) & (imp > -0.5 * FORCE_SCORE), 0.0, NEG)
        zero = jnp.zeros((dh, rq), BF16)
        q_rows = [qts[g], zero] if g % 2 == 0 else [zero, qts[g]]
        tail = jnp.where(tail_row == 0, trel_ref[g, far_idx, 0:1, :], 0.0)
        qaug_ref[g] = jnp.concatenate(q_rows + [jnp.concatenate([unsel] * NSA_R, axis=1).astype(BF16),
                                                tail.astype(BF16)], axis=0)

    def reset():
        m_ref[...] = jnp.full(m_ref.shape, NEG, F32)
        l_ref[...] = jnp.zeros(l_ref.shape, F32)
        acc_ref[...] = jnp.zeros(acc_ref.shape, F32)

    def online_step(g, s, v_t):
        m_old = m_ref[g]
        m_new = jnp.maximum(m_old, jnp.max(s, axis=0, keepdims=True))
        alpha = jnp.exp2(m_old - m_new)
        p = jnp.exp2(s - m_new)
        l_ref[g] = l_ref[g] * alpha + jnp.sum(p, axis=0, keepdims=True)
        acc_ref[g] = acc_ref[g] * alpha + lax.dot_general(v_t, p.astype(BF16), TN_DIMS, preferred_element_type=F32)
        m_ref[g] = m_new

    def mask_heads(s, ok):
        return jnp.concatenate([jnp.where(ok, s[:, r * TQ:(r + 1) * TQ], NEG) for r in range(NSA_R)], axis=1)

    reset()
    kt2 = SEL_STEP_TILES * KT

    def sel_step(kt, mode, n_tiles=SEL_STEP_TILES):
        nk = n_tiles * KT
        k0 = pl.multiple_of(kt * kt2, kt2)
        e_blk = lax.broadcasted_iota(jnp.int32, (nk, LANES), 0) // SEL_BLOCK
        e_lane = lax.broadcasted_iota(jnp.int32, (nk, LANES), 1)
        onehot = e_lane == kt * (kt2 // SEL_BLOCK) + e_blk
        if mode == "far":
            onehot = onehot | (e_lane == ns8)
        e_aug = jnp.where(onehot, 1.0, 0.0).astype(BF16)
        delta = t - SEL_STEP_TILES * kt
        idx = [jnp.clip(delta - i, 0, far_idx) for i in range(n_tiles)]
        for pair in range(NSA_N_KV // 2):
            k_aug = jnp.concatenate([kv_ref[pl.ds(WINDOW + k0, nk), pair * LANES:(pair + 1) * LANES], e_aug], axis=1)
            for g in (2 * pair, 2 * pair + 1):
                s = jnp.dot(k_aug, qaug_ref[g], preferred_element_type=F32)
                if mode != "far":
                    s = s + jnp.concatenate([trel_ref[g, i] for i in idx], axis=0)
                if mode == "last":
                    kj = lax.broadcasted_iota(jnp.int32, (nk, TQ), 0)
                    qi = lax.broadcasted_iota(jnp.int32, (nk, TQ), 1)
                    s = mask_heads(s, t * TQ + qi >= k0 + kj)
                s_ref[g, 0:nk, :] = s
        for g in range(NSA_N_KV):
            online_step(g, s_ref[g, 0:nk, :],
                        kv_ref[pl.ds(WINDOW + k0, nk), NSA_KV_W + g * dh:NSA_KV_W + (g + 1) * dh])

    n_steps = t // SEL_STEP_TILES + 1
    n_far_steps = jnp.maximum(t - far_idx + 1, 0) // SEL_STEP_TILES

    def far_body(kt, carry):
        sel_step(kt, "far")
        return carry

    def near_body(kt, carry):
        sel_step(kt, "near")
        return carry

    lax.fori_loop(0, n_far_steps, far_body, 0)
    lax.fori_loop(n_far_steps, n_steps - 1, near_body, 0)
    short_last = t % SEL_STEP_TILES < SEL_STEP_TILES // 2

    @pl.when(short_last)
    def _():
        sel_step(n_steps - 1, "last", SEL_STEP_TILES // 2)

    @pl.when(jnp.logical_not(short_last))
    def _():
        sel_step(n_steps - 1, "last")
    for g in range(NSA_N_KV):
        ob_ref[1, g] = acc_ref[g] / l_ref[g]

    assert SEL_STEP_TILES >= 2

    reset()
    row0 = pl.multiple_of(t * TQ, TQ)
    for deltas in ((0,), (2, 1), (4, 3)):
        n_keys = KT * len(deltas)
        kjw = lax.broadcasted_iota(jnp.int32, (n_keys, TQ), 0)
        qiw = lax.broadcasted_iota(jnp.int32, (n_keys, TQ), 1)
        dist = deltas[0] * KT + qiw - kjw
        ok = (dist >= 0) & (dist <= WINDOW) & (row0 + kjw >= deltas[0] * KT)
        k0 = row0 + (WINDOW - deltas[0] * KT)
        for g in range(NSA_N_KV):
            k_t = kv_ref[pl.ds(k0, n_keys), 2 * NSA_KV_W + g * dh:2 * NSA_KV_W + (g + 1) * dh]
            s = jnp.dot(k_t, qts[g], preferred_element_type=F32)
            s = s + jnp.concatenate([trel_ref[g, d] for d in deltas], axis=0)
            s_ref[g, 0:n_keys, :] = mask_heads(s, ok)
        for g in range(NSA_N_KV):
            online_step(g, s_ref[g, 0:n_keys, :],
                        kv_ref[pl.ds(k0, n_keys), 3 * NSA_KV_W + g * dh:3 * NSA_KV_W + (g + 1) * dh])
    for g in range(NSA_N_KV):
        ob_ref[2, g] = acc_ref[g] / l_ref[g]

    gate = jax.nn.sigmoid(gate_ref[...])
    z_refs = (z0_ref, z1_ref, z2_ref)
    for h in range(NSA_N_HEADS):
        g, r = divmod(h, NSA_R)
        mixed = None
        for c in range(N_BRANCH):
            o_hc = ob_ref[c, g, :, r * TQ:(r + 1) * TQ]
            term = o_hc * _silu(z_refs[c][h * dh:(h + 1) * dh, :]) * gate[h * N_BRANCH + c:h * N_BRANCH + c + 1, :]
            mixed = term if mixed is None else mixed + term
        o_ref[h * dh:(h + 1) * dh, :] = mixed.astype(o_ref.dtype)


def _nsa_prompt_tables(rel_bias):
    n_dist = max((N_REL_TILES + 1) * KT, CMP_STRIDE * CMP_FRONT + TQ)
    vec = _bias_by_dist(rel_bias, n_dist)
    near = _toeplitz(vec, 0, KT, N_REL_TILES * TQ, 1).reshape(NSA_N_HEADS, KT, N_REL_TILES, TQ)
    near = _lanes_by_group(near.transpose(0, 2, 1, 3).reshape(NSA_N_HEADS, N_REL_TILES * KT, TQ))
    near = near.reshape(NSA_N_KV, N_REL_TILES, KT, NSA_R * TQ)
    far = jnp.broadcast_to(rel_bias.astype(F32)[REL_BUCKETS - 1].reshape(NSA_N_HEADS, 1, 1), (NSA_N_HEADS, KT, TQ))
    far = _lanes_by_group(far)
    trel = jnp.concatenate([near, far[:, None]], axis=1) * LOG2E
    tcn = _lanes_by_group(_toeplitz(vec, CMP_STRIDE * CMP_FRONT - (CMP_BLOCK - 1), CMP_NEAR, TQ, CMP_STRIDE))
    return trel, tcn, far[:, :SUBLANES, :]


def _nsa_layer_prompt(xp, prm, layer, n_layers, kv_bufs):
    norm_w, w_in, pe, w1, w2, w_out, rel_bias = prm
    bsz, t, d = xp.shape
    assert t % (SEL_STEP_TILES * KT) == 0 and t % PAGE_SIZE == 0 and t >= WINDOW
    n_tiles = t // TQ
    ns = t // SEL_BLOCK
    assert ns <= LANES
    x2 = xp.reshape(bsz * t, d)
    w = _nsa_in_weights(w_in)
    w_qzg = jnp.concatenate([w[:, COL_Q:COL_Q + NSA_Q_W], w[:, COL_Z:COL_Z + N_BRANCH * NSA_Q_W], w[:, COL_GATE:]], axis=1)
    ut = _norm_matmul_t(x2, norm_w, w_qzg.T.astype(BF16), bsz, w_qzg.shape[1] // 3)
    w_kvt = jnp.concatenate([w[:, COL_KV:COL_KV + 4 * NSA_KV_W], w[:, COL_KVW:COL_KVW + 2 * NSA_KV_W]], axis=1)
    kv_bufs = _kv_proj_t(x2, norm_w, w_kvt.T.astype(BF16), bsz, layer, n_layers, 4 * NSA_KV_W, kv_bufs)
    w_kv = jnp.concatenate([w[:, COL_KV + 2 * NSA_KV_W:COL_KV + 4 * NSA_KV_W], w[:, COL_KVW:COL_KVW + 2 * NSA_KV_W]],
                           axis=1).astype(BF16)
    kvn = _norm_matmul(x2, norm_w, w_kv, 4 * NSA_KV_W, out_dtype=BF16)
    t_pad = WINDOW + t
    kvn = jnp.pad(kvn.reshape(bsz, t, 4 * NSA_KV_W), ((0, 0), (WINDOW, 0), (0, 0))).reshape(bsz * t_pad, 4 * NSA_KV_W)
    n_pages = t // PAGE_SIZE
    kc, vc = _compress(kv_bufs[0], lambda b, p, ids: (b, layer, 0, p), bsz, n_pages, jnp.zeros((1, 1), jnp.int32),
                       pe, w1, w2)
    nh = kc.shape[1]
    n_far = _round_up(CMP_FRONT + nh, LANES)
    padc = ((0, 0), (CMP_FRONT, n_far - CMP_FRONT - nh), (0, 0))
    kcp = jnp.pad(kc, padc)
    vcp = jnp.pad(vc, padc)
    ov = np.zeros((n_far, LANES), np.float32)
    ov[CMP_FRONT:CMP_FRONT + nh - 1, :ns] = _overlap_matrix(nh - 1, ns)
    trel, tcn, cfar = _nsa_prompt_tables(rel_bias)
    rq = NSA_R * TQ
    const2 = lambda b, i: (0, 0)
    const3 = lambda b, i: (0, 0, 0)
    const4 = lambda b, i: (0, 0, 0, 0)
    zb = 1
    gate_blk = (1 + N_BRANCH) * NSA_Q_W // LANES
    once = dict(pipeline_mode=pl.Buffered(1))
    mixed_t = pl.pallas_call(
        functools.partial(_nsa_attn_kernel, n_sel=ns, n_cmp_rows=max(nh - CMP_NEAR, SUBLANES)),
        out_shape=jax.ShapeDtypeStruct((bsz, NSA_Q_W, t), BF16),
        grid=(bsz, n_tiles),
        in_specs=[
            pl.BlockSpec((None, NSA_Q_W, TQ), lambda b, i: (b, 0, i)),
            pl.BlockSpec((t_pad, 4 * NSA_KV_W), lambda b, i: (b, 0), **once),
            pl.BlockSpec((1, n_far, NSA_KV_W), lambda b, i: (b, 0, 0), **once),
            pl.BlockSpec((1, n_far, NSA_KV_W), lambda b, i: (b, 0, 0), **once),
            pl.BlockSpec((n_far, LANES), const2, **once),
            pl.BlockSpec((NSA_N_KV, N_REL_TILES + 1, KT, rq), const4, **once),
            pl.BlockSpec((NSA_N_KV, CMP_NEAR, rq), const3, **once),
            pl.BlockSpec((NSA_N_KV, SUBLANES, rq), const3, **once),
            pl.BlockSpec((None, LANES, TQ), lambda b, i: (b, gate_blk, i)),
            pl.BlockSpec((None, NSA_Q_W, TQ), lambda b, i: (b, zb, i)),
            pl.BlockSpec((None, NSA_Q_W, TQ), lambda b, i: (b, zb + 1, i)),
            pl.BlockSpec((None, NSA_Q_W, TQ), lambda b, i: (b, zb + 2, i)),
        ],
        out_specs=pl.BlockSpec((None, NSA_Q_W, TQ), lambda b, i: (b, 0, i)),
        scratch_shapes=[
            pltpu.VMEM((NSA_N_KV, 2 * LANES, rq), BF16),
            pltpu.VMEM((LANES, TQ), F32),
            pltpu.VMEM((NSA_N_KV, 1, rq), F32),
            pltpu.VMEM((NSA_N_KV, 1, rq), F32),
            pltpu.VMEM((NSA_N_KV, NSA_HEAD_DIM, rq), F32),
            pltpu.VMEM((N_BRANCH, NSA_N_KV, NSA_HEAD_DIM, rq), F32),
            pltpu.VMEM((NSA_N_KV, SEL_STEP_TILES * KT, rq), F32),
        ],
        compiler_params=_cparams(("parallel", "arbitrary")),
        name="nsa_attn",
    )(ut, kvn, kcp, vcp, jnp.asarray(ov), trel, tcn, cfar, ut, ut, ut, ut)
    out = _matmul_res_t(mixed_t, w_out.astype(BF16), x2).reshape(bsz, t, d)
    return out, kv_bufs


def _block_diag_q(q2):
    hg = lax.broadcasted_iota(jnp.int32, q2.shape, 0) // NSA_R
    return jnp.concatenate([jnp.where(hg == g, q2, 0.0) for g in range(NSA_N_KV)], axis=1)


def _group_lanes(o):
    dh = NSA_HEAD_DIM
    hg = lax.broadcasted_iota(jnp.int32, (o.shape[0], dh), 0) // NSA_R
    return sum(jnp.where(hg == g, o[:, g * dh:(g + 1) * dh], 0.0) for g in range(NSA_N_KV))


def _nsa_step_kernel(*refs, n_pp, past_len, n_cmp, n_sel_pad):
    page_refs = refs[1:1 + n_pp]
    (q_ref, kvn_ref, kwn_ref, gate_ref, z_ref, kc_ref, vc_ref, win_ref, ovp_ref, bsel_ref, bcmp_ref, bwin_ref, b0_ref,
     o_ref, sel_ref, m_ref, l_ref, acc_ref, oc_ref) = refs[1 + n_pp:]
    step = pl.program_id(1)
    nh = NSA_N_HEADS
    dh = NSA_HEAD_DIM
    qbd = _block_diag_q(q_ref[0] * (dh ** -0.5))
    qbd16 = qbd.astype(BF16)
    cur = past_len // SEL_BLOCK

    @pl.when(step == 0)
    def _():
        s_c = lax.dot_general(qbd16, kc_ref[0].astype(BF16), NT_DIMS, preferred_element_type=F32) + bcmp_ref[...]
        cid = lax.broadcasted_iota(jnp.int32, s_c.shape, 1)
        (p_c,) = _masked_softmax_parts([s_c], [cid < n_cmp], axis=1)
        oc_ref[...] = _group_lanes(jnp.dot(p_c.astype(BF16), vc_ref[0].astype(BF16), preferred_element_type=F32))
        gr = lax.broadcasted_iota(jnp.int32, (SUBLANES, nh), 0)
        gh = lax.broadcasted_iota(jnp.int32, (SUBLANES, nh), 1) // NSA_R
        pg = jnp.dot((gr == gh).astype(F32), p_c, precision=HIGHEST, preferred_element_type=F32)
        imp = jnp.dot(pg, ovp_ref[...], precision=HIGHEST, preferred_element_type=F32)
        sid = lax.broadcasted_iota(jnp.int32, imp.shape, 1)
        forced = (sid == 0) | (sid == cur) | (sid == cur - 1)
        imp = jnp.where(forced, FORCE_SCORE, imp)
        imp = jnp.where(sid * SEL_BLOCK <= past_len, imp, -FORCE_SCORE)
        er = lax.broadcasted_iota(jnp.int32, (n_sel_pad, n_sel_pad), 0)
        ec = lax.broadcasted_iota(jnp.int32, (n_sel_pad, n_sel_pad), 1)
        rows = []
        for g in range(NSA_N_KV):
            row = imp[g:g + 1, :]
            col = jnp.sum(jnp.where(er == ec, row, 0.0), axis=1, keepdims=True)
            ahead = (col > row) | ((col == row) & (er < ec))
            cnt = jnp.sum(ahead.astype(jnp.int32), axis=0, keepdims=True)
            sel = ((cnt < SEL_TOPK) & (row > -0.5 * FORCE_SCORE)).astype(F32)
            rows.extend([sel] * NSA_R)
        sel_ref[...] = jnp.concatenate(rows, axis=0)
        m_ref[...] = jnp.full(m_ref.shape, NEG, F32)
        l_ref[...] = jnp.zeros(l_ref.shape, F32)
        acc_ref[...] = jnp.zeros(acc_ref.shape, F32)

    def online(s):
        m_old = m_ref[...]
        m_new = jnp.maximum(m_old, jnp.max(s, axis=-1, keepdims=True))
        alpha = jnp.exp(m_old - m_new)
        pr = jnp.exp(s - m_new)
        l_ref[...] = l_ref[...] * alpha + jnp.sum(pr, axis=-1, keepdims=True)
        m_ref[...] = m_new
        return alpha, pr

    k_all = jnp.concatenate([r[0:NSA_KV_W, :] for r in page_refs], axis=1).astype(BF16)
    v_all = jnp.concatenate([r[NSA_KV_W:2 * NSA_KV_W, :] for r in page_refs], axis=1).astype(BF16)
    n_tok = n_pp * PAGE_SIZE
    s = jnp.dot(qbd16, k_all, preferred_element_type=F32) + bsel_ref[0]
    eb = lax.broadcasted_iota(jnp.int32, (n_sel_pad, n_tok), 0)
    ej = lax.broadcasted_iota(jnp.int32, (n_sel_pad, n_tok), 1)
    expand = (eb == step * (n_tok // SEL_BLOCK) + ej // SEL_BLOCK).astype(BF16)
    chosen = jnp.dot(sel_ref[...].astype(BF16), expand, preferred_element_type=F32)
    s = jnp.where(chosen > 0.5, s, NEG)
    alpha, pr = online(s)
    acc_ref[...] = acc_ref[...] * alpha + lax.dot_general(pr.astype(BF16), v_all, NT_DIMS, preferred_element_type=F32)

    @pl.when(step == pl.num_programs(1) - 1)
    def _():
        b0 = b0_ref[:, 0:1]
        kn = kvn_ref[0, :, 2 * NSA_KV_W:3 * NSA_KV_W]
        vn = kvn_ref[0, :, 3 * NSA_KV_W:4 * NSA_KV_W]
        s_n = jnp.sum(qbd * kn, axis=-1, keepdims=True) + b0
        s_n = jnp.where(sel_ref[:, cur:cur + 1] > 0.5, s_n, NEG)
        alpha, pr = online(s_n)
        acc = acc_ref[...] * alpha + pr * vn
        o_s = _group_lanes(acc / l_ref[...])
        wk = win_ref[0, 0:NSA_KV_W, :].astype(BF16)
        wv = win_ref[0, NSA_KV_W:2 * NSA_KV_W, :].astype(BF16)
        n_win = win_ref.shape[2]
        s_w = jnp.dot(qbd16, wk, preferred_element_type=F32) + bwin_ref[...]
        wi = lax.broadcasted_iota(jnp.int32, s_w.shape, 1)
        s_wn = jnp.sum(qbd * kwn_ref[0, :, 0:NSA_KV_W], axis=-1, keepdims=True) + b0
        p_w, p_wn = _masked_softmax_parts([s_w, s_wn], [n_win - wi <= WINDOW, jnp.full(s_wn.shape, True)], axis=1)
        o_w = _group_lanes(lax.dot_general(p_w.astype(BF16), wv, NT_DIMS, preferred_element_type=F32)
                           + p_wn * kwn_ref[0, :, NSA_KV_W:2 * NSA_KV_W])
        gate = jax.nn.sigmoid(gate_ref[0])
        mixed = (oc_ref[...] * _silu(z_ref[0, 0]) * gate[:, 0:1] + o_s * _silu(z_ref[0, 1]) * gate[:, 1:2]
                 + o_w * _silu(z_ref[0, 2]) * gate[:, 2:3])
        o_ref[0] = mixed


def _nsa_layer_sample(xs_in, cache_t, win_t, page_table, prm, layer, n_layers):
    norm_w, w_in, pe, w1, w2, w_out, rel_bias = prm
    bsz, t, d = xs_in.shape
    assert t == 1
    n_pages = page_table.shape[1]
    past_len = n_pages * PAGE_SIZE
    n_win = win_t.shape[3]
    nh, dh = NSA_N_HEADS, NSA_HEAD_DIM
    x2 = xs_in.reshape(bsz, d)
    u = _norm_matmul(x2, norm_w, _nsa_in_weights(w_in).astype(BF16), _pick_tn(NSA_N_PAD, 1152))
    ids = (page_table * n_layers + layer).astype(jnp.int32)
    kc, vc = _compress(cache_t, lambda b, p, ids: (ids[b, p], 0, 0), bsz, n_pages, ids, pe, w1, w2)
    n_cmp_pad = kc.shape[1]
    n_cmp = n_cmp_pad - 1
    ns = past_len // SEL_BLOCK + 1
    n_sel_pad = _round_up(ns, LANES)
    ov = np.zeros((n_cmp_pad, n_sel_pad), np.float32)
    ov[:n_cmp, :ns] = _overlap_matrix(n_cmp, ns)
    n_pp = math.gcd(PAGES_PER_STEP, n_pages)
    n_steps = n_pages // n_pp
    n_tok = n_pp * PAGE_SIZE
    vec = _bias_by_dist(rel_bias, past_len + 1)
    bsel = jnp.moveaxis(jnp.flip(vec[:, 1:], axis=1).reshape(nh, n_steps, n_tok), 0, 1)
    bcmp = jnp.flip(vec[:, :past_len - (CMP_BLOCK - 1) + 1], axis=1)[:, ::CMP_STRIDE]
    bcmp = jnp.pad(bcmp, ((0, 0), (0, n_cmp_pad - bcmp.shape[1])))
    bwin = jnp.flip(vec[:, 1:n_win + 1], axis=1)
    b0 = jnp.broadcast_to(vec[:, 0:1], (nh, LANES))
    q3 = u[:, COL_Q:COL_Q + NSA_Q_W].reshape(bsz, nh, dh)
    kv_new = u[:, COL_KV:COL_KV + 4 * NSA_KV_W]
    kw_new = u[:, COL_KVW:COL_KVW + 2 * NSA_KV_W]
    gate = u[:, COL_GATE:COL_GATE + nh * N_BRANCH].reshape(bsz, nh, N_BRANCH)
    z4 = u[:, COL_Z:COL_Z + N_BRANCH * NSA_Q_W].reshape(bsz, N_BRANCH, nh, dh)
    per_b3 = lambda b, s, ids: (b, 0, 0)
    const2 = lambda b, s, ids: (0, 0)
    page_specs = [pl.BlockSpec((None, 2 * NSA_KV_W, PAGE_SIZE),
                               functools.partial(lambda b, s, ids, k: (ids[b, s * n_pp + k], 1, 0), k=k))
                  for k in range(n_pp)]
    grid_spec = pltpu.PrefetchScalarGridSpec(
        num_scalar_prefetch=1,
        grid=(bsz, n_steps),
        in_specs=page_specs + [
            pl.BlockSpec((1, nh, dh), per_b3),
            pl.BlockSpec((1, 1, 4 * NSA_KV_W), per_b3),
            pl.BlockSpec((1, 1, 2 * NSA_KV_W), per_b3),
            pl.BlockSpec((1, nh, N_BRANCH), per_b3),
            pl.BlockSpec((1, N_BRANCH, nh, dh), lambda b, s, ids: (b, 0, 0, 0)),
            pl.BlockSpec((1, n_cmp_pad, NSA_KV_W), per_b3),
            pl.BlockSpec((1, n_cmp_pad, NSA_KV_W), per_b3),
            pl.BlockSpec((None, 1, 2 * NSA_KV_W, n_win), lambda b, s, ids: (layer, b, 0, 0)),
            pl.BlockSpec((n_cmp_pad, n_sel_pad), const2),
            pl.BlockSpec((1, nh, n_tok), lambda b, s, ids: (s, 0, 0)),
            pl.BlockSpec((nh, n_cmp_pad), const2),
            pl.BlockSpec((nh, n_win), const2),
            pl.BlockSpec((nh, LANES), const2),
        ],
        out_specs=pl.BlockSpec((1, nh, dh), per_b3),
        scratch_shapes=[
            pltpu.VMEM((nh, n_sel_pad), F32),
            pltpu.VMEM((nh, 1), F32),
            pltpu.VMEM((nh, 1), F32),
            pltpu.VMEM((nh, NSA_KV_W), F32),
            pltpu.VMEM((nh, dh), F32),
        ],
    )
    mixed = pl.pallas_call(
        functools.partial(_nsa_step_kernel, n_pp=n_pp, past_len=past_len, n_cmp=n_cmp, n_sel_pad=n_sel_pad),
        out_shape=jax.ShapeDtypeStruct((bsz, nh, dh), F32),
        grid_spec=grid_spec,
        compiler_params=_cparams(("parallel", "arbitrary")),
        name="nsa_step",
    )(ids, *([cache_t] * n_pp), q3, kv_new.reshape(bsz, 1, -1), kw_new.reshape(bsz, 1, -1), gate, z4, kc, vc, win_t,
      jnp.asarray(ov), bsel, bcmp, bwin, b0)
    out = _matmul_res(mixed.reshape(bsz, NSA_Q_W).astype(BF16), w_out.astype(BF16), x2).reshape(bsz, 1, d)
    return out, kv_new, kw_new


def kernel(x_prompt, x_sample, state_ssm, state_conv, cache_kv, cache_win, page_table, rel_bias, final_norm,
           ssd_norm, ssd_w_in, ssd_conv_w, ssd_conv_b, ssd_dt_bias, ssd_a_log, ssd_d, ssd_gnorm, ssd_w_out,
           nsa_norm, nsa_w_in, nsa_cmp_pe, nsa_cmp_w1, nsa_cmp_w2, nsa_w_out):
    depth = ssd_norm.shape[0] + nsa_norm.shape[0]
    g, dh = NSA_N_KV, NSA_HEAD_DIM
    xp, xs = x_prompt, x_sample
    bp, tp, d = xp.shape
    bs = xs.shape[0]
    n_phys, n_nsa = cache_kv.shape[:2]
    win_len_s = cache_win.shape[2]
    win_len_p = min(WINDOW, tp)
    cache_t = cache_kv.transpose(0, 1, 3, 4, 5, 2).reshape(n_phys * n_nsa, 4 * NSA_KV_W, PAGE_SIZE)
    cwin_t = cache_win.transpose(0, 1, 3, 4, 5, 2).reshape(n_nsa, bs, 2 * NSA_KV_W, win_len_s)
    ssm_p, conv_p, kv_bufs = [], [], None
    ssm_s, conv_s, kv_s, win_s = [], [], [], []
    for layer in range(depth):
        j = layer // 2
        if layer % 2 == 0:
            prm = (ssd_norm[j], ssd_w_in[j], ssd_conv_w[j], ssd_conv_b[j], ssd_dt_bias[j], ssd_a_log[j],
                   ssd_d[j], ssd_gnorm[j], ssd_w_out[j])
            xp, c, s = _ssd_layer_prompt(xp, prm)
            conv_p.append(c)
            ssm_p.append(s)
            xs, c, s = _ssd_layer_sample(xs, state_conv[j], state_ssm, j, prm)
            conv_s.append(c)
            ssm_s.append(s)
        else:
            prm = (nsa_norm[j], nsa_w_in[j], nsa_cmp_pe[j], nsa_cmp_w1[j], nsa_cmp_w2[j], nsa_w_out[j], rel_bias)
            xp, kv_bufs = _nsa_layer_prompt(xp, prm, j, n_nsa, kv_bufs)
            xs, kv_new, kw_new = _nsa_layer_sample(xs, cache_t, cwin_t, page_table, prm, j, n_nsa)
            kv_s.append(kv_new.reshape(bs, 1, 4, g, dh))
            win_s.append(jnp.concatenate([cwin_t[j][:, :, 1:], kw_new[:, :, None]], axis=2))
    y_prompt = _rmsnorm(xp.reshape(bp * tp, d), final_norm).reshape(bp, tp, d)
    y_sample = _rmsnorm(xs.reshape(bs, d), final_norm).reshape(bs, 1, d)
    kv_prompt = kv_bufs[0].reshape(bp, n_nsa, 4, g, dh, tp).transpose(0, 5, 1, 2, 3, 4)
    win_prompt = kv_bufs[1][:, :, :, tp - win_len_p:].reshape(bp, n_nsa, 2, g, dh, win_len_p).transpose(1, 0, 5, 2, 3, 4)
    win_sample = jnp.stack(win_s).reshape(n_nsa, bs, 2, g, dh, win_len_s).transpose(0, 1, 5, 2, 3, 4)
    return (y_prompt, y_sample, jnp.stack(ssm_p), jnp.stack(conv_p), kv_prompt, win_prompt,
            jnp.stack(ssm_s), jnp.stack(conv_s), jnp.stack(kv_s, axis=2), win_sample)
```

```python
import functools
import math

import jax
import jax.numpy as jnp
import numpy as np
from jax import lax
from jax.experimental import pallas as pl
from jax.experimental.pallas import tpu as pltpu

F32 = jnp.float32
BF16 = jnp.bfloat16
HIGHEST = lax.Precision.HIGHEST

NORM_EPS = 1e-6
SSD_HEAD_DIM = 64
SSD_N_GROUPS = 8
SSD_D_STATE = 128
SSD_CONV_W = 4
SSD_CHUNK = 128
NSA_N_HEADS = 16
NSA_HEAD_DIM = 64
NSA_N_KV = 4
NSA_R = NSA_N_HEADS // NSA_N_KV
N_BRANCH = 3
CMP_BLOCK = 32
CMP_STRIDE = 16
CMP_HIDDEN = 128
SEL_BLOCK = 64
SEL_TOPK = 16
WINDOW = 512
FORCE_SCORE = 1e4
REL_BUCKETS = 32
REL_MAX_DIST = 1024
PAGE_SIZE = 128

LANES = 128
SUBLANES = 8
VMEM_LIMIT = 56 * 1024 * 1024

NEG = -1e30
LOG2E = math.log2(math.e)
NT_DIMS = (((1,), (1,)), ((), ()))
TN_DIMS = (((0,), (0,)), ((), ()))


def _cparams(sem):
    return pltpu.CompilerParams(dimension_semantics=sem, vmem_limit_bytes=VMEM_LIMIT)


def _round_up(x, m):
    return (x + m - 1) // m * m


def _silu(x):
    return x * jax.nn.sigmoid(x)


def _pick_tm(m, cap, quantum=SUBLANES):
    if m <= cap:
        return m
    for tm in range(cap // quantum * quantum, 0, -quantum):
        if m % tm == 0:
            return tm
    raise ValueError(f"no row tile for {m}")


def _pick_tn(n, cap=1024):
    best = LANES
    for tn in range(LANES, cap + 1, LANES):
        if n % tn == 0:
            best = tn
    return best


def _norm_matmul_kernel(x_ref, nw_ref, w_ref, o_ref, xn_ref, *, transpose_out):
    @pl.when(pl.program_id(1) == 0)
    def _():
        x = x_ref[...]
        ms = jnp.mean(x * x, axis=-1, keepdims=True)
        xn = x * lax.rsqrt(ms + NORM_EPS) * nw_ref[...]
        if transpose_out:
            xn_ref[...] = xn.T.astype(BF16)
        else:
            xn_ref[...] = xn.astype(BF16)

    if transpose_out:
        o_ref[...] = jnp.dot(w_ref[...], xn_ref[...], preferred_element_type=F32).astype(o_ref.dtype)
    else:
        o_ref[...] = jnp.dot(xn_ref[...], w_ref[...], preferred_element_type=F32).astype(o_ref.dtype)


def _norm_matmul(x, nw, w, tn, out_dtype=F32):
    m, k = x.shape
    n = w.shape[1]
    assert n % tn == 0
    whole_w = tn == n
    tm = _pick_tm(m, 512 if whole_w else 1024)
    w_mode = dict(pipeline_mode=pl.Buffered(1)) if whole_w else {}
    return pl.pallas_call(
        functools.partial(_norm_matmul_kernel, transpose_out=False),
        out_shape=jax.ShapeDtypeStruct((m, n), out_dtype),
        grid=(m // tm, n // tn),
        in_specs=[
            pl.BlockSpec((tm, k), lambda i, j: (i, 0)),
            pl.BlockSpec((1, k), lambda i, j: (0, 0)),
            pl.BlockSpec((k, tn), lambda i, j: (0, j), **w_mode),
        ],
        out_specs=pl.BlockSpec((tm, tn), lambda i, j: (i, j)),
        scratch_shapes=[pltpu.VMEM((tm, k), BF16)],
        compiler_params=_cparams(("parallel", "arbitrary")),
        name="norm_matmul",
    )(x, nw.reshape(1, k), w)


def _norm_matmul_t(x, nw, wt, bsz, tn):
    m, k = x.shape
    n = wt.shape[0]
    t = m // bsz
    assert n % tn == 0
    whole_w = tn == n
    tm = _pick_tm(t, 512 if whole_w else 1024, LANES)
    w_mode = dict(pipeline_mode=pl.Buffered(1)) if whole_w else {}
    tpb = t // tm
    return pl.pallas_call(
        functools.partial(_norm_matmul_kernel, transpose_out=True),
        out_shape=jax.ShapeDtypeStruct((bsz, n, t), F32),
        grid=(m // tm, n // tn),
        in_specs=[
            pl.BlockSpec((tm, k), lambda i, j: (i, 0)),
            pl.BlockSpec((1, k), lambda i, j: (0, 0)),
            pl.BlockSpec((tn, k), lambda i, j: (j, 0), **w_mode),
        ],
        out_specs=pl.BlockSpec((None, tn, tm), lambda i, j: (i // tpb, j, i % tpb)),
        scratch_shapes=[pltpu.VMEM((k, tm), BF16)],
        compiler_params=_cparams(("parallel", "arbitrary")),
        name="norm_matmul_t",
    )(x, nw.reshape(1, k), wt)


def _kv_proj_kernel(x_ref, nw_ref, w_ref, *rest, n_kv_tiles):
    okv_ref, okvw_ref, xn_ref = rest[-3:]
    j = pl.program_id(1)

    @pl.when(j == 0)
    def _():
        x = x_ref[...]
        ms = jnp.mean(x * x, axis=-1, keepdims=True)
        xn_ref[...] = (x * lax.rsqrt(ms + NORM_EPS) * nw_ref[...]).T.astype(BF16)

    res = jnp.dot(w_ref[...], xn_ref[...], preferred_element_type=F32)

    @pl.when(j < n_kv_tiles)
    def _():
        okv_ref[...] = res

    @pl.when(j >= n_kv_tiles)
    def _():
        okvw_ref[...] = res


def _kv_proj_t(x, nw, wt, bsz, layer, n_layers, kv_rows, bufs):
    m, k = x.shape
    n = wt.shape[0]
    t = m // bsz
    tn = n - kv_rows
    assert kv_rows % tn == 0 and tn % SUBLANES == 0
    n_kv_tiles = kv_rows // tn
    tm = _pick_tm(t, 1024, LANES)
    tpb = t // tm
    in_specs = [
        pl.BlockSpec((tm, k), lambda i, j: (i, 0)),
        pl.BlockSpec((1, k), lambda i, j: (0, 0)),
        pl.BlockSpec((tn, k), lambda i, j: (j, 0)),
    ]
    args = [x, nw.reshape(1, k), wt]
    aliases = {}
    if bufs is not None:
        in_specs += [pl.BlockSpec(memory_space=pl.ANY), pl.BlockSpec(memory_space=pl.ANY)]
        aliases = {len(args): 0, len(args) + 1: 1}
        args += list(bufs)
    return pl.pallas_call(
        functools.partial(_kv_proj_kernel, n_kv_tiles=n_kv_tiles),
        out_shape=(jax.ShapeDtypeStruct((bsz, n_layers, kv_rows, t), F32),
                   jax.ShapeDtypeStruct((bsz, n_layers, tn, t), F32)),
        grid=(m // tm, n // tn),
        in_specs=in_specs,
        out_specs=(pl.BlockSpec((None, None, tn, tm),
                                lambda i, j: (i // tpb, layer, jnp.minimum(j, n_kv_tiles - 1), i % tpb)),
                   pl.BlockSpec((None, None, tn, tm), lambda i, j: (i // tpb, layer, 0, i % tpb))),
        scratch_shapes=[pltpu.VMEM((k, tm), BF16)],
        input_output_aliases=aliases,
        compiler_params=_cparams(("parallel", "arbitrary")),
        name="kv_proj_t",
    )(*args)


def _matmul_res_kernel(y_ref, w_ref, x_ref, o_ref, *, y_transposed):
    dims = TN_DIMS if y_transposed else (((1,), (0,)), ((), ()))
    o_ref[...] = x_ref[...] + lax.dot_general(y_ref[...], w_ref[...], dims, preferred_element_type=F32)


def _matmul_res(y, w, x):
    m, k = y.shape
    n = w.shape[1]
    tm = min(m, 512)
    assert m % tm == 0
    return pl.pallas_call(
        functools.partial(_matmul_res_kernel, y_transposed=False),
        out_shape=jax.ShapeDtypeStruct((m, n), F32),
        grid=(m // tm,),
        in_specs=[
            pl.BlockSpec((tm, k), lambda i: (i, 0)),
            pl.BlockSpec((k, n), lambda i: (0, 0)),
            pl.BlockSpec((tm, n), lambda i: (i, 0)),
        ],
        out_specs=pl.BlockSpec((tm, n), lambda i: (i, 0)),
        compiler_params=_cparams(("parallel",)),
        name="matmul_res",
    )(y, w, x)


def _matmul_res_t(yt, w, x):
    bsz, k, t = yt.shape
    n = w.shape[1]
    tm = min(t, 512)
    assert t % tm == 0
    tpb = t // tm
    return pl.pallas_call(
        functools.partial(_matmul_res_kernel, y_transposed=True),
        out_shape=jax.ShapeDtypeStruct((bsz * t, n), F32),
        grid=(bsz * tpb,),
        in_specs=[
            pl.BlockSpec((None, k, tm), lambda i: (i // tpb, 0, i % tpb)),
            pl.BlockSpec((k, n), lambda i: (0, 0)),
            pl.BlockSpec((tm, n), lambda i: (i, 0)),
        ],
        out_specs=pl.BlockSpec((tm, n), lambda i: (i, 0)),
        compiler_params=_cparams(("parallel",)),
        name="matmul_res_t",
    )(yt, w, x)


def _rmsnorm_kernel(x_ref, nw_ref, o_ref):
    x = x_ref[...]
    ms = jnp.mean(x * x, axis=-1, keepdims=True)
    o_ref[...] = x * lax.rsqrt(ms + NORM_EPS) * nw_ref[...]


def _rmsnorm(x, nw):
    m, k = x.shape
    tm = min(m, 512)
    assert m % tm == 0
    return pl.pallas_call(
        _rmsnorm_kernel,
        out_shape=jax.ShapeDtypeStruct((m, k), F32),
        grid=(m // tm,),
        in_specs=[pl.BlockSpec((tm, k), lambda i: (i, 0)), pl.BlockSpec((1, k), lambda i: (0, 0))],
        out_specs=pl.BlockSpec((tm, k), lambda i: (i, 0)),
        compiler_params=_cparams(("parallel",)),
        name="final_rmsnorm",
    )(x, nw.reshape(1, k))


def _ssd_chunk_kernel(z_ref, x_ref, bc_ref, dt_ref, cprev_ref, sprev_ref, cw_ref, cb_ref, dtb_ref, alog_ref,
                      dskip_ref, gn_ref, y_ref, hlast_ref, xpad_ref, state_ref, *, d_inner, n_heads):
    lc = SSD_CHUNK
    g_n, n, p = SSD_N_GROUPS, SSD_D_STATE, SSD_HEAD_DIM
    r_n = n_heads // g_n
    c = pl.program_id(1)

    @pl.when(c == 0)
    def _():
        xpad_ref[0:SUBLANES, :] = cprev_ref[0]
        state_ref[...] = sprev_ref[0]

    @pl.when(c > 0)
    def _():
        xpad_ref[0:SUBLANES, :] = xpad_ref[lc:lc + SUBLANES, :]

    xpad_ref[SUBLANES:SUBLANES + lc, 0:d_inner] = x_ref[...]
    xpad_ref[SUBLANES:SUBLANES + lc, d_inner:] = bc_ref[...]

    conv = cb_ref[...]
    for k in range(SSD_CONV_W):
        off = SUBLANES - (SSD_CONV_W - 1) + k
        conv = conv + xpad_ref[off:off + lc, :] * cw_ref[k:k + 1, :]
    act = _silu(conv)
    xs = act[:, :d_inner]
    bm = act[:, d_inner:d_inner + g_n * n]
    cm = act[:, d_inner + g_n * n:]

    dt = jax.nn.softplus(dt_ref[...] + dtb_ref[...])
    a = -jnp.exp(alog_ref[...])
    dta = dt * a
    row = lax.broadcasted_iota(jnp.int32, (lc, lc), 0)
    col = lax.broadcasted_iota(jnp.int32, (lc, lc), 1)
    causal = row >= col
    tril = causal.astype(F32)
    a_cs = jnp.dot(tril, dta, precision=HIGHEST, preferred_element_type=F32)
    a_cs_t = a_cs.T
    a_end = a_cs[lc - 1:lc, :]
    to_end = jnp.exp(a_end - a_cs)
    e_cs = jnp.exp(a_cs)
    e_end = jnp.exp(a_end)

    assert 2 * p == LANES and r_n % 2 == 0
    lane_lo = lax.broadcasted_iota(jnp.int32, (1, LANES), 1) < p

    def pair_lanes(v, k):
        return jnp.where(lane_lo, v[:, 2 * k:2 * k + 1], v[:, 2 * k + 1:2 * k + 2])

    zs = _silu(z_ref[...])
    gw = r_n * p
    ys = []
    for g in range(g_n):
        bg16 = bm[:, g * n:(g + 1) * n].astype(BF16)
        cg16 = cm[:, g * n:(g + 1) * n].astype(BF16)
        cb = lax.dot_general(cg16, bg16, NT_DIMS, preferred_element_type=F32)
        pairs = range(g * r_n // 2, (g + 1) * r_n // 2)
        gl = slice(g * gw, (g + 1) * gw)
        xs_g = xs[:, gl]
        xdt_g = xs_g * jnp.concatenate([pair_lanes(dt, k) for k in pairs], axis=1)
        y_diag = []
        for i, k in enumerate(pairs):
            xdt_k = xdt_g[:, i * LANES:(i + 1) * LANES]
            acc = None
            for j, h in enumerate((2 * k, 2 * k + 1)):
                seg = a_cs[:, h:h + 1] - a_cs_t[h:h + 1, :]
                decay = jnp.exp(jnp.where(causal, seg, -jnp.inf))
                m_h = (cb * decay).astype(BF16)
                x_h = jnp.where(lane_lo if j == 0 else jnp.logical_not(lane_lo), xdt_k, 0.0).astype(BF16)
                d = jnp.dot(m_h, x_h, preferred_element_type=F32)
                acc = d if acc is None else acc + d
            y_diag.append(acc)
        y_diag = jnp.concatenate(y_diag, axis=1)
        sprev = state_ref[:, gl]
        y_off = (jnp.dot(cg16, sprev.astype(BF16), preferred_element_type=F32)
                 * jnp.concatenate([pair_lanes(e_cs, k) for k in pairs], axis=1))
        xw = (xdt_g * jnp.concatenate([pair_lanes(to_end, k) for k in pairs], axis=1)).astype(BF16)
        st = lax.dot_general(bg16, xw, TN_DIMS, preferred_element_type=F32)
        state_ref[:, gl] = sprev * jnp.concatenate([pair_lanes(e_end, k) for k in pairs], axis=1) + st
        yg = (y_diag + y_off + xs_g * dskip_ref[:, gl]) * zs[:, gl]
        ms = jnp.mean(yg * yg, axis=-1, keepdims=True)
        ys.append(yg * lax.rsqrt(ms + NORM_EPS))
    y = jnp.concatenate(ys, axis=1) * gn_ref[...]
    y_ref[...] = y.astype(y_ref.dtype)

    @pl.when(c == pl.num_programs(1) - 1)
    def _():
        hlast_ref[0] = state_ref[...]


def _pad_lanes(v, width=LANES):
    return jnp.pad(v.astype(F32), (0, width - v.shape[0])).reshape(1, width)


def _ssd_prompt_core(u, bsz, t, conv_prev8, ssm_prev, conv_w, conv_b, dt_bias, a_log, d_skip, gnorm):
    n_heads = dt_bias.shape[0]
    d_inner = n_heads * SSD_HEAD_DIM
    conv_dim = conv_w.shape[1]
    assert conv_dim == 2 * d_inner and t % SSD_CHUNK == 0
    lc = SSD_CHUNK
    nc = t // lc
    dt_blk = (d_inner + conv_dim) // LANES
    kern = functools.partial(_ssd_chunk_kernel, d_inner=d_inner, n_heads=n_heads)
    const = lambda b, c: (0, 0)
    state_t = ssm_prev.transpose(0, 3, 1, 2).reshape(bsz, SSD_D_STATE, d_inner)
    y, hlast_t = pl.pallas_call(
        kern,
        out_shape=(jax.ShapeDtypeStruct((bsz * t, d_inner), BF16),
                   jax.ShapeDtypeStruct((bsz, SSD_D_STATE, d_inner), F32)),
        grid=(bsz, nc),
        in_specs=[
            pl.BlockSpec((lc, d_inner), lambda b, c: (b * nc + c, 0)),
            pl.BlockSpec((lc, d_inner), lambda b, c: (b * nc + c, 1)),
            pl.BlockSpec((lc, d_inner), lambda b, c: (b * nc + c, 2)),
            pl.BlockSpec((lc, LANES), lambda b, c: (b * nc + c, dt_blk)),
            pl.BlockSpec((1, SUBLANES, conv_dim), lambda b, c: (b, 0, 0)),
            pl.BlockSpec((1, SSD_D_STATE, d_inner), lambda b, c: (b, 0, 0)),
            pl.BlockSpec((SSD_CONV_W, conv_dim), const),
            pl.BlockSpec((1, conv_dim), const),
            pl.BlockSpec((1, LANES), const),
            pl.BlockSpec((1, LANES), const),
            pl.BlockSpec((1, d_inner), const),
            pl.BlockSpec((1, d_inner), const),
        ],
        out_specs=(pl.BlockSpec((lc, d_inner), lambda b, c: (b * nc + c, 0)),
                   pl.BlockSpec((1, SSD_D_STATE, d_inner), lambda b, c: (b, 0, 0))),
        scratch_shapes=[pltpu.VMEM((lc + SUBLANES, conv_dim), F32),
                        pltpu.VMEM((SSD_D_STATE, d_inner), F32)],
        compiler_params=_cparams(("parallel", "arbitrary")),
        name="ssd_chunk",
    )(u, u, u, u, conv_prev8, state_t, conv_w, conv_b.reshape(1, conv_dim), _pad_lanes(dt_bias),
      _pad_lanes(a_log), jnp.repeat(d_skip.astype(F32), SSD_HEAD_DIM).reshape(1, d_inner), gnorm.reshape(1, d_inner))
    hlast = hlast_t.reshape(bsz, SSD_D_STATE, n_heads, SSD_HEAD_DIM).transpose(0, 2, 3, 1)
    return y, hlast


def _ssd_in_weights(w_in, d_inner, conv_dim):
    main = d_inner + conv_dim
    n_dt = w_in.shape[1] - main
    w = jnp.concatenate([w_in[:, :main], jnp.pad(w_in[:, main:], ((0, 0), (0, LANES - n_dt)))], axis=1)
    return w.astype(BF16)


def _ssd_layer_prompt(xp, prm):
    norm_w, w_in, conv_w, conv_b, dt_bias, a_log, d_skip, gnorm, w_out = prm
    bsz, t, d = xp.shape
    assert t >= SSD_CONV_W - 1
    n_heads = dt_bias.shape[0]
    d_inner = n_heads * SSD_HEAD_DIM
    conv_dim = conv_w.shape[1]
    w = _ssd_in_weights(w_in, d_inner, conv_dim)
    x2 = xp.reshape(bsz * t, d)
    u = _norm_matmul(x2, norm_w, w, w.shape[1])
    conv_prev8 = jnp.zeros((bsz, SUBLANES, conv_dim), F32)
    ssm_prev = jnp.zeros((bsz, n_heads, SSD_HEAD_DIM, SSD_D_STATE), F32)
    y, hlast = _ssd_prompt_core(u, bsz, t, conv_prev8, ssm_prev, conv_w, conv_b, dt_bias, a_log, d_skip, gnorm)
    out = _matmul_res(y, w_out.astype(BF16), x2).reshape(bsz, t, d)
    u3 = u.reshape(bsz, t, -1)
    new_conv = u3[:, t - (SSD_CONV_W - 1):, d_inner:d_inner + conv_dim]
    return out, new_conv, hlast


def _ssd_step_pre_kernel(u_ref, cst_ref, cw_ref, cb_ref, dtb_ref, alog_ref,
                         xs_ref, xdt_ref, dec_ref, bm_ref, cm_ref, ncst_ref, *, d_inner, n_heads):
    conv_dim = cw_ref.shape[1]
    gn = SSD_N_GROUPS * SSD_D_STATE
    xbc = u_ref[:, d_inner:d_inner + conv_dim]
    conv = cb_ref[...]
    for k in range(SSD_CONV_W - 1):
        conv = conv + cst_ref[k] * cw_ref[k:k + 1, :]
    conv = conv + xbc * cw_ref[SSD_CONV_W - 1:SSD_CONV_W, :]
    for k in range(SSD_CONV_W - 2):
        ncst_ref[k] = cst_ref[k + 1]
    ncst_ref[SSD_CONV_W - 2] = xbc
    act = _silu(conv)
    xs = act[:, :d_inner]
    dt = jax.nn.softplus(u_ref[:, d_inner + conv_dim:d_inner + conv_dim + LANES] + dtb_ref[...])
    dec = jnp.exp(dt * (-jnp.exp(alog_ref[...])))
    hrow = lax.broadcasted_iota(jnp.int32, (LANES, d_inner), 0)
    hcol = lax.broadcasted_iota(jnp.int32, (LANES, d_inner), 1) // SSD_HEAD_DIM
    expand = (hrow == hcol).astype(F32)
    dt_e = jnp.dot(dt, expand, precision=HIGHEST, preferred_element_type=F32)
    dec_e = jnp.dot(dec, expand, precision=HIGHEST, preferred_element_type=F32)
    xs_ref[...] = xs
    xdt_ref[...] = xs * dt_e
    dec_ref[...] = dec_e
    bm_ref[...] = act[:, d_inner:d_inner + gn]
    cm_ref[...] = act[:, d_inner + gn:]


def _ssd_step_state_kernel(s_ref, xdt_ref, dec_ref, b_ref, c_ref, snew_ref, y_ref, *, rows_per_group):
    for g in range(SSD_N_GROUPS):
        sl = slice(g * rows_per_group, (g + 1) * rows_per_group)
        s = s_ref[0, sl, :]
        bg = b_ref[0, g:g + 1, :]
        cg = c_ref[0, g:g + 1, :]
        xdt = xdt_ref[0, sl, :]
        dec = dec_ref[0, sl, :]
        cb = jnp.sum(bg * cg, axis=1, keepdims=True)
        y_off = jnp.sum(s * cg, axis=1, keepdims=True)
        snew_ref[0, sl, :] = s * dec + xdt * bg
        y_ref[0, sl, :] = cb * xdt + dec * y_off


def _ssd_step_post_kernel(y_ref, xs_ref, u_ref, dsk_ref, gn_ref, o_ref, *, d_inner, group_width):
    y = (y_ref[...] + xs_ref[...] * dsk_ref[...]) * _silu(u_ref[:, :d_inner])
    outs = []
    for g in range(SSD_N_GROUPS):
        yg = y[:, g * group_width:(g + 1) * group_width]
        ms = jnp.mean(yg * yg, axis=-1, keepdims=True)
        outs.append(yg * lax.rsqrt(ms + NORM_EPS))
    o_ref[...] = (jnp.concatenate(outs, axis=1) * gn_ref[...]).astype(o_ref.dtype)


def _ssd_layer_sample(xs_in, conv_state, ssm_states, layer, prm):
    norm_w, w_in, conv_w, conv_b, dt_bias, a_log, d_skip, gnorm, w_out = prm
    bsz, t, d = xs_in.shape
    assert t == 1
    n_heads = dt_bias.shape[0]
    d_inner = n_heads * SSD_HEAD_DIM
    conv_dim = conv_w.shape[1]
    gn = SSD_N_GROUPS * SSD_D_STATE
    w = _ssd_in_weights(w_in, d_inner, conv_dim)
    x2 = xs_in.reshape(bsz, d)
    u = _norm_matmul(x2, norm_w, w, _pick_tn(w.shape[1]))
    cst = conv_state.transpose(1, 0, 2)
    pre = functools.partial(_ssd_step_pre_kernel, d_inner=d_inner, n_heads=n_heads)
    xs, xdt, dec, bm, cm, ncst = pl.pallas_call(
        pre,
        out_shape=(jax.ShapeDtypeStruct((bsz, d_inner), F32), jax.ShapeDtypeStruct((bsz, d_inner), F32),
                   jax.ShapeDtypeStruct((bsz, d_inner), F32), jax.ShapeDtypeStruct((bsz, gn), F32),
                   jax.ShapeDtypeStruct((bsz, gn), F32), jax.ShapeDtypeStruct(cst.shape, F32)),
        compiler_params=pltpu.CompilerParams(vmem_limit_bytes=VMEM_LIMIT),
        name="ssd_step_pre",
    )(u, cst, conv_w, conv_b.reshape(1, conv_dim), _pad_lanes(dt_bias), _pad_lanes(a_log))
    rows = n_heads * SSD_HEAD_DIM
    rpg = rows // SSD_N_GROUPS
    st = functools.partial(_ssd_step_state_kernel, rows_per_group=rpg)
    snew, ycol = pl.pallas_call(
        st,
        out_shape=(jax.ShapeDtypeStruct((bsz, rows, SSD_D_STATE), F32), jax.ShapeDtypeStruct((bsz, rows, 1), F32)),
        grid=(bsz,),
        in_specs=[
            pl.BlockSpec((None, 1, rows, SSD_D_STATE), lambda b: (layer, b, 0, 0)),
            pl.BlockSpec((1, rows, 1), lambda b: (b, 0, 0)),
            pl.BlockSpec((1, rows, 1), lambda b: (b, 0, 0)),
            pl.BlockSpec((1, SSD_N_GROUPS, SSD_D_STATE), lambda b: (b, 0, 0)),
            pl.BlockSpec((1, SSD_N_GROUPS, SSD_D_STATE), lambda b: (b, 0, 0)),
        ],
        out_specs=(pl.BlockSpec((1, rows, SSD_D_STATE), lambda b: (b, 0, 0)),
                   pl.BlockSpec((1, rows, 1), lambda b: (b, 0, 0))),
        compiler_params=_cparams(("parallel",)),
        name="ssd_step_state",
    )(ssm_states.reshape(ssm_states.shape[0], bsz, rows, SSD_D_STATE), xdt.reshape(bsz, rows, 1),
      dec.reshape(bsz, rows, 1), bm.reshape(bsz, SSD_N_GROUPS, SSD_D_STATE),
      cm.reshape(bsz, SSD_N_GROUPS, SSD_D_STATE))
    post = functools.partial(_ssd_step_post_kernel, d_inner=d_inner, group_width=d_inner // SSD_N_GROUPS)
    y = pl.pallas_call(
        post,
        out_shape=jax.ShapeDtypeStruct((bsz, d_inner), BF16),
        compiler_params=pltpu.CompilerParams(vmem_limit_bytes=VMEM_LIMIT),
        name="ssd_step_post",
    )(ycol.reshape(bsz, rows), xs, u, jnp.repeat(d_skip.astype(F32), SSD_HEAD_DIM).reshape(1, d_inner),
      gnorm.reshape(1, d_inner))
    out = _matmul_res(y, w_out.astype(BF16), x2).reshape(bsz, 1, d)
    return out, ncst.transpose(1, 0, 2), snew.reshape(ssm_states.shape[1:])


NSA_Q_W = NSA_N_HEADS * NSA_HEAD_DIM
NSA_KV_W = NSA_N_KV * NSA_HEAD_DIM
COL_Q = 0
COL_KV = NSA_Q_W
COL_Z = COL_KV + 4 * NSA_KV_W
COL_KVW = COL_Z + N_BRANCH * NSA_Q_W
COL_GATE = COL_KVW + 2 * NSA_KV_W
NSA_N_PAD = COL_GATE + LANES
TQ = 128
KT = 128
N_REL_TILES = -(-(REL_MAX_DIST + TQ - 1) // KT)
CMP_NEAR = 128
CMP_FRONT = CMP_NEAR - TQ // CMP_STRIDE
PAGES_PER_STEP = 32
SEL_STEP_TILES = 4


def _nsa_in_weights(w_in):
    cuts = np.cumsum([NSA_Q_W, 4 * NSA_KV_W, 2 * NSA_KV_W, NSA_N_HEADS * N_BRANCH]).tolist()
    q, kv, kvw, gate, z = (w_in[:, a:b] for a, b in zip([0] + cuts, cuts + [w_in.shape[1]]))
    gate = jnp.pad(gate, ((0, 0), (0, LANES - gate.shape[1])))
    return jnp.concatenate([q, kv, z, kvw, gate], axis=1)


def _rel_bucket(dist):
    n = jnp.maximum(dist, 0)
    exact = REL_BUCKETS // 2
    logv = jnp.log(jnp.maximum(n, 1).astype(F32) / exact) / math.log(REL_MAX_DIST / exact)
    large = jnp.minimum(exact + (logv * (REL_BUCKETS - exact)).astype(jnp.int32), REL_BUCKETS - 1)
    return jnp.where(n < exact, n, large)


def _bias_by_dist(rel_bias, n):
    near = jnp.moveaxis(rel_bias.astype(F32)[_rel_bucket(jnp.arange(min(n, REL_MAX_DIST)))], -1, 0)
    if n <= REL_MAX_DIST:
        return near
    far = jnp.broadcast_to(rel_bias.astype(F32)[REL_BUCKETS - 1][:, None], (rel_bias.shape[1], n - REL_MAX_DIST))
    return jnp.concatenate([near, far], axis=1)


def _toeplitz(vec, first, n_rows, n_cols, row_step):
    lead = vec.shape[:-1]
    lo = first - row_step * (n_rows - 1)
    front = max(-lo, 0)
    if front:
        vec = jnp.concatenate([jnp.broadcast_to(vec[..., :1], lead + (front,)), vec], axis=-1)
    span = row_step * (n_rows - 1) + n_cols
    w = vec[..., lo + front:lo + front + span]
    pitch = span + row_step
    flat = jnp.tile(w, n_rows + 1)[..., :n_rows * pitch]
    s = flat.reshape(lead + (n_rows, pitch))[..., :n_cols]
    return jnp.flip(s, axis=-2)


def _lanes_by_group(t):
    h, n, q = t.shape
    return t.reshape(NSA_N_KV, NSA_R, n, q).transpose(0, 2, 1, 3).reshape(NSA_N_KV, n, NSA_R * q)


def _overlap_matrix(nc, ns):
    c_start = np.arange(nc) * CMP_STRIDE
    c_end = c_start + CMP_BLOCK - 1
    s_start = np.arange(ns) * SEL_BLOCK
    ov = np.clip(np.minimum(c_end[:, None], s_start[None, :] + SEL_BLOCK - 1)
                 - np.maximum(c_start[:, None], s_start[None, :]) + 1, 0, None).astype(np.float32) / CMP_STRIDE
    return ov


def _masked_softmax_parts(parts, valids, axis):
    m = None
    for s, v in zip(parts, valids):
        pm = jnp.max(jnp.where(v, s, -jnp.inf), axis=axis, keepdims=True)
        m = pm if m is None else jnp.maximum(m, pm)
    m = jnp.where(m > -jnp.inf, m, 0.0)
    es = [jnp.where(v, jnp.exp(s - m), 0.0) for s, v in zip(parts, valids)]
    den = None
    for e in es:
        d = jnp.sum(e, axis=axis, keepdims=True)
        den = d if den is None else den + d
    inv = 1.0 / jnp.where(den > 0, den, 1.0)
    return [e * inv for e in es]


def _compress_kernel(*refs, n_pp):
    page_refs = refs[1:1 + n_pp]
    perm_ref, pe_ref, w1_ref, w2_ref, kc_ref, vc_ref, x_ref = refs[1 + n_pp:]
    step = pl.program_id(1)
    hb = PAGE_SIZE // CMP_STRIDE
    dh = NSA_HEAD_DIM
    for k in range(n_pp):
        pt = lax.dot_general(perm_ref[...], page_refs[k][...].astype(BF16), NT_DIMS, preferred_element_type=F32)
        row0 = pl.multiple_of((step * n_pp + k) * hb, hb)
        for l in range(CMP_STRIDE):
            row = pt[l * hb:(l + 1) * hb, :]
            for kv in range(2):
                for g in range(NSA_N_KV):
                    c0 = kv * NSA_KV_W + g * dh
                    x_ref[kv, g, pl.ds(row0, hb), l * dh:(l + 1) * dh] = row[:, c0:c0 + dh]

    @pl.when(step == pl.num_programs(1) - 1)
    def _():
        hid_w = CMP_HIDDEN
        for kv, out_ref in ((0, kc_ref), (1, vc_ref)):
            w1 = w1_ref[kv]
            w2 = w2_ref[kv]
            pe_w = jnp.dot(pe_ref[kv].astype(BF16), w1, preferred_element_type=F32)
            pe_term = pe_w[0:1, :hid_w] + pe_w[1:2, hid_w:]
            for g in range(NSA_N_KV):
                both = jnp.dot(x_ref[kv, g].astype(BF16), w1, preferred_element_type=F32)
                bot = both[:, hid_w:]
                bot = jnp.concatenate([bot[1:], jnp.zeros((1, hid_w), F32)], axis=0)
                hid = _silu(both[:, :hid_w] + bot + pe_term).astype(BF16)
                out_ref[0, :, g * dh:(g + 1) * dh] = jnp.dot(hid, w2, preferred_element_type=F32)


def _compress(src, page_index, bsz, n_pages, ids, pe, w1, w2):
    n_pp = math.gcd(PAGES_PER_STEP, n_pages)
    hb = PAGE_SIZE // CMP_STRIDE
    nh = n_pages * hb
    half = CMP_STRIDE * NSA_HEAD_DIM
    blk = (None,) * (src.ndim - 2) + (2 * NSA_KV_W, PAGE_SIZE)
    page_specs = [pl.BlockSpec(blk, functools.partial(lambda b, s, ids, k: page_index(b, s * n_pp + k, ids), k=k))
                  for k in range(n_pp)]
    perm = np.zeros((PAGE_SIZE, PAGE_SIZE), np.float32)
    for l in range(CMP_STRIDE):
        for i in range(hb):
            perm[l * hb + i, CMP_STRIDE * i + l] = 1.0
    w1cat = jnp.concatenate([w1[:, :half], w1[:, half:]], axis=-1).astype(BF16)
    grid_spec = pltpu.PrefetchScalarGridSpec(
        num_scalar_prefetch=1,
        grid=(bsz, n_pages // n_pp),
        in_specs=page_specs + [
            pl.BlockSpec((PAGE_SIZE, PAGE_SIZE), lambda b, s, ids: (0, 0)),
            pl.BlockSpec((2, 2, half), lambda b, s, ids: (0, 0, 0)),
            pl.BlockSpec((2, half, 2 * CMP_HIDDEN), lambda b, s, ids: (0, 0, 0)),
            pl.BlockSpec((2, CMP_HIDDEN, NSA_HEAD_DIM), lambda b, s, ids: (0, 0, 0)),
        ],
        out_specs=(pl.BlockSpec((1, nh, NSA_KV_W), lambda b, s, ids: (b, 0, 0)),
                   pl.BlockSpec((1, nh, NSA_KV_W), lambda b, s, ids: (b, 0, 0))),
        scratch_shapes=[pltpu.VMEM((2, NSA_N_KV, nh, half), F32)],
    )
    return pl.pallas_call(
        functools.partial(_compress_kernel, n_pp=n_pp),
        out_shape=(jax.ShapeDtypeStruct((bsz, nh, NSA_KV_W), F32), jax.ShapeDtypeStruct((bsz, nh, NSA_KV_W), F32)),
        grid_spec=grid_spec,
        compiler_params=_cparams(("parallel", "arbitrary")),
        name="nsa_compress",
    )(ids, *([src] * n_pp), jnp.asarray(perm, BF16), pe.reshape(2, 2, half).astype(F32), w1cat, w2.astype(BF16))


def _nsa_attn_kernel(qt_ref, kv_ref, kc_ref, vc_ref, ov_ref, trel_ref, tcn_ref, cfar_ref, gate_ref, z0_ref, z1_ref,
                     z2_ref, o_ref, qaug_ref, imp_ref, m_ref, l_ref, acc_ref, ob_ref, s_ref, *, n_sel, n_cmp_rows):
    t = pl.program_id(1)
    dh = NSA_HEAD_DIM
    rq = NSA_R * TQ
    scale = dh ** -0.5
    far_idx = N_REL_TILES
    ns8 = _round_up(n_sel, 2 * SUBLANES)
    assert ns8 < LANES
    near0 = pl.multiple_of(t * (TQ // CMP_STRIDE), SUBLANES)
    far_rows = slice(CMP_FRONT, CMP_FRONT + n_cmp_rows)

    def cmp_branch(g, qt, with_far):
        gs = slice(g * dh, (g + 1) * dh)
        kc_near = kc_ref[0, pl.ds(near0, CMP_NEAR), gs].astype(BF16)
        vc_near = vc_ref[0, pl.ds(near0, CMP_NEAR), gs].astype(BF16)
        cn = lax.broadcasted_iota(jnp.int32, (CMP_NEAR, rq), 0)
        qn = lax.broadcasted_iota(jnp.int32, (CMP_NEAR, rq), 1) % TQ
        dist_n = qn - CMP_STRIDE * cn + (CMP_STRIDE * CMP_FRONT - (CMP_BLOCK - 1))
        parts = [jnp.dot(kc_near, qt, preferred_element_type=F32) + tcn_ref[g]]
        valids = [(dist_n >= 0) & (cn + near0 >= CMP_FRONT)]
        if with_far:
            kc_far = kc_ref[0, far_rows, gs].astype(BF16)
            parts.append(jnp.dot(kc_far, qt, preferred_element_type=F32) + cfar_ref[g, 0:1, :])
            valids.append(lax.broadcasted_iota(jnp.int32, (n_cmp_rows, rq), 0) + CMP_FRONT < near0)
        ps = _masked_softmax_parts(parts, valids, axis=0)

        def importance(ov, p):
            pg = sum(p[:, r * TQ:(r + 1) * TQ] for r in range(NSA_R))
            hi = pg.astype(BF16)
            lo = (pg - hi.astype(F32)).astype(BF16)
            ov16 = ov.astype(BF16)
            return (lax.dot_general(ov16, hi, TN_DIMS, preferred_element_type=F32)
                    + lax.dot_general(ov16, lo, TN_DIMS, preferred_element_type=F32))

        o_c = lax.dot_general(vc_near, ps[0].astype(BF16), TN_DIMS, preferred_element_type=F32)
        imp = importance(ov_ref[pl.ds(near0, CMP_NEAR), :], ps[0])
        if with_far:
            vc_far = vc_ref[0, far_rows, gs].astype(BF16)
            o_c = o_c + lax.dot_general(vc_far, ps[1].astype(BF16), TN_DIMS, preferred_element_type=F32)
            imp = imp + importance(ov_ref[far_rows, :], ps[1])
        ob_ref[0, g] = o_c
        imp_ref[...] = imp

    qts = []
    sid = lax.broadcasted_iota(jnp.int32, (ns8, TQ), 0)
    qpos = t * TQ + lax.broadcasted_iota(jnp.int32, (ns8, TQ), 1)
    cur = qpos // SEL_BLOCK
    forced = (sid == 0) | (sid == cur) | (sid == cur - 1)
    reachable = sid * SEL_BLOCK <= qpos
    sub = lax.broadcasted_iota(jnp.int32, (SUBLANES, TQ), 0)
    tail_row = lax.broadcasted_iota(jnp.int32, (LANES - ns8, rq), 0)
    for g in range(NSA_N_KV):
        q32 = jnp.concatenate([qt_ref[(g * NSA_R + r) * dh:(g * NSA_R + r + 1) * dh, :] for r in range(NSA_R)], axis=1)
        qt = (q32 * scale).astype(BF16)
        qts.append((q32 * (scale * LOG2E)).astype(BF16))
        has_far = near0 > CMP_FRONT

        @pl.when(has_far)
        def _():
            cmp_branch(g, qt, True)

        @pl.when(jnp.logical_not(has_far))
        def _():
            cmp_branch(g, qt, False)

        imp = jnp.where(forced, FORCE_SCORE, imp_ref[0:ns8, :])
        imp = jnp.where(reachable, imp, -FORCE_SCORE)
        blocks = [imp[SUBLANES * kb:SUBLANES * (kb + 1)] for kb in range(ns8 // SUBLANES)]
        cnts = [jnp.zeros((SUBLANES, TQ), F32) for _ in blocks]
        for sp in range(n_sel):
            other = imp[sp:sp + 1, :]
            for kb, blk in enumerate(blocks):
                if kb > sp // SUBLANES:
                    ahead = other >= blk
                elif kb < sp // SUBLANES:
                    ahead = other > blk
                else:
                    ahead = (other > blk) | ((other == blk) & (sub > sp % SUBLANES))
                cnts[kb] = cnts[kb] + jnp.where(ahead, 1.0, 0.0)
        cnt = jnp.concatenate(cnts, axis=0)
        unsel = jnp.where((cnt < SEL_TOPK) & (imp > -0.5 * FORCE_SCORE), 0.0, NEG)
        zero = jnp.zeros((dh, rq), BF16)
        q_rows = [qts[g], zero] if g % 2 == 0 else [zero, qts[g]]
        tail = jnp.where(tail_row == 0, trel_ref[g, far_idx, 0:1, :], 0.0)
        qaug_ref[g] = jnp.concatenate(q_rows + [jnp.concatenate([unsel] * NSA_R, axis=1).astype(BF16),
                                                tail.astype(BF16)], axis=0)

    def reset():
        m_ref[...] = jnp.full(m_ref.shape, NEG, F32)
        l_ref[...] = jnp.zeros(l_ref.shape, F32)
        acc_ref[...] = jnp.zeros(acc_ref.shape, F32)

    def online_step(g, s, v_t):
        m_old = m_ref[g]
        m_new = jnp.maximum(m_old, jnp.max(s, axis=0, keepdims=True))
        alpha = jnp.exp2(m_old - m_new)
        p = jnp.exp2(s - m_new)
        l_ref[g] = l_ref[g] * alpha + jnp.sum(p, axis=0, keepdims=True)
        acc_ref[g] = acc_ref[g] * alpha + lax.dot_general(v_t, p.astype(BF16), TN_DIMS, preferred_element_type=F32)
        m_ref[g] = m_new

    def mask_heads(s, ok):
        return jnp.concatenate([jnp.where(ok, s[:, r * TQ:(r + 1) * TQ], NEG) for r in range(NSA_R)], axis=1)

    reset()
    kt2 = SEL_STEP_TILES * KT

    def sel_step(kt, mode, n_tiles=SEL_STEP_TILES):
        nk = n_tiles * KT
        k0 = pl.multiple_of(kt * kt2, kt2)
        e_blk = lax.broadcasted_iota(jnp.int32, (nk, LANES), 0) // SEL_BLOCK
        e_lane = lax.broadcasted_iota(jnp.int32, (nk, LANES), 1)
        onehot = e_lane == kt * (kt2 // SEL_BLOCK) + e_blk
        if mode == "far":
            onehot = onehot | (e_lane == ns8)
        e_aug = jnp.where(onehot, 1.0, 0.0).astype(BF16)
        delta = t - SEL_STEP_TILES * kt
        idx = [jnp.clip(delta - i, 0, far_idx) for i in range(n_tiles)]
        for pair in range(NSA_N_KV // 2):
            k_aug = jnp.concatenate([kv_ref[pl.ds(WINDOW + k0, nk), pair * LANES:(pair + 1) * LANES], e_aug], axis=1)
            for g in (2 * pair, 2 * pair + 1):
                s = jnp.dot(k_aug, qaug_ref[g], preferred_element_type=F32)
                if mode != "far":
                    s = s + jnp.concatenate([trel_ref[g, i] for i in idx], axis=0)
                if mode == "last":
                    kj = lax.broadcasted_iota(jnp.int32, (nk, TQ), 0)
                    qi = lax.broadcasted_iota(jnp.int32, (nk, TQ), 1)
                    s = mask_heads(s, t * TQ + qi >= k0 + kj)
                s_ref[g, 0:nk, :] = s
        for g in range(NSA_N_KV):
            online_step(g, s_ref[g, 0:nk, :],
                        kv_ref[pl.ds(WINDOW + k0, nk), NSA_KV_W + g * dh:NSA_KV_W + (g + 1) * dh])

    n_steps = t // SEL_STEP_TILES + 1
    n_far_steps = jnp.maximum(t - far_idx + 1, 0) // SEL_STEP_TILES

    def far_body(kt, carry):
        sel_step(kt, "far")
        return carry

    def near_body(kt, carry):
        sel_step(kt, "near")
        return carry

    lax.fori_loop(0, n_far_steps, far_body, 0)
    lax.fori_loop(n_far_steps, n_steps - 1, near_body, 0)
    short_last = t % SEL_STEP_TILES < SEL_STEP_TILES // 2

    @pl.when(short_last)
    def _():
        sel_step(n_steps - 1, "last", SEL_STEP_TILES // 2)

    @pl.when(jnp.logical_not(short_last))
    def _():
        sel_step(n_steps - 1, "last")
    for g in range(NSA_N_KV):
        ob_ref[1, g] = acc_ref[g] / l_ref[g]

    assert SEL_STEP_TILES >= 2

    reset()
    row0 = pl.multiple_of(t * TQ, TQ)
    for deltas in ((0,), (2, 1), (4, 3)):
        n_keys = KT * len(deltas)
        kjw = lax.broadcasted_iota(jnp.int32, (n_keys, TQ), 0)
        qiw = lax.broadcasted_iota(jnp.int32, (n_keys, TQ), 1)
        dist = deltas[0] * KT + qiw - kjw
        ok = (dist >= 0) & (dist <= WINDOW) & (row0 + kjw >= deltas[0] * KT)
        k0 = row0 + (WINDOW - deltas[0] * KT)
        for g in range(NSA_N_KV):
            k_t = kv_ref[pl.ds(k0, n_keys), 2 * NSA_KV_W + g * dh:2 * NSA_KV_W + (g + 1) * dh]
            s = jnp.dot(k_t, qts[g], preferred_element_type=F32)
            s = s + jnp.concatenate([trel_ref[g, d] for d in deltas], axis=0)
            s_ref[g, 0:n_keys, :] = mask_heads(s, ok)
        for g in range(NSA_N_KV):
            online_step(g, s_ref[g, 0:n_keys, :],
                        kv_ref[pl.ds(k0, n_keys), 3 * NSA_KV_W + g * dh:3 * NSA_KV_W + (g + 1) * dh])
    for g in range(NSA_N_KV):
        ob_ref[2, g] = acc_ref[g] / l_ref[g]

    gate = jax.nn.sigmoid(gate_ref[...])
    z_refs = (z0_ref, z1_ref, z2_ref)
    for h in range(NSA_N_HEADS):
        g, r = divmod(h, NSA_R)
        mixed = None
        for c in range(N_BRANCH):
            o_hc = ob_ref[c, g, :, r * TQ:(r + 1) * TQ]
            term = o_hc * _silu(z_refs[c][h * dh:(h + 1) * dh, :]) * gate[h * N_BRANCH + c:h * N_BRANCH + c + 1, :]
            mixed = term if mixed is None else mixed + term
        o_ref[h * dh:(h + 1) * dh, :] = mixed.astype(o_ref.dtype)


def _nsa_prompt_tables(rel_bias):
    n_dist = max((N_REL_TILES + 1) * KT, CMP_STRIDE * CMP_FRONT + TQ)
    vec = _bias_by_dist(rel_bias, n_dist)
    near = _toeplitz(vec, 0, KT, N_REL_TILES * TQ, 1).reshape(NSA_N_HEADS, KT, N_REL_TILES, TQ)
    near = _lanes_by_group(near.transpose(0, 2, 1, 3).reshape(NSA_N_HEADS, N_REL_TILES * KT, TQ))
    near = near.reshape(NSA_N_KV, N_REL_TILES, KT, NSA_R * TQ)
    far = jnp.broadcast_to(rel_bias.astype(F32)[REL_BUCKETS - 1].reshape(NSA_N_HEADS, 1, 1), (NSA_N_HEADS, KT, TQ))
    far = _lanes_by_group(far)
    trel = jnp.concatenate([near, far[:, None]], axis=1) * LOG2E
    tcn = _lanes_by_group(_toeplitz(vec, CMP_STRIDE * CMP_FRONT - (CMP_BLOCK - 1), CMP_NEAR, TQ, CMP_STRIDE))
    return trel, tcn, far[:, :SUBLANES, :]


def _nsa_layer_prompt(xp, prm, layer, n_layers, kv_bufs):
    norm_w, w_in, pe, w1, w2, w_out, rel_bias = prm
    bsz, t, d = xp.shape
    assert t % (SEL_STEP_TILES * KT) == 0 and t % PAGE_SIZE == 0 and t >= WINDOW
    n_tiles = t // TQ
    ns = t // SEL_BLOCK
    assert ns <= LANES
    x2 = xp.reshape(bsz * t, d)
    w = _nsa_in_weights(w_in)
    w_qzg = jnp.concatenate([w[:, COL_Q:COL_Q + NSA_Q_W], w[:, COL_Z:COL_Z + N_BRANCH * NSA_Q_W], w[:, COL_GATE:]], axis=1)
    ut = _norm_matmul_t(x2, norm_w, w_qzg.T.astype(BF16), bsz, w_qzg.shape[1])
    w_kvt = jnp.concatenate([w[:, COL_KV:COL_KV + 4 * NSA_KV_W], w[:, COL_KVW:COL_KVW + 2 * NSA_KV_W]], axis=1)
    kv_bufs = _kv_proj_t(x2, norm_w, w_kvt.T.astype(BF16), bsz, layer, n_layers, 4 * NSA_KV_W, kv_bufs)
    w_kv = jnp.concatenate([w[:, COL_KV + 2 * NSA_KV_W:COL_KV + 4 * NSA_KV_W], w[:, COL_KVW:COL_KVW + 2 * NSA_KV_W]],
                           axis=1).astype(BF16)
    kvn = _norm_matmul(x2, norm_w, w_kv, 4 * NSA_KV_W, out_dtype=BF16)
    t_pad = WINDOW + t
    kvn = jnp.pad(kvn.reshape(bsz, t, 4 * NSA_KV_W), ((0, 0), (WINDOW, 0), (0, 0))).reshape(bsz * t_pad, 4 * NSA_KV_W)
    n_pages = t // PAGE_SIZE
    kc, vc = _compress(kv_bufs[0], lambda b, p, ids: (b, layer, 0, p), bsz, n_pages, jnp.zeros((1, 1), jnp.int32),
                       pe, w1, w2)
    nh = kc.shape[1]
    n_far = _round_up(CMP_FRONT + nh, LANES)
    padc = ((0, 0), (CMP_FRONT, n_far - CMP_FRONT - nh), (0, 0))
    kcp = jnp.pad(kc, padc)
    vcp = jnp.pad(vc, padc)
    ov = np.zeros((n_far, LANES), np.float32)
    ov[CMP_FRONT:CMP_FRONT + nh - 1, :ns] = _overlap_matrix(nh - 1, ns)
    trel, tcn, cfar = _nsa_prompt_tables(rel_bias)
    rq = NSA_R * TQ
    const2 = lambda b, i: (0, 0)
    const3 = lambda b, i: (0, 0, 0)
    const4 = lambda b, i: (0, 0, 0, 0)
    zb = 1
    gate_blk = (1 + N_BRANCH) * NSA_Q_W // LANES
    once = dict(pipeline_mode=pl.Buffered(1))
    mixed_t = pl.pallas_call(
        functools.partial(_nsa_attn_kernel, n_sel=ns, n_cmp_rows=max(nh - CMP_NEAR, SUBLANES)),
        out_shape=jax.ShapeDtypeStruct((bsz, NSA_Q_W, t), BF16),
        grid=(bsz, n_tiles),
        in_specs=[
            pl.BlockSpec((None, NSA_Q_W, TQ), lambda b, i: (b, 0, i)),
            pl.BlockSpec((t_pad, 4 * NSA_KV_W), lambda b, i: (b, 0), **once),
            pl.BlockSpec((1, n_far, NSA_KV_W), lambda b, i: (b, 0, 0), **once),
            pl.BlockSpec((1, n_far, NSA_KV_W), lambda b, i: (b, 0, 0), **once),
            pl.BlockSpec((n_far, LANES), const2, **once),
            pl.BlockSpec((NSA_N_KV, N_REL_TILES + 1, KT, rq), const4, **once),
            pl.BlockSpec((NSA_N_KV, CMP_NEAR, rq), const3, **once),
            pl.BlockSpec((NSA_N_KV, SUBLANES, rq), const3, **once),
            pl.BlockSpec((None, LANES, TQ), lambda b, i: (b, gate_blk, i)),
            pl.BlockSpec((None, NSA_Q_W, TQ), lambda b, i: (b, zb, i)),
            pl.BlockSpec((None, NSA_Q_W, TQ), lambda b, i: (b, zb + 1, i)),
            pl.BlockSpec((None, NSA_Q_W, TQ), lambda b, i: (b, zb + 2, i)),
        ],
        out_specs=pl.BlockSpec((None, NSA_Q_W, TQ), lambda b, i: (b, 0, i)),
        scratch_shapes=[
            pltpu.VMEM((NSA_N_KV, 2 * LANES, rq), BF16),
            pltpu.VMEM((LANES, TQ), F32),
            pltpu.VMEM((NSA_N_KV, 1, rq), F32),
            pltpu.VMEM((NSA_N_KV, 1, rq), F32),
            pltpu.VMEM((NSA_N_KV, NSA_HEAD_DIM, rq), F32),
            pltpu.VMEM((N_BRANCH, NSA_N_KV, NSA_HEAD_DIM, rq), F32),
            pltpu.VMEM((NSA_N_KV, SEL_STEP_TILES * KT, rq), F32),
        ],
        compiler_params=_cparams(("parallel", "arbitrary")),
        name="nsa_attn",
    )(ut, kvn, kcp, vcp, jnp.asarray(ov), trel, tcn, cfar, ut, ut, ut, ut)
    out = _matmul_res_t(mixed_t, w_out.astype(BF16), x2).reshape(bsz, t, d)
    return out, kv_bufs


def _block_diag_q(q2):
    hg = lax.broadcasted_iota(jnp.int32, q2.shape, 0) // NSA_R
    return jnp.concatenate([jnp.where(hg == g, q2, 0.0) for g in range(NSA_N_KV)], axis=1)


def _group_lanes(o):
    dh = NSA_HEAD_DIM
    hg = lax.broadcasted_iota(jnp.int32, (o.shape[0], dh), 0) // NSA_R
    return sum(jnp.where(hg == g, o[:, g * dh:(g + 1) * dh], 0.0) for g in range(NSA_N_KV))


def _nsa_step_kernel(*refs, n_pp, past_len, n_cmp, n_sel_pad):
    page_refs = refs[1:1 + n_pp]
    (q_ref, kvn_ref, kwn_ref, gate_ref, z_ref, kc_ref, vc_ref, win_ref, ovp_ref, bsel_ref, bcmp_ref, bwin_ref, b0_ref,
     o_ref, sel_ref, m_ref, l_ref, acc_ref, oc_ref) = refs[1 + n_pp:]
    step = pl.program_id(1)
    nh = NSA_N_HEADS
    dh = NSA_HEAD_DIM
    qbd = _block_diag_q(q_ref[0] * (dh ** -0.5))
    qbd16 = qbd.astype(BF16)
    cur = past_len // SEL_BLOCK

    @pl.when(step == 0)
    def _():
        s_c = lax.dot_general(qbd16, kc_ref[0].astype(BF16), NT_DIMS, preferred_element_type=F32) + bcmp_ref[...]
        cid = lax.broadcasted_iota(jnp.int32, s_c.shape, 1)
        (p_c,) = _masked_softmax_parts([s_c], [cid < n_cmp], axis=1)
        oc_ref[...] = _group_lanes(jnp.dot(p_c.astype(BF16), vc_ref[0].astype(BF16), preferred_element_type=F32))
        gr = lax.broadcasted_iota(jnp.int32, (SUBLANES, nh), 0)
        gh = lax.broadcasted_iota(jnp.int32, (SUBLANES, nh), 1) // NSA_R
        pg = jnp.dot((gr == gh).astype(F32), p_c, precision=HIGHEST, preferred_element_type=F32)
        imp = jnp.dot(pg, ovp_ref[...], precision=HIGHEST, preferred_element_type=F32)
        sid = lax.broadcasted_iota(jnp.int32, imp.shape, 1)
        forced = (sid == 0) | (sid == cur) | (sid == cur - 1)
        imp = jnp.where(forced, FORCE_SCORE, imp)
        imp = jnp.where(sid * SEL_BLOCK <= past_len, imp, -FORCE_SCORE)
        er = lax.broadcasted_iota(jnp.int32, (n_sel_pad, n_sel_pad), 0)
        ec = lax.broadcasted_iota(jnp.int32, (n_sel_pad, n_sel_pad), 1)
        rows = []
        for g in range(NSA_N_KV):
            row = imp[g:g + 1, :]
            col = jnp.sum(jnp.where(er == ec, row, 0.0), axis=1, keepdims=True)
            ahead = (col > row) | ((col == row) & (er < ec))
            cnt = jnp.sum(ahead.astype(jnp.int32), axis=0, keepdims=True)
            sel = ((cnt < SEL_TOPK) & (row > -0.5 * FORCE_SCORE)).astype(F32)
            rows.extend([sel] * NSA_R)
        sel_ref[...] = jnp.concatenate(rows, axis=0)
        m_ref[...] = jnp.full(m_ref.shape, NEG, F32)
        l_ref[...] = jnp.zeros(l_ref.shape, F32)
        acc_ref[...] = jnp.zeros(acc_ref.shape, F32)

    def online(s):
        m_old = m_ref[...]
        m_new = jnp.maximum(m_old, jnp.max(s, axis=-1, keepdims=True))
        alpha = jnp.exp(m_old - m_new)
        pr = jnp.exp(s - m_new)
        l_ref[...] = l_ref[...] * alpha + jnp.sum(pr, axis=-1, keepdims=True)
        m_ref[...] = m_new
        return alpha, pr

    k_all = jnp.concatenate([r[0:NSA_KV_W, :] for r in page_refs], axis=1).astype(BF16)
    v_all = jnp.concatenate([r[NSA_KV_W:2 * NSA_KV_W, :] for r in page_refs], axis=1).astype(BF16)
    n_tok = n_pp * PAGE_SIZE
    s = jnp.dot(qbd16, k_all, preferred_element_type=F32) + bsel_ref[0]
    eb = lax.broadcasted_iota(jnp.int32, (n_sel_pad, n_tok), 0)
    ej = lax.broadcasted_iota(jnp.int32, (n_sel_pad, n_tok), 1)
    expand = (eb == step * (n_tok // SEL_BLOCK) + ej // SEL_BLOCK).astype(BF16)
    chosen = jnp.dot(sel_ref[...].astype(BF16), expand, preferred_element_type=F32)
    s = jnp.where(chosen > 0.5, s, NEG)
    alpha, pr = online(s)
    acc_ref[...] = acc_ref[...] * alpha + lax.dot_general(pr.astype(BF16), v_all, NT_DIMS, preferred_element_type=F32)

    @pl.when(step == pl.num_programs(1) - 1)
    def _():
        b0 = b0_ref[:, 0:1]
        kn = kvn_ref[0, :, 2 * NSA_KV_W:3 * NSA_KV_W]
        vn = kvn_ref[0, :, 3 * NSA_KV_W:4 * NSA_KV_W]
        s_n = jnp.sum(qbd * kn, axis=-1, keepdims=True) + b0
        s_n = jnp.where(sel_ref[:, cur:cur + 1] > 0.5, s_n, NEG)
        alpha, pr = online(s_n)
        acc = acc_ref[...] * alpha + pr * vn
        o_s = _group_lanes(acc / l_ref[...])
        wk = win_ref[0, 0:NSA_KV_W, :].astype(BF16)
        wv = win_ref[0, NSA_KV_W:2 * NSA_KV_W, :].astype(BF16)
        n_win = win_ref.shape[2]
        s_w = jnp.dot(qbd16, wk, preferred_element_type=F32) + bwin_ref[...]
        wi = lax.broadcasted_iota(jnp.int32, s_w.shape, 1)
        s_wn = jnp.sum(qbd * kwn_ref[0, :, 0:NSA_KV_W], axis=-1, keepdims=True) + b0
        p_w, p_wn = _masked_softmax_parts([s_w, s_wn], [n_win - wi <= WINDOW, jnp.full(s_wn.shape, True)], axis=1)
        o_w = _group_lanes(lax.dot_general(p_w.astype(BF16), wv, NT_DIMS, preferred_element_type=F32)
                           + p_wn * kwn_ref[0, :, NSA_KV_W:2 * NSA_KV_W])
        gate = jax.nn.sigmoid(gate_ref[0])
        mixed = (oc_ref[...] * _silu(z_ref[0, 0]) * gate[:, 0:1] + o_s * _silu(z_ref[0, 1]) * gate[:, 1:2]
                 + o_w * _silu(z_ref[0, 2]) * gate[:, 2:3])
        o_ref[0] = mixed


def _nsa_layer_sample(xs_in, cache_t, win_t, page_table, prm, layer, n_layers):
    norm_w, w_in, pe, w1, w2, w_out, rel_bias = prm
    bsz, t, d = xs_in.shape
    assert t == 1
    n_pages = page_table.shape[1]
    past_len = n_pages * PAGE_SIZE
    n_win = win_t.shape[3]
    nh, dh = NSA_N_HEADS, NSA_HEAD_DIM
    x2 = xs_in.reshape(bsz, d)
    u = _norm_matmul(x2, norm_w, _nsa_in_weights(w_in).astype(BF16), _pick_tn(NSA_N_PAD, 1152))
    ids = (page_table * n_layers + layer).astype(jnp.int32)
    kc, vc = _compress(cache_t, lambda b, p, ids: (ids[b, p], 0, 0), bsz, n_pages, ids, pe, w1, w2)
    n_cmp_pad = kc.shape[1]
    n_cmp = n_cmp_pad - 1
    ns = past_len // SEL_BLOCK + 1
    n_sel_pad = _round_up(ns, LANES)
    ov = np.zeros((n_cmp_pad, n_sel_pad), np.float32)
    ov[:n_cmp, :ns] = _overlap_matrix(n_cmp, ns)
    n_pp = math.gcd(PAGES_PER_STEP, n_pages)
    n_steps = n_pages // n_pp
    n_tok = n_pp * PAGE_SIZE
    vec = _bias_by_dist(rel_bias, past_len + 1)
    bsel = jnp.moveaxis(jnp.flip(vec[:, 1:], axis=1).reshape(nh, n_steps, n_tok), 0, 1)
    bcmp = jnp.flip(vec[:, :past_len - (CMP_BLOCK - 1) + 1], axis=1)[:, ::CMP_STRIDE]
    bcmp = jnp.pad(bcmp, ((0, 0), (0, n_cmp_pad - bcmp.shape[1])))
    bwin = jnp.flip(vec[:, 1:n_win + 1], axis=1)
    b0 = jnp.broadcast_to(vec[:, 0:1], (nh, LANES))
    q3 = u[:, COL_Q:COL_Q + NSA_Q_W].reshape(bsz, nh, dh)
    kv_new = u[:, COL_KV:COL_KV + 4 * NSA_KV_W]
    kw_new = u[:, COL_KVW:COL_KVW + 2 * NSA_KV_W]
    gate = u[:, COL_GATE:COL_GATE + nh * N_BRANCH].reshape(bsz, nh, N_BRANCH)
    z4 = u[:, COL_Z:COL_Z + N_BRANCH * NSA_Q_W].reshape(bsz, N_BRANCH, nh, dh)
    per_b3 = lambda b, s, ids: (b, 0, 0)
    const2 = lambda b, s, ids: (0, 0)
    page_specs = [pl.BlockSpec((None, 2 * NSA_KV_W, PAGE_SIZE),
                               functools.partial(lambda b, s, ids, k: (ids[b, s * n_pp + k], 1, 0), k=k))
                  for k in range(n_pp)]
    grid_spec = pltpu.PrefetchScalarGridSpec(
        num_scalar_prefetch=1,
        grid=(bsz, n_steps),
        in_specs=page_specs + [
            pl.BlockSpec((1, nh, dh), per_b3),
            pl.BlockSpec((1, 1, 4 * NSA_KV_W), per_b3),
            pl.BlockSpec((1, 1, 2 * NSA_KV_W), per_b3),
            pl.BlockSpec((1, nh, N_BRANCH), per_b3),
            pl.BlockSpec((1, N_BRANCH, nh, dh), lambda b, s, ids: (b, 0, 0, 0)),
            pl.BlockSpec((1, n_cmp_pad, NSA_KV_W), per_b3),
            pl.BlockSpec((1, n_cmp_pad, NSA_KV_W), per_b3),
            pl.BlockSpec((None, 1, 2 * NSA_KV_W, n_win), lambda b, s, ids: (layer, b, 0, 0)),
            pl.BlockSpec((n_cmp_pad, n_sel_pad), const2),
            pl.BlockSpec((1, nh, n_tok), lambda b, s, ids: (s, 0, 0)),
            pl.BlockSpec((nh, n_cmp_pad), const2),
            pl.BlockSpec((nh, n_win), const2),
            pl.BlockSpec((nh, LANES), const2),
        ],
        out_specs=pl.BlockSpec((1, nh, dh), per_b3),
        scratch_shapes=[
            pltpu.VMEM((nh, n_sel_pad), F32),
            pltpu.VMEM((nh, 1), F32),
            pltpu.VMEM((nh, 1), F32),
            pltpu.VMEM((nh, NSA_KV_W), F32),
            pltpu.VMEM((nh, dh), F32),
        ],
    )
    mixed = pl.pallas_call(
        functools.partial(_nsa_step_kernel, n_pp=n_pp, past_len=past_len, n_cmp=n_cmp, n_sel_pad=n_sel_pad),
        out_shape=jax.ShapeDtypeStruct((bsz, nh, dh), F32),
        grid_spec=grid_spec,
        compiler_params=_cparams(("parallel", "arbitrary")),
        name="nsa_step",
    )(ids, *([cache_t] * n_pp), q3, kv_new.reshape(bsz, 1, -1), kw_new.reshape(bsz, 1, -1), gate, z4, kc, vc, win_t,
      jnp.asarray(ov), bsel, bcmp, bwin, b0)
    out = _matmul_res(mixed.reshape(bsz, NSA_Q_W).astype(BF16), w_out.astype(BF16), x2).reshape(bsz, 1, d)
    return out, kv_new, kw_new


def kernel(x_prompt, x_sample, state_ssm, state_conv, cache_kv, cache_win, page_table, rel_bias, final_norm,
           ssd_norm, ssd_w_in, ssd_conv_w, ssd_conv_b, ssd_dt_bias, ssd_a_log, ssd_d, ssd_gnorm, ssd_w_out,
           nsa_norm, nsa_w_in, nsa_cmp_pe, nsa_cmp_w1, nsa_cmp_w2, nsa_w_out):
    depth = ssd_norm.shape[0] + nsa_norm.shape[0]
    g, dh = NSA_N_KV, NSA_HEAD_DIM
    xp, xs = x_prompt, x_sample
    bp, tp, d = xp.shape
    bs = xs.shape[0]
    n_phys, n_nsa = cache_kv.shape[:2]
    win_len_s = cache_win.shape[2]
    win_len_p = min(WINDOW, tp)
    cache_t = cache_kv.transpose(0, 1, 3, 4, 5, 2).reshape(n_phys * n_nsa, 4 * NSA_KV_W, PAGE_SIZE)
    cwin_t = cache_win.transpose(0, 1, 3, 4, 5, 2).reshape(n_nsa, bs, 2 * NSA_KV_W, win_len_s)
    ssm_p, conv_p, kv_bufs = [], [], None
    ssm_s, conv_s, kv_s, win_s = [], [], [], []
    for layer in range(depth):
        j = layer // 2
        if layer % 2 == 0:
            prm = (ssd_norm[j], ssd_w_in[j], ssd_conv_w[j], ssd_conv_b[j], ssd_dt_bias[j], ssd_a_log[j],
                   ssd_d[j], ssd_gnorm[j], ssd_w_out[j])
            xp, c, s = _ssd_layer_prompt(xp, prm)
            conv_p.append(c)
            ssm_p.append(s)
            xs, c, s = _ssd_layer_sample(xs, state_conv[j], state_ssm, j, prm)
            conv_s.append(c)
            ssm_s.append(s)
        else:
            prm = (nsa_norm[j], nsa_w_in[j], nsa_cmp_pe[j], nsa_cmp_w1[j], nsa_cmp_w2[j], nsa_w_out[j], rel_bias)
            xp, kv_bufs = _nsa_layer_prompt(xp, prm, j, n_nsa, kv_bufs)
            xs, kv_new, kw_new = _nsa_layer_sample(xs, cache_t, cwin_t, page_table, prm, j, n_nsa)
            kv_s.append(kv_new.reshape(bs, 1, 4, g, dh))
            win_s.append(jnp.concatenate([cwin_t[j][:, :, 1:], kw_new[:, :, None]], axis=2))
    y_prompt = _rmsnorm(xp.reshape(bp * tp, d), final_norm).reshape(bp, tp, d)
    y_sample = _rmsnorm(xs.reshape(bs, d), final_norm).reshape(bs, 1, d)
    kv_prompt = kv_bufs[0].reshape(bp, n_nsa, 4, g, dh, tp).transpose(0, 5, 1, 2, 3, 4)
    win_prompt = kv_bufs[1][:, :, :, tp - win_len_p:].reshape(bp, n_nsa, 2, g, dh, win_len_p).transpose(1, 0, 5, 2, 3, 4)
    win_sample = jnp.stack(win_s).reshape(n_nsa, bs, 2, g, dh, win_len_s).transpose(0, 1, 5, 2, 3, 4)
    return (y_prompt, y_sample, jnp.stack(ssm_p), jnp.stack(conv_p), kv_prompt, win_prompt,
            jnp.stack(ssm_s), jnp.stack(conv_s), jnp.stack(kv_s, axis=2), win_sample)
```

```python
import functools
import math

import jax
import jax.numpy as jnp
import numpy as np
from jax import lax
from jax.experimental import pallas as pl
from jax.experimental.pallas import tpu as pltpu

F32 = jnp.float32
BF16 = jnp.bfloat16
HIGHEST = lax.Precision.HIGHEST

NORM_EPS = 1e-6
SSD_HEAD_DIM = 64
SSD_N_GROUPS = 8
SSD_D_STATE = 128
SSD_CONV_W = 4
SSD_CHUNK = 128
NSA_N_HEADS = 16
NSA_HEAD_DIM = 64
NSA_N_KV = 4
NSA_R = NSA_N_HEADS // NSA_N_KV
N_BRANCH = 3
CMP_BLOCK = 32
CMP_STRIDE = 16
CMP_HIDDEN = 128
SEL_BLOCK = 64
SEL_TOPK = 16
WINDOW = 512
FORCE_SCORE = 1e4
REL_BUCKETS = 32
REL_MAX_DIST = 1024
PAGE_SIZE = 128

LANES = 128
SUBLANES = 8
VMEM_LIMIT = 56 * 1024 * 1024

NEG = -1e30
LOG2E = math.log2(math.e)
NT_DIMS = (((1,), (1,)), ((), ()))
TN_DIMS = (((0,), (0,)), ((), ()))


def _cparams(sem):
    return pltpu.CompilerParams(dimension_semantics=sem, vmem_limit_bytes=VMEM_LIMIT)


def _round_up(x, m):
    return (x + m - 1) // m * m


def _silu(x):
    return x * jax.nn.sigmoid(x)


def _pick_tm(m, cap, quantum=SUBLANES):
    if m <= cap:
        return m
    for tm in range(cap // quantum * quantum, 0, -quantum):
        if m % tm == 0:
            return tm
    raise ValueError(f"no row tile for {m}")


def _pick_tn(n, cap=1024):
    best = LANES
    for tn in range(LANES, cap + 1, LANES):
        if n % tn == 0:
            best = tn
    return best


def _norm_matmul_kernel(x_ref, nw_ref, w_ref, o_ref, xn_ref, *, transpose_out):
    @pl.when(pl.program_id(1) == 0)
    def _():
        x = x_ref[...]
        ms = jnp.mean(x * x, axis=-1, keepdims=True)
        xn = x * lax.rsqrt(ms + NORM_EPS) * nw_ref[...]
        if transpose_out:
            xn_ref[...] = xn.T.astype(BF16)
        else:
            xn_ref[...] = xn.astype(BF16)

    if transpose_out:
        o_ref[...] = jnp.dot(w_ref[...], xn_ref[...], preferred_element_type=F32).astype(o_ref.dtype)
    else:
        o_ref[...] = jnp.dot(xn_ref[...], w_ref[...], preferred_element_type=F32).astype(o_ref.dtype)


def _norm_matmul(x, nw, w, tn, out_dtype=F32):
    m, k = x.shape
    n = w.shape[1]
    assert n % tn == 0
    whole_w = tn == n
    tm = _pick_tm(m, 512 if whole_w else 1024)
    w_mode = dict(pipeline_mode=pl.Buffered(1)) if whole_w else {}
    return pl.pallas_call(
        functools.partial(_norm_matmul_kernel, transpose_out=False),
        out_shape=jax.ShapeDtypeStruct((m, n), out_dtype),
        grid=(m // tm, n // tn),
        in_specs=[
            pl.BlockSpec((tm, k), lambda i, j: (i, 0)),
            pl.BlockSpec((1, k), lambda i, j: (0, 0)),
            pl.BlockSpec((k, tn), lambda i, j: (0, j), **w_mode),
        ],
        out_specs=pl.BlockSpec((tm, tn), lambda i, j: (i, j)),
        scratch_shapes=[pltpu.VMEM((tm, k), BF16)],
        compiler_params=_cparams(("parallel", "arbitrary")),
        name="norm_matmul",
    )(x, nw.reshape(1, k), w)


def _norm_matmul_t(x, nw, wt, bsz, tn):
    m, k = x.shape
    n = wt.shape[0]
    t = m // bsz
    assert n % tn == 0
    whole_w = tn == n
    tm = _pick_tm(t, 512 if whole_w else 1024, LANES)
    w_mode = dict(pipeline_mode=pl.Buffered(1)) if whole_w else {}
    tpb = t // tm
    return pl.pallas_call(
        functools.partial(_norm_matmul_kernel, transpose_out=True),
        out_shape=jax.ShapeDtypeStruct((bsz, n, t), F32),
        grid=(m // tm, n // tn),
        in_specs=[
            pl.BlockSpec((tm, k), lambda i, j: (i, 0)),
            pl.BlockSpec((1, k), lambda i, j: (0, 0)),
            pl.BlockSpec((tn, k), lambda i, j: (j, 0), **w_mode),
        ],
        out_specs=pl.BlockSpec((None, tn, tm), lambda i, j: (i // tpb, j, i % tpb)),
        scratch_shapes=[pltpu.VMEM((k, tm), BF16)],
        compiler_params=_cparams(("parallel", "arbitrary")),
        name="norm_matmul_t",
    )(x, nw.reshape(1, k), wt)


def _kv_proj_kernel(x_ref, nw_ref, w_ref, *rest, n_kv_tiles):
    okv_ref, okvw_ref, xn_ref = rest[-3:]
    j = pl.program_id(1)

    @pl.when(j == 0)
    def _():
        x = x_ref[...]
        ms = jnp.mean(x * x, axis=-1, keepdims=True)
        xn_ref[...] = (x * lax.rsqrt(ms + NORM_EPS) * nw_ref[...]).T.astype(BF16)

    res = jnp.dot(w_ref[...], xn_ref[...], preferred_element_type=F32)

    @pl.when(j < n_kv_tiles)
    def _():
        okv_ref[...] = res

    @pl.when(j >= n_kv_tiles)
    def _():
        okvw_ref[...] = res


def _kv_proj_t(x, nw, wt, bsz, layer, n_layers, kv_rows, bufs):
    m, k = x.shape
    n = wt.shape[0]
    t = m // bsz
    tn = n - kv_rows
    assert kv_rows % tn == 0 and tn % SUBLANES == 0
    n_kv_tiles = kv_rows // tn
    tm = _pick_tm(t, 1024, LANES)
    tpb = t // tm
    in_specs = [
        pl.BlockSpec((tm, k), lambda i, j: (i, 0)),
        pl.BlockSpec((1, k), lambda i, j: (0, 0)),
        pl.BlockSpec((tn, k), lambda i, j: (j, 0)),
    ]
    args = [x, nw.reshape(1, k), wt]
    aliases = {}
    if bufs is not None:
        in_specs += [pl.BlockSpec(memory_space=pl.ANY), pl.BlockSpec(memory_space=pl.ANY)]
        aliases = {len(args): 0, len(args) + 1: 1}
        args += list(bufs)
    return pl.pallas_call(
        functools.partial(_kv_proj_kernel, n_kv_tiles=n_kv_tiles),
        out_shape=(jax.ShapeDtypeStruct((bsz, n_layers, kv_rows, t), F32),
                   jax.ShapeDtypeStruct((bsz, n_layers, tn, t), F32)),
        grid=(m // tm, n // tn),
        in_specs=in_specs,
        out_specs=(pl.BlockSpec((None, None, tn, tm),
                                lambda i, j: (i // tpb, layer, jnp.minimum(j, n_kv_tiles - 1), i % tpb)),
                   pl.BlockSpec((None, None, tn, tm), lambda i, j: (i // tpb, layer, 0, i % tpb))),
        scratch_shapes=[pltpu.VMEM((k, tm), BF16)],
        input_output_aliases=aliases,
        compiler_params=_cparams(("parallel", "arbitrary")),
        name="kv_proj_t",
    )(*args)


def _matmul_res_kernel(y_ref, w_ref, x_ref, o_ref, *, y_transposed):
    dims = TN_DIMS if y_transposed else (((1,), (0,)), ((), ()))
    o_ref[...] = x_ref[...] + lax.dot_general(y_ref[...], w_ref[...], dims, preferred_element_type=F32)


def _matmul_res(y, w, x):
    m, k = y.shape
    n = w.shape[1]
    tm = min(m, 512)
    assert m % tm == 0
    return pl.pallas_call(
        functools.partial(_matmul_res_kernel, y_transposed=False),
        out_shape=jax.ShapeDtypeStruct((m, n), F32),
        grid=(m // tm,),
        in_specs=[
            pl.BlockSpec((tm, k), lambda i: (i, 0)),
            pl.BlockSpec((k, n), lambda i: (0, 0)),
            pl.BlockSpec((tm, n), lambda i: (i, 0)),
        ],
        out_specs=pl.BlockSpec((tm, n), lambda i: (i, 0)),
        compiler_params=_cparams(("parallel",)),
        name="matmul_res",
    )(y, w, x)


def _matmul_res_t(yt, w, x):
    bsz, k, t = yt.shape
    n = w.shape[1]
    tm = min(t, 512)
    assert t % tm == 0
    tpb = t // tm
    return pl.pallas_call(
        functools.partial(_matmul_res_kernel, y_transposed=True),
        out_shape=jax.ShapeDtypeStruct((bsz * t, n), F32),
        grid=(bsz * tpb,),
        in_specs=[
            pl.BlockSpec((None, k, tm), lambda i: (i // tpb, 0, i % tpb)),
            pl.BlockSpec((k, n), lambda i: (0, 0)),
            pl.BlockSpec((tm, n), lambda i: (i, 0)),
        ],
        out_specs=pl.BlockSpec((tm, n), lambda i: (i, 0)),
        compiler_params=_cparams(("parallel",)),
        name="matmul_res_t",
    )(yt, w, x)


def _rmsnorm_kernel(x_ref, nw_ref, o_ref):
    x = x_ref[...]
    ms = jnp.mean(x * x, axis=-1, keepdims=True)
    o_ref[...] = x * lax.rsqrt(ms + NORM_EPS) * nw_ref[...]


def _rmsnorm(x, nw):
    m, k = x.shape
    tm = min(m, 512)
    assert m % tm == 0
    return pl.pallas_call(
        _rmsnorm_kernel,
        out_shape=jax.ShapeDtypeStruct((m, k), F32),
        grid=(m // tm,),
        in_specs=[pl.BlockSpec((tm, k), lambda i: (i, 0)), pl.BlockSpec((1, k), lambda i: (0, 0))],
        out_specs=pl.BlockSpec((tm, k), lambda i: (i, 0)),
        compiler_params=_cparams(("parallel",)),
        name="final_rmsnorm",
    )(x, nw.reshape(1, k))


def _ssd_chunk_kernel(z_ref, x_ref, bc_ref, dt_ref, cprev_ref, sprev_ref, cw_ref, cb_ref, dtb_ref, alog_ref,
                      dskip_ref, gn_ref, y_ref, hlast_ref, xpad_ref, state_ref, *, d_inner, n_heads):
    lc = SSD_CHUNK
    g_n, n, p = SSD_N_GROUPS, SSD_D_STATE, SSD_HEAD_DIM
    r_n = n_heads // g_n
    c = pl.program_id(1)

    @pl.when(c == 0)
    def _():
        xpad_ref[0:SUBLANES, :] = cprev_ref[0]
        state_ref[...] = sprev_ref[0]

    @pl.when(c > 0)
    def _():
        xpad_ref[0:SUBLANES, :] = xpad_ref[lc:lc + SUBLANES, :]

    xpad_ref[SUBLANES:SUBLANES + lc, 0:d_inner] = x_ref[...]
    xpad_ref[SUBLANES:SUBLANES + lc, d_inner:] = bc_ref[...]

    conv = cb_ref[...]
    for k in range(SSD_CONV_W):
        off = SUBLANES - (SSD_CONV_W - 1) + k
        conv = conv + xpad_ref[off:off + lc, :] * cw_ref[k:k + 1, :]
    act = _silu(conv)
    xs = act[:, :d_inner]
    bm = act[:, d_inner:d_inner + g_n * n]
    cm = act[:, d_inner + g_n * n:]

    dt = jax.nn.softplus(dt_ref[...] + dtb_ref[...])
    a = -jnp.exp(alog_ref[...])
    dta = dt * a
    row = lax.broadcasted_iota(jnp.int32, (lc, lc), 0)
    col = lax.broadcasted_iota(jnp.int32, (lc, lc), 1)
    causal = row >= col
    tril = causal.astype(F32)
    a_cs = jnp.dot(tril, dta, precision=HIGHEST, preferred_element_type=F32)
    a_cs_t = a_cs.T
    a_end = a_cs[lc - 1:lc, :]
    to_end = jnp.exp(a_end - a_cs)
    e_cs = jnp.exp(a_cs)
    e_end = jnp.exp(a_end)

    assert 2 * p == LANES and r_n % 2 == 0
    lane_lo = lax.broadcasted_iota(jnp.int32, (1, LANES), 1) < p

    def pair_lanes(v, k):
        return jnp.where(lane_lo, v[:, 2 * k:2 * k + 1], v[:, 2 * k + 1:2 * k + 2])

    zs = _silu(z_ref[...])
    gw = r_n * p
    ys = []
    for g in range(g_n):
        bg16 = bm[:, g * n:(g + 1) * n].astype(BF16)
        cg16 = cm[:, g * n:(g + 1) * n].astype(BF16)
        cb = lax.dot_general(cg16, bg16, NT_DIMS, preferred_element_type=F32)
        pairs = range(g * r_n // 2, (g + 1) * r_n // 2)
        gl = slice(g * gw, (g + 1) * gw)
        xs_g = xs[:, gl]
        xdt_g = xs_g * jnp.concatenate([pair_lanes(dt, k) for k in pairs], axis=1)
        y_diag = []
        for i, k in enumerate(pairs):
            xdt_k = xdt_g[:, i * LANES:(i + 1) * LANES]
            acc = None
            for j, h in enumerate((2 * k, 2 * k + 1)):
                seg = a_cs[:, h:h + 1] - a_cs_t[h:h + 1, :]
                decay = jnp.exp(jnp.where(causal, seg, -jnp.inf))
                m_h = (cb * decay).astype(BF16)
                x_h = jnp.where(lane_lo if j == 0 else jnp.logical_not(lane_lo), xdt_k, 0.0).astype(BF16)
                d = jnp.dot(m_h, x_h, preferred_element_type=F32)
                acc = d if acc is None else acc + d
            y_diag.append(acc)
        y_diag = jnp.concatenate(y_diag, axis=1)
        sprev = state_ref[:, gl]
        y_off = (jnp.dot(cg16, sprev.astype(BF16), preferred_element_type=F32)
                 * jnp.concatenate([pair_lanes(e_cs, k) for k in pairs], axis=1))
        xw = (xdt_g * jnp.concatenate([pair_lanes(to_end, k) for k in pairs], axis=1)).astype(BF16)
        st = lax.dot_general(bg16, xw, TN_DIMS, preferred_element_type=F32)
        state_ref[:, gl] = sprev * jnp.concatenate([pair_lanes(e_end, k) for k in pairs], axis=1) + st
        yg = (y_diag + y_off + xs_g * dskip_ref[:, gl]) * zs[:, gl]
        ms = jnp.mean(yg * yg, axis=-1, keepdims=True)
        ys.append(yg * lax.rsqrt(ms + NORM_EPS))
    y = jnp.concatenate(ys, axis=1) * gn_ref[...]
    y_ref[...] = y.astype(y_ref.dtype)

    @pl.when(c == pl.num_programs(1) - 1)
    def _():
        hlast_ref[0] = state_ref[...]


def _pad_lanes(v, width=LANES):
    return jnp.pad(v.astype(F32), (0, width - v.shape[0])).reshape(1, width)


def _ssd_prompt_core(u, bsz, t, conv_prev8, ssm_prev, conv_w, conv_b, dt_bias, a_log, d_skip, gnorm):
    n_heads = dt_bias.shape[0]
    d_inner = n_heads * SSD_HEAD_DIM
    conv_dim = conv_w.shape[1]
    assert conv_dim == 2 * d_inner and t % SSD_CHUNK == 0
    lc = SSD_CHUNK
    nc = t // lc
    dt_blk = (d_inner + conv_dim) // LANES
    kern = functools.partial(_ssd_chunk_kernel, d_inner=d_inner, n_heads=n_heads)
    const = lambda b, c: (0, 0)
    state_t = ssm_prev.transpose(0, 3, 1, 2).reshape(bsz, SSD_D_STATE, d_inner)
    y, hlast_t = pl.pallas_call(
        kern,
        out_shape=(jax.ShapeDtypeStruct((bsz * t, d_inner), BF16),
                   jax.ShapeDtypeStruct((bsz, SSD_D_STATE, d_inner), F32)),
        grid=(bsz, nc),
        in_specs=[
            pl.BlockSpec((lc, d_inner), lambda b, c: (b * nc + c, 0)),
            pl.BlockSpec((lc, d_inner), lambda b, c: (b * nc + c, 1)),
            pl.BlockSpec((lc, d_inner), lambda b, c: (b * nc + c, 2)),
            pl.BlockSpec((lc, LANES), lambda b, c: (b * nc + c, dt_blk)),
            pl.BlockSpec((1, SUBLANES, conv_dim), lambda b, c: (b, 0, 0)),
            pl.BlockSpec((1, SSD_D_STATE, d_inner), lambda b, c: (b, 0, 0)),
            pl.BlockSpec((SSD_CONV_W, conv_dim), const),
            pl.BlockSpec((1, conv_dim), const),
            pl.BlockSpec((1, LANES), const),
            pl.BlockSpec((1, LANES), const),
            pl.BlockSpec((1, d_inner), const),
            pl.BlockSpec((1, d_inner), const),
        ],
        out_specs=(pl.BlockSpec((lc, d_inner), lambda b, c: (b * nc + c, 0)),
                   pl.BlockSpec((1, SSD_D_STATE, d_inner), lambda b, c: (b, 0, 0))),
        scratch_shapes=[pltpu.VMEM((lc + SUBLANES, conv_dim), F32),
                        pltpu.VMEM((SSD_D_STATE, d_inner), F32)],
        compiler_params=_cparams(("parallel", "arbitrary")),
        name="ssd_chunk",
    )(u, u, u, u, conv_prev8, state_t, conv_w, conv_b.reshape(1, conv_dim), _pad_lanes(dt_bias),
      _pad_lanes(a_log), jnp.repeat(d_skip.astype(F32), SSD_HEAD_DIM).reshape(1, d_inner), gnorm.reshape(1, d_inner))
    hlast = hlast_t.reshape(bsz, SSD_D_STATE, n_heads, SSD_HEAD_DIM).transpose(0, 2, 3, 1)
    return y, hlast


def _ssd_in_weights(w_in, d_inner, conv_dim):
    main = d_inner + conv_dim
    n_dt = w_in.shape[1] - main
    w = jnp.concatenate([w_in[:, :main], jnp.pad(w_in[:, main:], ((0, 0), (0, LANES - n_dt)))], axis=1)
    return w.astype(BF16)


def _ssd_layer_prompt(xp, prm):
    norm_w, w_in, conv_w, conv_b, dt_bias, a_log, d_skip, gnorm, w_out = prm
    bsz, t, d = xp.shape
    assert t >= SSD_CONV_W - 1
    n_heads = dt_bias.shape[0]
    d_inner = n_heads * SSD_HEAD_DIM
    conv_dim = conv_w.shape[1]
    w = _ssd_in_weights(w_in, d_inner, conv_dim)
    x2 = xp.reshape(bsz * t, d)
    u = _norm_matmul(x2, norm_w, w, w.shape[1])
    conv_prev8 = jnp.zeros((bsz, SUBLANES, conv_dim), F32)
    ssm_prev = jnp.zeros((bsz, n_heads, SSD_HEAD_DIM, SSD_D_STATE), F32)
    y, hlast = _ssd_prompt_core(u, bsz, t, conv_prev8, ssm_prev, conv_w, conv_b, dt_bias, a_log, d_skip, gnorm)
    out = _matmul_res(y, w_out.astype(BF16), x2).reshape(bsz, t, d)
    u3 = u.reshape(bsz, t, -1)
    new_conv = u3[:, t - (SSD_CONV_W - 1):, d_inner:d_inner + conv_dim]
    return out, new_conv, hlast


def _ssd_step_pre_kernel(u_ref, cst_ref, cw_ref, cb_ref, dtb_ref, alog_ref,
                         xs_ref, xdt_ref, dec_ref, bm_ref, cm_ref, ncst_ref, *, d_inner, n_heads):
    conv_dim = cw_ref.shape[1]
    gn = SSD_N_GROUPS * SSD_D_STATE
    xbc = u_ref[:, d_inner:d_inner + conv_dim]
    conv = cb_ref[...]
    for k in range(SSD_CONV_W - 1):
        conv = conv + cst_ref[k] * cw_ref[k:k + 1, :]
    conv = conv + xbc * cw_ref[SSD_CONV_W - 1:SSD_CONV_W, :]
    for k in range(SSD_CONV_W - 2):
        ncst_ref[k] = cst_ref[k + 1]
    ncst_ref[SSD_CONV_W - 2] = xbc
    act = _silu(conv)
    xs = act[:, :d_inner]
    dt = jax.nn.softplus(u_ref[:, d_inner + conv_dim:d_inner + conv_dim + LANES] + dtb_ref[...])
    dec = jnp.exp(dt * (-jnp.exp(alog_ref[...])))
    hrow = lax.broadcasted_iota(jnp.int32, (LANES, d_inner), 0)
    hcol = lax.broadcasted_iota(jnp.int32, (LANES, d_inner), 1) // SSD_HEAD_DIM
    expand = (hrow == hcol).astype(F32)
    dt_e = jnp.dot(dt, expand, precision=HIGHEST, preferred_element_type=F32)
    dec_e = jnp.dot(dec, expand, precision=HIGHEST, preferred_element_type=F32)
    xs_ref[...] = xs
    xdt_ref[...] = xs * dt_e
    dec_ref[...] = dec_e
    bm_ref[...] = act[:, d_inner:d_inner + gn]
    cm_ref[...] = act[:, d_inner + gn:]


def _ssd_step_state_kernel(s_ref, xdt_ref, dec_ref, b_ref, c_ref, snew_ref, y_ref, *, rows_per_group):
    for g in range(SSD_N_GROUPS):
        sl = slice(g * rows_per_group, (g + 1) * rows_per_group)
        s = s_ref[0, sl, :]
        bg = b_ref[0, g:g + 1, :]
        cg = c_ref[0, g:g + 1, :]
        xdt = xdt_ref[0, sl, :]
        dec = dec_ref[0, sl, :]
        cb = jnp.sum(bg * cg, axis=1, keepdims=True)
        y_off = jnp.sum(s * cg, axis=1, keepdims=True)
        snew_ref[0, sl, :] = s * dec + xdt * bg
        y_ref[0, sl, :] = cb * xdt + dec * y_off


def _ssd_step_post_kernel(y_ref, xs_ref, u_ref, dsk_ref, gn_ref, o_ref, *, d_inner, group_width):
    y = (y_ref[...] + xs_ref[...] * dsk_ref[...]) * _silu(u_ref[:, :d_inner])
    outs = []
    for g in range(SSD_N_GROUPS):
        yg = y[:, g * group_width:(g + 1) * group_width]
        ms = jnp.mean(yg * yg, axis=-1, keepdims=True)
        outs.append(yg * lax.rsqrt(ms + NORM_EPS))
    o_ref[...] = (jnp.concatenate(outs, axis=1) * gn_ref[...]).astype(o_ref.dtype)


def _ssd_layer_sample(xs_in, conv_state, ssm_states, layer, prm):
    norm_w, w_in, conv_w, conv_b, dt_bias, a_log, d_skip, gnorm, w_out = prm
    bsz, t, d = xs_in.shape
    assert t == 1
    n_heads = dt_bias.shape[0]
    d_inner = n_heads * SSD_HEAD_DIM
    conv_dim = conv_w.shape[1]
    gn = SSD_N_GROUPS * SSD_D_STATE
    w = _ssd_in_weights(w_in, d_inner, conv_dim)
    x2 = xs_in.reshape(bsz, d)
    u = _norm_matmul(x2, norm_w, w, _pick_tn(w.shape[1]))
    cst = conv_state.transpose(1, 0, 2)
    pre = functools.partial(_ssd_step_pre_kernel, d_inner=d_inner, n_heads=n_heads)
    xs, xdt, dec, bm, cm, ncst = pl.pallas_call(
        pre,
        out_shape=(jax.ShapeDtypeStruct((bsz, d_inner), F32), jax.ShapeDtypeStruct((bsz, d_inner), F32),
                   jax.ShapeDtypeStruct((bsz, d_inner), F32), jax.ShapeDtypeStruct((bsz, gn), F32),
                   jax.ShapeDtypeStruct((bsz, gn), F32), jax.ShapeDtypeStruct(cst.shape, F32)),
        compiler_params=pltpu.CompilerParams(vmem_limit_bytes=VMEM_LIMIT),
        name="ssd_step_pre",
    )(u, cst, conv_w, conv_b.reshape(1, conv_dim), _pad_lanes(dt_bias), _pad_lanes(a_log))
    rows = n_heads * SSD_HEAD_DIM
    rpg = rows // SSD_N_GROUPS
    st = functools.partial(_ssd_step_state_kernel, rows_per_group=rpg)
    snew, ycol = pl.pallas_call(
        st,
        out_shape=(jax.ShapeDtypeStruct((bsz, rows, SSD_D_STATE), F32), jax.ShapeDtypeStruct((bsz, rows, 1), F32)),
        grid=(bsz,),
        in_specs=[
            pl.BlockSpec((None, 1, rows, SSD_D_STATE), lambda b: (layer, b, 0, 0)),
            pl.BlockSpec((1, rows, 1), lambda b: (b, 0, 0)),
            pl.BlockSpec((1, rows, 1), lambda b: (b, 0, 0)),
            pl.BlockSpec((1, SSD_N_GROUPS, SSD_D_STATE), lambda b: (b, 0, 0)),
            pl.BlockSpec((1, SSD_N_GROUPS, SSD_D_STATE), lambda b: (b, 0, 0)),
        ],
        out_specs=(pl.BlockSpec((1, rows, SSD_D_STATE), lambda b: (b, 0, 0)),
                   pl.BlockSpec((1, rows, 1), lambda b: (b, 0, 0))),
        compiler_params=_cparams(("parallel",)),
        name="ssd_step_state",
    )(ssm_states.reshape(ssm_states.shape[0], bsz, rows, SSD_D_STATE), xdt.reshape(bsz, rows, 1),
      dec.reshape(bsz, rows, 1), bm.reshape(bsz, SSD_N_GROUPS, SSD_D_STATE),
      cm.reshape(bsz, SSD_N_GROUPS, SSD_D_STATE))
    post = functools.partial(_ssd_step_post_kernel, d_inner=d_inner, group_width=d_inner // SSD_N_GROUPS)
    y = pl.pallas_call(
        post,
        out_shape=jax.ShapeDtypeStruct((bsz, d_inner), BF16),
        compiler_params=pltpu.CompilerParams(vmem_limit_bytes=VMEM_LIMIT),
        name="ssd_step_post",
    )(ycol.reshape(bsz, rows), xs, u, jnp.repeat(d_skip.astype(F32), SSD_HEAD_DIM).reshape(1, d_inner),
      gnorm.reshape(1, d_inner))
    out = _matmul_res(y, w_out.astype(BF16), x2).reshape(bsz, 1, d)
    return out, ncst.transpose(1, 0, 2), snew.reshape(ssm_states.shape[1:])


NSA_Q_W = NSA_N_HEADS * NSA_HEAD_DIM
NSA_KV_W = NSA_N_KV * NSA_HEAD_DIM
COL_Q = 0
COL_KV = NSA_Q_W
COL_Z = COL_KV + 4 * NSA_KV_W
COL_KVW = COL_Z + N_BRANCH * NSA_Q_W
COL_GATE = COL_KVW + 2 * NSA_KV_W
NSA_N_PAD = COL_GATE + LANES
TQ = 128
KT = 128
N_REL_TILES = -(-(REL_MAX_DIST + TQ - 1) // KT)
CMP_NEAR = 128
CMP_FRONT = CMP_NEAR - TQ // CMP_STRIDE
PAGES_PER_STEP = 32
STEP_PAGES = 64
SEL_STEP_TILES = 4


def _nsa_in_weights(w_in):
    cuts = np.cumsum([NSA_Q_W, 4 * NSA_KV_W, 2 * NSA_KV_W, NSA_N_HEADS * N_BRANCH]).tolist()
    q, kv, kvw, gate, z = (w_in[:, a:b] for a, b in zip([0] + cuts, cuts + [w_in.shape[1]]))
    gate = jnp.pad(gate, ((0, 0), (0, LANES - gate.shape[1])))
    return jnp.concatenate([q, kv, z, kvw, gate], axis=1)


def _rel_bucket(dist):
    n = jnp.maximum(dist, 0)
    exact = REL_BUCKETS // 2
    logv = jnp.log(jnp.maximum(n, 1).astype(F32) / exact) / math.log(REL_MAX_DIST / exact)
    large = jnp.minimum(exact + (logv * (REL_BUCKETS - exact)).astype(jnp.int32), REL_BUCKETS - 1)
    return jnp.where(n < exact, n, large)


def _bias_by_dist(rel_bias, n):
    near = jnp.moveaxis(rel_bias.astype(F32)[_rel_bucket(jnp.arange(min(n, REL_MAX_DIST)))], -1, 0)
    if n <= REL_MAX_DIST:
        return near
    far = jnp.broadcast_to(rel_bias.astype(F32)[REL_BUCKETS - 1][:, None], (rel_bias.shape[1], n - REL_MAX_DIST))
    return jnp.concatenate([near, far], axis=1)


def _toeplitz(vec, first, n_rows, n_cols, row_step):
    lead = vec.shape[:-1]
    lo = first - row_step * (n_rows - 1)
    front = max(-lo, 0)
    if front:
        vec = jnp.concatenate([jnp.broadcast_to(vec[..., :1], lead + (front,)), vec], axis=-1)
    span = row_step * (n_rows - 1) + n_cols
    w = vec[..., lo + front:lo + front + span]
    pitch = span + row_step
    flat = jnp.tile(w, n_rows + 1)[..., :n_rows * pitch]
    s = flat.reshape(lead + (n_rows, pitch))[..., :n_cols]
    return jnp.flip(s, axis=-2)


def _lanes_by_group(t):
    h, n, q = t.shape
    return t.reshape(NSA_N_KV, NSA_R, n, q).transpose(0, 2, 1, 3).reshape(NSA_N_KV, n, NSA_R * q)


def _overlap_matrix(nc, ns):
    c_start = np.arange(nc) * CMP_STRIDE
    c_end = c_start + CMP_BLOCK - 1
    s_start = np.arange(ns) * SEL_BLOCK
    ov = np.clip(np.minimum(c_end[:, None], s_start[None, :] + SEL_BLOCK - 1)
                 - np.maximum(c_start[:, None], s_start[None, :]) + 1, 0, None).astype(np.float32) / CMP_STRIDE
    return ov


def _masked_softmax_parts(parts, valids, axis):
    m = None
    for s, v in zip(parts, valids):
        pm = jnp.max(jnp.where(v, s, -jnp.inf), axis=axis, keepdims=True)
        m = pm if m is None else jnp.maximum(m, pm)
    m = jnp.where(m > -jnp.inf, m, 0.0)
    es = [jnp.where(v, jnp.exp(s - m), 0.0) for s, v in zip(parts, valids)]
    den = None
    for e in es:
        d = jnp.sum(e, axis=axis, keepdims=True)
        den = d if den is None else den + d
    inv = 1.0 / jnp.where(den > 0, den, 1.0)
    return [e * inv for e in es]


def _compress_kernel(*refs, n_pp):
    page_refs = refs[1:1 + n_pp]
    perm_ref, pe_ref, w1_ref, w2_ref, kc_ref, vc_ref, x_ref = refs[1 + n_pp:]
    step = pl.program_id(1)
    hb = PAGE_SIZE // CMP_STRIDE
    dh = NSA_HEAD_DIM
    for k in range(n_pp):
        pt = lax.dot_general(perm_ref[...], page_refs[k][...].astype(BF16), NT_DIMS, preferred_element_type=F32)
        row0 = pl.multiple_of((step * n_pp + k) * hb, hb)
        for l in range(CMP_STRIDE):
            row = pt[l * hb:(l + 1) * hb, :]
            for kv in range(2):
                for g in range(NSA_N_KV):
                    c0 = kv * NSA_KV_W + g * dh
                    x_ref[kv, g, pl.ds(row0, hb), l * dh:(l + 1) * dh] = row[:, c0:c0 + dh]

    @pl.when(step == pl.num_programs(1) - 1)
    def _():
        hid_w = CMP_HIDDEN
        for kv, out_ref in ((0, kc_ref), (1, vc_ref)):
            w1 = w1_ref[kv]
            w2 = w2_ref[kv]
            pe_w = jnp.dot(pe_ref[kv].astype(BF16), w1, preferred_element_type=F32)
            pe_term = pe_w[0:1, :hid_w] + pe_w[1:2, hid_w:]
            for g in range(NSA_N_KV):
                both = jnp.dot(x_ref[kv, g].astype(BF16), w1, preferred_element_type=F32)
                bot = both[:, hid_w:]
                bot = jnp.concatenate([bot[1:], jnp.zeros((1, hid_w), F32)], axis=0)
                hid = _silu(both[:, :hid_w] + bot + pe_term).astype(BF16)
                out_ref[0, :, g * dh:(g + 1) * dh] = jnp.dot(hid, w2, preferred_element_type=F32)


def _compress(src, page_index, bsz, n_pages, ids, pe, w1, w2):
    n_pp = math.gcd(PAGES_PER_STEP, n_pages)
    hb = PAGE_SIZE // CMP_STRIDE
    nh = n_pages * hb
    half = CMP_STRIDE * NSA_HEAD_DIM
    blk = (None,) * (src.ndim - 2) + (2 * NSA_KV_W, PAGE_SIZE)
    page_specs = [pl.BlockSpec(blk, functools.partial(lambda b, s, ids, k: page_index(b, s * n_pp + k, ids), k=k))
                  for k in range(n_pp)]
    perm = np.zeros((PAGE_SIZE, PAGE_SIZE), np.float32)
    for l in range(CMP_STRIDE):
        for i in range(hb):
            perm[l * hb + i, CMP_STRIDE * i + l] = 1.0
    w1cat = jnp.concatenate([w1[:, :half], w1[:, half:]], axis=-1).astype(BF16)
    grid_spec = pltpu.PrefetchScalarGridSpec(
        num_scalar_prefetch=1,
        grid=(bsz, n_pages // n_pp),
        in_specs=page_specs + [
            pl.BlockSpec((PAGE_SIZE, PAGE_SIZE), lambda b, s, ids: (0, 0)),
            pl.BlockSpec((2, 2, half), lambda b, s, ids: (0, 0, 0)),
            pl.BlockSpec((2, half, 2 * CMP_HIDDEN), lambda b, s, ids: (0, 0, 0)),
            pl.BlockSpec((2, CMP_HIDDEN, NSA_HEAD_DIM), lambda b, s, ids: (0, 0, 0)),
        ],
        out_specs=(pl.BlockSpec((1, nh, NSA_KV_W), lambda b, s, ids: (b, 0, 0)),
                   pl.BlockSpec((1, nh, NSA_KV_W), lambda b, s, ids: (b, 0, 0))),
        scratch_shapes=[pltpu.VMEM((2, NSA_N_KV, nh, half), F32)],
    )
    return pl.pallas_call(
        functools.partial(_compress_kernel, n_pp=n_pp),
        out_shape=(jax.ShapeDtypeStruct((bsz, nh, NSA_KV_W), F32), jax.ShapeDtypeStruct((bsz, nh, NSA_KV_W), F32)),
        grid_spec=grid_spec,
        compiler_params=_cparams(("parallel", "arbitrary")),
        name="nsa_compress",
    )(ids, *([src] * n_pp), jnp.asarray(perm, BF16), pe.reshape(2, 2, half).astype(F32), w1cat, w2.astype(BF16))


def _nsa_attn_kernel(qt_ref, kv_ref, kc_ref, vc_ref, ov_ref, trel_ref, tcn_ref, cfar_ref, gate_ref, z0_ref, z1_ref,
                     z2_ref, o_ref, qaug_ref, imp_ref, m_ref, l_ref, acc_ref, ob_ref, s_ref, *, n_sel, n_cmp_rows):
    t = pl.program_id(1)
    dh = NSA_HEAD_DIM
    rq = NSA_R * TQ
    scale = dh ** -0.5
    far_idx = N_REL_TILES
    ns8 = _round_up(n_sel, 2 * SUBLANES)
    assert ns8 < LANES
    near0 = pl.multiple_of(t * (TQ // CMP_STRIDE), SUBLANES)
    far_rows = slice(CMP_FRONT, CMP_FRONT + n_cmp_rows)

    def cmp_branch(g, qt, with_far):
        gs = slice(g * dh, (g + 1) * dh)
        kc_near = kc_ref[0, pl.ds(near0, CMP_NEAR), gs].astype(BF16)
        vc_near = vc_ref[0, pl.ds(near0, CMP_NEAR), gs].astype(BF16)
        cn = lax.broadcasted_iota(jnp.int32, (CMP_NEAR, rq), 0)
        qn = lax.broadcasted_iota(jnp.int32, (CMP_NEAR, rq), 1) % TQ
        dist_n = qn - CMP_STRIDE * cn + (CMP_STRIDE * CMP_FRONT - (CMP_BLOCK - 1))
        parts = [jnp.dot(kc_near, qt, preferred_element_type=F32) + tcn_ref[g]]
        valids = [(dist_n >= 0) & (cn + near0 >= CMP_FRONT)]
        if with_far:
            kc_far = kc_ref[0, far_rows, gs].astype(BF16)
            parts.append(jnp.dot(kc_far, qt, preferred_element_type=F32) + cfar_ref[g, 0:1, :])
            valids.append(lax.broadcasted_iota(jnp.int32, (n_cmp_rows, rq), 0) + CMP_FRONT < near0)
        ps = _masked_softmax_parts(parts, valids, axis=0)

        def importance(ov, p):
            pg = sum(p[:, r * TQ:(r + 1) * TQ] for r in range(NSA_R))
            hi = pg.astype(BF16)
            lo = (pg - hi.astype(F32)).astype(BF16)
            ov16 = ov.astype(BF16)
            return (lax.dot_general(ov16, hi, TN_DIMS, preferred_element_type=F32)
                    + lax.dot_general(ov16, lo, TN_DIMS, preferred_element_type=F32))

        o_c = lax.dot_general(vc_near, ps[0].astype(BF16), TN_DIMS, preferred_element_type=F32)
        imp = importance(ov_ref[pl.ds(near0, CMP_NEAR), :], ps[0])
        if with_far:
            vc_far = vc_ref[0, far_rows, gs].astype(BF16)
            o_c = o_c + lax.dot_general(vc_far, ps[1].astype(BF16), TN_DIMS, preferred_element_type=F32)
            imp = imp + importance(ov_ref[far_rows, :], ps[1])
        ob_ref[0, g] = o_c
        imp_ref[...] = imp

    qts = []
    sid = lax.broadcasted_iota(jnp.int32, (ns8, TQ), 0)
    qpos = t * TQ + lax.broadcasted_iota(jnp.int32, (ns8, TQ), 1)
    cur = qpos // SEL_BLOCK
    forced = (sid == 0) | (sid == cur) | (sid == cur - 1)
    reachable = sid * SEL_BLOCK <= qpos
    sub = lax.broadcasted_iota(jnp.int32, (SUBLANES, TQ), 0)
    tail_row = lax.broadcasted_iota(jnp.int32, (LANES - ns8, rq), 0)
    for g in range(NSA_N_KV):
        q32 = jnp.concatenate([qt_ref[(g * NSA_R + r) * dh:(g * NSA_R + r + 1) * dh, :] for r in range(NSA_R)], axis=1)
        qt = (q32 * scale).astype(BF16)
        qts.append((q32 * (scale * LOG2E)).astype(BF16))
        has_far = near0 > CMP_FRONT

        @pl.when(has_far)
        def _():
            cmp_branch(g, qt, True)

        @pl.when(jnp.logical_not(has_far))
        def _():
            cmp_branch(g, qt, False)

        imp = jnp.where(forced, FORCE_SCORE, imp_ref[0:ns8, :])
        imp = jnp.where(reachable, imp, -FORCE_SCORE)
        blocks = [imp[SUBLANES * kb:SUBLANES * (kb + 1)] for kb in range(ns8 // SUBLANES)]
        cnts = [jnp.zeros((SUBLANES, TQ), F32) for _ in blocks]
        for sp in range(n_sel):
            other = imp[sp:sp + 1, :]
            for kb, blk in enumerate(blocks):
                if kb > sp // SUBLANES:
                    ahead = other >= blk
                elif kb < sp // SUBLANES:
                    ahead = other > blk
                else:
                    ahead = (other > blk) | ((other == blk) & (sub > sp % SUBLANES))
                cnts[kb] = cnts[kb] + jnp.where(ahead, 1.0, 0.0)
        cnt = jnp.concatenate(cnts, axis=0)
        unsel = jnp.where((cnt < SEL_TOPK) & (imp > -0.5 * FORCE_SCORE), 0.0, NEG)
        zero = jnp.zeros((dh, rq), BF16)
        q_rows = [qts[g], zero] if g % 2 == 0 else [zero, qts[g]]
        tail = jnp.where(tail_row == 0, trel_ref[g, far_idx, 0:1, :], 0.0)
        qaug_ref[g] = jnp.concatenate(q_rows + [jnp.concatenate([unsel] * NSA_R, axis=1).astype(BF16),
                                                tail.astype(BF16)], axis=0)

    def reset():
        m_ref[...] = jnp.full(m_ref.shape, NEG, F32)
        l_ref[...] = jnp.zeros(l_ref.shape, F32)
        acc_ref[...] = jnp.zeros(acc_ref.shape, F32)

    def online_step(g, s, v_t):
        m_old = m_ref[g]
        m_new = jnp.maximum(m_old, jnp.max(s, axis=0, keepdims=True))
        alpha = jnp.exp2(m_old - m_new)
        p = jnp.exp2(s - m_new)
        l_ref[g] = l_ref[g] * alpha + jnp.sum(p, axis=0, keepdims=True)
        acc_ref[g] = acc_ref[g] * alpha + lax.dot_general(v_t, p.astype(BF16), TN_DIMS, preferred_element_type=F32)
        m_ref[g] = m_new

    def mask_heads(s, ok):
        return jnp.concatenate([jnp.where(ok, s[:, r * TQ:(r + 1) * TQ], NEG) for r in range(NSA_R)], axis=1)

    reset()
    kt2 = SEL_STEP_TILES * KT

    def sel_step(kt, mode, n_tiles=SEL_STEP_TILES):
        nk = n_tiles * KT
        k0 = pl.multiple_of(kt * kt2, kt2)
        e_blk = lax.broadcasted_iota(jnp.int32, (nk, LANES), 0) // SEL_BLOCK
        e_lane = lax.broadcasted_iota(jnp.int32, (nk, LANES), 1)
        onehot = e_lane == kt * (kt2 // SEL_BLOCK) + e_blk
        if mode == "far":
            onehot = onehot | (e_lane == ns8)
        e_aug = jnp.where(onehot, 1.0, 0.0).astype(BF16)
        delta = t - SEL_STEP_TILES * kt
        idx = [jnp.clip(delta - i, 0, far_idx) for i in range(n_tiles)]
        for pair in range(NSA_N_KV // 2):
            k_aug = jnp.concatenate([kv_ref[pl.ds(WINDOW + k0, nk), pair * LANES:(pair + 1) * LANES], e_aug], axis=1)
            for g in (2 * pair, 2 * pair + 1):
                s = jnp.dot(k_aug, qaug_ref[g], preferred_element_type=F32)
                if mode != "far":
                    s = s + jnp.concatenate([trel_ref[g, i] for i in idx], axis=0)
                if mode == "last":
                    kj = lax.broadcasted_iota(jnp.int32, (nk, TQ), 0)
                    qi = lax.broadcasted_iota(jnp.int32, (nk, TQ), 1)
                    s = mask_heads(s, t * TQ + qi >= k0 + kj)
                s_ref[g, 0:nk, :] = s
        for g in range(NSA_N_KV):
            online_step(g, s_ref[g, 0:nk, :],
                        kv_ref[pl.ds(WINDOW + k0, nk), NSA_KV_W + g * dh:NSA_KV_W + (g + 1) * dh])

    n_steps = t // SEL_STEP_TILES + 1
    n_far_steps = jnp.maximum(t - far_idx + 1, 0) // SEL_STEP_TILES

    def far_body(kt, carry):
        sel_step(kt, "far")
        return carry

    def near_body(kt, carry):
        sel_step(kt, "near")
        return carry

    lax.fori_loop(0, n_far_steps, far_body, 0)
    lax.fori_loop(n_far_steps, n_steps - 1, near_body, 0)
    short_last = t % SEL_STEP_TILES < SEL_STEP_TILES // 2

    @pl.when(short_last)
    def _():
        sel_step(n_steps - 1, "last", SEL_STEP_TILES // 2)

    @pl.when(jnp.logical_not(short_last))
    def _():
        sel_step(n_steps - 1, "last")
    for g in range(NSA_N_KV):
        ob_ref[1, g] = acc_ref[g] / l_ref[g]

    assert SEL_STEP_TILES >= 2

    reset()
    row0 = pl.multiple_of(t * TQ, TQ)
    for deltas in ((0,), (2, 1), (4, 3)):
        n_keys = KT * len(deltas)
        kjw = lax.broadcasted_iota(jnp.int32, (n_keys, TQ), 0)
        qiw = lax.broadcasted_iota(jnp.int32, (n_keys, TQ), 1)
        dist = deltas[0] * KT + qiw - kjw
        ok = (dist >= 0) & (dist <= WINDOW) & (row0 + kjw >= deltas[0] * KT)
        k0 = row0 + (WINDOW - deltas[0] * KT)
        for g in range(NSA_N_KV):
            k_t = kv_ref[pl.ds(k0, n_keys), 2 * NSA_KV_W + g * dh:2 * NSA_KV_W + (g + 1) * dh]
            s = jnp.dot(k_t, qts[g], preferred_element_type=F32)
            s = s + jnp.concatenate([trel_ref[g, d] for d in deltas], axis=0)
            s_ref[g, 0:n_keys, :] = mask_heads(s, ok)
        for g in range(NSA_N_KV):
            online_step(g, s_ref[g, 0:n_keys, :],
                        kv_ref[pl.ds(k0, n_keys), 3 * NSA_KV_W + g * dh:3 * NSA_KV_W + (g + 1) * dh])
    for g in range(NSA_N_KV):
        ob_ref[2, g] = acc_ref[g] / l_ref[g]

    gate = jax.nn.sigmoid(gate_ref[...])
    z_refs = (z0_ref, z1_ref, z2_ref)
    for h in range(NSA_N_HEADS):
        g, r = divmod(h, NSA_R)
        mixed = None
        for c in range(N_BRANCH):
            o_hc = ob_ref[c, g, :, r * TQ:(r + 1) * TQ]
            term = o_hc * _silu(z_refs[c][h * dh:(h + 1) * dh, :]) * gate[h * N_BRANCH + c:h * N_BRANCH + c + 1, :]
            mixed = term if mixed is None else mixed + term
        o_ref[h * dh:(h + 1) * dh, :] = mixed.astype(o_ref.dtype)


def _nsa_prompt_tables(rel_bias):
    n_dist = max((N_REL_TILES + 1) * KT, CMP_STRIDE * CMP_FRONT + TQ)
    vec = _bias_by_dist(rel_bias, n_dist)
    near = _toeplitz(vec, 0, KT, N_REL_TILES * TQ, 1).reshape(NSA_N_HEADS, KT, N_REL_TILES, TQ)
    near = _lanes_by_group(near.transpose(0, 2, 1, 3).reshape(NSA_N_HEADS, N_REL_TILES * KT, TQ))
    near = near.reshape(NSA_N_KV, N_REL_TILES, KT, NSA_R * TQ)
    far = jnp.broadcast_to(rel_bias.astype(F32)[REL_BUCKETS - 1].reshape(NSA_N_HEADS, 1, 1), (NSA_N_HEADS, KT, TQ))
    far = _lanes_by_group(far)
    trel = jnp.concatenate([near, far[:, None]], axis=1) * LOG2E
    tcn = _lanes_by_group(_toeplitz(vec, CMP_STRIDE * CMP_FRONT - (CMP_BLOCK - 1), CMP_NEAR, TQ, CMP_STRIDE))
    return trel, tcn, far[:, :SUBLANES, :]


def _nsa_layer_prompt(xp, prm, layer, n_layers, kv_bufs):
    norm_w, w_in, pe, w1, w2, w_out, rel_bias = prm
    bsz, t, d = xp.shape
    assert t % (SEL_STEP_TILES * KT) == 0 and t % PAGE_SIZE == 0 and t >= WINDOW
    n_tiles = t // TQ
    ns = t // SEL_BLOCK
    assert ns <= LANES
    x2 = xp.reshape(bsz * t, d)
    w = _nsa_in_weights(w_in)
    w_qzg = jnp.concatenate([w[:, COL_Q:COL_Q + NSA_Q_W], w[:, COL_Z:COL_Z + N_BRANCH * NSA_Q_W], w[:, COL_GATE:]], axis=1)
    ut = _norm_matmul_t(x2, norm_w, w_qzg.T.astype(BF16), bsz, w_qzg.shape[1])
    w_kvt = jnp.concatenate([w[:, COL_KV:COL_KV + 4 * NSA_KV_W], w[:, COL_KVW:COL_KVW + 2 * NSA_KV_W]], axis=1)
    kv_bufs = _kv_proj_t(x2, norm_w, w_kvt.T.astype(BF16), bsz, layer, n_layers, 4 * NSA_KV_W, kv_bufs)
    w_kv = jnp.concatenate([w[:, COL_KV + 2 * NSA_KV_W:COL_KV + 4 * NSA_KV_W], w[:, COL_KVW:COL_KVW + 2 * NSA_KV_W]],
                           axis=1).astype(BF16)
    kvn = _norm_matmul(x2, norm_w, w_kv, 4 * NSA_KV_W, out_dtype=BF16)
    t_pad = WINDOW + t
    kvn = jnp.pad(kvn.reshape(bsz, t, 4 * NSA_KV_W), ((0, 0), (WINDOW, 0), (0, 0))).reshape(bsz * t_pad, 4 * NSA_KV_W)
    n_pages = t // PAGE_SIZE
    kc, vc = _compress(kv_bufs[0], lambda b, p, ids: (b, layer, 0, p), bsz, n_pages, jnp.zeros((1, 1), jnp.int32),
                       pe, w1, w2)
    nh = kc.shape[1]
    n_far = _round_up(CMP_FRONT + nh, LANES)
    padc = ((0, 0), (CMP_FRONT, n_far - CMP_FRONT - nh), (0, 0))
    kcp = jnp.pad(kc, padc)
    vcp = jnp.pad(vc, padc)
    ov = np.zeros((n_far, LANES), np.float32)
    ov[CMP_FRONT:CMP_FRONT + nh - 1, :ns] = _overlap_matrix(nh - 1, ns)
    trel, tcn, cfar = _nsa_prompt_tables(rel_bias)
    rq = NSA_R * TQ
    const2 = lambda b, i: (0, 0)
    const3 = lambda b, i: (0, 0, 0)
    const4 = lambda b, i: (0, 0, 0, 0)
    zb = 1
    gate_blk = (1 + N_BRANCH) * NSA_Q_W // LANES
    once = dict(pipeline_mode=pl.Buffered(1))
    mixed_t = pl.pallas_call(
        functools.partial(_nsa_attn_kernel, n_sel=ns, n_cmp_rows=max(nh - CMP_NEAR, SUBLANES)),
        out_shape=jax.ShapeDtypeStruct((bsz, NSA_Q_W, t), BF16),
        grid=(bsz, n_tiles),
        in_specs=[
            pl.BlockSpec((None, NSA_Q_W, TQ), lambda b, i: (b, 0, i)),
            pl.BlockSpec((t_pad, 4 * NSA_KV_W), lambda b, i: (b, 0), **once),
            pl.BlockSpec((1, n_far, NSA_KV_W), lambda b, i: (b, 0, 0), **once),
            pl.BlockSpec((1, n_far, NSA_KV_W), lambda b, i: (b, 0, 0), **once),
            pl.BlockSpec((n_far, LANES), const2, **once),
            pl.BlockSpec((NSA_N_KV, N_REL_TILES + 1, KT, rq), const4, **once),
            pl.BlockSpec((NSA_N_KV, CMP_NEAR, rq), const3, **once),
            pl.BlockSpec((NSA_N_KV, SUBLANES, rq), const3, **once),
            pl.BlockSpec((None, LANES, TQ), lambda b, i: (b, gate_blk, i)),
            pl.BlockSpec((None, NSA_Q_W, TQ), lambda b, i: (b, zb, i)),
            pl.BlockSpec((None, NSA_Q_W, TQ), lambda b, i: (b, zb + 1, i)),
            pl.BlockSpec((None, NSA_Q_W, TQ), lambda b, i: (b, zb + 2, i)),
        ],
        out_specs=pl.BlockSpec((None, NSA_Q_W, TQ), lambda b, i: (b, 0, i)),
        scratch_shapes=[
            pltpu.VMEM((NSA_N_KV, 2 * LANES, rq), BF16),
            pltpu.VMEM((LANES, TQ), F32),
            pltpu.VMEM((NSA_N_KV, 1, rq), F32),
            pltpu.VMEM((NSA_N_KV, 1, rq), F32),
            pltpu.VMEM((NSA_N_KV, NSA_HEAD_DIM, rq), F32),
            pltpu.VMEM((N_BRANCH, NSA_N_KV, NSA_HEAD_DIM, rq), F32),
            pltpu.VMEM((NSA_N_KV, SEL_STEP_TILES * KT, rq), F32),
        ],
        compiler_params=_cparams(("parallel", "arbitrary")),
        name="nsa_attn",
    )(ut, kvn, kcp, vcp, jnp.asarray(ov), trel, tcn, cfar, ut, ut, ut, ut)
    out = _matmul_res_t(mixed_t, w_out.astype(BF16), x2).reshape(bsz, t, d)
    return out, kv_bufs


def _block_diag_q(q2):
    hg = lax.broadcasted_iota(jnp.int32, q2.shape, 0) // NSA_R
    return jnp.concatenate([jnp.where(hg == g, q2, 0.0) for g in range(NSA_N_KV)], axis=1)


def _group_lanes(o):
    dh = NSA_HEAD_DIM
    hg = lax.broadcasted_iota(jnp.int32, (o.shape[0], dh), 0) // NSA_R
    return sum(jnp.where(hg == g, o[:, g * dh:(g + 1) * dh], 0.0) for g in range(NSA_N_KV))


def _nsa_step_kernel(*refs, n_pp, past_len, n_cmp, n_sel_pad):
    page_refs = refs[1:1 + n_pp]
    (q_ref, kvn_ref, kwn_ref, gate_ref, z_ref, kc_ref, vc_ref, win_ref, ovp_ref, bsel_ref, bcmp_ref, bwin_ref, b0_ref,
     o_ref, sel_ref, m_ref, l_ref, acc_ref, oc_ref) = refs[1 + n_pp:]
    step = pl.program_id(1)
    nh = NSA_N_HEADS
    dh = NSA_HEAD_DIM
    qbd = _block_diag_q(q_ref[0] * (dh ** -0.5))
    qbd16 = qbd.astype(BF16)
    cur = past_len // SEL_BLOCK

    @pl.when(step == 0)
    def _():
        s_c = lax.dot_general(qbd16, kc_ref[0].astype(BF16), NT_DIMS, preferred_element_type=F32) + bcmp_ref[...]
        cid = lax.broadcasted_iota(jnp.int32, s_c.shape, 1)
        (p_c,) = _masked_softmax_parts([s_c], [cid < n_cmp], axis=1)
        oc_ref[...] = _group_lanes(jnp.dot(p_c.astype(BF16), vc_ref[0].astype(BF16), preferred_element_type=F32))
        gr = lax.broadcasted_iota(jnp.int32, (SUBLANES, nh), 0)
        gh = lax.broadcasted_iota(jnp.int32, (SUBLANES, nh), 1) // NSA_R
        pg = jnp.dot((gr == gh).astype(F32), p_c, precision=HIGHEST, preferred_element_type=F32)
        imp = jnp.dot(pg, ovp_ref[...], precision=HIGHEST, preferred_element_type=F32)
        sid = lax.broadcasted_iota(jnp.int32, imp.shape, 1)
        forced = (sid == 0) | (sid == cur) | (sid == cur - 1)
        imp = jnp.where(forced, FORCE_SCORE, imp)
        imp = jnp.where(sid * SEL_BLOCK <= past_len, imp, -FORCE_SCORE)
        er = lax.broadcasted_iota(jnp.int32, (n_sel_pad, n_sel_pad), 0)
        ec = lax.broadcasted_iota(jnp.int32, (n_sel_pad, n_sel_pad), 1)
        rows = []
        for g in range(NSA_N_KV):
            row = imp[g:g + 1, :]
            col = jnp.sum(jnp.where(er == ec, row, 0.0), axis=1, keepdims=True)
            ahead = (col > row) | ((col == row) & (er < ec))
            cnt = jnp.sum(ahead.astype(jnp.int32), axis=0, keepdims=True)
            sel = ((cnt < SEL_TOPK) & (row > -0.5 * FORCE_SCORE)).astype(F32)
            rows.extend([sel] * NSA_R)
        sel_ref[...] = jnp.concatenate(rows, axis=0)
        m_ref[...] = jnp.full(m_ref.shape, NEG, F32)
        l_ref[...] = jnp.zeros(l_ref.shape, F32)
        acc_ref[...] = jnp.zeros(acc_ref.shape, F32)

    def online(s):
        m_old = m_ref[...]
        m_new = jnp.maximum(m_old, jnp.max(s, axis=-1, keepdims=True))
        alpha = jnp.exp(m_old - m_new)
        pr = jnp.exp(s - m_new)
        l_ref[...] = l_ref[...] * alpha + jnp.sum(pr, axis=-1, keepdims=True)
        m_ref[...] = m_new
        return alpha, pr

    k_all = jnp.concatenate([r[0:NSA_KV_W, :] for r in page_refs], axis=1).astype(BF16)
    v_all = jnp.concatenate([r[NSA_KV_W:2 * NSA_KV_W, :] for r in page_refs], axis=1).astype(BF16)
    n_tok = n_pp * PAGE_SIZE
    s = jnp.dot(qbd16, k_all, preferred_element_type=F32) + bsel_ref[0]
    eb = lax.broadcasted_iota(jnp.int32, (n_sel_pad, n_tok), 0)
    ej = lax.broadcasted_iota(jnp.int32, (n_sel_pad, n_tok), 1)
    expand = (eb == step * (n_tok // SEL_BLOCK) + ej // SEL_BLOCK).astype(BF16)
    chosen = jnp.dot(sel_ref[...].astype(BF16), expand, preferred_element_type=F32)
    s = jnp.where(chosen > 0.5, s, NEG)
    alpha, pr = online(s)
    acc_ref[...] = acc_ref[...] * alpha + lax.dot_general(pr.astype(BF16), v_all, NT_DIMS, preferred_element_type=F32)

    @pl.when(step == pl.num_programs(1) - 1)
    def _():
        b0 = b0_ref[:, 0:1]
        kn = kvn_ref[0, :, 2 * NSA_KV_W:3 * NSA_KV_W]
        vn = kvn_ref[0, :, 3 * NSA_KV_W:4 * NSA_KV_W]
        s_n = jnp.sum(qbd * kn, axis=-1, keepdims=True) + b0
        s_n = jnp.where(sel_ref[:, cur:cur + 1] > 0.5, s_n, NEG)
        alpha, pr = online(s_n)
        acc = acc_ref[...] * alpha + pr * vn
        o_s = _group_lanes(acc / l_ref[...])
        wk = win_ref[0, 0:NSA_KV_W, :].astype(BF16)
        wv = win_ref[0, NSA_KV_W:2 * NSA_KV_W, :].astype(BF16)
        n_win = win_ref.shape[2]
        s_w = jnp.dot(qbd16, wk, preferred_element_type=F32) + bwin_ref[...]
        wi = lax.broadcasted_iota(jnp.int32, s_w.shape, 1)
        s_wn = jnp.sum(qbd * kwn_ref[0, :, 0:NSA_KV_W], axis=-1, keepdims=True) + b0
        p_w, p_wn = _masked_softmax_parts([s_w, s_wn], [n_win - wi <= WINDOW, jnp.full(s_wn.shape, True)], axis=1)
        o_w = _group_lanes(lax.dot_general(p_w.astype(BF16), wv, NT_DIMS, preferred_element_type=F32)
                           + p_wn * kwn_ref[0, :, NSA_KV_W:2 * NSA_KV_W])
        gate = jax.nn.sigmoid(gate_ref[0])
        mixed = (oc_ref[...] * _silu(z_ref[0, 0]) * gate[:, 0:1] + o_s * _silu(z_ref[0, 1]) * gate[:, 1:2]
                 + o_w * _silu(z_ref[0, 2]) * gate[:, 2:3])
        o_ref[0] = mixed


def _nsa_layer_sample(xs_in, cache_t, win_t, page_table, prm, layer, n_layers):
    norm_w, w_in, pe, w1, w2, w_out, rel_bias = prm
    bsz, t, d = xs_in.shape
    assert t == 1
    n_pages = page_table.shape[1]
    past_len = n_pages * PAGE_SIZE
    n_win = win_t.shape[3]
    nh, dh = NSA_N_HEADS, NSA_HEAD_DIM
    x2 = xs_in.reshape(bsz, d)
    u = _norm_matmul(x2, norm_w, _nsa_in_weights(w_in).astype(BF16), _pick_tn(NSA_N_PAD, 1152))
    ids = (page_table * n_layers + layer).astype(jnp.int32)
    kc, vc = _compress(cache_t, lambda b, p, ids: (ids[b, p], 0, 0), bsz, n_pages, ids, pe, w1, w2)
    n_cmp_pad = kc.shape[1]
    n_cmp = n_cmp_pad - 1
    ns = past_len // SEL_BLOCK + 1
    n_sel_pad = _round_up(ns, LANES)
    ov = np.zeros((n_cmp_pad, n_sel_pad), np.float32)
    ov[:n_cmp, :ns] = _overlap_matrix(n_cmp, ns)
    n_pp = math.gcd(STEP_PAGES, n_pages)
    n_steps = n_pages // n_pp
    n_tok = n_pp * PAGE_SIZE
    vec = _bias_by_dist(rel_bias, past_len + 1)
    bsel = jnp.moveaxis(jnp.flip(vec[:, 1:], axis=1).reshape(nh, n_steps, n_tok), 0, 1)
    bcmp = jnp.flip(vec[:, :past_len - (CMP_BLOCK - 1) + 1], axis=1)[:, ::CMP_STRIDE]
    bcmp = jnp.pad(bcmp, ((0, 0), (0, n_cmp_pad - bcmp.shape[1])))
    bwin = jnp.flip(vec[:, 1:n_win + 1], axis=1)
    b0 = jnp.broadcast_to(vec[:, 0:1], (nh, LANES))
    q3 = u[:, COL_Q:COL_Q + NSA_Q_W].reshape(bsz, nh, dh)
    kv_new = u[:, COL_KV:COL_KV + 4 * NSA_KV_W]
    kw_new = u[:, COL_KVW:COL_KVW + 2 * NSA_KV_W]
    gate = u[:, COL_GATE:COL_GATE + nh * N_BRANCH].reshape(bsz, nh, N_BRANCH)
    z4 = u[:, COL_Z:COL_Z + N_BRANCH * NSA_Q_W].reshape(bsz, N_BRANCH, nh, dh)
    per_b3 = lambda b, s, ids: (b, 0, 0)
    const2 = lambda b, s, ids: (0, 0)
    page_specs = [pl.BlockSpec((None, 2 * NSA_KV_W, PAGE_SIZE),
                               functools.partial(lambda b, s, ids, k: (ids[b, s * n_pp + k], 1, 0), k=k))
                  for k in range(n_pp)]
    grid_spec = pltpu.PrefetchScalarGridSpec(
        num_scalar_prefetch=1,
        grid=(bsz, n_steps),
        in_specs=page_specs + [
            pl.BlockSpec((1, nh, dh), per_b3),
            pl.BlockSpec((1, 1, 4 * NSA_KV_W), per_b3),
            pl.BlockSpec((1, 1, 2 * NSA_KV_W), per_b3),
            pl.BlockSpec((1, nh, N_BRANCH), per_b3),
            pl.BlockSpec((1, N_BRANCH, nh, dh), lambda b, s, ids: (b, 0, 0, 0)),
            pl.BlockSpec((1, n_cmp_pad, NSA_KV_W), per_b3),
            pl.BlockSpec((1, n_cmp_pad, NSA_KV_W), per_b3),
            pl.BlockSpec((None, 1, 2 * NSA_KV_W, n_win), lambda b, s, ids: (layer, b, 0, 0)),
            pl.BlockSpec((n_cmp_pad, n_sel_pad), const2),
            pl.BlockSpec((1, nh, n_tok), lambda b, s, ids: (s, 0, 0)),
            pl.BlockSpec((nh, n_cmp_pad), const2),
            pl.BlockSpec((nh, n_win), const2),
            pl.BlockSpec((nh, LANES), const2),
        ],
        out_specs=pl.BlockSpec((1, nh, dh), per_b3),
        scratch_shapes=[
            pltpu.VMEM((nh, n_sel_pad), F32),
            pltpu.VMEM((nh, 1), F32),
            pltpu.VMEM((nh, 1), F32),
            pltpu.VMEM((nh, NSA_KV_W), F32),
            pltpu.VMEM((nh, dh), F32),
        ],
    )
    mixed = pl.pallas_call(
        functools.partial(_nsa_step_kernel, n_pp=n_pp, past_len=past_len, n_cmp=n_cmp, n_sel_pad=n_sel_pad),
        out_shape=jax.ShapeDtypeStruct((bsz, nh, dh), F32),
        grid_spec=grid_spec,
        compiler_params=_cparams(("parallel", "arbitrary")),
        name="nsa_step",
    )(ids, *([cache_t] * n_pp), q3, kv_new.reshape(bsz, 1, -1), kw_new.reshape(bsz, 1, -1), gate, z4, kc, vc, win_t,
      jnp.asarray(ov), bsel, bcmp, bwin, b0)
    out = _matmul_res(mixed.reshape(bsz, NSA_Q_W).astype(BF16), w_out.astype(BF16), x2).reshape(bsz, 1, d)
    return out, kv_new, kw_new


def kernel(x_prompt, x_sample, state_ssm, state_conv, cache_kv, cache_win, page_table, rel_bias, final_norm,
           ssd_norm, ssd_w_in, ssd_conv_w, ssd_conv_b, ssd_dt_bias, ssd_a_log, ssd_d, ssd_gnorm, ssd_w_out,
           nsa_norm, nsa_w_in, nsa_cmp_pe, nsa_cmp_w1, nsa_cmp_w2, nsa_w_out):
    depth = ssd_norm.shape[0] + nsa_norm.shape[0]
    g, dh = NSA_N_KV, NSA_HEAD_DIM
    xp, xs = x_prompt, x_sample
    bp, tp, d = xp.shape
    bs = xs.shape[0]
    n_phys, n_nsa = cache_kv.shape[:2]
    win_len_s = cache_win.shape[2]
    win_len_p = min(WINDOW, tp)
    cache_t = cache_kv.transpose(0, 1, 3, 4, 5, 2).reshape(n_phys * n_nsa, 4 * NSA_KV_W, PAGE_SIZE)
    cwin_t = cache_win.transpose(0, 1, 3, 4, 5, 2).reshape(n_nsa, bs, 2 * NSA_KV_W, win_len_s)
    ssm_p, conv_p, kv_bufs = [], [], None
    ssm_s, conv_s, kv_s, win_s = [], [], [], []
    for layer in range(depth):
        j = layer // 2
        if layer % 2 == 0:
            prm = (ssd_norm[j], ssd_w_in[j], ssd_conv_w[j], ssd_conv_b[j], ssd_dt_bias[j], ssd_a_log[j],
                   ssd_d[j], ssd_gnorm[j], ssd_w_out[j])
            xp, c, s = _ssd_layer_prompt(xp, prm)
            conv_p.append(c)
            ssm_p.append(s)
            xs, c, s = _ssd_layer_sample(xs, state_conv[j], state_ssm, j, prm)
            conv_s.append(c)
            ssm_s.append(s)
        else:
            prm = (nsa_norm[j], nsa_w_in[j], nsa_cmp_pe[j], nsa_cmp_w1[j], nsa_cmp_w2[j], nsa_w_out[j], rel_bias)
            xp, kv_bufs = _nsa_layer_prompt(xp, prm, j, n_nsa, kv_bufs)
            xs, kv_new, kw_new = _nsa_layer_sample(xs, cache_t, cwin_t, page_table, prm, j, n_nsa)
            kv_s.append(kv_new.reshape(bs, 1, 4, g, dh))
            win_s.append(jnp.concatenate([cwin_t[j][:, :, 1:], kw_new[:, :, None]], axis=2))
    y_prompt = _rmsnorm(xp.reshape(bp * tp, d), final_norm).reshape(bp, tp, d)
    y_sample = _rmsnorm(xs.reshape(bs, d), final_norm).reshape(bs, 1, d)
    kv_prompt = kv_bufs[0].reshape(bp, n_nsa, 4, g, dh, tp).transpose(0, 5, 1, 2, 3, 4)
    win_prompt = kv_bufs[1][:, :, :, tp - win_len_p:].reshape(bp, n_nsa, 2, g, dh, win_len_p).transpose(1, 0, 5, 2, 3, 4)
    win_sample = jnp.stack(win_s).reshape(n_nsa, bs, 2, g, dh, win_len_s).transpose(0, 1, 5, 2, 3, 4)
    return (y_prompt, y_sample, jnp.stack(ssm_p), jnp.stack(conv_p), kv_prompt, win_prompt,
            jnp.stack(ssm_s), jnp.stack(conv_s), jnp.stack(kv_s, axis=2), win_sample)
```
